```python
import jax, jax.numpy as jnp
from jax import lax
import numpy as np

D_MODEL = 1024
BATCH = 4
SEQ = 8192
DEPTH = 2

GRID_W = 64
CTX_LEN = 256
HEAD_DIM = 64
ATTN_HEADS = 8
KV_HEADS = 2
Q_PER_KV = ATTN_HEADS // KV_HEADS
ATTN_WIDTH = ATTN_HEADS * HEAD_DIM
KV_WIDTH = KV_HEADS * HEAD_DIM
WINDOW = 128
BLOCK = 128
FOURIER_GROUPS = 4
FOURIER_GROUP_DIM = 64
FOURIER_WIDTH = FOURIER_GROUPS * FOURIER_GROUP_DIM
CONV_GROUPS = 4
CONV_WIDTH = 256
CONV_TAPS = 3
MIX_WIDTH = ATTN_WIDTH + FOURIER_WIDTH + CONV_WIDTH
PROJ_SIZES = (ATTN_WIDTH, KV_WIDTH, KV_WIDTH, ATTN_WIDTH,
              FOURIER_WIDTH, FOURIER_WIDTH,
              CONV_WIDTH, CONV_WIDTH, CONV_WIDTH, CONV_WIDTH)
PROJ_WIDTH = 2 * ATTN_WIDTH + 2 * KV_WIDTH + 2 * FOURIER_WIDTH + 4 * CONV_WIDTH
ROPE_FREQS = HEAD_DIM // 4
ROPE_BASE = 10000.0
NORM_EPS = 1e-6

kernel_name = 'hybrid_parallel_groups_flow_block'


def rmsnorm(x, g):
    xf = x.astype(jnp.float32)
    y = xf * lax.rsqrt(jnp.mean(xf * xf, axis=-1, keepdims=True) + NORM_EPS)
    return (y * g.astype(jnp.float32)).astype(x.dtype)


def split_cols(p, sizes):
    out = []
    start = 0
    for s in sizes:
        out.append(p[..., start:start + s])
        start += s
    return out


def axial_rope_tables(n):
    rows = n // GRID_W
    row = jnp.repeat(jnp.arange(rows, dtype=jnp.float32), GRID_W)
    col = jnp.tile(jnp.arange(GRID_W, dtype=jnp.float32), rows)
    inv_freq = jnp.power(ROPE_BASE, -jnp.arange(ROPE_FREQS, dtype=jnp.float32) / ROPE_FREQS)
    ang = jnp.concatenate([row[:, None] * inv_freq, col[:, None] * inv_freq], axis=-1)
    return jnp.cos(ang), jnp.sin(ang)


def apply_axial_rope(x, cos, sin):
    b, n, h, d = x.shape
    xr = x.astype(jnp.float32).reshape(b, n, h, 2, 2, ROPE_FREQS)
    x1 = xr[..., 0, :]
    x2 = xr[..., 1, :]
    cs = cos.reshape(n, 1, 2, ROPE_FREQS)
    sn = sin.reshape(n, 1, 2, ROPE_FREQS)
    out = jnp.stack([x1 * cs - x2 * sn, x1 * sn + x2 * cs], axis=-2)
    return out.reshape(b, n, h, d).astype(x.dtype)


def sink_softmax(s, sink_b):
    m = jnp.maximum(jnp.max(s, axis=-1, keepdims=True), sink_b)
    e = jnp.exp(s - m)
    return e / (jnp.sum(e, axis=-1, keepdims=True) + jnp.exp(sink_b - m))


def window_attention(q, k, v, kc, vc, sink_b):
    b, n = q.shape[:2]
    nb = n // BLOCK
    span = BLOCK + 2 * WINDOW
    nc = kc.shape[1]
    qb = (q * (HEAD_DIM ** -0.5)).reshape(b, nb, BLOCK, KV_HEADS, Q_PER_KV, HEAD_DIM).transpose(1, 0, 2, 3, 4, 5)
    pad = ((0, 0), (WINDOW, WINDOW), (0, 0), (0, 0))
    kp = jnp.pad(k, pad)
    vp = jnp.pad(v, pad)
    offs_q = jnp.arange(BLOCK)
    offs_k = jnp.arange(span) - WINDOW

    def one_block(args):
        qi, i = args
        start = i * BLOCK
        ks = lax.dynamic_slice_in_dim(kp, start, span, axis=1)
        vs = lax.dynamic_slice_in_dim(vp, start, span, axis=1)
        qpos = start + offs_q
        kpos = start + offs_k
        valid = (jnp.abs(qpos[:, None] - kpos[None, :]) <= WINDOW) & (kpos >= 0)[None, :] & (kpos < n)[None, :]
        s_loc = jnp.einsum('bqkgd,bnkd->bkgqn', qi, ks).astype(jnp.float32)
        s_loc = jnp.where(valid, s_loc, -jnp.inf)
        s_ctx = jnp.einsum('bqkgd,bnkd->bkgqn', qi, kc).astype(jnp.float32)
        p = sink_softmax(jnp.concatenate([s_ctx, s_loc], axis=-1), sink_b).astype(v.dtype)
        return (jnp.einsum('bkgqn,bnkd->bqkgd', p[..., :nc], vc)
                + jnp.einsum('bkgqn,bnkd->bqkgd', p[..., nc:], vs))

    o = lax.map(one_block, (qb, jnp.arange(nb)))
    return o.transpose(1, 0, 2, 3, 4, 5).reshape(b, n, ATTN_WIDTH)


def context_attention(q, k, v, sink_b):
    b, n = q.shape[:2]
    qg = (q * (HEAD_DIM ** -0.5)).reshape(b, n, KV_HEADS, Q_PER_KV, HEAD_DIM)
    s = jnp.einsum('bqkgd,bnkd->bkgqn', qg, k).astype(jnp.float32)
    p = sink_softmax(s, sink_b).astype(v.dtype)
    return jnp.einsum('bkgqn,bnkd->bqkgd', p, v).reshape(b, n, ATTN_WIDTH)


def fourier_mix(u, w_f):
    b, n, _ = u.shape
    ug = u.reshape(b, n, FOURIER_GROUPS, FOURIER_GROUP_DIM).astype(jnp.float32)
    f = jnp.fft.fft2(ug, axes=(1, 3), norm='ortho').real.astype(u.dtype)
    return jnp.einsum('bngc,gcd->bngd', f, w_f).reshape(b, n, FOURIER_WIDTH)


def short_conv_mix(z, b_gate, c_gate, w, bias):
    t = c_gate * z
    tp = jnp.pad(t, ((0, 0), (1, 1), (0, 0)))
    y = tp[:, :-2] * w[0] + tp[:, 1:-1] * w[1] + tp[:, 2:] * w[2] + bias
    return b_gate * y


def gated_merge(a, ga, f, gf, s, gc, w_out):
    h = jnp.concatenate([a * jax.nn.silu(ga), f * jax.nn.silu(gf), s * jax.nn.silu(gc)], axis=-1)
    return h @ w_out


def layer(x, ctx, c, c_ctx, w_mod, b_mod, g_pre, g_post, w_in, w_out, sink, w_fourier, conv_w, conv_b,
          cos, sin, update_ctx):
    b, n = x.shape[:2]
    nc = ctx.shape[1]
    shift, scale, gate = jnp.split(jax.nn.silu(c) @ w_mod + b_mod, 3, axis=-1)
    shift_c, scale_c, gate_c = jnp.split(jax.nn.silu(c_ctx) @ w_mod + b_mod, 3, axis=-1)
    sink_b = sink.astype(jnp.float32).reshape(1, KV_HEADS, Q_PER_KV, 1, 1)

    hx = rmsnorm(x, g_pre) * (1 + scale[:, None]) + shift[:, None]
    hc = rmsnorm(ctx, g_pre) * (1 + scale_c) + shift_c

    if update_ctx:
        qc, kc, vc, gac, ufc, gfc, zcc, bcc, ccc, gcc = split_cols(hc @ w_in, PROJ_SIZES)
    else:
        kc, vc = split_cols(hc @ w_in[:, ATTN_WIDTH:ATTN_WIDTH + 2 * KV_WIDTH], (KV_WIDTH, KV_WIDTH))
    kc = kc.reshape(b, nc, KV_HEADS, HEAD_DIM)
    vc = vc.reshape(b, nc, KV_HEADS, HEAD_DIM)

    q, k, v, ga, uf, gf, zc, bc, cc, gc = split_cols(hx @ w_in, PROJ_SIZES)
    q = apply_axial_rope(q.reshape(b, n, ATTN_HEADS, HEAD_DIM), cos, sin)
    k = apply_axial_rope(k.reshape(b, n, KV_HEADS, HEAD_DIM), cos, sin)
    v = v.reshape(b, n, KV_HEADS, HEAD_DIM)
    a = window_attention(q, k, v, kc, vc, sink_b)
    f = fourier_mix(uf, w_fourier)
    s = short_conv_mix(zc, bc, cc, conv_w, conv_b)
    y = gated_merge(a, ga, f, gf, s, gc, w_out)
    x_new = x + gate[:, None] * rmsnorm(y, g_post)

    if update_ctx:
        ac = context_attention(qc.reshape(b, nc, ATTN_HEADS, HEAD_DIM), kc, vc, sink_b)
        fc = fourier_mix(ufc, w_fourier)
        sc = short_conv_mix(zcc, bcc, ccc, conv_w, conv_b)
        yc = gated_merge(ac, gac, fc, gfc, sc, gcc, w_out)
        ctx = ctx + gate_c * rmsnorm(yc, g_post)
    return x_new, ctx


def setup_inputs(seed: int = 0) -> dict:
    key = jax.random.key(seed)
    ks = jax.random.split(key, 14)
    nrm = jax.random.normal
    x = nrm(ks[0], (BATCH, SEQ, D_MODEL), jnp.float32)
    c = nrm(ks[1], (BATCH, D_MODEL), jnp.float32)
    ctx = nrm(ks[2], (BATCH, CTX_LEN, D_MODEL), jnp.float32)
    c_ctx = nrm(ks[3], (D_MODEL,), jnp.float32)
    w_mod = nrm(ks[4], (DEPTH, D_MODEL, 3 * D_MODEL), jnp.float32) * (0.5 * D_MODEL ** -0.5)
    b_mod = 0.01 * nrm(ks[5], (DEPTH, 3 * D_MODEL), jnp.float32)
    g_pre = 1.0 + 0.05 * nrm(ks[6], (DEPTH, D_MODEL), jnp.float32)
    g_post = 1.0 + 0.05 * nrm(ks[7], (DEPTH, D_MODEL), jnp.float32)
    w_in = nrm(ks[8], (DEPTH, D_MODEL, PROJ_WIDTH), jnp.float32) * (D_MODEL ** -0.5)
    w_out = nrm(ks[9], (DEPTH, MIX_WIDTH, D_MODEL), jnp.float32) * (MIX_WIDTH ** -0.5)
    sink = 0.5 * nrm(ks[10], (DEPTH, ATTN_HEADS), jnp.float32)
    w_fourier = nrm(ks[11], (DEPTH, FOURIER_GROUPS, FOURIER_GROUP_DIM, FOURIER_GROUP_DIM), jnp.float32) * (FOURIER_GROUP_DIM ** -0.5)
    conv_w = nrm(ks[12], (DEPTH, CONV_TAPS, CONV_WIDTH), jnp.float32) * (CONV_TAPS ** -0.5)
    conv_b = 0.01 * nrm(ks[13], (DEPTH, CONV_WIDTH), jnp.float32)
    return {'x': x, 'c': c, 'ctx': ctx, 'c_ctx': c_ctx, 'w_mod': w_mod, 'b_mod': b_mod,
            'g_pre': g_pre, 'g_post': g_post, 'w_in': w_in, 'w_out': w_out, 'sink': sink,
            'w_fourier': w_fourier, 'conv_w': conv_w, 'conv_b': conv_b}


def reference(x, c, ctx, c_ctx, w_mod, b_mod, g_pre, g_post, w_in, w_out, sink, w_fourier, conv_w, conv_b):
    cos, sin = axial_rope_tables(x.shape[1])
    for l in range(DEPTH):
        x, ctx = layer(x, ctx, c, c_ctx, w_mod[l], b_mod[l], g_pre[l], g_post[l], w_in[l], w_out[l],
                       sink[l], w_fourier[l], conv_w[l], conv_b[l], cos, sin, l < DEPTH - 1)
    return x
```

```python
import functools

import numpy as np
import jax
import jax.numpy as jnp
from jax import lax
from jax.experimental import pallas as pl
from jax.experimental.pallas import tpu as pltpu

D_MODEL = 1024
GRID_W = 64
HEAD_DIM = 64
ATTN_HEADS = 8
KV_HEADS = 2
Q_PER_KV = ATTN_HEADS // KV_HEADS
ATTN_WIDTH = ATTN_HEADS * HEAD_DIM
KV_WIDTH = KV_HEADS * HEAD_DIM
WINDOW = 128
FOURIER_GROUPS = 4
FOURIER_GROUP_DIM = 64
FOURIER_WIDTH = FOURIER_GROUPS * FOURIER_GROUP_DIM
CONV_WIDTH = 256
MIX_WIDTH = ATTN_WIDTH + FOURIER_WIDTH + CONV_WIDTH
PROJ_WIDTH = 2 * ATTN_WIDTH + 2 * KV_WIDTH + 2 * FOURIER_WIDTH + 4 * CONV_WIDTH
ROPE_FREQS = HEAD_DIM // 4
ROPE_BASE = 10000.0
NORM_EPS = 1e-6

C_Q = 0
C_K = C_Q + ATTN_WIDTH
C_V = C_K + KV_WIDTH
C_GA = C_V + KV_WIDTH
C_UF = C_GA + ATTN_WIDTH
C_GF = C_UF + FOURIER_WIDTH
C_ZC = C_GF + FOURIER_WIDTH
C_BC = C_ZC + CONV_WIDTH
C_CC = C_BC + CONV_WIDTH
C_GC = C_CC + CONV_WIDTH

LANES = 128
SUBLANES = 8
BF16_ROWS = 16
VMEM_LIMIT = 52 * 1024 * 1024

MOD_ROWS = 8
BF16 = jnp.bfloat16
F32 = jnp.float32

FFT_N2 = 64
K1_GROUP = SUBLANES
BLOCK_Q = 128


def _silu(x):
    return x / (1.0 + jnp.exp(-x))


def _dot(a, b):
    return jnp.dot(a, b, preferred_element_type=F32)


def _dot_nt(a, b):
    return lax.dot_general(a, b, (((1,), (1,)), ((), ())), preferred_element_type=F32)


def _cparams(*sem):
    return pltpu.CompilerParams(dimension_semantics=sem, vmem_limit_bytes=VMEM_LIMIT)


def _head_pair_perm():
    cols = []
    for g in range(Q_PER_KV):
        for kvh in range(KV_HEADS):
            h = kvh * Q_PER_KV + g
            cols.extend(range(h * HEAD_DIM, (h + 1) * HEAD_DIM))
    return np.asarray(cols, dtype=np.int32)


def _rope_tables(n):
    t = np.arange(n)
    row = (t // GRID_W).astype(np.float64)
    col = (t % GRID_W).astype(np.float64)
    inv = ROPE_BASE ** (-np.arange(ROPE_FREQS, dtype=np.float64) / ROPE_FREQS)
    ar = row[:, None] * inv
    ac = col[:, None] * inv
    z = np.zeros_like(ar)
    cos_h = np.concatenate([np.cos(ar), np.cos(ar), np.cos(ac), np.cos(ac)], axis=1)
    sin_up = np.concatenate([-np.sin(ar), z, -np.sin(ac), z], axis=1)
    sin_dn = np.concatenate([z, np.sin(ar), z, np.sin(ac)], axis=1)
    rep = LANES // HEAD_DIM
    return tuple(np.tile(a, (1, rep)).astype(np.float32) for a in (cos_h, sin_up, sin_dn))


def _channel_dft():
    c = np.arange(FOURIER_GROUP_DIM)
    ang = 2.0 * np.pi * np.outer(c, c) / FOURIER_GROUP_DIM
    groups_per_half = (LANES // FOURIER_GROUP_DIM)
    eye = np.eye(groups_per_half)
    return (np.kron(eye, np.cos(ang)).astype(np.float32),
            np.kron(eye, np.sin(ang)).astype(np.float32))


def _stage1_mats(n):
    n1_len = n // FFT_N2
    k1 = np.arange(n1_len)[:, None]
    n1 = np.arange(n1_len)[None, :]
    out = np.empty((FFT_N2, 2 * n1_len, 2 * n1_len), np.float32)
    for n2 in range(FFT_N2):
        ang = 2.0 * np.pi * ((k1 * (FFT_N2 * n1 + n2)) % n) / n
        ce, se = np.cos(ang), np.sin(ang)
        out[n2] = np.block([[ce, se], [-se, ce]])
    return out


def _stage2_mat():
    k2 = np.arange(FFT_N2)
    ang = 2.0 * np.pi * np.outer(k2, k2) / FFT_N2
    cs = np.stack([np.cos(ang), np.sin(ang)], axis=-1)
    eye = np.eye(K1_GROUP)
    m = np.einsum('knp,rs->krnps', cs, eye)
    return m.reshape(FFT_N2 * K1_GROUP, FFT_N2 * 2 * K1_GROUP).astype(np.float32)


def _dense_dft(n):
    t = np.arange(n)
    ang = 2.0 * np.pi * (np.outer(t, t) % n) / n
    return np.concatenate([np.cos(ang), np.sin(ang)], axis=1).astype(np.float32)


def _mod_kernel(c_ref, w_ref, b_ref, o_ref):
    o_ref[0] = _dot(_silu(c_ref[...]), w_ref[0]) + b_ref[0]


def _modulation(c_rows, w_mod, b_mod):
    depth = w_mod.shape[0]
    return pl.pallas_call(
        _mod_kernel,
        grid=(depth, 3),
        in_specs=[pl.BlockSpec((MOD_ROWS, D_MODEL), lambda l, j: (0, 0)),
                  pl.BlockSpec((1, D_MODEL, D_MODEL), lambda l, j: (l, 0, j)),
                  pl.BlockSpec((1, 1, D_MODEL), lambda l, j: (l, 0, j))],
        out_specs=pl.BlockSpec((1, MOD_ROWS, D_MODEL), lambda l, j: (l, 0, j)),
        out_shape=jax.ShapeDtypeStruct((depth, MOD_ROWS, 3 * D_MODEL), F32),
        compiler_params=_cparams("arbitrary", "arbitrary"),
        name="modulation",
    )(c_rows, w_mod, b_mod.reshape(depth, 1, 3 * D_MODEL))


def _inproj_kernel(*refs, rope, ctx_row):
    if rope:
        (x_ref, shift_ref, scale_ref, g_ref, w_ref, cos_ref, sup_ref, sdn_ref,
         q_ref, k_ref, v_ref, sga_ref, uf_ref, sgf_ref, t_ref, bg_ref) = refs
    else:
        (x_ref, shift_ref, scale_ref, g_ref, w_ref,
         q_ref, k_ref, v_ref, sga_ref, uf_ref, sgf_ref, t_ref, bg_ref) = refs
    row = pl.program_id(0) if ctx_row is None else ctx_row
    shift = shift_ref[0, pl.ds(row, 1), :]
    scale = scale_ref[0, pl.ds(row, 1), :]
    x = x_ref[0]
    r = lax.rsqrt(jnp.mean(x * x, axis=-1, keepdims=True) + NORM_EPS)
    h = ((x * r) * (g_ref[0] * (1.0 + scale)) + shift).astype(BF16)

    def proj(c0, width):
        return _dot(h, w_ref[0, :, c0:c0 + width])

    def rotate(y):
        if not rope:
            return y
        return (y * cos_ref[...] + pltpu.roll(y, LANES - ROPE_FREQS, 1) * sup_ref[...]
                + pltpu.roll(y, ROPE_FREQS, 1) * sdn_ref[...])

    q = proj(C_Q, ATTN_WIDTH)
    for p in range(ATTN_WIDTH // LANES):
        sl = slice(p * LANES, (p + 1) * LANES)
        q_ref[0, :, sl] = (rotate(q[:, sl]) * (HEAD_DIM ** -0.5)).astype(BF16)
    k_ref[0] = rotate(proj(C_K, KV_WIDTH)).astype(BF16)
    v_ref[0] = proj(C_V, KV_WIDTH).astype(BF16)
    sga_ref[0] = _silu(proj(C_GA, ATTN_WIDTH)).astype(BF16)
    uf = proj(C_UF, FOURIER_WIDTH)
    for s in range(FOURIER_WIDTH // LANES):
        uf_ref[0, s] = uf[:, s * LANES:(s + 1) * LANES]
    sgf_ref[0] = _silu(proj(C_GF, FOURIER_WIDTH)).astype(BF16)
    t_ref[0] = (proj(C_CC, CONV_WIDTH) * proj(C_ZC, CONV_WIDTH)).astype(BF16)
    bg_ref[0] = (proj(C_BC, CONV_WIDTH) * _silu(proj(C_GC, CONV_WIDTH))).astype(BF16)


def _in_projection(x, mod, g_pre, w_in, layer, *, rope_tabs, ctx_row, tm):
    b, n, _ = x.shape
    rope = rope_tabs is not None
    row3 = lambda width: pl.BlockSpec((1, tm, width), lambda bi, i: (bi, i, 0))
    in_specs = [row3(D_MODEL),
                pl.BlockSpec((1, MOD_ROWS, D_MODEL), lambda bi, i: (layer, 0, 0)),
                pl.BlockSpec((1, MOD_ROWS, D_MODEL), lambda bi, i: (layer, 0, 1)),
                pl.BlockSpec((1, 1, D_MODEL), lambda bi, i: (layer, 0, 0)),
                pl.BlockSpec((1, D_MODEL, PROJ_WIDTH), lambda bi, i: (layer, 0, 0))]
    args = [x, mod, mod, g_pre, w_in]
    if rope:
        in_specs += [pl.BlockSpec((tm, LANES), lambda bi, i: (i, 0))] * 3
        args += list(rope_tabs)
    widths = (ATTN_WIDTH, KV_WIDTH, KV_WIDTH, ATTN_WIDTH, None, FOURIER_WIDTH, CONV_WIDTH, CONV_WIDTH)
    out_specs, out_shape = [], []
    for wdt in widths:
        if wdt is None:
            slabs = FOURIER_WIDTH // LANES
            out_specs.append(pl.BlockSpec((1, slabs, tm, LANES), lambda bi, i: (bi, 0, i, 0)))
            out_shape.append(jax.ShapeDtypeStruct((b, slabs, n, LANES), F32))
        else:
            out_specs.append(row3(wdt))
            out_shape.append(jax.ShapeDtypeStruct((b, n, wdt), BF16))
    return pl.pallas_call(
        functools.partial(_inproj_kernel, rope=rope, ctx_row=ctx_row),
        grid=(b, n // tm),
        in_specs=in_specs, out_specs=out_specs, out_shape=out_shape,
        compiler_params=_cparams("arbitrary", "arbitrary"),
        name="in_projection_rope" if rope else "in_projection_ctx",
    )(*args)


def _lane_halves(x):
    lo = lax.broadcasted_iota(jnp.int32, x.shape, 1) < HEAD_DIM
    zero = jnp.zeros_like(x)
    return jnp.where(lo, x, zero), jnp.where(lo, zero, x)


def _head_softmax_pv(pieces, sink):
    m = sink
    for s, _ in pieces:
        m = jnp.maximum(m, jnp.max(s, axis=-1, keepdims=True))
    den = jnp.exp(sink - m)
    acc = None
    for s, v in pieces:
        e = jnp.exp(s - m)
        den = den + jnp.sum(e, axis=-1, keepdims=True)
        pv = _dot(e.astype(BF16), v)
        acc = pv if acc is None else acc + pv
    return acc / den


def _attn_kernel(sink_ref, q_ref, kp_ref, km_ref, kn_ref, vp_ref, vm_ref, vn_ref, kc_ref, vc_ref,
                 o_ref, klo_s, khi_s, v_s, kclo_s, kchi_s, *, tq, n_seq):
    i = pl.program_id(1)
    span = WINDOW + BLOCK_Q + WINDOW
    for off, kref, vref, rows in ((0, kp_ref, vp_ref, WINDOW), (WINDOW, km_ref, vm_ref, tq),
                                  (WINDOW + tq, kn_ref, vn_ref, WINDOW)):
        lo, hi = _lane_halves(kref[0])
        klo_s[off:off + rows] = lo
        khi_s[off:off + rows] = hi
        v_s[off:off + rows] = vref[0]
    lo, hi = _lane_halves(kc_ref[0])
    kclo_s[...] = lo
    kchi_s[...] = hi

    def body(sb, carry):
        r0 = pl.multiple_of(sb * BLOCK_Q, BLOCK_Q)
        q = q_ref[0, pl.ds(r0, BLOCK_Q), :]
        vw = v_s[pl.ds(r0, span), :]
        vc = vc_ref[0]
        ii = lax.broadcasted_iota(jnp.int32, (BLOCK_Q, span), 0)
        jj = lax.broadcasted_iota(jnp.int32, (BLOCK_Q, span), 1)
        d = jj - ii
        gpos = jj + (i * tq + r0 - WINDOW)
        valid = jnp.where(d >= 0, d, 2 * WINDOW + 1) <= 2 * WINDOW
        valid = jnp.where(valid, jnp.where(gpos >= 0, gpos, n_seq), n_seq) < n_seq
        for p in range(Q_PER_KV):
            qp = q[:, p * LANES:(p + 1) * LANES]
            outs = []
            for kc_s, kl_s, h in ((kclo_s, klo_s, p), (kchi_s, khi_s, Q_PER_KV + p)):
                s_c = _dot_nt(qp, kc_s[...])
                s_l = jnp.where(valid, _dot_nt(qp, kl_s[pl.ds(r0, span), :]), -jnp.inf)
                outs.append(_head_softmax_pv([(s_c, vc), (s_l, vw)], sink_ref[h]))
            lane_lo = lax.broadcasted_iota(jnp.int32, outs[0].shape, 1) < HEAD_DIM
            o_ref[0, pl.ds(r0, BLOCK_Q), p * LANES:(p + 1) * LANES] = (
                jnp.where(lane_lo, outs[0], outs[1]).astype(BF16))
        return carry

    lax.fori_loop(0, tq // BLOCK_Q, body, 0)


def _window_attention(q, k, v, kc, vc, sink, *, tq):
    b, n, _ = q.shape
    nc = kc.shape[1]
    per = tq // WINDOW
    last = n // WINDOW - 1
    main = lambda width: pl.BlockSpec((1, tq, width), lambda bi, i: (bi, i, 0))
    prev = pl.BlockSpec((1, WINDOW, KV_WIDTH), lambda bi, i: (bi, jnp.maximum(i * per - 1, 0), 0))
    nxt = pl.BlockSpec((1, WINDOW, KV_WIDTH), lambda bi, i: (bi, jnp.minimum((i + 1) * per, last), 0))
    ctx = pl.BlockSpec((1, nc, KV_WIDTH), lambda bi, i: (bi, 0, 0))
    span = tq + 2 * WINDOW
    return pl.pallas_call(
        functools.partial(_attn_kernel, tq=tq, n_seq=n),
        grid=(b, n // tq),
        in_specs=[pl.BlockSpec(memory_space=pltpu.SMEM), main(ATTN_WIDTH),
                  prev, main(KV_WIDTH), nxt, prev, main(KV_WIDTH), nxt, ctx, ctx],
        out_specs=main(ATTN_WIDTH),
        out_shape=jax.ShapeDtypeStruct((b, n, ATTN_WIDTH), BF16),
        scratch_shapes=[pltpu.VMEM((span, KV_WIDTH), BF16)] * 3 + [pltpu.VMEM((nc, KV_WIDTH), BF16)] * 2,
        compiler_params=_cparams("arbitrary", "arbitrary"),
        name="window_attention",
    )(sink, q, k, k, k, v, v, v, kc, vc)


def _ctx_attn_kernel(sink_ref, q_ref, kc_ref, vc_ref, o_ref):
    klo, khi = _lane_halves(kc_ref[0])
    vc = vc_ref[0]
    for p in range(Q_PER_KV):
        qp = q_ref[0, :, p * LANES:(p + 1) * LANES]
        outs = [_head_softmax_pv([(_dot_nt(qp, kk), vc)], sink_ref[h])
                for kk, h in ((klo, p), (khi, Q_PER_KV + p))]
        lane_lo = lax.broadcasted_iota(jnp.int32, outs[0].shape, 1) < HEAD_DIM
        o_ref[0, :, p * LANES:(p + 1) * LANES] = jnp.where(lane_lo, outs[0], outs[1]).astype(BF16)


def _context_attention(q, kc, vc, sink):
    b, nc, _ = q.shape
    blk = lambda width: pl.BlockSpec((1, nc, width), lambda bi: (bi, 0, 0))
    return pl.pallas_call(
        _ctx_attn_kernel,
        grid=(b,),
        in_specs=[pl.BlockSpec(memory_space=pltpu.SMEM), blk(ATTN_WIDTH), blk(KV_WIDTH), blk(KV_WIDTH)],
        out_specs=blk(ATTN_WIDTH),
        out_shape=jax.ShapeDtypeStruct((b, nc, ATTN_WIDTH), BF16),
        compiler_params=_cparams("arbitrary"),
        name="context_attention",
    )(sink, q, kc, vc)


def _channel_mix_matrix(cc_ref, sc_ref, wf_ref, scale):
    wf = wf_ref[0]
    return (jnp.concatenate([_dot(cc_ref[...], wf), -_dot(sc_ref[...], wf)], axis=1) * scale).astype(BF16)


def _fourier_kernel(uf_ref, cc_ref, sc_ref, wf_ref, g_ref, m2_ref, o_ref, y_s, *, n_seq):
    n1_len = n_seq // FFT_N2
    mix = _channel_mix_matrix(cc_ref, sc_ref, wf_ref, (n_seq * FOURIER_GROUP_DIM) ** -0.5)

    def stage1(n2, carry):
        rows = uf_ref[0, 0, pl.ds(n2, n1_len, stride=FFT_N2), :]
        z = _dot(rows.astype(BF16), mix)
        rhs = jnp.concatenate([z[:, :LANES], z[:, LANES:]], axis=0).astype(BF16)
        y_s[n2] = _dot(g_ref[n2], rhs).reshape(2, n1_len, LANES)
        return carry

    lax.fori_loop(0, FFT_N2, stage1, 0)

    def stage2(a, carry):
        r0 = pl.multiple_of(a * K1_GROUP, K1_GROUP)
        blk = y_s[:, :, pl.ds(r0, K1_GROUP), :]
        rhs = blk.reshape(FFT_N2 * 2 * K1_GROUP, LANES).astype(BF16)
        out = _dot(m2_ref[...], rhs)
        o_ref[0, 0, :, pl.ds(r0, K1_GROUP), :] = out.reshape(FFT_N2, K1_GROUP, LANES)
        return carry

    lax.fori_loop(0, n1_len // K1_GROUP, stage2, 0)


def _fourier_mix(uf, wf_half, consts):
    b, halves, n, _ = uf.shape
    n1_len = n // FFT_N2
    cc, sc, g, m2 = consts
    full = lambda shape: pl.BlockSpec(shape, lambda bi, hf: (0,) * len(shape))
    out = pl.pallas_call(
        functools.partial(_fourier_kernel, n_seq=n),
        grid=(b, halves),
        in_specs=[pl.BlockSpec((1, 1, n, LANES), lambda bi, hf: (bi, hf, 0, 0)),
                  full((LANES, LANES)), full((LANES, LANES)),
                  pl.BlockSpec((1, LANES, LANES), lambda bi, hf: (hf, 0, 0)),
                  full(g.shape), full(m2.shape)],
        out_specs=pl.BlockSpec((1, 1, FFT_N2, n1_len, LANES), lambda bi, hf: (bi, hf, 0, 0, 0)),
        out_shape=jax.ShapeDtypeStruct((b, halves, FFT_N2, n1_len, LANES), F32),
        scratch_shapes=[pltpu.VMEM((FFT_N2, 2, n1_len, LANES), F32)],
        compiler_params=_cparams("arbitrary", "arbitrary"),
        name="fourier_mix",
    )(uf, cc, sc, wf_half, g, m2)
    return out.reshape(b, halves, n, LANES)


def _ctx_fourier_kernel(uf_ref, cc_ref, sc_ref, wf_ref, dft_ref, o_ref, *, n_seq):
    mix = _channel_mix_matrix(cc_ref, sc_ref, wf_ref, (n_seq * FOURIER_GROUP_DIM) ** -0.5)
    z = _dot(uf_ref[0, 0].astype(BF16), mix)
    rhs = jnp.concatenate([z[:, :LANES], z[:, LANES:]], axis=0).astype(BF16)
    o_ref[0, 0] = _dot(dft_ref[...], rhs)


def _ctx_fourier_mix(uf, wf_half, cc, sc, dft):
    b, halves, n, _ = uf.shape
    full = lambda shape: pl.BlockSpec(shape, lambda bi, hf: (0,) * len(shape))
    blk = pl.BlockSpec((1, 1, n, LANES), lambda bi, hf: (bi, hf, 0, 0))
    return pl.pallas_call(
        functools.partial(_ctx_fourier_kernel, n_seq=n),
        grid=(b, halves),
        in_specs=[blk, full((LANES, LANES)), full((LANES, LANES)),
                  pl.BlockSpec((1, LANES, LANES), lambda bi, hf: (hf, 0, 0)), full(dft.shape)],
        out_specs=blk,
        out_shape=jax.ShapeDtypeStruct(uf.shape, F32),
        compiler_params=_cparams("arbitrary", "arbitrary"),
        name="context_fourier_mix",
    )(uf, cc, sc, wf_half, dft)


def _outproj_kernel(*refs, halo, ctx_row, tm):
    if halo:
        (x_ref, a_ref, sga_ref, f_ref, sgf_ref, t_ref, tp_ref, tn_ref, bg_ref, cw_ref, cb_ref,
         w_ref, g_ref, gate_ref, o_ref) = refs
    else:
        (x_ref, a_ref, sga_ref, f_ref, sgf_ref, t_ref, bg_ref, cw_ref, cb_ref,
         w_ref, g_ref, gate_ref, o_ref) = refs
    row = pl.program_id(0) if ctx_row is None else ctx_row
    gate = gate_ref[0, pl.ds(row, 1), :]
    t = t_ref[0].astype(F32)
    ridx = lax.broadcasted_iota(jnp.int32, t.shape, 0)
    up = jnp.where(ridx == 0, 0.0, pltpu.roll(t, 1, 0))
    dn = jnp.where(ridx == tm - 1, 0.0, pltpu.roll(t, tm - 1, 0))
    if halo:
        i = pl.program_id(1)
        prev_row = tp_ref[0].astype(F32)[BF16_ROWS - 1:BF16_ROWS, :]
        next_row = tn_ref[0].astype(F32)[0:1, :]
        prev_row = jnp.where(i > 0, prev_row, 0.0)
        next_row = jnp.where(i < pl.num_programs(1) - 1, next_row, 0.0)
        up = jnp.where(ridx == 0, prev_row, up)
        dn = jnp.where(ridx == tm - 1, next_row, dn)
    cw = cw_ref[0]
    conv = up * cw[0:1] + t * cw[1:2] + dn * cw[2:3] + cb_ref[0]
    sgf = sgf_ref[0].astype(F32)
    parts = [(a_ref[0].astype(F32) * sga_ref[0].astype(F32)).astype(BF16)]
    for s in range(FOURIER_WIDTH // LANES):
        parts.append((f_ref[0, s] * sgf[:, s * LANES:(s + 1) * LANES]).astype(BF16))
    parts.append((conv * bg_ref[0].astype(F32)).astype(BF16))
    y = _dot(jnp.concatenate(parts, axis=1), w_ref[0])
    r = lax.rsqrt(jnp.mean(y * y, axis=-1, keepdims=True) + NORM_EPS)
    o_ref[0] = x_ref[0] + gate * ((y * r) * g_ref[0])


def _out_projection(x, a, sga, f, sgf, t, bg, conv_w, conv_b, w_out, g_post, mod, layer, *, ctx_row, tm):
    b, n, _ = x.shape
    halo = n > tm
    row3 = lambda width: pl.BlockSpec((1, tm, width), lambda bi, i: (bi, i, 0))
    lay3 = lambda shape: pl.BlockSpec((1,) + shape, lambda bi, i: (layer, 0, 0))
    slabs = FOURIER_WIDTH // LANES
    in_specs = [row3(D_MODEL), row3(ATTN_WIDTH), row3(ATTN_WIDTH),
                pl.BlockSpec((1, slabs, tm, LANES), lambda bi, i: (bi, 0, i, 0)),
                row3(FOURIER_WIDTH), row3(CONV_WIDTH)]
    args = [x, a, sga, f, sgf, t]
    if halo:
        per = tm // BF16_ROWS
        last = n // BF16_ROWS - 1
        in_specs += [pl.BlockSpec((1, BF16_ROWS, CONV_WIDTH),
                                  lambda bi, i: (bi, jnp.maximum(i * per - 1, 0), 0)),
                     pl.BlockSpec((1, BF16_ROWS, CONV_WIDTH),
                                  lambda bi, i: (bi, jnp.minimum((i + 1) * per, last), 0))]
        args += [t, t]
    in_specs += [row3(CONV_WIDTH), lay3((3, CONV_WIDTH)), lay3((1, CONV_WIDTH)),
                 lay3((MIX_WIDTH, D_MODEL)), lay3((1, D_MODEL)),
                 pl.BlockSpec((1, MOD_ROWS, D_MODEL), lambda bi, i: (layer, 0, 2))]
    args += [bg, conv_w, conv_b, w_out, g_post, mod]
    return pl.pallas_call(
        functools.partial(_outproj_kernel, halo=halo, ctx_row=ctx_row, tm=tm),
        grid=(b, n // tm),
        in_specs=in_specs,
        out_specs=row3(D_MODEL),
        out_shape=jax.ShapeDtypeStruct(x.shape, F32),
        compiler_params=_cparams("arbitrary", "arbitrary"),
        name="out_projection" if halo else "out_projection_ctx",
    )(*args)


def kernel(x, c, ctx, c_ctx, w_mod, b_mod, g_pre, g_post, w_in, w_out, sink, w_fourier, conv_w, conv_b):
    depth = w_mod.shape[0]
    b, n, _ = x.shape
    nc = ctx.shape[1]
    assert b + 1 <= MOD_ROWS and n % (FFT_N2 * K1_GROUP) == 0 and n % GRID_W == 0

    perm = _head_pair_perm()
    in_cols = np.arange(PROJ_WIDTH)
    in_cols[C_Q:C_Q + ATTN_WIDTH] = C_Q + perm
    in_cols[C_GA:C_GA + ATTN_WIDTH] = C_GA + perm
    out_rows = np.arange(MIX_WIDTH)
    out_rows[:ATTN_WIDTH] = perm
    w_in_b = w_in[:, :, in_cols].astype(BF16)
    w_out_b = w_out[:, out_rows, :].astype(BF16)

    rope_tabs = tuple(jnp.asarray(t) for t in _rope_tables(n))
    cc, sc = (jnp.asarray(m) for m in _channel_dft())
    stage1 = jnp.asarray(_stage1_mats(n)).astype(BF16)
    stage2 = jnp.asarray(_stage2_mat()).astype(BF16)
    ctx_dft = jnp.asarray(_dense_dft(nc)).astype(BF16)
    groups_per_half = LANES // FOURIER_GROUP_DIM
    wf_half = jnp.zeros((depth, FOURIER_GROUPS // groups_per_half, LANES, LANES), F32)
    for g in range(FOURIER_GROUPS):
        o = (g % groups_per_half) * FOURIER_GROUP_DIM
        wf_half = wf_half.at[:, g // groups_per_half, o:o + FOURIER_GROUP_DIM,
                             o:o + FOURIER_GROUP_DIM].set(w_fourier[:, g])

    c_rows = jnp.zeros((MOD_ROWS, D_MODEL), F32).at[:b].set(c).at[b].set(c_ctx)
    mod = _modulation(c_rows, w_mod, b_mod)
    g_pre3 = g_pre.reshape(depth, 1, D_MODEL)
    g_post3 = g_post.reshape(depth, 1, D_MODEL)
    conv_b3 = conv_b.reshape(depth, 1, CONV_WIDTH)

    for l in range(depth):
        qc, kc, vc, sgac, ufc, sgfc, tc, bgc = _in_projection(
            ctx, mod, g_pre3, w_in_b, l, rope_tabs=None, ctx_row=b, tm=nc)
        q, k, v, sga, uf, sgf, t, bg = _in_projection(
            x, mod, g_pre3, w_in_b, l, rope_tabs=rope_tabs, ctx_row=None, tm=512)
        a = _window_attention(q, k, v, kc, vc, sink[l], tq=512)
        f = _fourier_mix(uf, wf_half[l], (cc, sc, stage1, stage2))
        x = _out_projection(x, a, sga, f, sgf, t, bg, conv_w, conv_b3, w_out_b, g_post3, mod, l,
                            ctx_row=None, tm=512)
        if l < depth - 1:
            ac = _context_attention(qc, kc, vc, sink[l])
            fc = _ctx_fourier_mix(ufc, wf_half[l], cc, sc, ctx_dft)
            ctx = _out_projection(ctx, ac, sgac, fc, sgfc, tc, bgc, conv_w, conv_b3, w_out_b, g_post3,
                                  mod, l, ctx_row=b, tm=nc)
    return x
```

```python
import functools
import math

import numpy as np
import jax
import jax.numpy as jnp
from jax import lax
from jax.experimental import pallas as pl
from jax.experimental.pallas import tpu as pltpu

D_MODEL = 1024
GRID_W = 64
HEAD_DIM = 64
ATTN_HEADS = 8
KV_HEADS = 2
Q_PER_KV = ATTN_HEADS // KV_HEADS
ATTN_WIDTH = ATTN_HEADS * HEAD_DIM
KV_WIDTH = KV_HEADS * HEAD_DIM
WINDOW = 128
FOURIER_GROUPS = 4
FOURIER_GROUP_DIM = 64
FOURIER_WIDTH = FOURIER_GROUPS * FOURIER_GROUP_DIM
CONV_WIDTH = 256
MIX_WIDTH = ATTN_WIDTH + FOURIER_WIDTH + CONV_WIDTH
PROJ_WIDTH = 2 * ATTN_WIDTH + 2 * KV_WIDTH + 2 * FOURIER_WIDTH + 4 * CONV_WIDTH
ROPE_FREQS = HEAD_DIM // 4
ROPE_BASE = 10000.0
NORM_EPS = 1e-6
LOG2E = math.log2(math.e)

C_Q = 0
C_K = C_Q + ATTN_WIDTH
C_V = C_K + KV_WIDTH
C_GA = C_V + KV_WIDTH
C_UF = C_GA + ATTN_WIDTH
C_GF = C_UF + FOURIER_WIDTH
C_ZC = C_GF + FOURIER_WIDTH
C_BC = C_ZC + CONV_WIDTH
C_CC = C_BC + CONV_WIDTH
C_GC = C_CC + CONV_WIDTH

LANES = 128
SUBLANES = 8
BF16_ROWS = 16
VMEM_LIMIT = 52 * 1024 * 1024

MOD_ROWS = 8
BF16 = jnp.bfloat16
F32 = jnp.float32

FFT_N2 = 64
K1_GROUP = SUBLANES
STAGE1_UNROLL = 8
STAGE2_UNROLL = 2
Z_CHUNK = 1024

BLOCK_Q = 128
KEY_SPAN = BLOCK_Q + 2 * WINDOW
HEAD_PAIRS = ATTN_WIDTH // LANES
PAIRS_PER_KV = HEAD_PAIRS // KV_HEADS

TILE_ROWS = 512
ATTN_ROWS = 1024


def _silu(x):
    return x / (1.0 + jnp.exp(-x))


def _dot(a, b):
    return jnp.dot(a, b, preferred_element_type=F32)


def _dot_nt(a, b):
    return lax.dot_general(a, b, (((1,), (1,)), ((), ())), preferred_element_type=F32)


def _cparams(*sem):
    return pltpu.CompilerParams(dimension_semantics=sem, vmem_limit_bytes=VMEM_LIMIT)


def _rope_tables(n):
    t = np.arange(n)
    row = (t // GRID_W).astype(np.float64)
    col = (t % GRID_W).astype(np.float64)
    inv = ROPE_BASE ** (-np.arange(ROPE_FREQS, dtype=np.float64) / ROPE_FREQS)
    ar = row[:, None] * inv
    ac = col[:, None] * inv
    z = np.zeros_like(ar)
    cos_h = np.concatenate([np.cos(ar), np.cos(ar), np.cos(ac), np.cos(ac)], axis=1)
    sin_up = np.concatenate([-np.sin(ar), z, -np.sin(ac), z], axis=1)
    sin_dn = np.concatenate([z, np.sin(ar), z, np.sin(ac)], axis=1)
    rep = LANES // HEAD_DIM
    return tuple(np.tile(a, (1, rep)).astype(np.float32) for a in (cos_h, sin_up, sin_dn))


def _channel_dft():
    c = np.arange(FOURIER_GROUP_DIM)
    ang = 2.0 * np.pi * np.outer(c, c) / FOURIER_GROUP_DIM
    eye = np.eye(LANES // FOURIER_GROUP_DIM)
    return (np.kron(eye, np.cos(ang)).astype(np.float32),
            np.kron(eye, np.sin(ang)).astype(np.float32))


def _stage1_mats(n):
    n1_len = n // FFT_N2
    k1 = np.arange(n1_len)[:, None]
    n1 = np.arange(n1_len)[None, :]
    out = np.empty((FFT_N2, 2 * n1_len, 2 * n1_len), np.float32)
    for n2 in range(FFT_N2):
        ang = 2.0 * np.pi * ((k1 * (FFT_N2 * n1 + n2)) % n) / n
        ce, se = np.cos(ang), np.sin(ang)
        out[n2] = np.block([[ce, se], [-se, ce]])
    return out


def _stage2_mat():
    k2 = np.arange(FFT_N2)
    ang = 2.0 * np.pi * np.outer(k2, k2) / FFT_N2
    cs = np.stack([np.cos(ang), np.sin(ang)], axis=-1)
    eye = np.eye(K1_GROUP)
    m = np.einsum('knp,rs->krnps', cs, eye)
    return m.reshape(FFT_N2 * K1_GROUP, FFT_N2 * 2 * K1_GROUP).astype(np.float32)


def _dense_dft(n):
    t = np.arange(n)
    ang = 2.0 * np.pi * (np.outer(t, t) % n) / n
    return np.concatenate([np.cos(ang), np.sin(ang)], axis=1).astype(np.float32)


def _mod_kernel(c_ref, w_ref, b_ref, o_ref):
    o_ref[0] = _dot(_silu(c_ref[...]), w_ref[0]) + b_ref[0]


def _modulation(c_rows, w_mod, b_mod):
    depth = w_mod.shape[0]
    return pl.pallas_call(
        _mod_kernel,
        grid=(depth, 3),
        in_specs=[pl.BlockSpec((MOD_ROWS, D_MODEL), lambda l, j: (0, 0)),
                  pl.BlockSpec((1, D_MODEL, D_MODEL), lambda l, j: (l, 0, j)),
                  pl.BlockSpec((1, 1, D_MODEL), lambda l, j: (l, 0, j))],
        out_specs=pl.BlockSpec((1, MOD_ROWS, D_MODEL), lambda l, j: (l, 0, j)),
        out_shape=jax.ShapeDtypeStruct((depth, MOD_ROWS, 3 * D_MODEL), F32),
        compiler_params=_cparams("arbitrary", "arbitrary"),
        name="modulation",
    )(c_rows, w_mod, b_mod.reshape(depth, 1, 3 * D_MODEL))


def _inproj_kernel(*refs, rope, ctx_row):
    if rope:
        (x_ref, shift_ref, scale_ref, g_ref, w_ref, cos_ref, sup_ref, sdn_ref,
         q_ref, k_ref, v_ref, sga_ref, uf_ref, sgf_ref, t_ref, bg_ref) = refs
    else:
        (x_ref, shift_ref, scale_ref, g_ref, w_ref,
         q_ref, k_ref, v_ref, sga_ref, uf_ref, sgf_ref, t_ref, bg_ref) = refs
    row = pl.program_id(0) if ctx_row is None else ctx_row
    shift = shift_ref[0, pl.ds(row, 1), :]
    scale = scale_ref[0, pl.ds(row, 1), :]
    x = x_ref[0]
    r = lax.rsqrt(jnp.mean(x * x, axis=-1, keepdims=True) + NORM_EPS)
    h = ((x * r) * (g_ref[0] * (1.0 + scale)) + shift).astype(BF16)

    def proj(c0, width):
        return _dot(h, w_ref[0, :, c0:c0 + width])

    def rotate(y):
        if not rope:
            return y
        return (y * cos_ref[...] + pltpu.roll(y, LANES - ROPE_FREQS, 1) * sup_ref[...]
                + pltpu.roll(y, ROPE_FREQS, 1) * sdn_ref[...])

    def with_swapped_heads(y):
        return jnp.concatenate([y, pltpu.roll(y, HEAD_DIM, 1)], axis=1).astype(BF16)

    q = proj(C_Q, ATTN_WIDTH)
    for p in range(HEAD_PAIRS):
        sl = slice(p * LANES, (p + 1) * LANES)
        q_ref[0, :, sl] = (rotate(q[:, sl]) * (HEAD_DIM ** -0.5 * LOG2E)).astype(BF16)
    kv = proj(C_K, 2 * KV_WIDTH)
    k_ref[0] = with_swapped_heads(rotate(kv[:, :KV_WIDTH]))
    v_ref[0] = with_swapped_heads(kv[:, KV_WIDTH:])
    sga_ref[0] = _silu(proj(C_GA, ATTN_WIDTH)).astype(BF16)
    uf_ref[0] = proj(C_UF, FOURIER_WIDTH).astype(BF16)
    sgf_ref[0] = _silu(proj(C_GF, FOURIER_WIDTH)).astype(BF16)
    t_ref[0] = (proj(C_CC, CONV_WIDTH) * proj(C_ZC, CONV_WIDTH)).astype(BF16)
    bg_ref[0] = (proj(C_BC, CONV_WIDTH) * _silu(proj(C_GC, CONV_WIDTH))).astype(BF16)


def _in_projection(x, mod, g_pre, w_in, layer, *, rope_tabs, ctx_row, tm):
    b, n, _ = x.shape
    rope = rope_tabs is not None
    row3 = lambda width: pl.BlockSpec((1, tm, width), lambda bi, i: (bi, i, 0))
    in_specs = [row3(D_MODEL),
                pl.BlockSpec((1, MOD_ROWS, D_MODEL), lambda bi, i: (layer, 0, 0)),
                pl.BlockSpec((1, MOD_ROWS, D_MODEL), lambda bi, i: (layer, 0, 1)),
                pl.BlockSpec((1, 1, D_MODEL), lambda bi, i: (layer, 0, 0)),
                pl.BlockSpec((1, D_MODEL, PROJ_WIDTH), lambda bi, i: (layer, 0, 0))]
    args = [x, mod, mod, g_pre, w_in]
    if rope:
        in_specs += [pl.BlockSpec((tm, LANES), lambda bi, i: (i, 0))] * 3
        args += list(rope_tabs)
    widths = (ATTN_WIDTH, 2 * KV_WIDTH, 2 * KV_WIDTH, ATTN_WIDTH, FOURIER_WIDTH, FOURIER_WIDTH,
              CONV_WIDTH, CONV_WIDTH)
    return pl.pallas_call(
        functools.partial(_inproj_kernel, rope=rope, ctx_row=ctx_row),
        grid=(b, n // tm),
        in_specs=in_specs,
        out_specs=[row3(w) for w in widths],
        out_shape=[jax.ShapeDtypeStruct((b, n, w), BF16) for w in widths],
        compiler_params=_cparams("arbitrary", "arbitrary"),
        name="in_projection_rope" if rope else "in_projection_ctx",
    )(*args)


def _masked_keys(blk):
    straight, swapped = blk[:, :KV_WIDTH], blk[:, KV_WIDTH:]
    lo = lax.broadcasted_iota(jnp.int32, straight.shape, 1) < HEAD_DIM
    zero = jnp.zeros_like(straight)
    return (jnp.where(lo, straight, zero), jnp.where(lo, zero, swapped),
            jnp.where(lo, swapped, zero), jnp.where(lo, zero, straight))


def _value_variant(kv, half):
    return half if kv == 0 else 1 - half


def _softmax_head(s_ctx, s_loc, sink2):
    cols = [s_ctx[:, i * LANES:(i + 1) * LANES] for i in range(s_ctx.shape[1] // LANES)]
    cols += [s_loc[:, i * LANES:(i + 1) * LANES] for i in range(s_loc.shape[1] // LANES)]
    n_ctx = s_ctx.shape[1] // LANES
    mx = functools.reduce(jnp.maximum, cols)
    m = jnp.maximum(jnp.max(mx, axis=-1, keepdims=True), sink2)
    es = [jnp.exp2(c - m) for c in cols]
    den = jnp.sum(functools.reduce(jnp.add, es), axis=-1, keepdims=True) + jnp.exp2(sink2 - m)
    p_ctx = jnp.concatenate(es[:n_ctx], axis=1).astype(BF16)
    p_loc = jnp.concatenate(es[n_ctx:], axis=1).astype(BF16)
    return p_ctx, p_loc, 1.0 / den


def _attn_kernel(sink_ref, q_ref, kp_ref, km_ref, kn_ref, vp_ref, vm_ref, vn_ref, kc_ref, vc_ref,
                 o_ref, k_s, v_s, kctx_s, s_s, p_s, r_s, *, tq, n_seq, n_ctx):
    i = pl.program_id(1)
    for off, kref, vref, rows in ((0, kp_ref, vp_ref, WINDOW), (WINDOW, km_ref, vm_ref, tq),
                                  (WINDOW + tq, kn_ref, vn_ref, WINDOW)):
        for idx, kk in enumerate(_masked_keys(kref[0])):
            k_s[idx, off:off + rows] = kk
        v_s[0, off:off + rows] = vref[0, :, :KV_WIDTH]
        v_s[1, off:off + rows] = vref[0, :, KV_WIDTH:]
    for idx, kk in enumerate(_masked_keys(kc_ref[0])):
        kctx_s[idx // 2, (idx % 2) * n_ctx:(idx % 2 + 1) * n_ctx] = kk

    c_w = 2 * n_ctx

    def scores(sb):
        par, r0 = sb % 2, sb * BLOCK_Q
        q = q_ref[0, r0:r0 + BLOCK_Q, :]
        for p in range(HEAD_PAIRS):
            kv = p // PAIRS_PER_KV
            qp = q[:, p * LANES:(p + 1) * LANES]
            s_s[par, p, :, :c_w] = _dot_nt(qp, kctx_s[kv])
            kwin = jnp.concatenate([k_s[2 * kv, r0:r0 + KEY_SPAN], k_s[2 * kv + 1, r0:r0 + KEY_SPAN]],
                                   axis=0)
            s_s[par, p, :, c_w:] = _dot_nt(qp, kwin)

    def softmax(sb):
        par, r0 = sb % 2, sb * BLOCK_Q
        ii = lax.broadcasted_iota(jnp.int32, (BLOCK_Q, KEY_SPAN), 0)
        jj = lax.broadcasted_iota(jnp.int32, (BLOCK_Q, KEY_SPAN), 1)
        d = jj - ii
        gpos = jj + (i * tq + (r0 - WINDOW))
        valid = jnp.where(d >= 0, d, 2 * WINDOW + 1) <= 2 * WINDOW
        valid = jnp.where(valid, jnp.where(gpos >= 0, gpos, n_seq), n_seq) < n_seq
        for p in range(HEAD_PAIRS):
            for half in range(2):
                h = 2 * p + half
                c0, l0 = half * n_ctx, c_w + half * KEY_SPAN
                s_loc = jnp.where(valid, s_s[par, p, :, l0:l0 + KEY_SPAN], -jnp.inf)
                p_ctx, p_loc, rden = _softmax_head(s_s[par, p, :, c0:c0 + n_ctx], s_loc,
                                                   sink_ref[h] * LOG2E)
                p_s[par, p, :, c0:c0 + n_ctx] = p_ctx
                p_s[par, p, :, l0:l0 + KEY_SPAN] = p_loc
                r_s[par, h] = jnp.broadcast_to(rden, (BLOCK_Q, LANES))

    def weighted_values(sb):
        par, r0 = sb % 2, sb * BLOCK_Q
        for p in range(HEAD_PAIRS):
            kv = p // PAIRS_PER_KV
            outs = []
            for half in range(2):
                h = 2 * p + half
                c0, l0 = half * n_ctx, c_w + half * KEY_SPAN
                vi = _value_variant(kv, half)
                pv = (_dot(p_s[par, p, :, c0:c0 + n_ctx], vc_ref[0, :, vi * KV_WIDTH:(vi + 1) * KV_WIDTH])
                      + _dot(p_s[par, p, :, l0:l0 + KEY_SPAN], v_s[vi, r0:r0 + KEY_SPAN]))
                outs.append(pv * r_s[par, h])
            lane_lo = lax.broadcasted_iota(jnp.int32, outs[0].shape, 1) < HEAD_DIM
            o_ref[0, r0:r0 + BLOCK_Q, p * LANES:(p + 1) * LANES] = (
                jnp.where(lane_lo, outs[0], outs[1]).astype(BF16))

    n_blocks = tq // BLOCK_Q
    scores(0)
    for sb in range(n_blocks):
        if sb + 1 < n_blocks:
            scores(sb + 1)
        softmax(sb)
        weighted_values(sb)


def _window_attention(q, k, v, kc, vc, sink, *, tq):
    b, n, _ = q.shape
    nc = kc.shape[1]
    per = tq // WINDOW
    last = n // WINDOW - 1
    kvw = 2 * KV_WIDTH
    main = lambda width: pl.BlockSpec((1, tq, width), lambda bi, i: (bi, i, 0))
    prev = pl.BlockSpec((1, WINDOW, kvw), lambda bi, i: (bi, jnp.maximum(i * per - 1, 0), 0))
    nxt = pl.BlockSpec((1, WINDOW, kvw), lambda bi, i: (bi, jnp.minimum((i + 1) * per, last), 0))
    ctx = pl.BlockSpec((1, nc, kvw), lambda bi, i: (bi, 0, 0))
    span = tq + 2 * WINDOW
    s_cols = 2 * nc + 2 * KEY_SPAN
    return pl.pallas_call(
        functools.partial(_attn_kernel, tq=tq, n_seq=n, n_ctx=nc),
        grid=(b, n // tq),
        in_specs=[pl.BlockSpec(memory_space=pltpu.SMEM), main(ATTN_WIDTH),
                  prev, main(kvw), nxt, prev, main(kvw), nxt, ctx, ctx],
        out_specs=main(ATTN_WIDTH),
        out_shape=jax.ShapeDtypeStruct((b, n, ATTN_WIDTH), BF16),
        scratch_shapes=[pltpu.VMEM((2 * KV_HEADS, span, KV_WIDTH), BF16),
                        pltpu.VMEM((2, span, KV_WIDTH), BF16),
                        pltpu.VMEM((KV_HEADS, 2 * nc, KV_WIDTH), BF16),
                        pltpu.VMEM((2, HEAD_PAIRS, BLOCK_Q, s_cols), F32),
                        pltpu.VMEM((2, HEAD_PAIRS, BLOCK_Q, s_cols), BF16),
                        pltpu.VMEM((2, ATTN_HEADS, BLOCK_Q, LANES), F32)],
        compiler_params=_cparams("arbitrary", "arbitrary"),
        name="window_attention",
    )(sink, q, k, k, k, v, v, v, kc, vc)


def _ctx_attn_kernel(sink_ref, q_ref, kc_ref, vc_ref, o_ref):
    keys = _masked_keys(kc_ref[0])
    for p in range(HEAD_PAIRS):
        kv = p // PAIRS_PER_KV
        qp = q_ref[0, :, p * LANES:(p + 1) * LANES]
        outs = []
        for half in range(2):
            s = _dot_nt(qp, keys[2 * kv + half])
            sink2 = sink_ref[2 * p + half] * LOG2E
            m = jnp.maximum(jnp.max(s, axis=-1, keepdims=True), sink2)
            e = jnp.exp2(s - m)
            den = jnp.sum(e, axis=-1, keepdims=True) + jnp.exp2(sink2 - m)
            vi = _value_variant(kv, half)
            outs.append(_dot(e.astype(BF16), vc_ref[0, :, vi * KV_WIDTH:(vi + 1) * KV_WIDTH]) / den)
        lane_lo = lax.broadcasted_iota(jnp.int32, outs[0].shape, 1) < HEAD_DIM
        o_ref[0, :, p * LANES:(p + 1) * LANES] = jnp.where(lane_lo, outs[0], outs[1]).astype(BF16)


def _context_attention(q, kc, vc, sink):
    b, nc, _ = q.shape
    blk = lambda width: pl.BlockSpec((1, nc, width), lambda bi: (bi, 0, 0))
    return pl.pallas_call(
        _ctx_attn_kernel,
        grid=(b,),
        in_specs=[pl.BlockSpec(memory_space=pltpu.SMEM), blk(ATTN_WIDTH), blk(2 * KV_WIDTH),
                  blk(2 * KV_WIDTH)],
        out_specs=blk(ATTN_WIDTH),
        out_shape=jax.ShapeDtypeStruct((b, nc, ATTN_WIDTH), BF16),
        compiler_params=_cparams("arbitrary"),
        name="context_attention",
    )(sink, q, kc, vc)


def _channel_mix_matrix(cc_ref, sc_ref, wf_ref, scale):
    wf = wf_ref[0]
    return (jnp.concatenate([_dot(cc_ref[...], wf), -_dot(sc_ref[...], wf)], axis=1) * scale).astype(BF16)


def _fourier_kernel(uf_ref, cc_ref, sc_ref, wf_ref, g_ref, m2_ref, o_ref, z_s, y_s, *, n_seq):
    n1_len = n_seq // FFT_N2
    mix = _channel_mix_matrix(cc_ref, sc_ref, wf_ref, (n_seq * FOURIER_GROUP_DIM) ** -0.5)

    for c in range(n_seq // Z_CHUNK):
        rows = slice(c * Z_CHUNK, (c + 1) * Z_CHUNK)
        z = _dot(uf_ref[0, rows, :], mix)
        z_s[0, rows] = z[:, :LANES]
        z_s[1, rows] = z[:, LANES:]

    def stage1(t, carry):
        for u in range(STAGE1_UNROLL):
            n2 = t * STAGE1_UNROLL + u
            rhs = jnp.concatenate([z_s[part, pl.ds(n2, n1_len, stride=FFT_N2), :] for part in range(2)],
                                  axis=0).astype(BF16)
            y_s[n2] = _dot(g_ref[n2], rhs).reshape(2, n1_len, LANES)
        return carry

    lax.fori_loop(0, FFT_N2 // STAGE1_UNROLL, stage1, 0)

    def stage2(t, carry):
        for u in range(STAGE2_UNROLL):
            r0 = pl.multiple_of((t * STAGE2_UNROLL + u) * K1_GROUP, K1_GROUP)
            blk = y_s[:, :, pl.ds(r0, K1_GROUP), :]
            rhs = blk.reshape(FFT_N2 * 2 * K1_GROUP, LANES).astype(BF16)
            out = _dot(m2_ref[...], rhs)
            o_ref[0, 0, :, pl.ds(r0, K1_GROUP), :] = out.reshape(FFT_N2, K1_GROUP, LANES)
        return carry

    lax.fori_loop(0, n1_len // (K1_GROUP * STAGE2_UNROLL), stage2, 0)


def _fourier_mix(uf, wf_half, consts):
    b, n, _ = uf.shape
    halves = FOURIER_WIDTH // LANES
    n1_len = n // FFT_N2
    cc, sc, g, m2 = consts
    full = lambda shape: pl.BlockSpec(shape, lambda bi, hf: (0,) * len(shape))
    out = pl.pallas_call(
        functools.partial(_fourier_kernel, n_seq=n),
        grid=(b, halves),
        in_specs=[pl.BlockSpec((1, n, LANES), lambda bi, hf: (bi, 0, hf)),
                  full((LANES, LANES)), full((LANES, LANES)),
                  pl.BlockSpec((1, LANES, LANES), lambda bi, hf: (hf, 0, 0)),
                  full(g.shape), full(m2.shape)],
        out_specs=pl.BlockSpec((1, 1, FFT_N2, n1_len, LANES), lambda bi, hf: (bi, hf, 0, 0, 0)),
        out_shape=jax.ShapeDtypeStruct((b, halves, FFT_N2, n1_len, LANES), F32),
        scratch_shapes=[pltpu.VMEM((2, n, LANES), F32),
                        pltpu.VMEM((FFT_N2, 2, n1_len, LANES), F32)],
        compiler_params=_cparams("arbitrary", "arbitrary"),
        name="fourier_mix",
    )(uf, cc, sc, wf_half, g, m2)
    return out.reshape(b, halves, n, LANES)


def _ctx_fourier_kernel(uf_ref, cc_ref, sc_ref, wf_ref, dft_ref, o_ref, *, n_seq):
    mix = _channel_mix_matrix(cc_ref, sc_ref, wf_ref, (n_seq * FOURIER_GROUP_DIM) ** -0.5)
    z = _dot(uf_ref[0], mix)
    rhs = jnp.concatenate([z[:, :LANES], z[:, LANES:]], axis=0).astype(BF16)
    o_ref[0, 0] = _dot(dft_ref[...], rhs)


def _ctx_fourier_mix(uf, wf_half, cc, sc, dft):
    b, n, _ = uf.shape
    halves = FOURIER_WIDTH // LANES
    full = lambda shape: pl.BlockSpec(shape, lambda bi, hf: (0,) * len(shape))
    return pl.pallas_call(
        functools.partial(_ctx_fourier_kernel, n_seq=n),
        grid=(b, halves),
        in_specs=[pl.BlockSpec((1, n, LANES), lambda bi, hf: (bi, 0, hf)),
                  full((LANES, LANES)), full((LANES, LANES)),
                  pl.BlockSpec((1, LANES, LANES), lambda bi, hf: (hf, 0, 0)), full(dft.shape)],
        out_specs=pl.BlockSpec((1, 1, n, LANES), lambda bi, hf: (bi, hf, 0, 0)),
        out_shape=jax.ShapeDtypeStruct((b, halves, n, LANES), F32),
        compiler_params=_cparams("arbitrary", "arbitrary"),
        name="context_fourier_mix",
    )(uf, cc, sc, wf_half, dft)


def _outproj_kernel(*refs, halo, ctx_row, tm):
    if halo:
        (x_ref, a_ref, sga_ref, f_ref, sgf_ref, t_ref, tp_ref, tn_ref, bg_ref, cw_ref, cb_ref,
         w_ref, g_ref, gate_ref, o_ref) = refs
    else:
        (x_ref, a_ref, sga_ref, f_ref, sgf_ref, t_ref, bg_ref, cw_ref, cb_ref,
         w_ref, g_ref, gate_ref, o_ref) = refs
    row = pl.program_id(0) if ctx_row is None else ctx_row
    gate = gate_ref[0, pl.ds(row, 1), :]
    t = t_ref[0].astype(F32)
    ridx = lax.broadcasted_iota(jnp.int32, t.shape, 0)
    up = jnp.where(ridx == 0, 0.0, pltpu.roll(t, 1, 0))
    dn = jnp.where(ridx == tm - 1, 0.0, pltpu.roll(t, tm - 1, 0))
    if halo:
        i = pl.program_id(1)
        prev_row = tp_ref[0].astype(F32)[BF16_ROWS - 1:BF16_ROWS, :]
        next_row = tn_ref[0].astype(F32)[0:1, :]
        prev_row = jnp.where(i > 0, prev_row, 0.0)
        next_row = jnp.where(i < pl.num_programs(1) - 1, next_row, 0.0)
        up = jnp.where(ridx == 0, prev_row, up)
        dn = jnp.where(ridx == tm - 1, next_row, dn)
    cw = cw_ref[0]
    conv = up * cw[0:1] + t * cw[1:2] + dn * cw[2:3] + cb_ref[0]
    sgf = sgf_ref[0].astype(F32)
    parts = [(a_ref[0].astype(F32) * sga_ref[0].astype(F32)).astype(BF16)]
    for s in range(FOURIER_WIDTH // LANES):
        parts.append((f_ref[0, s] * sgf[:, s * LANES:(s + 1) * LANES]).astype(BF16))
    parts.append((conv * bg_ref[0].astype(F32)).astype(BF16))
    y = _dot(jnp.concatenate(parts, axis=1), w_ref[0])
    r = lax.rsqrt(jnp.mean(y * y, axis=-1, keepdims=True) + NORM_EPS)
    o_ref[0] = x_ref[0] + gate * ((y * r) * g_ref[0])


def _out_projection(x, a, sga, f, sgf, t, bg, conv_w, conv_b, w_out, g_post, mod, layer, *, ctx_row, tm):
    b, n, _ = x.shape
    halo = n > tm
    row3 = lambda width: pl.BlockSpec((1, tm, width), lambda bi, i: (bi, i, 0))
    lay3 = lambda shape: pl.BlockSpec((1,) + shape, lambda bi, i: (layer, 0, 0))
    slabs = FOURIER_WIDTH // LANES
    in_specs = [row3(D_MODEL), row3(ATTN_WIDTH), row3(ATTN_WIDTH),
                pl.BlockSpec((1, slabs, tm, LANES), lambda bi, i: (bi, 0, i, 0)),
                row3(FOURIER_WIDTH), row3(CONV_WIDTH)]
    args = [x, a, sga, f, sgf, t]
    if halo:
        per = tm // BF16_ROWS
        last = n // BF16_ROWS - 1
        in_specs += [pl.BlockSpec((1, BF16_ROWS, CONV_WIDTH),
                                  lambda bi, i: (bi, jnp.maximum(i * per - 1, 0), 0)),
                     pl.BlockSpec((1, BF16_ROWS, CONV_WIDTH),
                                  lambda bi, i: (bi, jnp.minimum((i + 1) * per, last), 0))]
        args += [t, t]
    in_specs += [row3(CONV_WIDTH), lay3((3, CONV_WIDTH)), lay3((1, CONV_WIDTH)),
                 lay3((MIX_WIDTH, D_MODEL)), lay3((1, D_MODEL)),
                 pl.BlockSpec((1, MOD_ROWS, D_MODEL), lambda bi, i: (layer, 0, 2))]
    args += [bg, conv_w, conv_b, w_out, g_post, mod]
    return pl.pallas_call(
        functools.partial(_outproj_kernel, halo=halo, ctx_row=ctx_row, tm=tm),
        grid=(b, n // tm),
        in_specs=in_specs,
        out_specs=row3(D_MODEL),
        out_shape=jax.ShapeDtypeStruct(x.shape, F32),
        compiler_params=_cparams("arbitrary", "arbitrary"),
        name="out_projection" if halo else "out_projection_ctx",
    )(*args)


def kernel(x, c, ctx, c_ctx, w_mod, b_mod, g_pre, g_post, w_in, w_out, sink, w_fourier, conv_w, conv_b):
    depth = w_mod.shape[0]
    b, n, _ = x.shape
    nc = ctx.shape[1]
    assert b + 1 <= MOD_ROWS and n % GRID_W == 0 and n % ATTN_ROWS == 0 and n % Z_CHUNK == 0
    assert (n // FFT_N2) % (K1_GROUP * STAGE2_UNROLL) == 0 and FFT_N2 % STAGE1_UNROLL == 0

    w_in_b = w_in.astype(BF16)
    w_out_b = w_out.astype(BF16)

    rope_tabs = tuple(jnp.asarray(t) for t in _rope_tables(n))
    cc, sc = (jnp.asarray(m) for m in _channel_dft())
    stage1 = jnp.asarray(_stage1_mats(n)).astype(BF16)
    stage2 = jnp.asarray(_stage2_mat()).astype(BF16)
    ctx_dft = jnp.asarray(_dense_dft(nc)).astype(BF16)
    groups_per_half = LANES // FOURIER_GROUP_DIM
    wf_half = jnp.zeros((depth, FOURIER_GROUPS // groups_per_half, LANES, LANES), F32)
    for g in range(FOURIER_GROUPS):
        o = (g % groups_per_half) * FOURIER_GROUP_DIM
        wf_half = wf_half.at[:, g // groups_per_half, o:o + FOURIER_GROUP_DIM,
                             o:o + FOURIER_GROUP_DIM].set(w_fourier[:, g])

    c_rows = jnp.zeros((MOD_ROWS, D_MODEL), F32).at[:b].set(c).at[b].set(c_ctx)
    mod = _modulation(c_rows, w_mod, b_mod)
    g_pre3 = g_pre.reshape(depth, 1, D_MODEL)
    g_post3 = g_post.reshape(depth, 1, D_MODEL)
    conv_b3 = conv_b.reshape(depth, 1, CONV_WIDTH)

    for l in range(depth):
        qc, kc, vc, sgac, ufc, sgfc, tc, bgc = _in_projection(
            ctx, mod, g_pre3, w_in_b, l, rope_tabs=None, ctx_row=b, tm=nc)
        q, k, v, sga, uf, sgf, t, bg = _in_projection(
            x, mod, g_pre3, w_in_b, l, rope_tabs=rope_tabs, ctx_row=None, tm=TILE_ROWS)
        a = _window_attention(q, k, v, kc, vc, sink[l], tq=ATTN_ROWS)
        f = _fourier_mix(uf, wf_half[l], (cc, sc, stage1, stage2))
        x = _out_projection(x, a, sga, f, sgf, t, bg, conv_w, conv_b3, w_out_b, g_post3, mod, l,
                            ctx_row=None, tm=TILE_ROWS)
        if l < depth - 1:
            ac = _context_attention(qc, kc, vc, sink[l])
            fc = _ctx_fourier_mix(ufc, wf_half[l], cc, sc, ctx_dft)
            ctx = _out_projection(ctx, ac, sgac, fc, sgfc, tc, bgc, conv_w, conv_b3, w_out_b, g_post3,
                                  mod, l, ctx_row=b, tm=nc)
    return x
```

```python
import functools
import math

import numpy as np
import jax
import jax.numpy as jnp
from jax import lax
from jax.experimental import pallas as pl
from jax.experimental.pallas import tpu as pltpu

D_MODEL = 1024
GRID_W = 64
HEAD_DIM = 64
ATTN_HEADS = 8
KV_HEADS = 2
Q_PER_KV = ATTN_HEADS // KV_HEADS
ATTN_WIDTH = ATTN_HEADS * HEAD_DIM
KV_WIDTH = KV_HEADS * HEAD_DIM
WINDOW = 128
FOURIER_GROUPS = 4
FOURIER_GROUP_DIM = 64
FOURIER_WIDTH = FOURIER_GROUPS * FOURIER_GROUP_DIM
CONV_WIDTH = 256
MIX_WIDTH = ATTN_WIDTH + FOURIER_WIDTH + CONV_WIDTH
PROJ_WIDTH = 2 * ATTN_WIDTH + 2 * KV_WIDTH + 2 * FOURIER_WIDTH + 4 * CONV_WIDTH
ROPE_FREQS = HEAD_DIM // 4
ROPE_BASE = 10000.0
NORM_EPS = 1e-6
LOG2E = math.log2(math.e)

C_Q = 0
C_K = C_Q + ATTN_WIDTH
C_V = C_K + KV_WIDTH
C_GA = C_V + KV_WIDTH
C_UF = C_GA + ATTN_WIDTH
C_GF = C_UF + FOURIER_WIDTH
C_ZC = C_GF + FOURIER_WIDTH
C_BC = C_ZC + CONV_WIDTH
C_CC = C_BC + CONV_WIDTH
C_GC = C_CC + CONV_WIDTH

LANES = 128
SUBLANES = 8
BF16_ROWS = 16
VMEM_LIMIT = 52 * 1024 * 1024

MOD_ROWS = 8
BF16 = jnp.bfloat16
F32 = jnp.float32

FFT_N2 = 64
K1_GROUP = SUBLANES
STAGE1_UNROLL = 8
Z_CHUNK = 1024
Z_PITCH = 136
SOFTMAX_ROWS = 32

BLOCK_Q = 128
KEY_SPAN = BLOCK_Q + 2 * WINDOW
HEAD_PAIRS = ATTN_WIDTH // LANES
PAIRS_PER_KV = HEAD_PAIRS // KV_HEADS

TILE_ROWS = 512
ATTN_ROWS = 1024


def _silu(x):
    return x / (1.0 + jnp.exp(-x))


def _dot(a, b):
    return jnp.dot(a, b, preferred_element_type=F32)


def _dot_nt(a, b):
    return lax.dot_general(a, b, (((1,), (1,)), ((), ())), preferred_element_type=F32)


def _cparams(*sem):
    return pltpu.CompilerParams(dimension_semantics=sem, vmem_limit_bytes=VMEM_LIMIT)


def _rope_tables(n):
    t = np.arange(n)
    row = (t // GRID_W).astype(np.float64)
    col = (t % GRID_W).astype(np.float64)
    inv = ROPE_BASE ** (-np.arange(ROPE_FREQS, dtype=np.float64) / ROPE_FREQS)
    ar = row[:, None] * inv
    ac = col[:, None] * inv
    z = np.zeros_like(ar)
    cos_h = np.concatenate([np.cos(ar), np.cos(ar), np.cos(ac), np.cos(ac)], axis=1)
    sin_up = np.concatenate([-np.sin(ar), z, -np.sin(ac), z], axis=1)
    sin_dn = np.concatenate([z, np.sin(ar), z, np.sin(ac)], axis=1)
    rep = LANES // HEAD_DIM
    return tuple(np.tile(a, (1, rep)).astype(np.float32) for a in (cos_h, sin_up, sin_dn))


def _channel_dft():
    c = np.arange(FOURIER_GROUP_DIM)
    ang = 2.0 * np.pi * np.outer(c, c) / FOURIER_GROUP_DIM
    eye = np.eye(LANES // FOURIER_GROUP_DIM)
    return (np.kron(eye, np.cos(ang)).astype(np.float32),
            np.kron(eye, np.sin(ang)).astype(np.float32))


def _stage1_mats(n):
    n1_len = n // FFT_N2
    k1 = np.arange(n1_len)[:, None]
    n1 = np.arange(n1_len)[None, :]
    out = np.empty((FFT_N2, 2 * n1_len, 2 * n1_len), np.float32)
    for n2 in range(FFT_N2):
        ang = 2.0 * np.pi * ((k1 * (FFT_N2 * n1 + n2)) % n) / n
        ce, se = np.cos(ang), np.sin(ang)
        out[n2] = np.block([[ce, se], [-se, ce]])
    return out


def _stage2_mat():
    k2 = np.arange(FFT_N2)
    ang = 2.0 * np.pi * np.outer(k2, k2) / FFT_N2
    cs = np.stack([np.cos(ang), np.sin(ang)], axis=-1)
    eye = np.eye(K1_GROUP)
    m = np.einsum('knp,rs->krnps', cs, eye)
    return m.reshape(FFT_N2 * K1_GROUP, FFT_N2 * 2 * K1_GROUP).astype(np.float32)


def _dense_dft(n):
    t = np.arange(n)
    ang = 2.0 * np.pi * (np.outer(t, t) % n) / n
    return np.concatenate([np.cos(ang), np.sin(ang)], axis=1).astype(np.float32)


def _mod_kernel(c_ref, w_ref, b_ref, o_ref):
    o_ref[0] = _dot(_silu(c_ref[...]), w_ref[0]) + b_ref[0]


def _modulation(c_rows, w_mod, b_mod):
    depth = w_mod.shape[0]
    return pl.pallas_call(
        _mod_kernel,
        grid=(depth, 3),
        in_specs=[pl.BlockSpec((MOD_ROWS, D_MODEL), lambda l, j: (0, 0)),
                  pl.BlockSpec((1, D_MODEL, D_MODEL), lambda l, j: (l, 0, j)),
                  pl.BlockSpec((1, 1, D_MODEL), lambda l, j: (l, 0, j))],
        out_specs=pl.BlockSpec((1, MOD_ROWS, D_MODEL), lambda l, j: (l, 0, j)),
        out_shape=jax.ShapeDtypeStruct((depth, MOD_ROWS, 3 * D_MODEL), F32),
        compiler_params=_cparams("arbitrary", "arbitrary"),
        name="modulation",
    )(c_rows, w_mod, b_mod.reshape(depth, 1, 3 * D_MODEL))


def _inproj_kernel(*refs, rope, ctx_row):
    if rope:
        (x_ref, shift_ref, scale_ref, g_ref, w_ref, cos_ref, sup_ref, sdn_ref,
         q_ref, k_ref, v_ref, sga_ref, uf_ref, sgf_ref, t_ref, bg_ref) = refs
    else:
        (x_ref, shift_ref, scale_ref, g_ref, w_ref,
         q_ref, k_ref, v_ref, sga_ref, uf_ref, sgf_ref, t_ref, bg_ref) = refs
    row = pl.program_id(0) if ctx_row is None else ctx_row
    shift = shift_ref[0, pl.ds(row, 1), :]
    scale = scale_ref[0, pl.ds(row, 1), :]
    x = x_ref[0]
    r = lax.rsqrt(jnp.mean(x * x, axis=-1, keepdims=True) + NORM_EPS)
    h = ((x * r) * (g_ref[0] * (1.0 + scale)) + shift).astype(BF16)

    def proj(c0, width):
        return _dot(h, w_ref[0, :, c0:c0 + width])

    def rotate(y):
        if not rope:
            return y
        return (y * cos_ref[...] + pltpu.roll(y, LANES - ROPE_FREQS, 1) * sup_ref[...]
                + pltpu.roll(y, ROPE_FREQS, 1) * sdn_ref[...])

    def with_swapped_heads(y):
        return jnp.concatenate([y, pltpu.roll(y, HEAD_DIM, 1)], axis=1).astype(BF16)

    q = proj(C_Q, ATTN_WIDTH)
    for p in range(HEAD_PAIRS):
        sl = slice(p * LANES, (p + 1) * LANES)
        q_ref[0, :, sl] = (rotate(q[:, sl]) * (HEAD_DIM ** -0.5 * LOG2E)).astype(BF16)
    kv = proj(C_K, 2 * KV_WIDTH)
    k_ref[0] = with_swapped_heads(rotate(kv[:, :KV_WIDTH]))
    v_ref[0] = with_swapped_heads(kv[:, KV_WIDTH:])
    sga_ref[0] = _silu(proj(C_GA, ATTN_WIDTH)).astype(BF16)
    uf_ref[0] = proj(C_UF, FOURIER_WIDTH).astype(BF16)
    sgf_ref[0] = _silu(proj(C_GF, FOURIER_WIDTH)).astype(BF16)
    t_ref[0] = (proj(C_CC, CONV_WIDTH) * proj(C_ZC, CONV_WIDTH)).astype(BF16)
    bg_ref[0] = (proj(C_BC, CONV_WIDTH) * _silu(proj(C_GC, CONV_WIDTH))).astype(BF16)


def _in_projection(x, mod, g_pre, w_in, layer, *, rope_tabs, ctx_row, tm):
    b, n, _ = x.shape
    rope = rope_tabs is not None
    row3 = lambda width: pl.BlockSpec((1, tm, width), lambda bi, i: (bi, i, 0))
    in_specs = [row3(D_MODEL),
                pl.BlockSpec((1, MOD_ROWS, D_MODEL), lambda bi, i: (layer, 0, 0)),
                pl.BlockSpec((1, MOD_ROWS, D_MODEL), lambda bi, i: (layer, 0, 1)),
                pl.BlockSpec((1, 1, D_MODEL), lambda bi, i: (layer, 0, 0)),
                pl.BlockSpec((1, D_MODEL, PROJ_WIDTH), lambda bi, i: (layer, 0, 0))]
    args = [x, mod, mod, g_pre, w_in]
    if rope:
        in_specs += [pl.BlockSpec((tm, LANES), lambda bi, i: (i, 0))] * 3
        args += list(rope_tabs)
    widths = (ATTN_WIDTH, 2 * KV_WIDTH, 2 * KV_WIDTH, ATTN_WIDTH, FOURIER_WIDTH, FOURIER_WIDTH,
              CONV_WIDTH, CONV_WIDTH)
    return pl.pallas_call(
        functools.partial(_inproj_kernel, rope=rope, ctx_row=ctx_row),
        grid=(b, n // tm),
        in_specs=in_specs,
        out_specs=[row3(w) for w in widths],
        out_shape=[jax.ShapeDtypeStruct((b, n, w), BF16) for w in widths],
        compiler_params=_cparams("arbitrary", "arbitrary"),
        name="in_projection_rope" if rope else "in_projection_ctx",
    )(*args)


def _lane_half_variants(blk, fill):
    straight, swapped = blk[:, :KV_WIDTH], blk[:, KV_WIDTH:]
    lo = lax.broadcasted_iota(jnp.int32, straight.shape, 1) < HEAD_DIM
    other = jnp.full_like(straight, fill)
    return (jnp.where(lo, straight, other), jnp.where(lo, other, swapped),
            jnp.where(lo, swapped, other), jnp.where(lo, other, straight))


def _merge_head_pair(pv_lo, pv_hi, sink_lo, sink_hi):
    lane_lo = lax.broadcasted_iota(jnp.int32, pv_lo.shape, 1) < HEAD_DIM
    num = jnp.where(lane_lo, pv_lo, pv_hi)
    den = pltpu.roll(jnp.where(lane_lo, pv_hi, pv_lo), HEAD_DIM, 1) + jnp.where(lane_lo, sink_lo, sink_hi)
    return num / den


def _attn_kernel(sink_ref, q_ref, g_ref, kp_ref, km_ref, kn_ref, vp_ref, vm_ref, vn_ref, kc_ref, vc_ref,
                 o_ref, k_s, v_s, kctx_s, vctx_s, bias_s, s_s, p_s, r_s, *, tq, n_seq, n_ctx):
    i = pl.program_id(1)
    n_blocks = tq // BLOCK_Q
    for off, kref, vref, rows in ((0, kp_ref, vp_ref, WINDOW), (WINDOW, km_ref, vm_ref, tq),
                                  (WINDOW + tq, kn_ref, vn_ref, WINDOW)):
        for idx, kk in enumerate(_lane_half_variants(kref[0], 0.0)):
            k_s[idx, off:off + rows] = kk
        for idx, vv in enumerate(_lane_half_variants(vref[0], 1.0)):
            v_s[idx, off:off + rows] = vv
    for idx, kk in enumerate(_lane_half_variants(kc_ref[0], 0.0)):
        kctx_s[idx] = kk
    for idx, vv in enumerate(_lane_half_variants(vc_ref[0], 1.0)):
        vctx_s[idx] = vv

    ii = lax.broadcasted_iota(jnp.int32, (BLOCK_Q, WINDOW), 0)
    jj = lax.broadcasted_iota(jnp.int32, (BLOCK_Q, WINDOW), 1)
    head_band = jnp.where(jj >= ii, 0.0, -jnp.inf)
    tail_band = jnp.where(jj <= ii, 0.0, -jnp.inf)
    bias_s[0] = head_band
    bias_s[1] = tail_band
    bias_s[2] = jnp.where(i == 0, -jnp.inf, head_band)
    bias_s[3] = jnp.where(i == pl.num_programs(1) - 1, -jnp.inf, tail_band)

    head_w = n_ctx + KEY_SPAN

    def scores(sb):
        par, r0 = sb % 2, sb * BLOCK_Q
        q = q_ref[0, r0:r0 + BLOCK_Q, :]
        for p in range(HEAD_PAIRS):
            kv = p // PAIRS_PER_KV
            keys = jnp.concatenate([kctx_s[2 * kv], k_s[2 * kv, r0:r0 + KEY_SPAN],
                                    kctx_s[2 * kv + 1], k_s[2 * kv + 1, r0:r0 + KEY_SPAN]], axis=0)
            s_s[par, p] = _dot_nt(q[:, p * LANES:(p + 1) * LANES], keys)

    def softmax(sb):
        par = sb % 2
        head_bias = 2 if sb == 0 else 0
        tail_bias = 3 if sb == n_blocks - 1 else 1
        n_cols = head_w // LANES
        for p in range(HEAD_PAIRS):
            for half in range(2):
                h = 2 * p + half
                sink2 = sink_ref[h] * LOG2E
                for rs in range(BLOCK_Q // SOFTMAX_ROWS):
                    rows = slice(rs * SOFTMAX_ROWS, (rs + 1) * SOFTMAX_ROWS)
                    cols = [s_s[par, p, rows, half * head_w + t * LANES:half * head_w + (t + 1) * LANES]
                            for t in range(n_cols)]
                    first_local = n_ctx // LANES
                    cols[first_local] = cols[first_local] + bias_s[head_bias, rows]
                    cols[-1] = cols[-1] + bias_s[tail_bias, rows]
                    m = jnp.maximum(jnp.max(functools.reduce(jnp.maximum, cols), axis=-1, keepdims=True),
                                    sink2)
                    for t, col in enumerate(cols):
                        p_s[par, p, rows, half * head_w + t * LANES:half * head_w + (t + 1) * LANES] = (
                            jnp.exp2(col - m).astype(BF16))
                    r_s[par, h, rows] = jnp.broadcast_to(jnp.exp2(sink2 - m), (SOFTMAX_ROWS, LANES))

    def weighted_values(sb):
        par, r0 = sb % 2, sb * BLOCK_Q
        for p in range(HEAD_PAIRS):
            kv = p // PAIRS_PER_KV
            pv = []
            for half in range(2):
                vals = jnp.concatenate([vctx_s[2 * kv + half], v_s[2 * kv + half, r0:r0 + KEY_SPAN]], axis=0)
                pv.append(_dot(p_s[par, p, :, half * head_w:(half + 1) * head_w], vals))
            out = _merge_head_pair(pv[0], pv[1], r_s[par, 2 * p], r_s[par, 2 * p + 1])
            gate = g_ref[0, r0:r0 + BLOCK_Q, p * LANES:(p + 1) * LANES].astype(F32)
            o_ref[0, r0:r0 + BLOCK_Q, p * LANES:(p + 1) * LANES] = (out * gate).astype(BF16)

    scores(0)
    for sb in range(n_blocks):
        if sb + 1 < n_blocks:
            scores(sb + 1)
        softmax(sb)
        weighted_values(sb)


def _window_attention(q, gate, k, v, kc, vc, sink, *, tq):
    b, n, _ = q.shape
    nc = kc.shape[1]
    per = tq // WINDOW
    last = n // WINDOW - 1
    kvw = 2 * KV_WIDTH
    variants = 2 * KV_HEADS
    main = lambda width: pl.BlockSpec((1, tq, width), lambda bi, i: (bi, i, 0))
    prev = pl.BlockSpec((1, WINDOW, kvw), lambda bi, i: (bi, jnp.maximum(i * per - 1, 0), 0))
    nxt = pl.BlockSpec((1, WINDOW, kvw), lambda bi, i: (bi, jnp.minimum((i + 1) * per, last), 0))
    ctx = pl.BlockSpec((1, nc, kvw), lambda bi, i: (bi, 0, 0))
    span = tq + 2 * WINDOW
    s_cols = 2 * (nc + KEY_SPAN)
    return pl.pallas_call(
        functools.partial(_attn_kernel, tq=tq, n_seq=n, n_ctx=nc),
        grid=(b, n // tq),
        in_specs=[pl.BlockSpec(memory_space=pltpu.SMEM), main(ATTN_WIDTH), main(ATTN_WIDTH),
                  prev, main(kvw), nxt, prev, main(kvw), nxt, ctx, ctx],
        out_specs=main(ATTN_WIDTH),
        out_shape=jax.ShapeDtypeStruct((b, n, ATTN_WIDTH), BF16),
        scratch_shapes=[pltpu.VMEM((variants, span, KV_WIDTH), BF16),
                        pltpu.VMEM((variants, span, KV_WIDTH), BF16),
                        pltpu.VMEM((variants, nc, KV_WIDTH), BF16),
                        pltpu.VMEM((variants, nc, KV_WIDTH), BF16),
                        pltpu.VMEM((4, BLOCK_Q, WINDOW), F32),
                        pltpu.VMEM((2, HEAD_PAIRS, BLOCK_Q, s_cols), F32),
                        pltpu.VMEM((2, HEAD_PAIRS, BLOCK_Q, s_cols), BF16),
                        pltpu.VMEM((2, ATTN_HEADS, BLOCK_Q, LANES), F32)],
        compiler_params=_cparams("arbitrary", "arbitrary"),
        name="window_attention",
    )(sink, q, gate, k, k, k, v, v, v, kc, vc)


def _ctx_attn_kernel(sink_ref, q_ref, g_ref, kc_ref, vc_ref, o_ref):
    keys = _lane_half_variants(kc_ref[0], 0.0)
    vals = _lane_half_variants(vc_ref[0], 1.0)
    for p in range(HEAD_PAIRS):
        kv = p // PAIRS_PER_KV
        qp = q_ref[0, :, p * LANES:(p + 1) * LANES]
        pv, sink_terms = [], []
        for half in range(2):
            s = _dot_nt(qp, keys[2 * kv + half])
            sink2 = sink_ref[2 * p + half] * LOG2E
            m = jnp.maximum(jnp.max(s, axis=-1, keepdims=True), sink2)
            pv.append(_dot(jnp.exp2(s - m).astype(BF16), vals[2 * kv + half]))
            sink_terms.append(jnp.exp2(sink2 - m))
        out = _merge_head_pair(pv[0], pv[1], sink_terms[0], sink_terms[1])
        gate = g_ref[0, :, p * LANES:(p + 1) * LANES].astype(F32)
        o_ref[0, :, p * LANES:(p + 1) * LANES] = (out * gate).astype(BF16)


def _context_attention(q, gate, kc, vc, sink):
    b, nc, _ = q.shape
    blk = lambda width: pl.BlockSpec((1, nc, width), lambda bi: (bi, 0, 0))
    return pl.pallas_call(
        _ctx_attn_kernel,
        grid=(b,),
        in_specs=[pl.BlockSpec(memory_space=pltpu.SMEM), blk(ATTN_WIDTH), blk(ATTN_WIDTH),
                  blk(2 * KV_WIDTH), blk(2 * KV_WIDTH)],
        out_specs=blk(ATTN_WIDTH),
        out_shape=jax.ShapeDtypeStruct((b, nc, ATTN_WIDTH), BF16),
        compiler_params=_cparams("arbitrary"),
        name="context_attention",
    )(sink, q, gate, kc, vc)


def _channel_mix_matrix(cc_ref, sc_ref, wf_ref, scale):
    wf = wf_ref[0]
    return (jnp.concatenate([_dot(cc_ref[...], wf), -_dot(sc_ref[...], wf)], axis=1) * scale).astype(BF16)


def _fourier_kernel(uf_ref, gate_ref, cc_ref, sc_ref, wf_ref, g_ref, m2_ref, o_ref, z_s, y_s, *, n_seq):
    n1_len = n_seq // FFT_N2
    mix = _channel_mix_matrix(cc_ref, sc_ref, wf_ref, (n_seq * FOURIER_GROUP_DIM) ** -0.5)

    per_chunk = Z_CHUNK // FFT_N2
    for c in range(n_seq // Z_CHUNK):
        z = _dot(uf_ref[0, c * Z_CHUNK:(c + 1) * Z_CHUNK, :], mix)
        for j in range(per_chunk):
            for part in range(2):
                z_s[part, pl.ds(c * per_chunk + j, FFT_N2, stride=Z_PITCH), :] = (
                    z[j * FFT_N2:(j + 1) * FFT_N2, part * LANES:(part + 1) * LANES])

    def stage1(t, carry):
        for u in range(STAGE1_UNROLL):
            n2 = t * STAGE1_UNROLL + u
            z0 = pl.multiple_of(n2 * Z_PITCH, SUBLANES)
            rhs = jnp.concatenate([z_s[part, pl.ds(z0, n1_len), :] for part in range(2)],
                                  axis=0).astype(BF16)
            y_s[n2] = _dot(g_ref[n2], rhs).reshape(2, n1_len, LANES)
        return carry

    lax.fori_loop(0, FFT_N2 // STAGE1_UNROLL, stage1, 0)

    def stage2(t, carry):
        r0 = pl.multiple_of(t * BF16_ROWS, BF16_ROWS)
        outs = []
        for u in range(BF16_ROWS // K1_GROUP):
            blk = y_s[:, :, pl.ds(r0 + u * K1_GROUP, K1_GROUP), :]
            rhs = blk.reshape(FFT_N2 * 2 * K1_GROUP, LANES).astype(BF16)
            outs.append(_dot(m2_ref[...], rhs).reshape(FFT_N2, K1_GROUP, LANES))
        gate = gate_ref[0, :, pl.ds(r0, BF16_ROWS), :].astype(F32)
        o_ref[0, 0, :, pl.ds(r0, BF16_ROWS), :] = (jnp.concatenate(outs, axis=1) * gate).astype(BF16)
        return carry

    lax.fori_loop(0, n1_len // BF16_ROWS, stage2, 0)


def _fourier_mix(uf, gate, wf_half, consts):
    b, n, _ = uf.shape
    halves = FOURIER_WIDTH // LANES
    n1_len = n // FFT_N2
    cc, sc, g, m2 = consts
    full = lambda shape: pl.BlockSpec(shape, lambda bi, hf: (0,) * len(shape))
    out = pl.pallas_call(
        functools.partial(_fourier_kernel, n_seq=n),
        grid=(b, halves),
        in_specs=[pl.BlockSpec((1, n, LANES), lambda bi, hf: (bi, 0, hf)),
                  pl.BlockSpec((1, FFT_N2, n1_len, LANES), lambda bi, hf: (bi, 0, 0, hf)),
                  full((LANES, LANES)), full((LANES, LANES)),
                  pl.BlockSpec((1, LANES, LANES), lambda bi, hf: (hf, 0, 0)),
                  full(g.shape), full(m2.shape)],
        out_specs=pl.BlockSpec((1, 1, FFT_N2, n1_len, LANES), lambda bi, hf: (bi, hf, 0, 0, 0)),
        out_shape=jax.ShapeDtypeStruct((b, halves, FFT_N2, n1_len, LANES), BF16),
        scratch_shapes=[pltpu.VMEM((2, FFT_N2 * Z_PITCH, LANES), F32),
                        pltpu.VMEM((FFT_N2, 2, n1_len, LANES), F32)],
        compiler_params=_cparams("arbitrary", "arbitrary"),
        name="fourier_mix",
    )(uf, gate.reshape(b, FFT_N2, n1_len, FOURIER_WIDTH), cc, sc, wf_half, g, m2)
    return out.reshape(b, halves, n, LANES)


def _ctx_fourier_kernel(uf_ref, gate_ref, cc_ref, sc_ref, wf_ref, dft_ref, o_ref, *, n_seq):
    mix = _channel_mix_matrix(cc_ref, sc_ref, wf_ref, (n_seq * FOURIER_GROUP_DIM) ** -0.5)
    z = _dot(uf_ref[0], mix)
    rhs = jnp.concatenate([z[:, :LANES], z[:, LANES:]], axis=0).astype(BF16)
    o_ref[0, 0] = (_dot(dft_ref[...], rhs) * gate_ref[0].astype(F32)).astype(BF16)


def _ctx_fourier_mix(uf, gate, wf_half, cc, sc, dft):
    b, n, _ = uf.shape
    halves = FOURIER_WIDTH // LANES
    full = lambda shape: pl.BlockSpec(shape, lambda bi, hf: (0,) * len(shape))
    half = pl.BlockSpec((1, n, LANES), lambda bi, hf: (bi, 0, hf))
    return pl.pallas_call(
        functools.partial(_ctx_fourier_kernel, n_seq=n),
        grid=(b, halves),
        in_specs=[half, half, full((LANES, LANES)), full((LANES, LANES)),
                  pl.BlockSpec((1, LANES, LANES), lambda bi, hf: (hf, 0, 0)), full(dft.shape)],
        out_specs=pl.BlockSpec((1, 1, n, LANES), lambda bi, hf: (bi, hf, 0, 0)),
        out_shape=jax.ShapeDtypeStruct((b, halves, n, LANES), BF16),
        compiler_params=_cparams("arbitrary", "arbitrary"),
        name="context_fourier_mix",
    )(uf, gate, cc, sc, wf_half, dft)


def _outproj_kernel(*refs, halo, ctx_row, tm):
    if halo:
        (x_ref, a_ref, f_ref, t_ref, tp_ref, tn_ref, bg_ref, cw_ref, cb_ref,
         w_ref, g_ref, gate_ref, o_ref) = refs
    else:
        (x_ref, a_ref, f_ref, t_ref, bg_ref, cw_ref, cb_ref, w_ref, g_ref, gate_ref, o_ref) = refs
    row = pl.program_id(0) if ctx_row is None else ctx_row
    gate = gate_ref[0, pl.ds(row, 1), :]
    t = t_ref[0].astype(F32)
    ridx = lax.broadcasted_iota(jnp.int32, t.shape, 0)
    up = jnp.where(ridx == 0, 0.0, pltpu.roll(t, 1, 0))
    dn = jnp.where(ridx == tm - 1, 0.0, pltpu.roll(t, tm - 1, 0))
    if halo:
        i = pl.program_id(1)
        prev_row = tp_ref[0].astype(F32)[BF16_ROWS - 1:BF16_ROWS, :]
        next_row = tn_ref[0].astype(F32)[0:1, :]
        prev_row = jnp.where(i > 0, prev_row, 0.0)
        next_row = jnp.where(i < pl.num_programs(1) - 1, next_row, 0.0)
        up = jnp.where(ridx == 0, prev_row, up)
        dn = jnp.where(ridx == tm - 1, next_row, dn)
    cw = cw_ref[0]
    conv = up * cw[0:1] + t * cw[1:2] + dn * cw[2:3] + cb_ref[0]
    parts = [a_ref[0]] + [f_ref[0, s] for s in range(FOURIER_WIDTH // LANES)]
    parts.append((conv * bg_ref[0].astype(F32)).astype(BF16))
    y = _dot(jnp.concatenate(parts, axis=1), w_ref[0])
    r = lax.rsqrt(jnp.mean(y * y, axis=-1, keepdims=True) + NORM_EPS)
    o_ref[0] = x_ref[0] + gate * ((y * r) * g_ref[0])


def _out_projection(x, a, f, t, bg, conv_w, conv_b, w_out, g_post, mod, layer, *, ctx_row, tm):
    b, n, _ = x.shape
    halo = n > tm
    row3 = lambda width: pl.BlockSpec((1, tm, width), lambda bi, i: (bi, i, 0))
    lay3 = lambda shape: pl.BlockSpec((1,) + shape, lambda bi, i: (layer, 0, 0))
    slabs = FOURIER_WIDTH // LANES
    in_specs = [row3(D_MODEL), row3(ATTN_WIDTH),
                pl.BlockSpec((1, slabs, tm, LANES), lambda bi, i: (bi, 0, i, 0)), row3(CONV_WIDTH)]
    args = [x, a, f, t]
    if halo:
        per = tm // BF16_ROWS
        last = n // BF16_ROWS - 1
        in_specs += [pl.BlockSpec((1, BF16_ROWS, CONV_WIDTH),
                                  lambda bi, i: (bi, jnp.maximum(i * per - 1, 0), 0)),
                     pl.BlockSpec((1, BF16_ROWS, CONV_WIDTH),
                                  lambda bi, i: (bi, jnp.minimum((i + 1) * per, last), 0))]
        args += [t, t]
    in_specs += [row3(CONV_WIDTH), lay3((3, CONV_WIDTH)), lay3((1, CONV_WIDTH)),
                 lay3((MIX_WIDTH, D_MODEL)), lay3((1, D_MODEL)),
                 pl.BlockSpec((1, MOD_ROWS, D_MODEL), lambda bi, i: (layer, 0, 2))]
    args += [bg, conv_w, conv_b, w_out, g_post, mod]
    return pl.pallas_call(
        functools.partial(_outproj_kernel, halo=halo, ctx_row=ctx_row, tm=tm),
        grid=(b, n // tm),
        in_specs=in_specs,
        out_specs=row3(D_MODEL),
        out_shape=jax.ShapeDtypeStruct(x.shape, F32),
        compiler_params=_cparams("arbitrary", "arbitrary"),
        name="out_projection" if halo else "out_projection_ctx",
    )(*args)


def kernel(x, c, ctx, c_ctx, w_mod, b_mod, g_pre, g_post, w_in, w_out, sink, w_fourier, conv_w, conv_b):
    depth = w_mod.shape[0]
    b, n, _ = x.shape
    nc = ctx.shape[1]
    assert b + 1 <= MOD_ROWS and n % GRID_W == 0 and n % ATTN_ROWS == 0 and n % Z_CHUNK == 0
    assert (n // FFT_N2) % BF16_ROWS == 0 and FFT_N2 % STAGE1_UNROLL == 0 and n // FFT_N2 <= Z_PITCH

    w_in_b = w_in.astype(BF16)
    w_out_b = w_out.astype(BF16)

    rope_tabs = tuple(jnp.asarray(t) for t in _rope_tables(n))
    cc, sc = (jnp.asarray(m) for m in _channel_dft())
    stage1 = jnp.asarray(_stage1_mats(n)).astype(BF16)
    stage2 = jnp.asarray(_stage2_mat()).astype(BF16)
    ctx_dft = jnp.asarray(_dense_dft(nc)).astype(BF16)
    groups_per_half = LANES // FOURIER_GROUP_DIM
    wf_half = jnp.zeros((depth, FOURIER_GROUPS // groups_per_half, LANES, LANES), F32)
    for g in range(FOURIER_GROUPS):
        o = (g % groups_per_half) * FOURIER_GROUP_DIM
        wf_half = wf_half.at[:, g // groups_per_half, o:o + FOURIER_GROUP_DIM,
                             o:o + FOURIER_GROUP_DIM].set(w_fourier[:, g])

    c_rows = jnp.zeros((MOD_ROWS, D_MODEL), F32).at[:b].set(c).at[b].set(c_ctx)
    mod = _modulation(c_rows, w_mod, b_mod)
    g_pre3 = g_pre.reshape(depth, 1, D_MODEL)
    g_post3 = g_post.reshape(depth, 1, D_MODEL)
    conv_b3 = conv_b.reshape(depth, 1, CONV_WIDTH)

    for l in range(depth):
        qc, kc, vc, sgac, ufc, sgfc, tc, bgc = _in_projection(
            ctx, mod, g_pre3, w_in_b, l, rope_tabs=None, ctx_row=b, tm=nc)
        q, k, v, sga, uf, sgf, t, bg = _in_projection(
            x, mod, g_pre3, w_in_b, l, rope_tabs=rope_tabs, ctx_row=None, tm=TILE_ROWS)
        a = _window_attention(q, sga, k, v, kc, vc, sink[l], tq=ATTN_ROWS)
        f = _fourier_mix(uf, sgf, wf_half[l], (cc, sc, stage1, stage2))
        x = _out_projection(x, a, f, t, bg, conv_w, conv_b3, w_out_b, g_post3, mod, l,
                            ctx_row=None, tm=TILE_ROWS)
        if l < depth - 1:
            ac = _context_attention(qc, sgac, kc, vc, sink[l])
            fc = _ctx_fourier_mix(ufc, sgfc, wf_half[l], cc, sc, ctx_dft)
            ctx = _out_projection(ctx, ac, fc, tc, bgc, conv_w, conv_b3, w_out_b, g_post3,
                                  mod, l, ctx_row=b, tm=nc)
    return x
```

```python
import functools
import math

import numpy as np
import jax
import jax.numpy as jnp
from jax import lax
from jax.experimental import pallas as pl
from jax.experimental.pallas import tpu as pltpu

D_MODEL = 1024
GRID_W = 64
HEAD_DIM = 64
ATTN_HEADS = 8
KV_HEADS = 2
Q_PER_KV = ATTN_HEADS // KV_HEADS
ATTN_WIDTH = ATTN_HEADS * HEAD_DIM
KV_WIDTH = KV_HEADS * HEAD_DIM
WINDOW = 128
FOURIER_GROUPS = 4
FOURIER_GROUP_DIM = 64
FOURIER_WIDTH = FOURIER_GROUPS * FOURIER_GROUP_DIM
CONV_WIDTH = 256
MIX_WIDTH = ATTN_WIDTH + FOURIER_WIDTH + CONV_WIDTH
PROJ_WIDTH = 2 * ATTN_WIDTH + 2 * KV_WIDTH + 2 * FOURIER_WIDTH + 4 * CONV_WIDTH
ROPE_FREQS = HEAD_DIM // 4
ROPE_BASE = 10000.0
NORM_EPS = 1e-6
LOG2E = math.log2(math.e)

C_Q = 0
C_K = C_Q + ATTN_WIDTH
C_V = C_K + KV_WIDTH
C_GA = C_V + KV_WIDTH
C_UF = C_GA + ATTN_WIDTH
C_GF = C_UF + FOURIER_WIDTH
C_ZC = C_GF + FOURIER_WIDTH
C_BC = C_ZC + CONV_WIDTH
C_CC = C_BC + CONV_WIDTH
C_GC = C_CC + CONV_WIDTH

LANES = 128
SUBLANES = 8
BF16_ROWS = 16
VMEM_LIMIT = 52 * 1024 * 1024

MOD_ROWS = 8
BF16 = jnp.bfloat16
F32 = jnp.float32

FFT_N2 = 64
K1_GROUP = SUBLANES
STAGE1_UNROLL = 8
Z_CHUNK = 1024
Z_PITCH = 136
SOFTMAX_ROWS = 32

BLOCK_Q = 128
KEY_SPAN = BLOCK_Q + 2 * WINDOW
HEAD_PAIRS = ATTN_WIDTH // LANES
PAIRS_PER_KV = HEAD_PAIRS // KV_HEADS

TILE_ROWS = 512
ATTN_ROWS = 1024
MIX_ROWS = 256


def _silu(x):
    return x / (1.0 + jnp.exp(-x))


def _dot(a, b):
    return jnp.dot(a, b, preferred_element_type=F32)


def _dot_nt(a, b):
    return lax.dot_general(a, b, (((1,), (1,)), ((), ())), preferred_element_type=F32)


def _cparams(*sem):
    return pltpu.CompilerParams(dimension_semantics=sem, vmem_limit_bytes=VMEM_LIMIT)


def _rope_tables(n):
    t = np.arange(n)
    row = (t // GRID_W).astype(np.float64)
    col = (t % GRID_W).astype(np.float64)
    inv = ROPE_BASE ** (-np.arange(ROPE_FREQS, dtype=np.float64) / ROPE_FREQS)
    ar = row[:, None] * inv
    ac = col[:, None] * inv
    z = np.zeros_like(ar)
    cos_h = np.concatenate([np.cos(ar), np.cos(ar), np.cos(ac), np.cos(ac)], axis=1)
    sin_up = np.concatenate([-np.sin(ar), z, -np.sin(ac), z], axis=1)
    sin_dn = np.concatenate([z, np.sin(ar), z, np.sin(ac)], axis=1)
    rep = LANES // HEAD_DIM
    return tuple(np.tile(a, (1, rep)).astype(np.float32) for a in (cos_h, sin_up, sin_dn))


def _channel_dft():
    c = np.arange(FOURIER_GROUP_DIM)
    ang = 2.0 * np.pi * np.outer(c, c) / FOURIER_GROUP_DIM
    eye = np.eye(LANES // FOURIER_GROUP_DIM)
    return (np.kron(eye, np.cos(ang)).astype(np.float32),
            np.kron(eye, np.sin(ang)).astype(np.float32))


def _stage1_mats(n):
    n1_len = n // FFT_N2
    k1 = np.arange(n1_len)[:, None]
    n1 = np.arange(n1_len)[None, :]
    out = np.empty((FFT_N2, 2 * n1_len, 2 * n1_len), np.float32)
    for n2 in range(FFT_N2):
        ang = 2.0 * np.pi * ((k1 * (FFT_N2 * n1 + n2)) % n) / n
        ce, se = np.cos(ang), np.sin(ang)
        out[n2] = np.block([[ce, se], [-se, ce]])
    return out


def _stage2_mat():
    k2 = np.arange(FFT_N2)
    ang = 2.0 * np.pi * np.outer(k2, k2) / FFT_N2
    cs = np.stack([np.cos(ang), np.sin(ang)], axis=-1)
    eye = np.eye(K1_GROUP)
    m = np.einsum('knp,rs->krnps', cs, eye)
    return m.reshape(FFT_N2 * K1_GROUP, FFT_N2 * 2 * K1_GROUP).astype(np.float32)


def _dense_dft(n):
    t = np.arange(n)
    ang = 2.0 * np.pi * (np.outer(t, t) % n) / n
    return np.concatenate([np.cos(ang), np.sin(ang)], axis=1).astype(np.float32)


def _mod_kernel(c_ref, w_ref, b_ref, o_ref):
    o_ref[0] = _dot(_silu(c_ref[...]), w_ref[0]) + b_ref[0]


def _modulation(c_rows, w_mod, b_mod):
    depth = w_mod.shape[0]
    return pl.pallas_call(
        _mod_kernel,
        grid=(depth, 3),
        in_specs=[pl.BlockSpec((MOD_ROWS, D_MODEL), lambda l, j: (0, 0)),
                  pl.BlockSpec((1, D_MODEL, D_MODEL), lambda l, j: (l, 0, j)),
                  pl.BlockSpec((1, 1, D_MODEL), lambda l, j: (l, 0, j))],
        out_specs=pl.BlockSpec((1, MOD_ROWS, D_MODEL), lambda l, j: (l, 0, j)),
        out_shape=jax.ShapeDtypeStruct((depth, MOD_ROWS, 3 * D_MODEL), F32),
        compiler_params=_cparams("arbitrary", "arbitrary"),
        name="modulation",
    )(c_rows, w_mod, b_mod.reshape(depth, 1, 3 * D_MODEL))


def _inproj_kernel(*refs, rope, ctx_row):
    if rope:
        (x_ref, shift_ref, scale_ref, g_ref, w_ref, cos_ref, sup_ref, sdn_ref,
         q_ref, k_ref, v_ref, sga_ref, uf_ref, sgf_ref, t_ref, bg_ref) = refs
    else:
        (x_ref, shift_ref, scale_ref, g_ref, w_ref,
         q_ref, k_ref, v_ref, sga_ref, uf_ref, sgf_ref, t_ref, bg_ref) = refs
    row = pl.program_id(0) if ctx_row is None else ctx_row
    shift = shift_ref[0, pl.ds(row, 1), :]
    scale = scale_ref[0, pl.ds(row, 1), :]
    x = x_ref[0]
    r = lax.rsqrt(jnp.mean(x * x, axis=-1, keepdims=True) + NORM_EPS)
    h = ((x * r) * (g_ref[0] * (1.0 + scale)) + shift).astype(BF16)

    def proj(c0, width):
        return _dot(h, w_ref[0, :, c0:c0 + width])

    def rotate(y):
        if not rope:
            return y
        return (y * cos_ref[...] + pltpu.roll(y, LANES - ROPE_FREQS, 1) * sup_ref[...]
                + pltpu.roll(y, ROPE_FREQS, 1) * sdn_ref[...])

    def with_swapped_heads(y):
        return jnp.concatenate([y, pltpu.roll(y, HEAD_DIM, 1)], axis=1).astype(BF16)

    q = proj(C_Q, ATTN_WIDTH)
    for p in range(HEAD_PAIRS):
        sl = slice(p * LANES, (p + 1) * LANES)
        q_ref[0, :, sl] = (rotate(q[:, sl]) * (HEAD_DIM ** -0.5 * LOG2E)).astype(BF16)
    kv = proj(C_K, 2 * KV_WIDTH)
    k_ref[0] = with_swapped_heads(rotate(kv[:, :KV_WIDTH]))
    v_ref[0] = with_swapped_heads(kv[:, KV_WIDTH:])
    sga_ref[0] = _silu(proj(C_GA, ATTN_WIDTH)).astype(BF16)
    uf_ref[0] = proj(C_UF, FOURIER_WIDTH).astype(BF16)
    sgf_ref[0] = _silu(proj(C_GF, FOURIER_WIDTH)).astype(BF16)
    t_ref[0] = (proj(C_CC, CONV_WIDTH) * proj(C_ZC, CONV_WIDTH)).astype(BF16)
    bg_ref[0] = (proj(C_BC, CONV_WIDTH) * _silu(proj(C_GC, CONV_WIDTH))).astype(BF16)


def _in_projection(x, mod, g_pre, w_in, layer, *, rope_tabs, ctx_row, tm):
    b, n, _ = x.shape
    rope = rope_tabs is not None
    row3 = lambda width: pl.BlockSpec((1, tm, width), lambda bi, i: (bi, i, 0))
    in_specs = [row3(D_MODEL),
                pl.BlockSpec((1, MOD_ROWS, D_MODEL), lambda bi, i: (layer, 0, 0)),
                pl.BlockSpec((1, MOD_ROWS, D_MODEL), lambda bi, i: (layer, 0, 1)),
                pl.BlockSpec((1, 1, D_MODEL), lambda bi, i: (layer, 0, 0)),
                pl.BlockSpec((1, D_MODEL, PROJ_WIDTH), lambda bi, i: (layer, 0, 0))]
    args = [x, mod, mod, g_pre, w_in]
    if rope:
        in_specs += [pl.BlockSpec((tm, LANES), lambda bi, i: (i, 0))] * 3
        args += list(rope_tabs)
    widths = (ATTN_WIDTH, 2 * KV_WIDTH, 2 * KV_WIDTH, ATTN_WIDTH, FOURIER_WIDTH, FOURIER_WIDTH,
              CONV_WIDTH, CONV_WIDTH)
    return pl.pallas_call(
        functools.partial(_inproj_kernel, rope=rope, ctx_row=ctx_row),
        grid=(b, n // tm),
        in_specs=in_specs,
        out_specs=[row3(w) for w in widths],
        out_shape=[jax.ShapeDtypeStruct((b, n, w), BF16) for w in widths],
        compiler_params=_cparams("arbitrary", "arbitrary"),
        name="in_projection_rope" if rope else "in_projection_ctx",
    )(*args)


def _lane_half_variants(blk, fill):
    straight, swapped = blk[:, :KV_WIDTH], blk[:, KV_WIDTH:]
    lo = lax.broadcasted_iota(jnp.int32, straight.shape, 1) < HEAD_DIM
    other = jnp.full_like(straight, fill)
    return (jnp.where(lo, straight, other), jnp.where(lo, other, swapped),
            jnp.where(lo, swapped, other), jnp.where(lo, other, straight))


def _merge_head_pair(pv_lo, pv_hi, sink_lo, sink_hi):
    lane_lo = lax.broadcasted_iota(jnp.int32, pv_lo.shape, 1) < HEAD_DIM
    num = jnp.where(lane_lo, pv_lo, pv_hi)
    den = pltpu.roll(jnp.where(lane_lo, pv_hi, pv_lo), HEAD_DIM, 1) + jnp.where(lane_lo, sink_lo, sink_hi)
    return num / den


def _mix_rows(a, f_parts, t, above, below, bg, cw, cb, w, g_post, gate, x):
    n_rows = t.shape[0]
    ridx = lax.broadcasted_iota(jnp.int32, t.shape, 0)
    up = jnp.where(ridx == 0, above, pltpu.roll(t, 1, 0))
    dn = jnp.where(ridx == n_rows - 1, below, pltpu.roll(t, n_rows - 1, 0))
    conv = up * cw[0:1] + t * cw[1:2] + dn * cw[2:3] + cb
    h = jnp.concatenate([a] + f_parts + [(conv * bg.astype(F32)).astype(BF16)], axis=1)
    y = _dot(h, w)
    r = lax.rsqrt(jnp.mean(y * y, axis=-1, keepdims=True) + NORM_EPS)
    return x + (y * r) * (gate * g_post)


def _attn_kernel(sink_ref, q_ref, g_ref, kp_ref, km_ref, kn_ref, vp_ref, vm_ref, vn_ref, kc_ref, vc_ref,
                 x_ref, f_ref, t_ref, tp_ref, tn_ref, bg_ref, cw_ref, cb_ref, w_ref, gpost_ref, mgate_ref,
                 o_ref, k_s, v_s, kctx_s, vctx_s, bias_s, s_s, p_s, r_s, a_s, *, tq, n_seq, n_ctx):
    i = pl.program_id(1)
    n_blocks = tq // BLOCK_Q
    for off, kref, vref, rows in ((0, kp_ref, vp_ref, WINDOW), (WINDOW, km_ref, vm_ref, tq),
                                  (WINDOW + tq, kn_ref, vn_ref, WINDOW)):
        for idx, kk in enumerate(_lane_half_variants(kref[0], 0.0)):
            k_s[idx, off:off + rows] = kk
        for idx, vv in enumerate(_lane_half_variants(vref[0], 1.0)):
            v_s[idx, off:off + rows] = vv
    for idx, kk in enumerate(_lane_half_variants(kc_ref[0], 0.0)):
        kctx_s[idx] = kk
    for idx, vv in enumerate(_lane_half_variants(vc_ref[0], 1.0)):
        vctx_s[idx] = vv

    ii = lax.broadcasted_iota(jnp.int32, (BLOCK_Q, WINDOW), 0)
    jj = lax.broadcasted_iota(jnp.int32, (BLOCK_Q, WINDOW), 1)
    head_band = jnp.where(jj >= ii, 0.0, -jnp.inf)
    tail_band = jnp.where(jj <= ii, 0.0, -jnp.inf)
    bias_s[0] = head_band
    bias_s[1] = tail_band
    bias_s[2] = jnp.where(i == 0, -jnp.inf, head_band)
    bias_s[3] = jnp.where(i == pl.num_programs(1) - 1, -jnp.inf, tail_band)

    head_w = n_ctx + KEY_SPAN

    def scores(sb):
        par, r0 = sb % 2, sb * BLOCK_Q
        q = q_ref[0, r0:r0 + BLOCK_Q, :]
        for p in range(HEAD_PAIRS):
            kv = p // PAIRS_PER_KV
            keys = jnp.concatenate([kctx_s[2 * kv], k_s[2 * kv, r0:r0 + KEY_SPAN],
                                    kctx_s[2 * kv + 1], k_s[2 * kv + 1, r0:r0 + KEY_SPAN]], axis=0)
            s_s[par, p] = _dot_nt(q[:, p * LANES:(p + 1) * LANES], keys)

    def softmax(sb):
        par = sb % 2
        head_bias = 2 if sb == 0 else 0
        tail_bias = 3 if sb == n_blocks - 1 else 1
        n_cols = head_w // LANES
        for p in range(HEAD_PAIRS):
            for half in range(2):
                h = 2 * p + half
                sink2 = sink_ref[h] * LOG2E
                for rs in range(BLOCK_Q // SOFTMAX_ROWS):
                    rows = slice(rs * SOFTMAX_ROWS, (rs + 1) * SOFTMAX_ROWS)
                    cols = [s_s[par, p, rows, half * head_w + t * LANES:half * head_w + (t + 1) * LANES]
                            for t in range(n_cols)]
                    first_local = n_ctx // LANES
                    cols[first_local] = cols[first_local] + bias_s[head_bias, rows]
                    cols[-1] = cols[-1] + bias_s[tail_bias, rows]
                    m = jnp.maximum(jnp.max(functools.reduce(jnp.maximum, cols), axis=-1, keepdims=True),
                                    sink2)
                    for t, col in enumerate(cols):
                        p_s[par, p, rows, half * head_w + t * LANES:half * head_w + (t + 1) * LANES] = (
                            jnp.exp2(col - m).astype(BF16))
                    r_s[par, h, rows] = jnp.broadcast_to(jnp.exp2(sink2 - m), (SOFTMAX_ROWS, LANES))

    def weighted_values(sb):
        par, r0 = sb % 2, sb * BLOCK_Q
        for p in range(HEAD_PAIRS):
            kv = p // PAIRS_PER_KV
            pv = []
            for half in range(2):
                vals = jnp.concatenate([vctx_s[2 * kv + half], v_s[2 * kv + half, r0:r0 + KEY_SPAN]], axis=0)
                pv.append(_dot(p_s[par, p, :, half * head_w:(half + 1) * head_w], vals))
            out = _merge_head_pair(pv[0], pv[1], r_s[par, 2 * p], r_s[par, 2 * p + 1])
            gate = g_ref[0, r0:r0 + BLOCK_Q, p * LANES:(p + 1) * LANES].astype(F32)
            a_s[r0:r0 + BLOCK_Q, p * LANES:(p + 1) * LANES] = (out * gate).astype(BF16)

    def conv_edge(ref, row, keep):
        return jnp.where(keep, ref[0].astype(F32)[row:row + 1, :], 0.0)

    def mix(r0):
        rows = slice(r0, r0 + MIX_ROWS)
        if r0 == 0:
            above = conv_edge(tp_ref, BF16_ROWS - 1, i > 0)
        else:
            above = t_ref[0, r0 - BF16_ROWS:r0, :].astype(F32)[BF16_ROWS - 1:, :]
        if r0 + MIX_ROWS == tq:
            below = conv_edge(tn_ref, 0, i < pl.num_programs(1) - 1)
        else:
            below = t_ref[0, r0 + MIX_ROWS:r0 + MIX_ROWS + BF16_ROWS, :].astype(F32)[:1, :]
        o_ref[0, rows] = _mix_rows(
            a_s[rows], [f_ref[0, s, rows] for s in range(FOURIER_WIDTH // LANES)],
            t_ref[0, rows].astype(F32), above, below, bg_ref[0, rows], cw_ref[0], cb_ref[0],
            w_ref[0], gpost_ref[0], mgate_ref[0, pl.ds(pl.program_id(0), 1), :], x_ref[0, rows])

    per_mix = MIX_ROWS // BLOCK_Q
    scores(0)
    for sb in range(n_blocks):
        if sb + 1 < n_blocks:
            scores(sb + 1)
        softmax(sb)
        weighted_values(sb)
        if (sb + 1) % per_mix == 0:
            mix((sb + 1 - per_mix) * BLOCK_Q)


def _attention_and_mix(x, q, gate, k, v, kc, vc, sink, f, t, bg, conv_w, conv_b, w_out, g_post, mod,
                       layer, *, tq):
    b, n, _ = q.shape
    nc = kc.shape[1]
    per = tq // WINDOW
    last = n // WINDOW - 1
    kvw = 2 * KV_WIDTH
    variants = 2 * KV_HEADS
    main = lambda width: pl.BlockSpec((1, tq, width), lambda bi, i: (bi, i, 0))
    prev = pl.BlockSpec((1, WINDOW, kvw), lambda bi, i: (bi, jnp.maximum(i * per - 1, 0), 0))
    nxt = pl.BlockSpec((1, WINDOW, kvw), lambda bi, i: (bi, jnp.minimum((i + 1) * per, last), 0))
    ctx = pl.BlockSpec((1, nc, kvw), lambda bi, i: (bi, 0, 0))
    lay3 = lambda shape: pl.BlockSpec((1,) + shape, lambda bi, i: (layer, 0, 0))
    t_per = tq // BF16_ROWS
    t_last = n // BF16_ROWS - 1
    t_prev = pl.BlockSpec((1, BF16_ROWS, CONV_WIDTH), lambda bi, i: (bi, jnp.maximum(i * t_per - 1, 0), 0))
    t_next = pl.BlockSpec((1, BF16_ROWS, CONV_WIDTH),
                          lambda bi, i: (bi, jnp.minimum((i + 1) * t_per, t_last), 0))
    span = tq + 2 * WINDOW
    s_cols = 2 * (nc + KEY_SPAN)
    return pl.pallas_call(
        functools.partial(_attn_kernel, tq=tq, n_seq=n, n_ctx=nc),
        grid=(b, n // tq),
        in_specs=[pl.BlockSpec(memory_space=pltpu.SMEM), main(ATTN_WIDTH), main(ATTN_WIDTH),
                  prev, main(kvw), nxt, prev, main(kvw), nxt, ctx, ctx,
                  main(D_MODEL),
                  pl.BlockSpec((1, FOURIER_WIDTH // LANES, tq, LANES), lambda bi, i: (bi, 0, i, 0)),
                  main(CONV_WIDTH), t_prev, t_next, main(CONV_WIDTH),
                  lay3((3, CONV_WIDTH)), lay3((1, CONV_WIDTH)), lay3((MIX_WIDTH, D_MODEL)),
                  lay3((1, D_MODEL)),
                  pl.BlockSpec((1, MOD_ROWS, D_MODEL), lambda bi, i: (layer, 0, 2))],
        out_specs=main(D_MODEL),
        out_shape=jax.ShapeDtypeStruct(x.shape, F32),
        scratch_shapes=[pltpu.VMEM((variants, span, KV_WIDTH), BF16),
                        pltpu.VMEM((variants, span, KV_WIDTH), BF16),
                        pltpu.VMEM((variants, nc, KV_WIDTH), BF16),
                        pltpu.VMEM((variants, nc, KV_WIDTH), BF16),
                        pltpu.VMEM((4, BLOCK_Q, WINDOW), F32),
                        pltpu.VMEM((2, HEAD_PAIRS, BLOCK_Q, s_cols), F32),
                        pltpu.VMEM((2, HEAD_PAIRS, BLOCK_Q, s_cols), BF16),
                        pltpu.VMEM((2, ATTN_HEADS, BLOCK_Q, LANES), F32),
                        pltpu.VMEM((tq, ATTN_WIDTH), BF16)],
        compiler_params=_cparams("arbitrary", "arbitrary"),
        name="attention_and_mix",
    )(sink, q, gate, k, k, k, v, v, v, kc, vc, x, f, t, t, t, bg, conv_w, conv_b, w_out, g_post, mod)


def _ctx_attn_kernel(sink_ref, q_ref, g_ref, kc_ref, vc_ref, o_ref):
    keys = _lane_half_variants(kc_ref[0], 0.0)
    vals = _lane_half_variants(vc_ref[0], 1.0)
    for p in range(HEAD_PAIRS):
        kv = p // PAIRS_PER_KV
        qp = q_ref[0, :, p * LANES:(p + 1) * LANES]
        pv, sink_terms = [], []
        for half in range(2):
            s = _dot_nt(qp, keys[2 * kv + half])
            sink2 = sink_ref[2 * p + half] * LOG2E
            m = jnp.maximum(jnp.max(s, axis=-1, keepdims=True), sink2)
            pv.append(_dot(jnp.exp2(s - m).astype(BF16), vals[2 * kv + half]))
            sink_terms.append(jnp.exp2(sink2 - m))
        out = _merge_head_pair(pv[0], pv[1], sink_terms[0], sink_terms[1])
        gate = g_ref[0, :, p * LANES:(p + 1) * LANES].astype(F32)
        o_ref[0, :, p * LANES:(p + 1) * LANES] = (out * gate).astype(BF16)


def _context_attention(q, gate, kc, vc, sink):
    b, nc, _ = q.shape
    blk = lambda width: pl.BlockSpec((1, nc, width), lambda bi: (bi, 0, 0))
    return pl.pallas_call(
        _ctx_attn_kernel,
        grid=(b,),
        in_specs=[pl.BlockSpec(memory_space=pltpu.SMEM), blk(ATTN_WIDTH), blk(ATTN_WIDTH),
                  blk(2 * KV_WIDTH), blk(2 * KV_WIDTH)],
        out_specs=blk(ATTN_WIDTH),
        out_shape=jax.ShapeDtypeStruct((b, nc, ATTN_WIDTH), BF16),
        compiler_params=_cparams("arbitrary"),
        name="context_attention",
    )(sink, q, gate, kc, vc)


def _channel_mix_matrix(cc_ref, sc_ref, wf_ref, scale):
    wf = wf_ref[0]
    return (jnp.concatenate([_dot(cc_ref[...], wf), -_dot(sc_ref[...], wf)], axis=1) * scale).astype(BF16)


def _fourier_kernel(uf_ref, gate_ref, cc_ref, sc_ref, wf_ref, g_ref, m2_ref, o_ref, z_s, y_s, *, n_seq):
    n1_len = n_seq // FFT_N2
    mix = _channel_mix_matrix(cc_ref, sc_ref, wf_ref, (n_seq * FOURIER_GROUP_DIM) ** -0.5)

    per_chunk = Z_CHUNK // FFT_N2
    for c in range(n_seq // Z_CHUNK):
        z = _dot(uf_ref[0, c * Z_CHUNK:(c + 1) * Z_CHUNK, :], mix)
        for j in range(per_chunk):
            for part in range(2):
                z_s[part, pl.ds(c * per_chunk + j, FFT_N2, stride=Z_PITCH), :] = (
                    z[j * FFT_N2:(j + 1) * FFT_N2, part * LANES:(part + 1) * LANES])

    def stage1(t, carry):
        for u in range(STAGE1_UNROLL):
            n2 = t * STAGE1_UNROLL + u
            z0 = pl.multiple_of(n2 * Z_PITCH, SUBLANES)
            rhs = jnp.concatenate([z_s[part, pl.ds(z0, n1_len), :] for part in range(2)],
                                  axis=0).astype(BF16)
            y_s[n2] = _dot(g_ref[n2], rhs).reshape(2, n1_len, LANES)
        return carry

    lax.fori_loop(0, FFT_N2 // STAGE1_UNROLL, stage1, 0)

    def stage2(t, carry):
        r0 = pl.multiple_of(t * BF16_ROWS, BF16_ROWS)
        outs = []
        for u in range(BF16_ROWS // K1_GROUP):
            blk = y_s[:, :, pl.ds(r0 + u * K1_GROUP, K1_GROUP), :]
            rhs = blk.reshape(FFT_N2 * 2 * K1_GROUP, LANES).astype(BF16)
            outs.append(_dot(m2_ref[...], rhs).reshape(FFT_N2, K1_GROUP, LANES))
        gate = gate_ref[0, :, pl.ds(r0, BF16_ROWS), :].astype(F32)
        o_ref[0, 0, :, pl.ds(r0, BF16_ROWS), :] = (jnp.concatenate(outs, axis=1) * gate).astype(BF16)
        return carry

    lax.fori_loop(0, n1_len // BF16_ROWS, stage2, 0)


def _fourier_mix(uf, gate, wf_half, consts):
    b, n, _ = uf.shape
    halves = FOURIER_WIDTH // LANES
    n1_len = n // FFT_N2
    cc, sc, g, m2 = consts
    full = lambda shape: pl.BlockSpec(shape, lambda bi, hf: (0,) * len(shape))
    out = pl.pallas_call(
        functools.partial(_fourier_kernel, n_seq=n),
        grid=(b, halves),
        in_specs=[pl.BlockSpec((1, n, LANES), lambda bi, hf: (bi, 0, hf)),
                  pl.BlockSpec((1, FFT_N2, n1_len, LANES), lambda bi, hf: (bi, 0, 0, hf)),
                  full((LANES, LANES)), full((LANES, LANES)),
                  pl.BlockSpec((1, LANES, LANES), lambda bi, hf: (hf, 0, 0)),
                  full(g.shape), full(m2.shape)],
        out_specs=pl.BlockSpec((1, 1, FFT_N2, n1_len, LANES), lambda bi, hf: (bi, hf, 0, 0, 0)),
        out_shape=jax.ShapeDtypeStruct((b, halves, FFT_N2, n1_len, LANES), BF16),
        scratch_shapes=[pltpu.VMEM((2, FFT_N2 * Z_PITCH, LANES), F32),
                        pltpu.VMEM((FFT_N2, 2, n1_len, LANES), F32)],
        compiler_params=_cparams("arbitrary", "arbitrary"),
        name="fourier_mix",
    )(uf, gate.reshape(b, FFT_N2, n1_len, FOURIER_WIDTH), cc, sc, wf_half, g, m2)
    return out.reshape(b, halves, n, LANES)


def _ctx_fourier_kernel(uf_ref, gate_ref, cc_ref, sc_ref, wf_ref, dft_ref, o_ref, *, n_seq):
    mix = _channel_mix_matrix(cc_ref, sc_ref, wf_ref, (n_seq * FOURIER_GROUP_DIM) ** -0.5)
    z = _dot(uf_ref[0], mix)
    rhs = jnp.concatenate([z[:, :LANES], z[:, LANES:]], axis=0).astype(BF16)
    o_ref[0, 0] = (_dot(dft_ref[...], rhs) * gate_ref[0].astype(F32)).astype(BF16)


def _ctx_fourier_mix(uf, gate, wf_half, cc, sc, dft):
    b, n, _ = uf.shape
    halves = FOURIER_WIDTH // LANES
    full = lambda shape: pl.BlockSpec(shape, lambda bi, hf: (0,) * len(shape))
    half = pl.BlockSpec((1, n, LANES), lambda bi, hf: (bi, 0, hf))
    return pl.pallas_call(
        functools.partial(_ctx_fourier_kernel, n_seq=n),
        grid=(b, halves),
        in_specs=[half, half, full((LANES, LANES)), full((LANES, LANES)),
                  pl.BlockSpec((1, LANES, LANES), lambda bi, hf: (hf, 0, 0)), full(dft.shape)],
        out_specs=pl.BlockSpec((1, 1, n, LANES), lambda bi, hf: (bi, hf, 0, 0)),
        out_shape=jax.ShapeDtypeStruct((b, halves, n, LANES), BF16),
        compiler_params=_cparams("arbitrary", "arbitrary"),
        name="context_fourier_mix",
    )(uf, gate, cc, sc, wf_half, dft)


def _ctx_mix_kernel(x_ref, a_ref, f_ref, t_ref, bg_ref, cw_ref, cb_ref, w_ref, g_ref, gate_ref, o_ref,
                    *, ctx_row):
    edge = jnp.zeros((1, CONV_WIDTH), F32)
    o_ref[0] = _mix_rows(a_ref[0], [f_ref[0, s] for s in range(FOURIER_WIDTH // LANES)],
                         t_ref[0].astype(F32), edge, edge, bg_ref[0], cw_ref[0], cb_ref[0], w_ref[0],
                         g_ref[0], gate_ref[0, ctx_row:ctx_row + 1, :], x_ref[0])


def _context_mix(x, a, f, t, bg, conv_w, conv_b, w_out, g_post, mod, layer, *, ctx_row):
    b, n, _ = x.shape
    row3 = lambda width: pl.BlockSpec((1, n, width), lambda bi: (bi, 0, 0))
    lay3 = lambda shape: pl.BlockSpec((1,) + shape, lambda bi: (layer, 0, 0))
    slabs = FOURIER_WIDTH // LANES
    return pl.pallas_call(
        functools.partial(_ctx_mix_kernel, ctx_row=ctx_row),
        grid=(b,),
        in_specs=[row3(D_MODEL), row3(ATTN_WIDTH),
                  pl.BlockSpec((1, slabs, n, LANES), lambda bi: (bi, 0, 0, 0)),
                  row3(CONV_WIDTH), row3(CONV_WIDTH), lay3((3, CONV_WIDTH)), lay3((1, CONV_WIDTH)),
                  lay3((MIX_WIDTH, D_MODEL)), lay3((1, D_MODEL)),
                  pl.BlockSpec((1, MOD_ROWS, D_MODEL), lambda bi: (layer, 0, 2))],
        out_specs=row3(D_MODEL),
        out_shape=jax.ShapeDtypeStruct(x.shape, F32),
        compiler_params=_cparams("arbitrary"),
        name="context_mix",
    )(x, a, f, t, bg, conv_w, conv_b, w_out, g_post, mod)


def kernel(x, c, ctx, c_ctx, w_mod, b_mod, g_pre, g_post, w_in, w_out, sink, w_fourier, conv_w, conv_b):
    depth = w_mod.shape[0]
    b, n, _ = x.shape
    nc = ctx.shape[1]
    assert b + 1 <= MOD_ROWS and n % GRID_W == 0 and n % ATTN_ROWS == 0 and n % Z_CHUNK == 0
    assert (n // FFT_N2) % BF16_ROWS == 0 and FFT_N2 % STAGE1_UNROLL == 0 and n // FFT_N2 <= Z_PITCH

    w_in_b = w_in.astype(BF16)
    w_out_b = w_out.astype(BF16)

    rope_tabs = tuple(jnp.asarray(t) for t in _rope_tables(n))
    cc, sc = (jnp.asarray(m) for m in _channel_dft())
    stage1 = jnp.asarray(_stage1_mats(n)).astype(BF16)
    stage2 = jnp.asarray(_stage2_mat()).astype(BF16)
    ctx_dft = jnp.asarray(_dense_dft(nc)).astype(BF16)
    groups_per_half = LANES // FOURIER_GROUP_DIM
    wf_half = jnp.zeros((depth, FOURIER_GROUPS // groups_per_half, LANES, LANES), F32)
    for g in range(FOURIER_GROUPS):
        o = (g % groups_per_half) * FOURIER_GROUP_DIM
        wf_half = wf_half.at[:, g // groups_per_half, o:o + FOURIER_GROUP_DIM,
                             o:o + FOURIER_GROUP_DIM].set(w_fourier[:, g])

    c_rows = jnp.zeros((MOD_ROWS, D_MODEL), F32).at[:b].set(c).at[b].set(c_ctx)
    mod = _modulation(c_rows, w_mod, b_mod)
    g_pre3 = g_pre.reshape(depth, 1, D_MODEL)
    g_post3 = g_post.reshape(depth, 1, D_MODEL)
    conv_b3 = conv_b.reshape(depth, 1, CONV_WIDTH)

    for l in range(depth):
        qc, kc, vc, sgac, ufc, sgfc, tc, bgc = _in_projection(
            ctx, mod, g_pre3, w_in_b, l, rope_tabs=None, ctx_row=b, tm=nc)
        q, k, v, sga, uf, sgf, t, bg = _in_projection(
            x, mod, g_pre3, w_in_b, l, rope_tabs=rope_tabs, ctx_row=None, tm=TILE_ROWS)
        f = _fourier_mix(uf, sgf, wf_half[l], (cc, sc, stage1, stage2))
        x = _attention_and_mix(x, q, sga, k, v, kc, vc, sink[l], f, t, bg, conv_w, conv_b3, w_out_b,
                               g_post3, mod, l, tq=ATTN_ROWS)
        if l < depth - 1:
            ac = _context_attention(qc, sgac, kc, vc, sink[l])
            fc = _ctx_fourier_mix(ufc, sgfc, wf_half[l], cc, sc, ctx_dft)
            ctx = _context_mix(ctx, ac, fc, tc, bgc, conv_w, conv_b3, w_out_b, g_post3, mod, l,
                               ctx_row=b)
    return x
```

```python
import functools
import math

import numpy as np
import jax
import jax.numpy as jnp
from jax import lax
from jax.experimental import pallas as pl
from jax.experimental.pallas import tpu as pltpu

D_MODEL = 1024
GRID_W = 64
HEAD_DIM = 64
ATTN_HEADS = 8
KV_HEADS = 2
Q_PER_KV = ATTN_HEADS // KV_HEADS
ATTN_WIDTH = ATTN_HEADS * HEAD_DIM
KV_WIDTH = KV_HEADS * HEAD_DIM
WINDOW = 128
FOURIER_GROUPS = 4
FOURIER_GROUP_DIM = 64
FOURIER_WIDTH = FOURIER_GROUPS * FOURIER_GROUP_DIM
CONV_WIDTH = 256
MIX_WIDTH = ATTN_WIDTH + FOURIER_WIDTH + CONV_WIDTH
PROJ_WIDTH = 2 * ATTN_WIDTH + 2 * KV_WIDTH + 2 * FOURIER_WIDTH + 4 * CONV_WIDTH
ROPE_FREQS = HEAD_DIM // 4
ROPE_BASE = 10000.0
NORM_EPS = 1e-6
LOG2E = math.log2(math.e)

C_Q = 0
C_K = C_Q + ATTN_WIDTH
C_V = C_K + KV_WIDTH
C_GA = C_V + KV_WIDTH
C_UF = C_GA + ATTN_WIDTH
C_GF = C_UF + FOURIER_WIDTH
C_ZC = C_GF + FOURIER_WIDTH
C_BC = C_ZC + CONV_WIDTH
C_CC = C_BC + CONV_WIDTH
C_GC = C_CC + CONV_WIDTH

LANES = 128
SUBLANES = 8
BF16_ROWS = 16
VMEM_LIMIT = 52 * 1024 * 1024

MOD_ROWS = 8
BF16 = jnp.bfloat16
F32 = jnp.float32

FFT_N2 = 64
K1_GROUP = SUBLANES
STAGE1_UNROLL = 8
Z_CHUNK = 1024
Z_PITCH = 136
SOFTMAX_ROWS = 32

BLOCK_Q = 128
KEY_SPAN = BLOCK_Q + 2 * WINDOW
HEAD_PAIRS = ATTN_WIDTH // LANES
PAIRS_PER_KV = HEAD_PAIRS // KV_HEADS

TILE_ROWS = 1024
PROJ_SUB_ROWS = 512
ATTN_ROWS = 512
MIX_ROWS = 256


def _silu(x):
    return x / (1.0 + jnp.exp(-x))


def _dot(a, b):
    return jnp.dot(a, b, preferred_element_type=F32)


def _dot_nt(a, b):
    return lax.dot_general(a, b, (((1,), (1,)), ((), ())), preferred_element_type=F32)


def _cparams(*sem):
    return pltpu.CompilerParams(dimension_semantics=sem, vmem_limit_bytes=VMEM_LIMIT)


def _rope_tables(n):
    t = np.arange(n)
    row = (t // GRID_W).astype(np.float64)
    col = (t % GRID_W).astype(np.float64)
    inv = ROPE_BASE ** (-np.arange(ROPE_FREQS, dtype=np.float64) / ROPE_FREQS)
    ar = row[:, None] * inv
    ac = col[:, None] * inv
    z = np.zeros_like(ar)
    cos_h = np.concatenate([np.cos(ar), np.cos(ar), np.cos(ac), np.cos(ac)], axis=1)
    sin_up = np.concatenate([-np.sin(ar), z, -np.sin(ac), z], axis=1)
    sin_dn = np.concatenate([z, np.sin(ar), z, np.sin(ac)], axis=1)
    rep = LANES // HEAD_DIM
    return tuple(np.tile(a, (1, rep)).astype(np.float32) for a in (cos_h, sin_up, sin_dn))


def _channel_dft():
    c = np.arange(FOURIER_GROUP_DIM)
    ang = 2.0 * np.pi * np.outer(c, c) / FOURIER_GROUP_DIM
    eye = np.eye(LANES // FOURIER_GROUP_DIM)
    return (np.kron(eye, np.cos(ang)).astype(np.float32),
            np.kron(eye, np.sin(ang)).astype(np.float32))


def _stage1_mats(n):
    n1_len = n // FFT_N2
    k1 = np.arange(n1_len)[:, None]
    n1 = np.arange(n1_len)[None, :]
    out = np.empty((FFT_N2, 2 * n1_len, 2 * n1_len), np.float32)
    for n2 in range(FFT_N2):
        ang = 2.0 * np.pi * ((k1 * (FFT_N2 * n1 + n2)) % n) / n
        ce, se = np.cos(ang), np.sin(ang)
        out[n2] = np.block([[ce, se], [-se, ce]])
    return out


def _stage2_mat():
    k2 = np.arange(FFT_N2)
    ang = 2.0 * np.pi * np.outer(k2, k2) / FFT_N2
    cs = np.stack([np.cos(ang), np.sin(ang)], axis=-1)
    eye = np.eye(K1_GROUP)
    m = np.einsum('knp,rs->krnps', cs, eye)
    return m.reshape(FFT_N2 * K1_GROUP, FFT_N2 * 2 * K1_GROUP).astype(np.float32)


def _dense_dft(n):
    t = np.arange(n)
    ang = 2.0 * np.pi * (np.outer(t, t) % n) / n
    return np.concatenate([np.cos(ang), np.sin(ang)], axis=1).astype(np.float32)


def _mod_kernel(c_ref, w_ref, b_ref, o_ref):
    o_ref[0] = _dot(_silu(c_ref[...]), w_ref[0]) + b_ref[0]


def _modulation(c_rows, w_mod, b_mod):
    depth = w_mod.shape[0]
    return pl.pallas_call(
        _mod_kernel,
        grid=(depth, 3),
        in_specs=[pl.BlockSpec((MOD_ROWS, D_MODEL), lambda l, j: (0, 0)),
                  pl.BlockSpec((1, D_MODEL, D_MODEL), lambda l, j: (l, 0, j)),
                  pl.BlockSpec((1, 1, D_MODEL), lambda l, j: (l, 0, j))],
        out_specs=pl.BlockSpec((1, MOD_ROWS, D_MODEL), lambda l, j: (l, 0, j)),
        out_shape=jax.ShapeDtypeStruct((depth, MOD_ROWS, 3 * D_MODEL), F32),
        compiler_params=_cparams("arbitrary", "arbitrary"),
        name="modulation",
    )(c_rows, w_mod, b_mod.reshape(depth, 1, 3 * D_MODEL))


def _inproj_kernel(*refs, rope, ctx_row, kv_only, tm):
    x_ref, shift_ref, scale_ref, g_ref, w_ref = refs[:5]
    cos_ref, sup_ref, sdn_ref = refs[5:8] if rope else (None,) * 3
    outs = refs[8:] if rope else refs[5:]
    row = pl.program_id(0) if ctx_row is None else ctx_row
    shift = shift_ref[0, pl.ds(row, 1), :]
    gain = g_ref[0] * (1.0 + scale_ref[0, pl.ds(row, 1), :])

    def with_swapped_heads(y):
        return jnp.concatenate([y, pltpu.roll(y, HEAD_DIM, 1)], axis=1).astype(BF16)

    sub = min(tm, PROJ_SUB_ROWS)
    for s in range(tm // sub):
        rows = slice(s * sub, (s + 1) * sub)
        x = x_ref[0, rows]
        r = lax.rsqrt(jnp.mean(x * x, axis=-1, keepdims=True) + NORM_EPS)
        h = ((x * r) * gain + shift).astype(BF16)

        def proj(c0, width):
            return _dot(h, w_ref[0, :, c0:c0 + width])

        def rotate(y):
            if not rope:
                return y
            return (y * cos_ref[rows] + pltpu.roll(y, LANES - ROPE_FREQS, 1) * sup_ref[rows]
                    + pltpu.roll(y, ROPE_FREQS, 1) * sdn_ref[rows])

        kv = proj(C_K, 2 * KV_WIDTH)
        k_out = with_swapped_heads(rotate(kv[:, :KV_WIDTH]))
        v_out = with_swapped_heads(kv[:, KV_WIDTH:])
        if kv_only:
            k_ref, v_ref = outs
            k_ref[0, rows] = k_out
            v_ref[0, rows] = v_out
            continue
        q_ref, k_ref, v_ref, sga_ref, uf_ref, sgf_ref, t_ref, bg_ref = outs
        k_ref[0, rows] = k_out
        v_ref[0, rows] = v_out
        q = proj(C_Q, ATTN_WIDTH)
        for p in range(HEAD_PAIRS):
            sl = slice(p * LANES, (p + 1) * LANES)
            q_ref[0, rows, sl] = (rotate(q[:, sl]) * (HEAD_DIM ** -0.5 * LOG2E)).astype(BF16)
        sga_ref[0, rows] = _silu(proj(C_GA, ATTN_WIDTH)).astype(BF16)
        uf_ref[0, rows] = proj(C_UF, FOURIER_WIDTH).astype(BF16)
        sgf_ref[0, rows] = _silu(proj(C_GF, FOURIER_WIDTH)).astype(BF16)
        t_ref[0, rows] = (proj(C_CC, CONV_WIDTH) * proj(C_ZC, CONV_WIDTH)).astype(BF16)
        bg_ref[0, rows] = (proj(C_BC, CONV_WIDTH) * _silu(proj(C_GC, CONV_WIDTH))).astype(BF16)


def _in_projection(x, mod, g_pre, w_in, layer, *, rope_tabs, ctx_row, tm, kv_only=False):
    b, n, _ = x.shape
    rope = rope_tabs is not None
    row3 = lambda width: pl.BlockSpec((1, tm, width), lambda bi, i: (bi, i, 0))
    in_specs = [row3(D_MODEL),
                pl.BlockSpec((1, MOD_ROWS, D_MODEL), lambda bi, i: (layer, 0, 0)),
                pl.BlockSpec((1, MOD_ROWS, D_MODEL), lambda bi, i: (layer, 0, 1)),
                pl.BlockSpec((1, 1, D_MODEL), lambda bi, i: (layer, 0, 0)),
                pl.BlockSpec((1, D_MODEL, PROJ_WIDTH), lambda bi, i: (layer, 0, 0))]
    args = [x, mod, mod, g_pre, w_in]
    if rope:
        in_specs += [pl.BlockSpec((tm, LANES), lambda bi, i: (i, 0))] * 3
        args += list(rope_tabs)
    widths = (ATTN_WIDTH, 2 * KV_WIDTH, 2 * KV_WIDTH, ATTN_WIDTH, FOURIER_WIDTH, FOURIER_WIDTH,
              CONV_WIDTH, CONV_WIDTH)
    if kv_only:
        widths = widths[1:3]
    return pl.pallas_call(
        functools.partial(_inproj_kernel, rope=rope, ctx_row=ctx_row, kv_only=kv_only, tm=tm),
        grid=(b, n // tm),
        in_specs=in_specs,
        out_specs=[row3(w) for w in widths],
        out_shape=[jax.ShapeDtypeStruct((b, n, w), BF16) for w in widths],
        compiler_params=_cparams("arbitrary", "arbitrary"),
        name="in_projection_rope" if rope else ("in_projection_ctx_kv" if kv_only else "in_projection_ctx"),
    )(*args)


def _lane_half_variants(blk, fill):
    straight, swapped = blk[:, :KV_WIDTH], blk[:, KV_WIDTH:]
    lo = lax.broadcasted_iota(jnp.int32, straight.shape, 1) < HEAD_DIM
    other = jnp.full_like(straight, fill)
    return (jnp.where(lo, straight, other), jnp.where(lo, other, swapped),
            jnp.where(lo, swapped, other), jnp.where(lo, other, straight))


def _merge_head_pair(pv_lo, pv_hi, sink_lo, sink_hi):
    lane_lo = lax.broadcasted_iota(jnp.int32, pv_lo.shape, 1) < HEAD_DIM
    num = jnp.where(lane_lo, pv_lo, pv_hi)
    den = pltpu.roll(jnp.where(lane_lo, pv_hi, pv_lo), HEAD_DIM, 1) + jnp.where(lane_lo, sink_lo, sink_hi)
    return num / den


def _mix_rows(a, f_parts, t, above, below, bg, cw, cb, w, g_post, gate, x):
    n_rows = t.shape[0]
    ridx = lax.broadcasted_iota(jnp.int32, t.shape, 0)
    up = jnp.where(ridx == 0, above, pltpu.roll(t, 1, 0))
    dn = jnp.where(ridx == n_rows - 1, below, pltpu.roll(t, n_rows - 1, 0))
    conv = up * cw[0:1] + t * cw[1:2] + dn * cw[2:3] + cb
    h = jnp.concatenate([a] + f_parts + [(conv * bg.astype(F32)).astype(BF16)], axis=1)
    y = _dot(h, w)
    r = lax.rsqrt(jnp.mean(y * y, axis=-1, keepdims=True) + NORM_EPS)
    return x + (y * r) * (gate * g_post)


def _attn_kernel(sink_ref, q_ref, g_ref, kp_ref, km_ref, kn_ref, vp_ref, vm_ref, vn_ref, kc_ref, vc_ref,
                 x_ref, f_ref, t_ref, tp_ref, tn_ref, bg_ref, cw_ref, cb_ref, w_ref, gpost_ref, mgate_ref,
                 o_ref, k_s, v_s, kctx_s, vctx_s, bias_s, s_s, p_s, r_s, a_s, *, tq, n_seq, n_ctx):
    i = pl.program_id(1)
    n_blocks = tq // BLOCK_Q
    for off, kref, vref, rows in ((0, kp_ref, vp_ref, WINDOW), (WINDOW, km_ref, vm_ref, tq),
                                  (WINDOW + tq, kn_ref, vn_ref, WINDOW)):
        for idx, kk in enumerate(_lane_half_variants(kref[0], 0.0)):
            k_s[idx, off:off + rows] = kk
        for idx, vv in enumerate(_lane_half_variants(vref[0], 1.0)):
            v_s[idx, off:off + rows] = vv
    for idx, kk in enumerate(_lane_half_variants(kc_ref[0], 0.0)):
        kctx_s[idx] = kk
    for idx, vv in enumerate(_lane_half_variants(vc_ref[0], 1.0)):
        vctx_s[idx] = vv

    ii = lax.broadcasted_iota(jnp.int32, (BLOCK_Q, WINDOW), 0)
    jj = lax.broadcasted_iota(jnp.int32, (BLOCK_Q, WINDOW), 1)
    head_band = jnp.where(jj >= ii, 0.0, -jnp.inf)
    tail_band = jnp.where(jj <= ii, 0.0, -jnp.inf)
    bias_s[0] = head_band
    bias_s[1] = tail_band
    bias_s[2] = jnp.where(i == 0, -jnp.inf, head_band)
    bias_s[3] = jnp.where(i == pl.num_programs(1) - 1, -jnp.inf, tail_band)

    head_w = n_ctx + KEY_SPAN

    def scores(sb):
        par, r0 = sb % 2, sb * BLOCK_Q
        q = q_ref[0, r0:r0 + BLOCK_Q, :]
        for p in range(HEAD_PAIRS):
            kv = p // PAIRS_PER_KV
            keys = jnp.concatenate([kctx_s[2 * kv], k_s[2 * kv, r0:r0 + KEY_SPAN],
                                    kctx_s[2 * kv + 1], k_s[2 * kv + 1, r0:r0 + KEY_SPAN]], axis=0)
            s_s[par, p] = _dot_nt(q[:, p * LANES:(p + 1) * LANES], keys)

    def softmax(sb):
        par = sb % 2
        head_bias = 2 if sb == 0 else 0
        tail_bias = 3 if sb == n_blocks - 1 else 1
        n_cols = head_w // LANES
        for p in range(HEAD_PAIRS):
            for half in range(2):
                h = 2 * p + half
                sink2 = sink_ref[h] * LOG2E
                for rs in range(BLOCK_Q // SOFTMAX_ROWS):
                    rows = slice(rs * SOFTMAX_ROWS, (rs + 1) * SOFTMAX_ROWS)
                    cols = [s_s[par, p, rows, half * head_w + t * LANES:half * head_w + (t + 1) * LANES]
                            for t in range(n_cols)]
                    first_local = n_ctx // LANES
                    cols[first_local] = cols[first_local] + bias_s[head_bias, rows]
                    cols[-1] = cols[-1] + bias_s[tail_bias, rows]
                    m = jnp.maximum(jnp.max(functools.reduce(jnp.maximum, cols), axis=-1, keepdims=True),
                                    sink2)
                    for t, col in enumerate(cols):
                        p_s[par, p, rows, half * head_w + t * LANES:half * head_w + (t + 1) * LANES] = (
                            jnp.exp2(col - m).astype(BF16))
                    r_s[par, h, rows] = jnp.broadcast_to(jnp.exp2(sink2 - m), (SOFTMAX_ROWS, LANES))

    def weighted_values(sb):
        par, r0 = sb % 2, sb * BLOCK_Q
        for p in range(HEAD_PAIRS):
            kv = p // PAIRS_PER_KV
            pv = []
            for half in range(2):
                vals = jnp.concatenate([vctx_s[2 * kv + half], v_s[2 * kv + half, r0:r0 + KEY_SPAN]], axis=0)
                pv.append(_dot(p_s[par, p, :, half * head_w:(half + 1) * head_w], vals))
            out = _merge_head_pair(pv[0], pv[1], r_s[par, 2 * p], r_s[par, 2 * p + 1])
            gate = g_ref[0, r0:r0 + BLOCK_Q, p * LANES:(p + 1) * LANES].astype(F32)
            a_s[r0:r0 + BLOCK_Q, p * LANES:(p + 1) * LANES] = (out * gate).astype(BF16)

    def conv_edge(ref, row, keep):
        return jnp.where(keep, ref[0].astype(F32)[row:row + 1, :], 0.0)

    def mix(r0):
        rows = slice(r0, r0 + MIX_ROWS)
        if r0 == 0:
            above = conv_edge(tp_ref, BF16_ROWS - 1, i > 0)
        else:
            above = t_ref[0, r0 - BF16_ROWS:r0, :].astype(F32)[BF16_ROWS - 1:, :]
        if r0 + MIX_ROWS == tq:
            below = conv_edge(tn_ref, 0, i < pl.num_programs(1) - 1)
        else:
            below = t_ref[0, r0 + MIX_ROWS:r0 + MIX_ROWS + BF16_ROWS, :].astype(F32)[:1, :]
        o_ref[0, rows] = _mix_rows(
            a_s[rows], [f_ref[0, s, rows] for s in range(FOURIER_WIDTH // LANES)],
            t_ref[0, rows].astype(F32), above, below, bg_ref[0, rows], cw_ref[0], cb_ref[0],
            w_ref[0], gpost_ref[0], mgate_ref[0, pl.ds(pl.program_id(0), 1), :], x_ref[0, rows])

    per_mix = MIX_ROWS // BLOCK_Q
    scores(0)
    for sb in range(n_blocks):
        if sb + 1 < n_blocks:
            scores(sb + 1)
        softmax(sb)
        weighted_values(sb)
        if (sb + 1) % per_mix == 0:
            mix((sb + 1 - per_mix) * BLOCK_Q)


def _attention_and_mix(x, q, gate, k, v, kc, vc, sink, f, t, bg, conv_w, conv_b, w_out, g_post, mod,
                       layer, *, tq):
    b, n, _ = q.shape
    nc = kc.shape[1]
    per = tq // WINDOW
    last = n // WINDOW - 1
    kvw = 2 * KV_WIDTH
    variants = 2 * KV_HEADS
    main = lambda width: pl.BlockSpec((1, tq, width), lambda bi, i: (bi, i, 0))
    prev = pl.BlockSpec((1, WINDOW, kvw), lambda bi, i: (bi, jnp.maximum(i * per - 1, 0), 0))
    nxt = pl.BlockSpec((1, WINDOW, kvw), lambda bi, i: (bi, jnp.minimum((i + 1) * per, last), 0))
    ctx = pl.BlockSpec((1, nc, kvw), lambda bi, i: (bi, 0, 0))
    lay3 = lambda shape: pl.BlockSpec((1,) + shape, lambda bi, i: (layer, 0, 0))
    t_per = tq // BF16_ROWS
    t_last = n // BF16_ROWS - 1
    t_prev = pl.BlockSpec((1, BF16_ROWS, CONV_WIDTH), lambda bi, i: (bi, jnp.maximum(i * t_per - 1, 0), 0))
    t_next = pl.BlockSpec((1, BF16_ROWS, CONV_WIDTH),
                          lambda bi, i: (bi, jnp.minimum((i + 1) * t_per, t_last), 0))
    span = tq + 2 * WINDOW
    s_cols = 2 * (nc + KEY_SPAN)
    return pl.pallas_call(
        functools.partial(_attn_kernel, tq=tq, n_seq=n, n_ctx=nc),
        grid=(b, n // tq),
        in_specs=[pl.BlockSpec(memory_space=pltpu.SMEM), main(ATTN_WIDTH), main(ATTN_WIDTH),
                  prev, main(kvw), nxt, prev, main(kvw), nxt, ctx, ctx,
                  main(D_MODEL),
                  pl.BlockSpec((1, FOURIER_WIDTH // LANES, tq, LANES), lambda bi, i: (bi, 0, i, 0)),
                  main(CONV_WIDTH), t_prev, t_next, main(CONV_WIDTH),
                  lay3((3, CONV_WIDTH)), lay3((1, CONV_WIDTH)), lay3((MIX_WIDTH, D_MODEL)),
                  lay3((1, D_MODEL)),
                  pl.BlockSpec((1, MOD_ROWS, D_MODEL), lambda bi, i: (layer, 0, 2))],
        out_specs=main(D_MODEL),
        out_shape=jax.ShapeDtypeStruct(x.shape, F32),
        scratch_shapes=[pltpu.VMEM((variants, span, KV_WIDTH), BF16),
                        pltpu.VMEM((variants, span, KV_WIDTH), BF16),
                        pltpu.VMEM((variants, nc, KV_WIDTH), BF16),
                        pltpu.VMEM((variants, nc, KV_WIDTH), BF16),
                        pltpu.VMEM((4, BLOCK_Q, WINDOW), F32),
                        pltpu.VMEM((2, HEAD_PAIRS, BLOCK_Q, s_cols), F32),
                        pltpu.VMEM((2, HEAD_PAIRS, BLOCK_Q, s_cols), BF16),
                        pltpu.VMEM((2, ATTN_HEADS, BLOCK_Q, LANES), F32),
                        pltpu.VMEM((tq, ATTN_WIDTH), BF16)],
        compiler_params=_cparams("arbitrary", "arbitrary"),
        name="attention_and_mix",
    )(sink, q, gate, k, k, k, v, v, v, kc, vc, x, f, t, t, t, bg, conv_w, conv_b, w_out, g_post, mod)


def _ctx_attn_kernel(sink_ref, q_ref, g_ref, kc_ref, vc_ref, o_ref):
    keys = _lane_half_variants(kc_ref[0], 0.0)
    vals = _lane_half_variants(vc_ref[0], 1.0)
    for p in range(HEAD_PAIRS):
        kv = p // PAIRS_PER_KV
        qp = q_ref[0, :, p * LANES:(p + 1) * LANES]
        pv, sink_terms = [], []
        for half in range(2):
            s = _dot_nt(qp, keys[2 * kv + half])
            sink2 = sink_ref[2 * p + half] * LOG2E
            m = jnp.maximum(jnp.max(s, axis=-1, keepdims=True), sink2)
            pv.append(_dot(jnp.exp2(s - m).astype(BF16), vals[2 * kv + half]))
            sink_terms.append(jnp.exp2(sink2 - m))
        out = _merge_head_pair(pv[0], pv[1], sink_terms[0], sink_terms[1])
        gate = g_ref[0, :, p * LANES:(p + 1) * LANES].astype(F32)
        o_ref[0, :, p * LANES:(p + 1) * LANES] = (out * gate).astype(BF16)


def _context_attention(q, gate, kc, vc, sink):
    b, nc, _ = q.shape
    blk = lambda width: pl.BlockSpec((1, nc, width), lambda bi: (bi, 0, 0))
    return pl.pallas_call(
        _ctx_attn_kernel,
        grid=(b,),
        in_specs=[pl.BlockSpec(memory_space=pltpu.SMEM), blk(ATTN_WIDTH), blk(ATTN_WIDTH),
                  blk(2 * KV_WIDTH), blk(2 * KV_WIDTH)],
        out_specs=blk(ATTN_WIDTH),
        out_shape=jax.ShapeDtypeStruct((b, nc, ATTN_WIDTH), BF16),
        compiler_params=_cparams("arbitrary"),
        name="context_attention",
    )(sink, q, gate, kc, vc)


def _channel_mix_matrix(cc_ref, sc_ref, wf_ref, scale):
    wf = wf_ref[0]
    return (jnp.concatenate([_dot(cc_ref[...], wf), -_dot(sc_ref[...], wf)], axis=1) * scale).astype(BF16)


def _fourier_kernel(uf_ref, gate_ref, cc_ref, sc_ref, wf_ref, g_ref, m2_ref, o_ref, z_s, y_s, *, n_seq):
    n1_len = n_seq // FFT_N2
    mix = _channel_mix_matrix(cc_ref, sc_ref, wf_ref, (n_seq * FOURIER_GROUP_DIM) ** -0.5)

    per_chunk = Z_CHUNK // FFT_N2
    for c in range(n_seq // Z_CHUNK):
        z = _dot(uf_ref[0, c * Z_CHUNK:(c + 1) * Z_CHUNK, :], mix)
        for j in range(per_chunk):
            for part in range(2):
                z_s[part, pl.ds(c * per_chunk + j, FFT_N2, stride=Z_PITCH), :] = (
                    z[j * FFT_N2:(j + 1) * FFT_N2, part * LANES:(part + 1) * LANES])

    def stage1(t, carry):
        for u in range(STAGE1_UNROLL):
            n2 = t * STAGE1_UNROLL + u
            z0 = pl.multiple_of(n2 * Z_PITCH, SUBLANES)
            rhs = jnp.concatenate([z_s[part, pl.ds(z0, n1_len), :] for part in range(2)],
                                  axis=0).astype(BF16)
            y_s[n2] = _dot(g_ref[n2], rhs).reshape(2, n1_len, LANES)
        return carry

    lax.fori_loop(0, FFT_N2 // STAGE1_UNROLL, stage1, 0)

    def stage2(t, carry):
        r0 = pl.multiple_of(t * BF16_ROWS, BF16_ROWS)
        outs = []
        for u in range(BF16_ROWS // K1_GROUP):
            blk = y_s[:, :, pl.ds(r0 + u * K1_GROUP, K1_GROUP), :]
            rhs = blk.reshape(FFT_N2 * 2 * K1_GROUP, LANES).astype(BF16)
            outs.append(_dot(m2_ref[...], rhs).reshape(FFT_N2, K1_GROUP, LANES))
        gate = gate_ref[0, :, pl.ds(r0, BF16_ROWS), :].astype(F32)
        o_ref[0, 0, :, pl.ds(r0, BF16_ROWS), :] = (jnp.concatenate(outs, axis=1) * gate).astype(BF16)
        return carry

    lax.fori_loop(0, n1_len // BF16_ROWS, stage2, 0)


def _fourier_mix(uf, gate, wf_half, consts):
    b, n, _ = uf.shape
    halves = FOURIER_WIDTH // LANES
    n1_len = n // FFT_N2
    cc, sc, g, m2 = consts
    full = lambda shape: pl.BlockSpec(shape, lambda bi, hf: (0,) * len(shape))
    out = pl.pallas_call(
        functools.partial(_fourier_kernel, n_seq=n),
        grid=(b, halves),
        in_specs=[pl.BlockSpec((1, n, LANES), lambda bi, hf: (bi, 0, hf)),
                  pl.BlockSpec((1, FFT_N2, n1_len, LANES), lambda bi, hf: (bi, 0, 0, hf)),
                  full((LANES, LANES)), full((LANES, LANES)),
                  pl.BlockSpec((1, LANES, LANES), lambda bi, hf: (hf, 0, 0)),
                  full(g.shape), full(m2.shape)],
        out_specs=pl.BlockSpec((1, 1, FFT_N2, n1_len, LANES), lambda bi, hf: (bi, hf, 0, 0, 0)),
        out_shape=jax.ShapeDtypeStruct((b, halves, FFT_N2, n1_len, LANES), BF16),
        scratch_shapes=[pltpu.VMEM((2, FFT_N2 * Z_PITCH, LANES), F32),
                        pltpu.VMEM((FFT_N2, 2, n1_len, LANES), F32)],
        compiler_params=_cparams("arbitrary", "arbitrary"),
        name="fourier_mix",
    )(uf, gate.reshape(b, FFT_N2, n1_len, FOURIER_WIDTH), cc, sc, wf_half, g, m2)
    return out.reshape(b, halves, n, LANES)


def _ctx_fourier_kernel(uf_ref, gate_ref, cc_ref, sc_ref, wf_ref, dft_ref, o_ref, *, n_seq):
    mix = _channel_mix_matrix(cc_ref, sc_ref, wf_ref, (n_seq * FOURIER_GROUP_DIM) ** -0.5)
    z = _dot(uf_ref[0], mix)
    rhs = jnp.concatenate([z[:, :LANES], z[:, LANES:]], axis=0).astype(BF16)
    o_ref[0, 0] = (_dot(dft_ref[...], rhs) * gate_ref[0].astype(F32)).astype(BF16)


def _ctx_fourier_mix(uf, gate, wf_half, cc, sc, dft):
    b, n, _ = uf.shape
    halves = FOURIER_WIDTH // LANES
    full = lambda shape: pl.BlockSpec(shape, lambda bi, hf: (0,) * len(shape))
    half = pl.BlockSpec((1, n, LANES), lambda bi, hf: (bi, 0, hf))
    return pl.pallas_call(
        functools.partial(_ctx_fourier_kernel, n_seq=n),
        grid=(b, halves),
        in_specs=[half, half, full((LANES, LANES)), full((LANES, LANES)),
                  pl.BlockSpec((1, LANES, LANES), lambda bi, hf: (hf, 0, 0)), full(dft.shape)],
        out_specs=pl.BlockSpec((1, 1, n, LANES), lambda bi, hf: (bi, hf, 0, 0)),
        out_shape=jax.ShapeDtypeStruct((b, halves, n, LANES), BF16),
        compiler_params=_cparams("arbitrary", "arbitrary"),
        name="context_fourier_mix",
    )(uf, gate, cc, sc, wf_half, dft)


def _ctx_mix_kernel(x_ref, a_ref, f_ref, t_ref, bg_ref, cw_ref, cb_ref, w_ref, g_ref, gate_ref, o_ref,
                    *, ctx_row):
    edge = jnp.zeros((1, CONV_WIDTH), F32)
    o_ref[0] = _mix_rows(a_ref[0], [f_ref[0, s] for s in range(FOURIER_WIDTH // LANES)],
                         t_ref[0].astype(F32), edge, edge, bg_ref[0], cw_ref[0], cb_ref[0], w_ref[0],
                         g_ref[0], gate_ref[0, ctx_row:ctx_row + 1, :], x_ref[0])


def _context_mix(x, a, f, t, bg, conv_w, conv_b, w_out, g_post, mod, layer, *, ctx_row):
    b, n, _ = x.shape
    row3 = lambda width: pl.BlockSpec((1, n, width), lambda bi: (bi, 0, 0))
    lay3 = lambda shape: pl.BlockSpec((1,) + shape, lambda bi: (layer, 0, 0))
    slabs = FOURIER_WIDTH // LANES
    return pl.pallas_call(
        functools.partial(_ctx_mix_kernel, ctx_row=ctx_row),
        grid=(b,),
        in_specs=[row3(D_MODEL), row3(ATTN_WIDTH),
                  pl.BlockSpec((1, slabs, n, LANES), lambda bi: (bi, 0, 0, 0)),
                  row3(CONV_WIDTH), row3(CONV_WIDTH), lay3((3, CONV_WIDTH)), lay3((1, CONV_WIDTH)),
                  lay3((MIX_WIDTH, D_MODEL)), lay3((1, D_MODEL)),
                  pl.BlockSpec((1, MOD_ROWS, D_MODEL), lambda bi: (layer, 0, 2))],
        out_specs=row3(D_MODEL),
        out_shape=jax.ShapeDtypeStruct(x.shape, F32),
        compiler_params=_cparams("arbitrary"),
        name="context_mix",
    )(x, a, f, t, bg, conv_w, conv_b, w_out, g_post, mod)


def kernel(x, c, ctx, c_ctx, w_mod, b_mod, g_pre, g_post, w_in, w_out, sink, w_fourier, conv_w, conv_b):
    depth = w_mod.shape[0]
    b, n, _ = x.shape
    nc = ctx.shape[1]
    assert b + 1 <= MOD_ROWS and n % GRID_W == 0 and n % ATTN_ROWS == 0 and n % Z_CHUNK == 0
    assert (n // FFT_N2) % BF16_ROWS == 0 and FFT_N2 % STAGE1_UNROLL == 0 and n // FFT_N2 <= Z_PITCH

    w_in_b = w_in.astype(BF16)
    w_out_b = w_out.astype(BF16)

    rope_tabs = tuple(jnp.asarray(t) for t in _rope_tables(n))
    cc, sc = (jnp.asarray(m) for m in _channel_dft())
    stage1 = jnp.asarray(_stage1_mats(n)).astype(BF16)
    stage2 = jnp.asarray(_stage2_mat()).astype(BF16)
    ctx_dft = jnp.asarray(_dense_dft(nc)).astype(BF16)
    groups_per_half = LANES // FOURIER_GROUP_DIM
    wf_half = jnp.zeros((depth, FOURIER_GROUPS // groups_per_half, LANES, LANES), F32)
    for g in range(FOURIER_GROUPS):
        o = (g % groups_per_half) * FOURIER_GROUP_DIM
        wf_half = wf_half.at[:, g // groups_per_half, o:o + FOURIER_GROUP_DIM,
                             o:o + FOURIER_GROUP_DIM].set(w_fourier[:, g])

    c_rows = jnp.zeros((MOD_ROWS, D_MODEL), F32).at[:b].set(c).at[b].set(c_ctx)
    mod = _modulation(c_rows, w_mod, b_mod)
    g_pre3 = g_pre.reshape(depth, 1, D_MODEL)
    g_post3 = g_post.reshape(depth, 1, D_MODEL)
    conv_b3 = conv_b.reshape(depth, 1, CONV_WIDTH)

    for l in range(depth):
        update_ctx = l < depth - 1
        if update_ctx:
            qc, kc, vc, sgac, ufc, sgfc, tc, bgc = _in_projection(
                ctx, mod, g_pre3, w_in_b, l, rope_tabs=None, ctx_row=b, tm=nc)
        else:
            kc, vc = _in_projection(ctx, mod, g_pre3, w_in_b, l, rope_tabs=None, ctx_row=b, tm=nc,
                                    kv_only=True)
        q, k, v, sga, uf, sgf, t, bg = _in_projection(
            x, mod, g_pre3, w_in_b, l, rope_tabs=rope_tabs, ctx_row=None, tm=TILE_ROWS)
        f = _fourier_mix(uf, sgf, wf_half[l], (cc, sc, stage1, stage2))
        x = _attention_and_mix(x, q, sga, k, v, kc, vc, sink[l], f, t, bg, conv_w, conv_b3, w_out_b,
                               g_post3, mod, l, tq=ATTN_ROWS)
        if update_ctx:
            ac = _context_attention(qc, sgac, kc, vc, sink[l])
            fc = _ctx_fourier_mix(ufc, sgfc, wf_half[l], cc, sc, ctx_dft)
            ctx = _context_mix(ctx, ac, fc, tc, bgc, conv_w, conv_b3, w_out_b, g_post3, mod, l,
                               ctx_row=b)
    return x
```

```python
import functools
import math

import numpy as np
import jax
import jax.numpy as jnp
from jax import lax
from jax.experimental import pallas as pl
from jax.experimental.pallas import tpu as pltpu

D_MODEL = 1024
GRID_W = 64
HEAD_DIM = 64
ATTN_HEADS = 8
KV_HEADS = 2
Q_PER_KV = ATTN_HEADS // KV_HEADS
ATTN_WIDTH = ATTN_HEADS * HEAD_DIM
KV_WIDTH = KV_HEADS * HEAD_DIM
WINDOW = 128
FOURIER_GROUPS = 4
FOURIER_GROUP_DIM = 64
FOURIER_WIDTH = FOURIER_GROUPS * FOURIER_GROUP_DIM
CONV_WIDTH = 256
MIX_WIDTH = ATTN_WIDTH + FOURIER_WIDTH + CONV_WIDTH
PROJ_WIDTH = 2 * ATTN_WIDTH + 2 * KV_WIDTH + 2 * FOURIER_WIDTH + 4 * CONV_WIDTH
ROPE_FREQS = HEAD_DIM // 4
ROPE_BASE = 10000.0
NORM_EPS = 1e-6
LOG2E = math.log2(math.e)

C_Q = 0
C_K = C_Q + ATTN_WIDTH
C_V = C_K + KV_WIDTH
C_GA = C_V + KV_WIDTH
C_UF = C_GA + ATTN_WIDTH
C_GF = C_UF + FOURIER_WIDTH
C_ZC = C_GF + FOURIER_WIDTH
C_BC = C_ZC + CONV_WIDTH
C_CC = C_BC + CONV_WIDTH
C_GC = C_CC + CONV_WIDTH

LANES = 128
SUBLANES = 8
BF16_ROWS = 16
VMEM_LIMIT = 52 * 1024 * 1024

MOD_ROWS = 8
BF16 = jnp.bfloat16
F32 = jnp.float32

FFT_N2 = 64
K1_GROUP = SUBLANES
N2_CHUNK = 8
UF_PITCH = FFT_N2 + SUBLANES
SOFTMAX_ROWS = 32

BLOCK_Q = 128
KEY_SPAN = BLOCK_Q + 2 * WINDOW
HEAD_PAIRS = ATTN_WIDTH // LANES
PAIRS_PER_KV = HEAD_PAIRS // KV_HEADS

TILE_ROWS = 1024
PROJ_SUB_ROWS = 512
ATTN_ROWS = 1024
MIX_ROWS = 256


def _silu(x):
    return x / (1.0 + jnp.exp(-x))


def _dot(a, b):
    return jnp.dot(a, b, preferred_element_type=F32)


def _dot_nt(a, b):
    return lax.dot_general(a, b, (((1,), (1,)), ((), ())), preferred_element_type=F32)


def _cparams(*sem):
    return pltpu.CompilerParams(dimension_semantics=sem, vmem_limit_bytes=VMEM_LIMIT)


def _rope_tables(n):
    t = np.arange(n)
    row = (t // GRID_W).astype(np.float64)
    col = (t % GRID_W).astype(np.float64)
    inv = ROPE_BASE ** (-np.arange(ROPE_FREQS, dtype=np.float64) / ROPE_FREQS)
    ar = row[:, None] * inv
    ac = col[:, None] * inv
    z = np.zeros_like(ar)
    cos_h = np.concatenate([np.cos(ar), np.cos(ar), np.cos(ac), np.cos(ac)], axis=1)
    sin_up = np.concatenate([-np.sin(ar), z, -np.sin(ac), z], axis=1)
    sin_dn = np.concatenate([z, np.sin(ar), z, np.sin(ac)], axis=1)
    rep = LANES // HEAD_DIM
    return tuple(np.tile(a, (1, rep)).astype(np.float32) for a in (cos_h, sin_up, sin_dn))


def _channel_dft():
    c = np.arange(FOURIER_GROUP_DIM)
    ang = 2.0 * np.pi * np.outer(c, c) / FOURIER_GROUP_DIM
    eye = np.eye(LANES // FOURIER_GROUP_DIM)
    return (np.kron(eye, np.cos(ang)).astype(np.float32),
            np.kron(eye, np.sin(ang)).astype(np.float32))


def _stage1_mats(n):
    n1_len = n // FFT_N2
    k1 = np.arange(n1_len)[:, None]
    n1 = np.arange(n1_len)[None, :]
    out = np.empty((FFT_N2, 2 * n1_len, 2 * n1_len), np.float32)
    for n2 in range(FFT_N2):
        ang = 2.0 * np.pi * ((k1 * (FFT_N2 * n1 + n2)) % n) / n
        ce, se = np.cos(ang), np.sin(ang)
        by_part = np.block([[ce, se], [-se, ce]]).reshape(2, n1_len // K1_GROUP, K1_GROUP, 2 * n1_len)
        out[n2] = by_part.transpose(1, 0, 2, 3).reshape(2 * n1_len, 2 * n1_len)
    return out


def _stage2_mat():
    k2 = np.arange(FFT_N2)
    ang = 2.0 * np.pi * np.outer(k2, k2) / FFT_N2
    cs = np.stack([np.cos(ang), np.sin(ang)], axis=-1)
    eye = np.eye(K1_GROUP)
    m = np.einsum('knp,rs->krnps', cs, eye)
    return m.reshape(FFT_N2 * K1_GROUP, FFT_N2 * 2 * K1_GROUP).astype(np.float32)


def _dense_dft(n):
    t = np.arange(n)
    ang = 2.0 * np.pi * (np.outer(t, t) % n) / n
    return np.concatenate([np.cos(ang), np.sin(ang)], axis=1).astype(np.float32)


def _mod_kernel(c_ref, w_ref, b_ref, o_ref):
    o_ref[0] = _dot(_silu(c_ref[...]), w_ref[0]) + b_ref[0]


def _modulation(c_rows, w_mod, b_mod):
    depth = w_mod.shape[0]
    return pl.pallas_call(
        _mod_kernel,
        grid=(depth, 3),
        in_specs=[pl.BlockSpec((MOD_ROWS, D_MODEL), lambda l, j: (0, 0)),
                  pl.BlockSpec((1, D_MODEL, D_MODEL), lambda l, j: (l, 0, j)),
                  pl.BlockSpec((1, 1, D_MODEL), lambda l, j: (l, 0, j))],
        out_specs=pl.BlockSpec((1, MOD_ROWS, D_MODEL), lambda l, j: (l, 0, j)),
        out_shape=jax.ShapeDtypeStruct((depth, MOD_ROWS, 3 * D_MODEL), F32),
        compiler_params=_cparams("arbitrary", "arbitrary"),
        name="modulation",
    )(c_rows, w_mod, b_mod.reshape(depth, 1, 3 * D_MODEL))


def _inproj_kernel(*refs, rope, ctx_row, kv_only, tm):
    x_ref, shift_ref, scale_ref, g_ref, w_ref = refs[:5]
    cos_ref, sup_ref, sdn_ref = refs[5:8] if rope else (None,) * 3
    outs = refs[8:-1] if rope else refs[5:]
    uf_s = refs[-1] if rope else None
    row = pl.program_id(0) if ctx_row is None else ctx_row
    shift = shift_ref[0, pl.ds(row, 1), :]
    gain = g_ref[0] * (1.0 + scale_ref[0, pl.ds(row, 1), :])

    def with_swapped_heads(y):
        return jnp.concatenate([y, pltpu.roll(y, HEAD_DIM, 1)], axis=1).astype(BF16)

    sub = min(tm, PROJ_SUB_ROWS)
    for s in range(tm // sub):
        rows = slice(s * sub, (s + 1) * sub)
        x = x_ref[0, rows]
        r = lax.rsqrt(jnp.mean(x * x, axis=-1, keepdims=True) + NORM_EPS)
        h = ((x * r) * gain + shift).astype(BF16)

        def proj(c0, width):
            return _dot(h, w_ref[0, :, c0:c0 + width])

        def rotate(y):
            if not rope:
                return y
            return (y * cos_ref[rows] + pltpu.roll(y, LANES - ROPE_FREQS, 1) * sup_ref[rows]
                    + pltpu.roll(y, ROPE_FREQS, 1) * sdn_ref[rows])

        kv = proj(C_K, 2 * KV_WIDTH)
        k_out = with_swapped_heads(rotate(kv[:, :KV_WIDTH]))
        v_out = with_swapped_heads(kv[:, KV_WIDTH:])
        if kv_only:
            k_ref, v_ref = outs
            k_ref[0, rows] = k_out
            v_ref[0, rows] = v_out
            continue
        q_ref, k_ref, v_ref, sga_ref, uf_ref, sgf_ref, t_ref, bg_ref = outs
        k_ref[0, rows] = k_out
        v_ref[0, rows] = v_out
        q = proj(C_Q, ATTN_WIDTH)
        for p in range(HEAD_PAIRS):
            sl = slice(p * LANES, (p + 1) * LANES)
            q_ref[0, rows, sl] = (rotate(q[:, sl]) * (HEAD_DIM ** -0.5 * LOG2E)).astype(BF16)
        sga_ref[0, rows] = _silu(proj(C_GA, ATTN_WIDTH)).astype(BF16)
        uf = proj(C_UF, FOURIER_WIDTH)
        if rope:
            per_sub = sub // FFT_N2
            for slab in range(FOURIER_WIDTH // LANES):
                for j in range(per_sub):
                    uf_s[s, slab, j * UF_PITCH:j * UF_PITCH + FFT_N2] = (
                        uf[j * FFT_N2:(j + 1) * FFT_N2, slab * LANES:(slab + 1) * LANES])
            for n2 in range(FFT_N2):
                uf_ref[0, n2, s * per_sub:(s + 1) * per_sub, :] = jnp.concatenate(
                    [uf_s[s, slab, pl.ds(n2, per_sub, stride=UF_PITCH), :]
                     for slab in range(FOURIER_WIDTH // LANES)], axis=1)
        else:
            uf_ref[0, rows] = uf.astype(BF16)
        sgf_ref[0, rows] = _silu(proj(C_GF, FOURIER_WIDTH)).astype(BF16)
        t_ref[0, rows] = (proj(C_CC, CONV_WIDTH) * proj(C_ZC, CONV_WIDTH)).astype(BF16)
        bg_ref[0, rows] = (proj(C_BC, CONV_WIDTH) * _silu(proj(C_GC, CONV_WIDTH))).astype(BF16)


def _in_projection(x, mod, g_pre, w_in, layer, *, rope_tabs, ctx_row, tm, kv_only=False):
    b, n, _ = x.shape
    rope = rope_tabs is not None
    row3 = lambda width: pl.BlockSpec((1, tm, width), lambda bi, i: (bi, i, 0))
    in_specs = [row3(D_MODEL),
                pl.BlockSpec((1, MOD_ROWS, D_MODEL), lambda bi, i: (layer, 0, 0)),
                pl.BlockSpec((1, MOD_ROWS, D_MODEL), lambda bi, i: (layer, 0, 1)),
                pl.BlockSpec((1, 1, D_MODEL), lambda bi, i: (layer, 0, 0)),
                pl.BlockSpec((1, D_MODEL, PROJ_WIDTH), lambda bi, i: (layer, 0, 0))]
    args = [x, mod, mod, g_pre, w_in]
    if rope:
        in_specs += [pl.BlockSpec((tm, LANES), lambda bi, i: (i, 0))] * 3
        args += list(rope_tabs)
    widths = (ATTN_WIDTH, 2 * KV_WIDTH, 2 * KV_WIDTH, ATTN_WIDTH, FOURIER_WIDTH, FOURIER_WIDTH,
              CONV_WIDTH, CONV_WIDTH)
    if kv_only:
        widths = widths[1:3]
    out_specs = [row3(w) for w in widths]
    out_shape = [jax.ShapeDtypeStruct((b, n, w), BF16) for w in widths]
    scratch = []
    if rope:
        sub = min(tm, PROJ_SUB_ROWS)
        assert sub % (FFT_N2 * SUBLANES) == 0
        out_specs[4] = pl.BlockSpec((1, FFT_N2, tm // FFT_N2, FOURIER_WIDTH), lambda bi, i: (bi, 0, i, 0))
        out_shape[4] = jax.ShapeDtypeStruct((b, FFT_N2, n // FFT_N2, FOURIER_WIDTH), F32)
        scratch = [pltpu.VMEM((tm // sub, FOURIER_WIDTH // LANES, sub // FFT_N2 * UF_PITCH, LANES), F32)]
    return pl.pallas_call(
        functools.partial(_inproj_kernel, rope=rope, ctx_row=ctx_row, kv_only=kv_only, tm=tm),
        grid=(b, n // tm),
        in_specs=in_specs,
        out_specs=out_specs,
        out_shape=out_shape,
        scratch_shapes=scratch,
        compiler_params=_cparams("arbitrary", "arbitrary"),
        name="in_projection_rope" if rope else ("in_projection_ctx_kv" if kv_only else "in_projection_ctx"),
    )(*args)


def _lane_half_variants(blk, fill):
    straight, swapped = blk[:, :KV_WIDTH], blk[:, KV_WIDTH:]
    lo = lax.broadcasted_iota(jnp.int32, straight.shape, 1) < HEAD_DIM
    other = jnp.full_like(straight, fill)
    return (jnp.where(lo, straight, other), jnp.where(lo, other, swapped),
            jnp.where(lo, swapped, other), jnp.where(lo, other, straight))


def _merge_head_pair(pv_lo, pv_hi, sink_lo, sink_hi):
    lane_lo = lax.broadcasted_iota(jnp.int32, pv_lo.shape, 1) < HEAD_DIM
    num = jnp.where(lane_lo, pv_lo, pv_hi)
    den = pltpu.roll(jnp.where(lane_lo, pv_hi, pv_lo), HEAD_DIM, 1) + jnp.where(lane_lo, sink_lo, sink_hi)
    return num / den


def _mix_rows(a, f, t, above, below, bg, cw, cb, w, g_post, gate, x):
    n_rows = t.shape[0]
    ridx = lax.broadcasted_iota(jnp.int32, t.shape, 0)
    up = jnp.where(ridx == 0, above, pltpu.roll(t, 1, 0))
    dn = jnp.where(ridx == n_rows - 1, below, pltpu.roll(t, n_rows - 1, 0))
    conv = up * cw[0:1] + t * cw[1:2] + dn * cw[2:3] + cb
    h = jnp.concatenate([a, f, (conv * bg.astype(F32)).astype(BF16)], axis=1)
    y = _dot(h, w)
    r = lax.rsqrt(jnp.mean(y * y, axis=-1, keepdims=True) + NORM_EPS)
    return x + (y * r) * (gate * g_post)


def _attn_kernel(sink_ref, q_ref, g_ref, kp_ref, km_ref, kn_ref, vp_ref, vm_ref, vn_ref, kc_ref, vc_ref,
                 x_ref, f_ref, t_ref, tp_ref, tn_ref, bg_ref, cw_ref, cb_ref, w_ref, gpost_ref, mgate_ref,
                 o_ref, k_s, v_s, kctx_s, vctx_s, bias_s, s_s, p_s, r_s, a_s, *, tq, n_seq, n_ctx):
    i = pl.program_id(1)
    n_blocks = tq // BLOCK_Q
    for off, kref, vref, rows in ((0, kp_ref, vp_ref, WINDOW), (WINDOW, km_ref, vm_ref, tq),
                                  (WINDOW + tq, kn_ref, vn_ref, WINDOW)):
        for idx, kk in enumerate(_lane_half_variants(kref[0], 0.0)):
            k_s[idx, off:off + rows] = kk
        for idx, vv in enumerate(_lane_half_variants(vref[0], 1.0)):
            v_s[idx, off:off + rows] = vv
    for idx, kk in enumerate(_lane_half_variants(kc_ref[0], 0.0)):
        kctx_s[idx] = kk
    for idx, vv in enumerate(_lane_half_variants(vc_ref[0], 1.0)):
        vctx_s[idx] = vv

    ii = lax.broadcasted_iota(jnp.int32, (BLOCK_Q, WINDOW), 0)
    jj = lax.broadcasted_iota(jnp.int32, (BLOCK_Q, WINDOW), 1)
    head_band = jnp.where(jj >= ii, 0.0, -jnp.inf)
    tail_band = jnp.where(jj <= ii, 0.0, -jnp.inf)
    bias_s[0] = head_band
    bias_s[1] = tail_band
    bias_s[2] = jnp.where(i == 0, -jnp.inf, head_band)
    bias_s[3] = jnp.where(i == pl.num_programs(1) - 1, -jnp.inf, tail_band)

    head_w = n_ctx + KEY_SPAN

    def scores(sb):
        par, r0 = sb % 2, sb * BLOCK_Q
        q = q_ref[0, r0:r0 + BLOCK_Q, :]
        for kv in range(KV_HEADS):
            pairs = range(kv * PAIRS_PER_KV, (kv + 1) * PAIRS_PER_KV)
            qg = jnp.concatenate([q[:, p * LANES:(p + 1) * LANES] for p in pairs], axis=0)
            keys = jnp.concatenate([kctx_s[2 * kv], k_s[2 * kv, r0:r0 + KEY_SPAN],
                                    kctx_s[2 * kv + 1], k_s[2 * kv + 1, r0:r0 + KEY_SPAN]], axis=0)
            s_s[par, pairs.start:pairs.stop] = _dot_nt(qg, keys).reshape(PAIRS_PER_KV, BLOCK_Q, 2 * head_w)

    def softmax(sb):
        par = sb % 2
        head_bias = 2 if sb == 0 else 0
        tail_bias = 3 if sb == n_blocks - 1 else 1
        n_cols = head_w // LANES
        for p in range(HEAD_PAIRS):
            for half in range(2):
                h = 2 * p + half
                sink2 = sink_ref[h] * LOG2E
                for rs in range(BLOCK_Q // SOFTMAX_ROWS):
                    rows = slice(rs * SOFTMAX_ROWS, (rs + 1) * SOFTMAX_ROWS)
                    cols = [s_s[par, p, rows, half * head_w + t * LANES:half * head_w + (t + 1) * LANES]
                            for t in range(n_cols)]
                    first_local = n_ctx // LANES
                    cols[first_local] = cols[first_local] + bias_s[head_bias, rows]
                    cols[-1] = cols[-1] + bias_s[tail_bias, rows]
                    m = jnp.maximum(jnp.max(functools.reduce(jnp.maximum, cols), axis=-1, keepdims=True),
                                    sink2)
                    for t, col in enumerate(cols):
                        p_s[par, p, rows, half * head_w + t * LANES:half * head_w + (t + 1) * LANES] = (
                            jnp.exp2(col - m).astype(BF16))
                    r_s[par, h, rows] = jnp.broadcast_to(jnp.exp2(sink2 - m), (SOFTMAX_ROWS, LANES))

    def weighted_values(sb):
        par, r0 = sb % 2, sb * BLOCK_Q
        for kv in range(KV_HEADS):
            pairs = range(kv * PAIRS_PER_KV, (kv + 1) * PAIRS_PER_KV)
            pv = []
            for half in range(2):
                vals = jnp.concatenate([vctx_s[2 * kv + half], v_s[2 * kv + half, r0:r0 + KEY_SPAN]], axis=0)
                probs = p_s[par, pairs.start:pairs.stop, :, half * head_w:(half + 1) * head_w]
                pv.append(_dot(probs.reshape(PAIRS_PER_KV * BLOCK_Q, head_w), vals))
            for j, p in enumerate(pairs):
                rows = slice(j * BLOCK_Q, (j + 1) * BLOCK_Q)
                out = _merge_head_pair(pv[0][rows], pv[1][rows], r_s[par, 2 * p], r_s[par, 2 * p + 1])
                gate = g_ref[0, r0:r0 + BLOCK_Q, p * LANES:(p + 1) * LANES].astype(F32)
                a_s[r0:r0 + BLOCK_Q, p * LANES:(p + 1) * LANES] = (out * gate).astype(BF16)

    def conv_edge(ref, row, keep):
        return jnp.where(keep, ref[0].astype(F32)[row:row + 1, :], 0.0)

    def mix(r0):
        rows = slice(r0, r0 + MIX_ROWS)
        if r0 == 0:
            above = conv_edge(tp_ref, BF16_ROWS - 1, i > 0)
        else:
            above = t_ref[0, r0 - BF16_ROWS:r0, :].astype(F32)[BF16_ROWS - 1:, :]
        if r0 + MIX_ROWS == tq:
            below = conv_edge(tn_ref, 0, i < pl.num_programs(1) - 1)
        else:
            below = t_ref[0, r0 + MIX_ROWS:r0 + MIX_ROWS + BF16_ROWS, :].astype(F32)[:1, :]
        o_ref[0, rows] = _mix_rows(
            a_s[rows], f_ref[0, rows],
            t_ref[0, rows].astype(F32), above, below, bg_ref[0, rows], cw_ref[0], cb_ref[0],
            w_ref[0], gpost_ref[0], mgate_ref[0, pl.ds(pl.program_id(0), 1), :], x_ref[0, rows])

    per_mix = MIX_ROWS // BLOCK_Q
    scores(0)
    for sb in range(n_blocks):
        if sb + 1 < n_blocks:
            scores(sb + 1)
        softmax(sb)
        weighted_values(sb)
        if (sb + 1) % per_mix == 0:
            mix((sb + 1 - per_mix) * BLOCK_Q)


def _attention_and_mix(x, q, gate, k, v, kc, vc, sink, f, t, bg, conv_w, conv_b, w_out, g_post, mod,
                       layer, *, tq):
    b, n, _ = q.shape
    nc = kc.shape[1]
    per = tq // WINDOW
    last = n // WINDOW - 1
    kvw = 2 * KV_WIDTH
    variants = 2 * KV_HEADS
    main = lambda width: pl.BlockSpec((1, tq, width), lambda bi, i: (bi, i, 0))
    prev = pl.BlockSpec((1, WINDOW, kvw), lambda bi, i: (bi, jnp.maximum(i * per - 1, 0), 0))
    nxt = pl.BlockSpec((1, WINDOW, kvw), lambda bi, i: (bi, jnp.minimum((i + 1) * per, last), 0))
    ctx = pl.BlockSpec((1, nc, kvw), lambda bi, i: (bi, 0, 0))
    lay3 = lambda shape: pl.BlockSpec((1,) + shape, lambda bi, i: (layer, 0, 0))
    t_per = tq // BF16_ROWS
    t_last = n // BF16_ROWS - 1
    t_prev = pl.BlockSpec((1, BF16_ROWS, CONV_WIDTH), lambda bi, i: (bi, jnp.maximum(i * t_per - 1, 0), 0))
    t_next = pl.BlockSpec((1, BF16_ROWS, CONV_WIDTH),
                          lambda bi, i: (bi, jnp.minimum((i + 1) * t_per, t_last), 0))
    span = tq + 2 * WINDOW
    s_cols = 2 * (nc + KEY_SPAN)
    return pl.pallas_call(
        functools.partial(_attn_kernel, tq=tq, n_seq=n, n_ctx=nc),
        grid=(b, n // tq),
        in_specs=[pl.BlockSpec(memory_space=pltpu.SMEM), main(ATTN_WIDTH), main(ATTN_WIDTH),
                  prev, main(kvw), nxt, prev, main(kvw), nxt, ctx, ctx,
                  main(D_MODEL),
                  main(FOURIER_WIDTH), main(CONV_WIDTH), t_prev, t_next, main(CONV_WIDTH),
                  lay3((3, CONV_WIDTH)), lay3((1, CONV_WIDTH)), lay3((MIX_WIDTH, D_MODEL)),
                  lay3((1, D_MODEL)),
                  pl.BlockSpec((1, MOD_ROWS, D_MODEL), lambda bi, i: (layer, 0, 2))],
        out_specs=main(D_MODEL),
        out_shape=jax.ShapeDtypeStruct(x.shape, F32),
        scratch_shapes=[pltpu.VMEM((variants, span, KV_WIDTH), BF16),
                        pltpu.VMEM((variants, span, KV_WIDTH), BF16),
                        pltpu.VMEM((variants, nc, KV_WIDTH), BF16),
                        pltpu.VMEM((variants, nc, KV_WIDTH), BF16),
                        pltpu.VMEM((4, BLOCK_Q, WINDOW), F32),
                        pltpu.VMEM((2, HEAD_PAIRS, BLOCK_Q, s_cols), F32),
                        pltpu.VMEM((2, HEAD_PAIRS, BLOCK_Q, s_cols), BF16),
                        pltpu.VMEM((2, ATTN_HEADS, BLOCK_Q, LANES), F32),
                        pltpu.VMEM((tq, ATTN_WIDTH), BF16)],
        compiler_params=_cparams("arbitrary", "arbitrary"),
        name="attention_and_mix",
    )(sink, q, gate, k, k, k, v, v, v, kc, vc, x, f, t, t, t, bg, conv_w, conv_b, w_out, g_post, mod)


def _ctx_attn_kernel(sink_ref, q_ref, g_ref, kc_ref, vc_ref, o_ref):
    keys = _lane_half_variants(kc_ref[0], 0.0)
    vals = _lane_half_variants(vc_ref[0], 1.0)
    for p in range(HEAD_PAIRS):
        kv = p // PAIRS_PER_KV
        qp = q_ref[0, :, p * LANES:(p + 1) * LANES]
        pv, sink_terms = [], []
        for half in range(2):
            s = _dot_nt(qp, keys[2 * kv + half])
            sink2 = sink_ref[2 * p + half] * LOG2E
            m = jnp.maximum(jnp.max(s, axis=-1, keepdims=True), sink2)
            pv.append(_dot(jnp.exp2(s - m).astype(BF16), vals[2 * kv + half]))
            sink_terms.append(jnp.exp2(sink2 - m))
        out = _merge_head_pair(pv[0], pv[1], sink_terms[0], sink_terms[1])
        gate = g_ref[0, :, p * LANES:(p + 1) * LANES].astype(F32)
        o_ref[0, :, p * LANES:(p + 1) * LANES] = (out * gate).astype(BF16)


def _context_attention(q, gate, kc, vc, sink):
    b, nc, _ = q.shape
    blk = lambda width: pl.BlockSpec((1, nc, width), lambda bi: (bi, 0, 0))
    return pl.pallas_call(
        _ctx_attn_kernel,
        grid=(b,),
        in_specs=[pl.BlockSpec(memory_space=pltpu.SMEM), blk(ATTN_WIDTH), blk(ATTN_WIDTH),
                  blk(2 * KV_WIDTH), blk(2 * KV_WIDTH)],
        out_specs=blk(ATTN_WIDTH),
        out_shape=jax.ShapeDtypeStruct((b, nc, ATTN_WIDTH), BF16),
        compiler_params=_cparams("arbitrary"),
        name="context_attention",
    )(sink, q, gate, kc, vc)


def _channel_mix_matrix(cc_ref, sc_ref, wf_ref, scale):
    wf = wf_ref[0]
    return (jnp.concatenate([_dot(cc_ref[...], wf), -_dot(sc_ref[...], wf)], axis=1) * scale).astype(BF16)


def _fourier_kernel(uf_ref, gate_ref, cc_ref, sc_ref, wf_ref, g_ref, m2_ref, o_ref, mix_s, y_s, *, n_seq):
    n1_len = n_seq // FFT_N2
    chunk = pl.program_id(1)
    halves = FOURIER_WIDTH // LANES

    @pl.when(chunk == 0)
    def _():
        zero = jnp.zeros((LANES, LANES), F32)
        scale = (n_seq * FOURIER_GROUP_DIM) ** -0.5
        re = [_dot(cc_ref[...], wf_ref[h]) * scale for h in range(halves)]
        im = [_dot(sc_ref[...], wf_ref[h]) * -scale for h in range(halves)]
        for h in range(halves):
            row = [re[h] if j == h else zero for j in range(halves)]
            row += [im[h] if j == h else zero for j in range(halves)]
            mix_s[h * LANES:(h + 1) * LANES] = jnp.concatenate(row, axis=1).astype(BF16)

    zs = [_dot(uf_ref[0, u].astype(BF16), mix_s[...]) for u in range(N2_CHUNK)]
    for u, z in enumerate(zs):
        rhs = jnp.concatenate([z[:, :FOURIER_WIDTH], z[:, FOURIER_WIDTH:]], axis=0).astype(BF16)
        y = _dot(g_ref[u], rhs)
        y_s[chunk * N2_CHUNK + u] = y.astype(BF16).reshape(n1_len // K1_GROUP, 2 * K1_GROUP, FOURIER_WIDTH)

    @pl.when(chunk == pl.num_programs(1) - 1)
    def _():
        per_store = BF16_ROWS // K1_GROUP

        def stage2(t, carry):
            outs = []
            for u in range(per_store):
                rhs = y_s[:, t * per_store + u].reshape(FFT_N2 * 2 * K1_GROUP, FOURIER_WIDTH)
                outs.append(_dot(m2_ref[...], rhs).reshape(FFT_N2, K1_GROUP, FOURIER_WIDTH))
            r0 = pl.multiple_of(t * BF16_ROWS, BF16_ROWS)
            gate = gate_ref[0, :, pl.ds(r0, BF16_ROWS), :].astype(F32)
            o_ref[0, :, pl.ds(r0, BF16_ROWS), :] = (jnp.concatenate(outs, axis=1) * gate).astype(BF16)
            return carry

        lax.fori_loop(0, n1_len // BF16_ROWS, stage2, 0)


def _fourier_mix(uf, gate, wf_half, consts):
    b, _, n1_len, _ = uf.shape
    n = FFT_N2 * n1_len
    cc, sc, g, m2 = consts
    full = lambda shape: pl.BlockSpec(shape, lambda bi, c: (0,) * len(shape))
    whole = pl.BlockSpec((1, FFT_N2, n1_len, FOURIER_WIDTH), lambda bi, c: (bi, 0, 0, 0))
    out = pl.pallas_call(
        functools.partial(_fourier_kernel, n_seq=n),
        grid=(b, FFT_N2 // N2_CHUNK),
        in_specs=[pl.BlockSpec((1, N2_CHUNK, n1_len, FOURIER_WIDTH), lambda bi, c: (bi, c, 0, 0)),
                  whole, full((LANES, LANES)), full((LANES, LANES)), full(wf_half.shape),
                  pl.BlockSpec((N2_CHUNK,) + g.shape[1:], lambda bi, c: (c, 0, 0)), full(m2.shape)],
        out_specs=whole,
        out_shape=jax.ShapeDtypeStruct((b, FFT_N2, n1_len, FOURIER_WIDTH), BF16),
        scratch_shapes=[pltpu.VMEM((FOURIER_WIDTH, 2 * FOURIER_WIDTH), BF16),
                        pltpu.VMEM((FFT_N2, n1_len // K1_GROUP, 2 * K1_GROUP, FOURIER_WIDTH), BF16)],
        compiler_params=_cparams("arbitrary", "arbitrary"),
        name="fourier_mix",
    )(uf, gate.reshape(b, FFT_N2, n1_len, FOURIER_WIDTH), cc, sc, wf_half, g, m2)
    return out.reshape(b, n, FOURIER_WIDTH)


def _ctx_fourier_kernel(uf_ref, gate_ref, cc_ref, sc_ref, wf_ref, dft_ref, o_ref, *, n_seq):
    mix = _channel_mix_matrix(cc_ref, sc_ref, wf_ref, (n_seq * FOURIER_GROUP_DIM) ** -0.5)
    z = _dot(uf_ref[0], mix)
    rhs = jnp.concatenate([z[:, :LANES], z[:, LANES:]], axis=0).astype(BF16)
    o_ref[0] = (_dot(dft_ref[...], rhs) * gate_ref[0].astype(F32)).astype(BF16)


def _ctx_fourier_mix(uf, gate, wf_half, cc, sc, dft):
    b, n, _ = uf.shape
    halves = FOURIER_WIDTH // LANES
    full = lambda shape: pl.BlockSpec(shape, lambda bi, hf: (0,) * len(shape))
    half = pl.BlockSpec((1, n, LANES), lambda bi, hf: (bi, 0, hf))
    return pl.pallas_call(
        functools.partial(_ctx_fourier_kernel, n_seq=n),
        grid=(b, halves),
        in_specs=[half, half, full((LANES, LANES)), full((LANES, LANES)),
                  pl.BlockSpec((1, LANES, LANES), lambda bi, hf: (hf, 0, 0)), full(dft.shape)],
        out_specs=half,
        out_shape=jax.ShapeDtypeStruct((b, n, FOURIER_WIDTH), BF16),
        compiler_params=_cparams("arbitrary", "arbitrary"),
        name="context_fourier_mix",
    )(uf, gate, cc, sc, wf_half, dft)


def _ctx_mix_kernel(x_ref, a_ref, f_ref, t_ref, bg_ref, cw_ref, cb_ref, w_ref, g_ref, gate_ref, o_ref,
                    *, ctx_row):
    edge = jnp.zeros((1, CONV_WIDTH), F32)
    o_ref[0] = _mix_rows(a_ref[0], f_ref[0], t_ref[0].astype(F32), edge, edge, bg_ref[0], cw_ref[0],
                         cb_ref[0], w_ref[0], g_ref[0], gate_ref[0, ctx_row:ctx_row + 1, :], x_ref[0])


def _context_mix(x, a, f, t, bg, conv_w, conv_b, w_out, g_post, mod, layer, *, ctx_row):
    b, n, _ = x.shape
    row3 = lambda width: pl.BlockSpec((1, n, width), lambda bi: (bi, 0, 0))
    lay3 = lambda shape: pl.BlockSpec((1,) + shape, lambda bi: (layer, 0, 0))
    return pl.pallas_call(
        functools.partial(_ctx_mix_kernel, ctx_row=ctx_row),
        grid=(b,),
        in_specs=[row3(D_MODEL), row3(ATTN_WIDTH), row3(FOURIER_WIDTH), row3(CONV_WIDTH), row3(CONV_WIDTH), lay3((3, CONV_WIDTH)), lay3((1, CONV_WIDTH)),
                  lay3((MIX_WIDTH, D_MODEL)), lay3((1, D_MODEL)),
                  pl.BlockSpec((1, MOD_ROWS, D_MODEL), lambda bi: (layer, 0, 2))],
        out_specs=row3(D_MODEL),
        out_shape=jax.ShapeDtypeStruct(x.shape, F32),
        compiler_params=_cparams("arbitrary"),
        name="context_mix",
    )(x, a, f, t, bg, conv_w, conv_b, w_out, g_post, mod)


def kernel(x, c, ctx, c_ctx, w_mod, b_mod, g_pre, g_post, w_in, w_out, sink, w_fourier, conv_w, conv_b):
    depth = w_mod.shape[0]
    b, n, _ = x.shape
    nc = ctx.shape[1]
    assert b + 1 <= MOD_ROWS and n % GRID_W == 0 and n % ATTN_ROWS == 0 and n % TILE_ROWS == 0
    assert (n // FFT_N2) % BF16_ROWS == 0 and FFT_N2 % N2_CHUNK == 0

    w_in_b = w_in.astype(BF16)
    w_out_b = w_out.astype(BF16)

    rope_tabs = tuple(jnp.asarray(t) for t in _rope_tables(n))
    cc, sc = (jnp.asarray(m) for m in _channel_dft())
    stage1 = jnp.asarray(_stage1_mats(n)).astype(BF16)
    stage2 = jnp.asarray(_stage2_mat()).astype(BF16)
    ctx_dft = jnp.asarray(_dense_dft(nc)).astype(BF16)
    groups_per_half = LANES // FOURIER_GROUP_DIM
    wf_half = jnp.zeros((depth, FOURIER_GROUPS // groups_per_half, LANES, LANES), F32)
    for g in range(FOURIER_GROUPS):
        o = (g % groups_per_half) * FOURIER_GROUP_DIM
        wf_half = wf_half.at[:, g // groups_per_half, o:o + FOURIER_GROUP_DIM,
                             o:o + FOURIER_GROUP_DIM].set(w_fourier[:, g])

    c_rows = jnp.zeros((MOD_ROWS, D_MODEL), F32).at[:b].set(c).at[b].set(c_ctx)
    mod = _modulation(c_rows, w_mod, b_mod)
    g_pre3 = g_pre.reshape(depth, 1, D_MODEL)
    g_post3 = g_post.reshape(depth, 1, D_MODEL)
    conv_b3 = conv_b.reshape(depth, 1, CONV_WIDTH)

    for l in range(depth):
        update_ctx = l < depth - 1
        if update_ctx:
            qc, kc, vc, sgac, ufc, sgfc, tc, bgc = _in_projection(
                ctx, mod, g_pre3, w_in_b, l, rope_tabs=None, ctx_row=b, tm=nc)
        else:
            kc, vc = _in_projection(ctx, mod, g_pre3, w_in_b, l, rope_tabs=None, ctx_row=b, tm=nc,
                                    kv_only=True)
        q, k, v, sga, uf, sgf, t, bg = _in_projection(
            x, mod, g_pre3, w_in_b, l, rope_tabs=rope_tabs, ctx_row=None, tm=TILE_ROWS)
        f = _fourier_mix(uf, sgf, wf_half[l], (cc, sc, stage1, stage2))
        x = _attention_and_mix(x, q, sga, k, v, kc, vc, sink[l], f, t, bg, conv_w, conv_b3, w_out_b,
                               g_post3, mod, l, tq=ATTN_ROWS)
        if update_ctx:
            ac = _context_attention(qc, sgac, kc, vc, sink[l])
            fc = _ctx_fourier_mix(ufc, sgfc, wf_half[l], cc, sc, ctx_dft)
            ctx = _context_mix(ctx, ac, fc, tc, bgc, conv_w, conv_b3, w_out_b, g_post3, mod, l,
                               ctx_row=b)
    return x
```

```python
import functools
import math

import numpy as np
import jax
import jax.numpy as jnp
from jax import lax
from jax.experimental import pallas as pl
from jax.experimental.pallas import tpu as pltpu

D_MODEL = 1024
GRID_W = 64
HEAD_DIM = 64
ATTN_HEADS = 8
KV_HEADS = 2
Q_PER_KV = ATTN_HEADS // KV_HEADS
ATTN_WIDTH = ATTN_HEADS * HEAD_DIM
KV_WIDTH = KV_HEADS * HEAD_DIM
WINDOW = 128
FOURIER_GROUPS = 4
FOURIER_GROUP_DIM = 64
FOURIER_WIDTH = FOURIER_GROUPS * FOURIER_GROUP_DIM
CONV_WIDTH = 256
MIX_WIDTH = ATTN_WIDTH + FOURIER_WIDTH + CONV_WIDTH
PROJ_WIDTH = 2 * ATTN_WIDTH + 2 * KV_WIDTH + 2 * FOURIER_WIDTH + 4 * CONV_WIDTH
ROPE_FREQS = HEAD_DIM // 4
ROPE_BASE = 10000.0
NORM_EPS = 1e-6
LOG2E = math.log2(math.e)

C_Q = 0
C_K = C_Q + ATTN_WIDTH
C_V = C_K + KV_WIDTH
C_GA = C_V + KV_WIDTH
C_UF = C_GA + ATTN_WIDTH
C_GF = C_UF + FOURIER_WIDTH
C_ZC = C_GF + FOURIER_WIDTH
C_BC = C_ZC + CONV_WIDTH
C_CC = C_BC + CONV_WIDTH
C_GC = C_CC + CONV_WIDTH

LANES = 128
SUBLANES = 8
BF16_ROWS = 16
VMEM_LIMIT = 52 * 1024 * 1024

MOD_ROWS = 8
BF16 = jnp.bfloat16
F32 = jnp.float32

FFT_N2 = 64
K1_GROUP = SUBLANES
N2_CHUNK = 8
UF_PITCH = FFT_N2 + SUBLANES
SOFTMAX_ROWS = 32

BLOCK_Q = 128
KEY_SPAN = BLOCK_Q + 2 * WINDOW
HEAD_PAIRS = ATTN_WIDTH // LANES
PAIRS_PER_KV = HEAD_PAIRS // KV_HEADS

TILE_ROWS = 1024
PROJ_SUB_ROWS = 512
ATTN_ROWS = 1024
MIX_ROWS = 256


def _silu(x):
    return x / (1.0 + jnp.exp(-x))


def _dot(a, b):
    return jnp.dot(a, b, preferred_element_type=F32)


def _dot_nt(a, b):
    return lax.dot_general(a, b, (((1,), (1,)), ((), ())), preferred_element_type=F32)


def _cparams(*sem):
    return pltpu.CompilerParams(dimension_semantics=sem, vmem_limit_bytes=VMEM_LIMIT)


def _rope_tables(n):
    t = np.arange(n)
    row = (t // GRID_W).astype(np.float64)
    col = (t % GRID_W).astype(np.float64)
    inv = ROPE_BASE ** (-np.arange(ROPE_FREQS, dtype=np.float64) / ROPE_FREQS)
    ar = row[:, None] * inv
    ac = col[:, None] * inv
    z = np.zeros_like(ar)
    cos_h = np.concatenate([np.cos(ar), np.cos(ar), np.cos(ac), np.cos(ac)], axis=1)
    sin_up = np.concatenate([-np.sin(ar), z, -np.sin(ac), z], axis=1)
    sin_dn = np.concatenate([z, np.sin(ar), z, np.sin(ac)], axis=1)
    rep = LANES // HEAD_DIM
    return tuple(np.tile(a, (1, rep)).astype(np.float32) for a in (cos_h, sin_up, sin_dn))


def _channel_dft():
    c = np.arange(FOURIER_GROUP_DIM)
    ang = 2.0 * np.pi * np.outer(c, c) / FOURIER_GROUP_DIM
    eye = np.eye(LANES // FOURIER_GROUP_DIM)
    return (np.kron(eye, np.cos(ang)).astype(np.float32),
            np.kron(eye, np.sin(ang)).astype(np.float32))


def _stage1_mats(n):
    n1_len = n // FFT_N2
    k1 = np.arange(n1_len)[:, None]
    n1 = np.arange(n1_len)[None, :]
    out = np.empty((FFT_N2, 2 * n1_len, 2 * n1_len), np.float32)
    for n2 in range(FFT_N2):
        ang = 2.0 * np.pi * ((k1 * (FFT_N2 * n1 + n2)) % n) / n
        ce, se = np.cos(ang), np.sin(ang)
        by_part = np.block([[ce, se], [-se, ce]]).reshape(2, n1_len // K1_GROUP, K1_GROUP, 2 * n1_len)
        out[n2] = by_part.transpose(1, 0, 2, 3).reshape(2 * n1_len, 2 * n1_len)
    return out


def _stage2_mat():
    k2 = np.arange(FFT_N2)
    ang = 2.0 * np.pi * np.outer(k2, k2) / FFT_N2
    cs = np.stack([np.cos(ang), np.sin(ang)], axis=-1)
    eye = np.eye(K1_GROUP)
    m = np.einsum('knp,rs->krnps', cs, eye)
    return m.reshape(FFT_N2 * K1_GROUP, FFT_N2 * 2 * K1_GROUP).astype(np.float32)


def _dense_dft(n):
    t = np.arange(n)
    ang = 2.0 * np.pi * (np.outer(t, t) % n) / n
    return np.concatenate([np.cos(ang), np.sin(ang)], axis=1).astype(np.float32)


def _mod_kernel(c_ref, w_ref, b_ref, o_ref):
    o_ref[0] = _dot(_silu(c_ref[...]), w_ref[0]) + b_ref[0]


def _modulation(c_rows, w_mod, b_mod):
    depth = w_mod.shape[0]
    return pl.pallas_call(
        _mod_kernel,
        grid=(depth, 3),
        in_specs=[pl.BlockSpec((MOD_ROWS, D_MODEL), lambda l, j: (0, 0)),
                  pl.BlockSpec((1, D_MODEL, D_MODEL), lambda l, j: (l, 0, j)),
                  pl.BlockSpec((1, 1, D_MODEL), lambda l, j: (l, 0, j))],
        out_specs=pl.BlockSpec((1, MOD_ROWS, D_MODEL), lambda l, j: (l, 0, j)),
        out_shape=jax.ShapeDtypeStruct((depth, MOD_ROWS, 3 * D_MODEL), F32),
        compiler_params=_cparams("arbitrary", "arbitrary"),
        name="modulation",
    )(c_rows, w_mod, b_mod.reshape(depth, 1, 3 * D_MODEL))


def _inproj_kernel(*refs, rope, ctx_row, kv_only, tm):
    x_ref, shift_ref, scale_ref, g_ref, w_ref = refs[:5]
    cos_ref, sup_ref, sdn_ref = refs[5:8] if rope else (None,) * 3
    outs = refs[8:-1] if rope else refs[5:]
    uf_s = refs[-1] if rope else None
    row = pl.program_id(0) if ctx_row is None else ctx_row
    shift = shift_ref[0, pl.ds(row, 1), :]
    gain = g_ref[0] * (1.0 + scale_ref[0, pl.ds(row, 1), :])

    def with_swapped_heads(y):
        return jnp.concatenate([y, pltpu.roll(y, HEAD_DIM, 1)], axis=1).astype(BF16)

    sub = min(tm, PROJ_SUB_ROWS)
    for s in range(tm // sub):
        rows = slice(s * sub, (s + 1) * sub)
        x = x_ref[0, rows]
        r = lax.rsqrt(jnp.mean(x * x, axis=-1, keepdims=True) + NORM_EPS)
        h = ((x * r) * gain + shift).astype(BF16)

        def proj(c0, width):
            return _dot(h, w_ref[0, :, c0:c0 + width])

        def rotate(y):
            if not rope:
                return y
            return (y * cos_ref[rows] + pltpu.roll(y, LANES - ROPE_FREQS, 1) * sup_ref[rows]
                    + pltpu.roll(y, ROPE_FREQS, 1) * sdn_ref[rows])

        kv = proj(C_K, 2 * KV_WIDTH)
        k_out = with_swapped_heads(rotate(kv[:, :KV_WIDTH]))
        v_out = with_swapped_heads(kv[:, KV_WIDTH:])
        if kv_only:
            k_ref, v_ref = outs
            k_ref[0, rows] = k_out
            v_ref[0, rows] = v_out
            continue
        q_ref, k_ref, v_ref, sga_ref, uf_ref, sgf_ref, t_ref, bg_ref = outs
        k_ref[0, rows] = k_out
        v_ref[0, rows] = v_out
        q = proj(C_Q, ATTN_WIDTH)
        for p in range(HEAD_PAIRS):
            sl = slice(p * LANES, (p + 1) * LANES)
            q_ref[0, rows, sl] = (rotate(q[:, sl]) * (HEAD_DIM ** -0.5 * LOG2E)).astype(BF16)
        sga_ref[0, rows] = _silu(proj(C_GA, ATTN_WIDTH)).astype(BF16)
        uf = proj(C_UF, FOURIER_WIDTH)
        if rope:
            per_sub = sub // FFT_N2
            for slab in range(FOURIER_WIDTH // LANES):
                for j in range(per_sub):
                    uf_s[s, slab, j * UF_PITCH:j * UF_PITCH + FFT_N2] = (
                        uf[j * FFT_N2:(j + 1) * FFT_N2, slab * LANES:(slab + 1) * LANES])
            for n2 in range(FFT_N2):
                uf_ref[0, n2, s * per_sub:(s + 1) * per_sub, :] = jnp.concatenate(
                    [uf_s[s, slab, pl.ds(n2, per_sub, stride=UF_PITCH), :]
                     for slab in range(FOURIER_WIDTH // LANES)], axis=1)
        else:
            uf_ref[0, rows] = uf.astype(BF16)
        sgf_ref[0, rows] = _silu(proj(C_GF, FOURIER_WIDTH)).astype(BF16)
        t_ref[0, rows] = (proj(C_CC, CONV_WIDTH) * proj(C_ZC, CONV_WIDTH)).astype(BF16)
        bg_ref[0, rows] = (proj(C_BC, CONV_WIDTH) * _silu(proj(C_GC, CONV_WIDTH))).astype(BF16)


def _in_projection(x, mod, g_pre, w_in, layer, *, rope_tabs, ctx_row, tm, kv_only=False):
    b, n, _ = x.shape
    rope = rope_tabs is not None
    row3 = lambda width: pl.BlockSpec((1, tm, width), lambda bi, i: (bi, i, 0))
    in_specs = [row3(D_MODEL),
                pl.BlockSpec((1, MOD_ROWS, D_MODEL), lambda bi, i: (layer, 0, 0)),
                pl.BlockSpec((1, MOD_ROWS, D_MODEL), lambda bi, i: (layer, 0, 1)),
                pl.BlockSpec((1, 1, D_MODEL), lambda bi, i: (layer, 0, 0)),
                pl.BlockSpec((1, D_MODEL, PROJ_WIDTH), lambda bi, i: (layer, 0, 0))]
    args = [x, mod, mod, g_pre, w_in]
    if rope:
        in_specs += [pl.BlockSpec((tm, LANES), lambda bi, i: (i, 0))] * 3
        args += list(rope_tabs)
    widths = (ATTN_WIDTH, 2 * KV_WIDTH, 2 * KV_WIDTH, ATTN_WIDTH, FOURIER_WIDTH, FOURIER_WIDTH,
              CONV_WIDTH, CONV_WIDTH)
    if kv_only:
        widths = widths[1:3]
    out_specs = [row3(w) for w in widths]
    out_shape = [jax.ShapeDtypeStruct((b, n, w), BF16) for w in widths]
    scratch = []
    if rope:
        sub = min(tm, PROJ_SUB_ROWS)
        assert sub % (FFT_N2 * SUBLANES) == 0
        out_specs[4] = pl.BlockSpec((1, FFT_N2, tm // FFT_N2, FOURIER_WIDTH), lambda bi, i: (bi, 0, i, 0))
        out_shape[4] = jax.ShapeDtypeStruct((b, FFT_N2, n // FFT_N2, FOURIER_WIDTH), F32)
        scratch = [pltpu.VMEM((tm // sub, FOURIER_WIDTH // LANES, sub // FFT_N2 * UF_PITCH, LANES), F32)]
    return pl.pallas_call(
        functools.partial(_inproj_kernel, rope=rope, ctx_row=ctx_row, kv_only=kv_only, tm=tm),
        grid=(b, n // tm),
        in_specs=in_specs,
        out_specs=out_specs,
        out_shape=out_shape,
        scratch_shapes=scratch,
        compiler_params=_cparams("arbitrary", "arbitrary"),
        name="in_projection_rope" if rope else ("in_projection_ctx_kv" if kv_only else "in_projection_ctx"),
    )(*args)


def _lane_half_variants(blk, fill):
    straight, swapped = blk[:, :KV_WIDTH], blk[:, KV_WIDTH:]
    lo = lax.broadcasted_iota(jnp.int32, straight.shape, 1) < HEAD_DIM
    other = jnp.full_like(straight, fill)
    return (jnp.where(lo, straight, other), jnp.where(lo, other, swapped),
            jnp.where(lo, swapped, other), jnp.where(lo, other, straight))


def _transposed_key_variants(blk):
    straight_t = blk[:, :KV_WIDTH].astype(F32).T
    swapped_t = blk[:, KV_WIDTH:].astype(F32).T
    lo = lax.broadcasted_iota(jnp.int32, straight_t.shape, 0) < HEAD_DIM
    pick = lambda a, b: jnp.where(lo, a, b).astype(BF16)
    return pick(straight_t, 0.0), pick(0.0, swapped_t), pick(swapped_t, 0.0), pick(0.0, straight_t)


def _merge_head_pair(pv_lo, pv_hi, sink_lo, sink_hi):
    lane_lo = lax.broadcasted_iota(jnp.int32, pv_lo.shape, 1) < HEAD_DIM
    num = jnp.where(lane_lo, pv_lo, pv_hi)
    den = pltpu.roll(jnp.where(lane_lo, pv_hi, pv_lo), HEAD_DIM, 1) + jnp.where(lane_lo, sink_lo, sink_hi)
    return num / den


def _mix_rows(a, f, t, above, below, bg, cw, cb, w, g_post, gate, x):
    n_rows = t.shape[0]
    ridx = lax.broadcasted_iota(jnp.int32, t.shape, 0)
    up = jnp.where(ridx == 0, above, pltpu.roll(t, 1, 0))
    dn = jnp.where(ridx == n_rows - 1, below, pltpu.roll(t, n_rows - 1, 0))
    conv = up * cw[0:1] + t * cw[1:2] + dn * cw[2:3] + cb
    h = jnp.concatenate([a, f, (conv * bg.astype(F32)).astype(BF16)], axis=1)
    y = _dot(h, w)
    r = lax.rsqrt(jnp.mean(y * y, axis=-1, keepdims=True) + NORM_EPS)
    return x + (y * r) * (gate * g_post)


def _attn_kernel(sink_ref, q_ref, g_ref, kp_ref, km_ref, kn_ref, vp_ref, vm_ref, vn_ref, kc_ref, vc_ref,
                 x_ref, f_ref, t_ref, tp_ref, tn_ref, bg_ref, cw_ref, cb_ref, w_ref, gpost_ref, mgate_ref,
                 o_ref, k_s, v_s, kctx_s, vctx_s, bias_s, s_s, p_s, r_s, a_s, *, tq, n_seq, n_ctx):
    i = pl.program_id(1)
    n_blocks = tq // BLOCK_Q
    for off, kref, vref, rows in ((0, kp_ref, vp_ref, WINDOW), (WINDOW, km_ref, vm_ref, tq),
                                  (WINDOW + tq, kn_ref, vn_ref, WINDOW)):
        for j in range(rows // LANES):
            blk = slice(j * LANES, (j + 1) * LANES)
            for idx, kk in enumerate(_transposed_key_variants(kref[0, blk, :])):
                k_s[idx, :, off + j * LANES:off + (j + 1) * LANES] = kk
        for idx, vv in enumerate(_lane_half_variants(vref[0], 1.0)):
            v_s[idx, off:off + rows] = vv
    for j in range(n_ctx // LANES):
        blk = slice(j * LANES, (j + 1) * LANES)
        for idx, kk in enumerate(_transposed_key_variants(kc_ref[0, blk, :])):
            kctx_s[idx, :, blk] = kk
    for idx, vv in enumerate(_lane_half_variants(vc_ref[0], 1.0)):
        vctx_s[idx] = vv

    ii = lax.broadcasted_iota(jnp.int32, (BLOCK_Q, WINDOW), 0)
    jj = lax.broadcasted_iota(jnp.int32, (BLOCK_Q, WINDOW), 1)
    head_band = jnp.where(jj >= ii, 0.0, -jnp.inf)
    tail_band = jnp.where(jj <= ii, 0.0, -jnp.inf)
    bias_s[0] = head_band
    bias_s[1] = tail_band
    bias_s[2] = jnp.where(i == 0, -jnp.inf, head_band)
    bias_s[3] = jnp.where(i == pl.num_programs(1) - 1, -jnp.inf, tail_band)

    head_w = n_ctx + KEY_SPAN

    def scores(sb):
        par, r0 = sb % 2, sb * BLOCK_Q
        q = q_ref[0, r0:r0 + BLOCK_Q, :]
        for kv in range(KV_HEADS):
            pairs = range(kv * PAIRS_PER_KV, (kv + 1) * PAIRS_PER_KV)
            qg = jnp.concatenate([q[:, p * LANES:(p + 1) * LANES] for p in pairs], axis=0)
            keys_t = jnp.concatenate([kctx_s[2 * kv], k_s[2 * kv, :, r0:r0 + KEY_SPAN],
                                      kctx_s[2 * kv + 1], k_s[2 * kv + 1, :, r0:r0 + KEY_SPAN]], axis=1)
            s_s[par, pairs.start:pairs.stop] = _dot(qg, keys_t).reshape(PAIRS_PER_KV, BLOCK_Q, 2 * head_w)

    def softmax(sb):
        par = sb % 2
        head_bias = 2 if sb == 0 else 0
        tail_bias = 3 if sb == n_blocks - 1 else 1
        n_cols = head_w // LANES
        for p in range(HEAD_PAIRS):
            for half in range(2):
                h = 2 * p + half
                sink2 = sink_ref[h] * LOG2E
                for rs in range(BLOCK_Q // SOFTMAX_ROWS):
                    rows = slice(rs * SOFTMAX_ROWS, (rs + 1) * SOFTMAX_ROWS)
                    cols = [s_s[par, p, rows, half * head_w + t * LANES:half * head_w + (t + 1) * LANES]
                            for t in range(n_cols)]
                    first_local = n_ctx // LANES
                    cols[first_local] = cols[first_local] + bias_s[head_bias, rows]
                    cols[-1] = cols[-1] + bias_s[tail_bias, rows]
                    m = jnp.maximum(jnp.max(functools.reduce(jnp.maximum, cols), axis=-1, keepdims=True),
                                    sink2)
                    for t, col in enumerate(cols):
                        p_s[par, p, rows, half * head_w + t * LANES:half * head_w + (t + 1) * LANES] = (
                            jnp.exp2(col - m).astype(BF16))
                    r_s[par, h, rows] = jnp.broadcast_to(jnp.exp2(sink2 - m), (SOFTMAX_ROWS, LANES))

    def weighted_values(sb):
        par, r0 = sb % 2, sb * BLOCK_Q
        for kv in range(KV_HEADS):
            pairs = range(kv * PAIRS_PER_KV, (kv + 1) * PAIRS_PER_KV)
            pv = []
            for half in range(2):
                vals = jnp.concatenate([vctx_s[2 * kv + half], v_s[2 * kv + half, r0:r0 + KEY_SPAN]], axis=0)
                probs = p_s[par, pairs.start:pairs.stop, :, half * head_w:(half + 1) * head_w]
                pv.append(_dot(probs.reshape(PAIRS_PER_KV * BLOCK_Q, head_w), vals))
            for j, p in enumerate(pairs):
                rows = slice(j * BLOCK_Q, (j + 1) * BLOCK_Q)
                out = _merge_head_pair(pv[0][rows], pv[1][rows], r_s[par, 2 * p], r_s[par, 2 * p + 1])
                gate = g_ref[0, r0:r0 + BLOCK_Q, p * LANES:(p + 1) * LANES].astype(F32)
                a_s[r0:r0 + BLOCK_Q, p * LANES:(p + 1) * LANES] = (out * gate).astype(BF16)

    def conv_edge(ref, row, keep):
        return jnp.where(keep, ref[0].astype(F32)[row:row + 1, :], 0.0)

    def mix(r0):
        rows = slice(r0, r0 + MIX_ROWS)
        if r0 == 0:
            above = conv_edge(tp_ref, BF16_ROWS - 1, i > 0)
        else:
            above = t_ref[0, r0 - BF16_ROWS:r0, :].astype(F32)[BF16_ROWS - 1:, :]
        if r0 + MIX_ROWS == tq:
            below = conv_edge(tn_ref, 0, i < pl.num_programs(1) - 1)
        else:
            below = t_ref[0, r0 + MIX_ROWS:r0 + MIX_ROWS + BF16_ROWS, :].astype(F32)[:1, :]
        o_ref[0, rows] = _mix_rows(
            a_s[rows], f_ref[0, rows],
            t_ref[0, rows].astype(F32), above, below, bg_ref[0, rows], cw_ref[0], cb_ref[0],
            w_ref[0], gpost_ref[0], mgate_ref[0, pl.ds(pl.program_id(0), 1), :], x_ref[0, rows])

    per_mix = MIX_ROWS // BLOCK_Q
    scores(0)
    for sb in range(n_blocks):
        if sb + 1 < n_blocks:
            scores(sb + 1)
        softmax(sb)
        weighted_values(sb)
        if (sb + 1) % per_mix == 0:
            mix((sb + 1 - per_mix) * BLOCK_Q)


def _attention_and_mix(x, q, gate, k, v, kc, vc, sink, f, t, bg, conv_w, conv_b, w_out, g_post, mod,
                       layer, *, tq):
    b, n, _ = q.shape
    nc = kc.shape[1]
    per = tq // WINDOW
    last = n // WINDOW - 1
    kvw = 2 * KV_WIDTH
    variants = 2 * KV_HEADS
    main = lambda width: pl.BlockSpec((1, tq, width), lambda bi, i: (bi, i, 0))
    prev = pl.BlockSpec((1, WINDOW, kvw), lambda bi, i: (bi, jnp.maximum(i * per - 1, 0), 0))
    nxt = pl.BlockSpec((1, WINDOW, kvw), lambda bi, i: (bi, jnp.minimum((i + 1) * per, last), 0))
    ctx = pl.BlockSpec((1, nc, kvw), lambda bi, i: (bi, 0, 0))
    lay3 = lambda shape: pl.BlockSpec((1,) + shape, lambda bi, i: (layer, 0, 0))
    t_per = tq // BF16_ROWS
    t_last = n // BF16_ROWS - 1
    t_prev = pl.BlockSpec((1, BF16_ROWS, CONV_WIDTH), lambda bi, i: (bi, jnp.maximum(i * t_per - 1, 0), 0))
    t_next = pl.BlockSpec((1, BF16_ROWS, CONV_WIDTH),
                          lambda bi, i: (bi, jnp.minimum((i + 1) * t_per, t_last), 0))
    span = tq + 2 * WINDOW
    s_cols = 2 * (nc + KEY_SPAN)
    return pl.pallas_call(
        functools.partial(_attn_kernel, tq=tq, n_seq=n, n_ctx=nc),
        grid=(b, n // tq),
        in_specs=[pl.BlockSpec(memory_space=pltpu.SMEM), main(ATTN_WIDTH), main(ATTN_WIDTH),
                  prev, main(kvw), nxt, prev, main(kvw), nxt, ctx, ctx,
                  main(D_MODEL),
                  main(FOURIER_WIDTH), main(CONV_WIDTH), t_prev, t_next, main(CONV_WIDTH),
                  lay3((3, CONV_WIDTH)), lay3((1, CONV_WIDTH)), lay3((MIX_WIDTH, D_MODEL)),
                  lay3((1, D_MODEL)),
                  pl.BlockSpec((1, MOD_ROWS, D_MODEL), lambda bi, i: (layer, 0, 2))],
        out_specs=main(D_MODEL),
        out_shape=jax.ShapeDtypeStruct(x.shape, F32),
        scratch_shapes=[pltpu.VMEM((variants, KV_WIDTH, span), BF16),
                        pltpu.VMEM((variants, span, KV_WIDTH), BF16),
                        pltpu.VMEM((variants, KV_WIDTH, nc), BF16),
                        pltpu.VMEM((variants, nc, KV_WIDTH), BF16),
                        pltpu.VMEM((4, BLOCK_Q, WINDOW), F32),
                        pltpu.VMEM((2, HEAD_PAIRS, BLOCK_Q, s_cols), F32),
                        pltpu.VMEM((2, HEAD_PAIRS, BLOCK_Q, s_cols), BF16),
                        pltpu.VMEM((2, ATTN_HEADS, BLOCK_Q, LANES), F32),
                        pltpu.VMEM((tq, ATTN_WIDTH), BF16)],
        compiler_params=_cparams("arbitrary", "arbitrary"),
        name="attention_and_mix",
    )(sink, q, gate, k, k, k, v, v, v, kc, vc, x, f, t, t, t, bg, conv_w, conv_b, w_out, g_post, mod)


def _ctx_attn_kernel(sink_ref, q_ref, g_ref, kc_ref, vc_ref, o_ref):
    keys = _lane_half_variants(kc_ref[0], 0.0)
    vals = _lane_half_variants(vc_ref[0], 1.0)
    for p in range(HEAD_PAIRS):
        kv = p // PAIRS_PER_KV
        qp = q_ref[0, :, p * LANES:(p + 1) * LANES]
        pv, sink_terms = [], []
        for half in range(2):
            s = _dot_nt(qp, keys[2 * kv + half])
            sink2 = sink_ref[2 * p + half] * LOG2E
            m = jnp.maximum(jnp.max(s, axis=-1, keepdims=True), sink2)
            pv.append(_dot(jnp.exp2(s - m).astype(BF16), vals[2 * kv + half]))
            sink_terms.append(jnp.exp2(sink2 - m))
        out = _merge_head_pair(pv[0], pv[1], sink_terms[0], sink_terms[1])
        gate = g_ref[0, :, p * LANES:(p + 1) * LANES].astype(F32)
        o_ref[0, :, p * LANES:(p + 1) * LANES] = (out * gate).astype(BF16)


def _context_attention(q, gate, kc, vc, sink):
    b, nc, _ = q.shape
    blk = lambda width: pl.BlockSpec((1, nc, width), lambda bi: (bi, 0, 0))
    return pl.pallas_call(
        _ctx_attn_kernel,
        grid=(b,),
        in_specs=[pl.BlockSpec(memory_space=pltpu.SMEM), blk(ATTN_WIDTH), blk(ATTN_WIDTH),
                  blk(2 * KV_WIDTH), blk(2 * KV_WIDTH)],
        out_specs=blk(ATTN_WIDTH),
        out_shape=jax.ShapeDtypeStruct((b, nc, ATTN_WIDTH), BF16),
        compiler_params=_cparams("arbitrary"),
        name="context_attention",
    )(sink, q, gate, kc, vc)


def _channel_mix_matrix(cc_ref, sc_ref, wf_ref, scale):
    wf = wf_ref[0]
    return (jnp.concatenate([_dot(cc_ref[...], wf), -_dot(sc_ref[...], wf)], axis=1) * scale).astype(BF16)


def _fourier_kernel(uf_ref, gate_ref, cc_ref, sc_ref, wf_ref, g_ref, m2_ref, o_ref, mix_s, y_s, *, n_seq):
    n1_len = n_seq // FFT_N2
    chunk = pl.program_id(1)
    halves = FOURIER_WIDTH // LANES

    @pl.when(chunk == 0)
    def _():
        zero = jnp.zeros((LANES, LANES), F32)
        scale = (n_seq * FOURIER_GROUP_DIM) ** -0.5
        re = [_dot(cc_ref[...], wf_ref[h]) * scale for h in range(halves)]
        im = [_dot(sc_ref[...], wf_ref[h]) * -scale for h in range(halves)]
        for h in range(halves):
            row = [re[h] if j == h else zero for j in range(halves)]
            row += [im[h] if j == h else zero for j in range(halves)]
            mix_s[h * LANES:(h + 1) * LANES] = jnp.concatenate(row, axis=1).astype(BF16)

    zs = [_dot(uf_ref[0, u].astype(BF16), mix_s[...]) for u in range(N2_CHUNK)]
    for u, z in enumerate(zs):
        rhs = jnp.concatenate([z[:, :FOURIER_WIDTH], z[:, FOURIER_WIDTH:]], axis=0).astype(BF16)
        y = _dot(g_ref[chunk * N2_CHUNK + u], rhs)
        y_s[chunk * N2_CHUNK + u] = y.astype(BF16).reshape(n1_len // K1_GROUP, 2 * K1_GROUP, FOURIER_WIDTH)

    @pl.when(chunk == pl.num_programs(1) - 1)
    def _():
        per_store = BF16_ROWS // K1_GROUP

        def stage2(t, carry):
            outs = []
            for u in range(per_store):
                rhs = y_s[:, t * per_store + u].reshape(FFT_N2 * 2 * K1_GROUP, FOURIER_WIDTH)
                outs.append(_dot(m2_ref[...], rhs).reshape(FFT_N2, K1_GROUP, FOURIER_WIDTH))
            r0 = pl.multiple_of(t * BF16_ROWS, BF16_ROWS)
            gate = gate_ref[0, :, pl.ds(r0, BF16_ROWS), :].astype(F32)
            o_ref[0, :, pl.ds(r0, BF16_ROWS), :] = (jnp.concatenate(outs, axis=1) * gate).astype(BF16)
            return carry

        lax.fori_loop(0, n1_len // BF16_ROWS, stage2, 0)


def _fourier_mix(uf, gate, wf_half, consts):
    b, _, n1_len, _ = uf.shape
    n = FFT_N2 * n1_len
    cc, sc, g, m2 = consts
    full = lambda shape: pl.BlockSpec(shape, lambda bi, c: (0,) * len(shape))
    whole = pl.BlockSpec((1, FFT_N2, n1_len, FOURIER_WIDTH), lambda bi, c: (bi, 0, 0, 0))
    out = pl.pallas_call(
        functools.partial(_fourier_kernel, n_seq=n),
        grid=(b, FFT_N2 // N2_CHUNK),
        in_specs=[pl.BlockSpec((1, N2_CHUNK, n1_len, FOURIER_WIDTH), lambda bi, c: (bi, c, 0, 0)),
                  whole, full((LANES, LANES)), full((LANES, LANES)), full(wf_half.shape),
                  full(g.shape), full(m2.shape)],
        out_specs=whole,
        out_shape=jax.ShapeDtypeStruct((b, FFT_N2, n1_len, FOURIER_WIDTH), BF16),
        scratch_shapes=[pltpu.VMEM((FOURIER_WIDTH, 2 * FOURIER_WIDTH), BF16),
                        pltpu.VMEM((FFT_N2, n1_len // K1_GROUP, 2 * K1_GROUP, FOURIER_WIDTH), BF16)],
        compiler_params=_cparams("arbitrary", "arbitrary"),
        name="fourier_mix",
    )(uf, gate.reshape(b, FFT_N2, n1_len, FOURIER_WIDTH), cc, sc, wf_half, g, m2)
    return out.reshape(b, n, FOURIER_WIDTH)


def _ctx_fourier_kernel(uf_ref, gate_ref, cc_ref, sc_ref, wf_ref, dft_ref, o_ref, *, n_seq):
    mix = _channel_mix_matrix(cc_ref, sc_ref, wf_ref, (n_seq * FOURIER_GROUP_DIM) ** -0.5)
    z = _dot(uf_ref[0], mix)
    rhs = jnp.concatenate([z[:, :LANES], z[:, LANES:]], axis=0).astype(BF16)
    o_ref[0] = (_dot(dft_ref[...], rhs) * gate_ref[0].astype(F32)).astype(BF16)


def _ctx_fourier_mix(uf, gate, wf_half, cc, sc, dft):
    b, n, _ = uf.shape
    halves = FOURIER_WIDTH // LANES
    full = lambda shape: pl.BlockSpec(shape, lambda bi, hf: (0,) * len(shape))
    half = pl.BlockSpec((1, n, LANES), lambda bi, hf: (bi, 0, hf))
    return pl.pallas_call(
        functools.partial(_ctx_fourier_kernel, n_seq=n),
        grid=(b, halves),
        in_specs=[half, half, full((LANES, LANES)), full((LANES, LANES)),
                  pl.BlockSpec((1, LANES, LANES), lambda bi, hf: (hf, 0, 0)), full(dft.shape)],
        out_specs=half,
        out_shape=jax.ShapeDtypeStruct((b, n, FOURIER_WIDTH), BF16),
        compiler_params=_cparams("arbitrary", "arbitrary"),
        name="context_fourier_mix",
    )(uf, gate, cc, sc, wf_half, dft)


def _ctx_mix_kernel(x_ref, a_ref, f_ref, t_ref, bg_ref, cw_ref, cb_ref, w_ref, g_ref, gate_ref, o_ref,
                    *, ctx_row):
    edge = jnp.zeros((1, CONV_WIDTH), F32)
    o_ref[0] = _mix_rows(a_ref[0], f_ref[0], t_ref[0].astype(F32), edge, edge, bg_ref[0], cw_ref[0],
                         cb_ref[0], w_ref[0], g_ref[0], gate_ref[0, ctx_row:ctx_row + 1, :], x_ref[0])


def _context_mix(x, a, f, t, bg, conv_w, conv_b, w_out, g_post, mod, layer, *, ctx_row):
    b, n, _ = x.shape
    row3 = lambda width: pl.BlockSpec((1, n, width), lambda bi: (bi, 0, 0))
    lay3 = lambda shape: pl.BlockSpec((1,) + shape, lambda bi: (layer, 0, 0))
    return pl.pallas_call(
        functools.partial(_ctx_mix_kernel, ctx_row=ctx_row),
        grid=(b,),
        in_specs=[row3(D_MODEL), row3(ATTN_WIDTH), row3(FOURIER_WIDTH), row3(CONV_WIDTH), row3(CONV_WIDTH), lay3((3, CONV_WIDTH)), lay3((1, CONV_WIDTH)),
                  lay3((MIX_WIDTH, D_MODEL)), lay3((1, D_MODEL)),
                  pl.BlockSpec((1, MOD_ROWS, D_MODEL), lambda bi: (layer, 0, 2))],
        out_specs=row3(D_MODEL),
        out_shape=jax.ShapeDtypeStruct(x.shape, F32),
        compiler_params=_cparams("arbitrary"),
        name="context_mix",
    )(x, a, f, t, bg, conv_w, conv_b, w_out, g_post, mod)


def kernel(x, c, ctx, c_ctx, w_mod, b_mod, g_pre, g_post, w_in, w_out, sink, w_fourier, conv_w, conv_b):
    depth = w_mod.shape[0]
    b, n, _ = x.shape
    nc = ctx.shape[1]
    assert b + 1 <= MOD_ROWS and n % GRID_W == 0 and n % ATTN_ROWS == 0 and n % TILE_ROWS == 0
    assert (n // FFT_N2) % BF16_ROWS == 0 and FFT_N2 % N2_CHUNK == 0

    w_in_b = w_in.astype(BF16)
    w_out_b = w_out.astype(BF16)

    rope_tabs = tuple(jnp.asarray(t) for t in _rope_tables(n))
    cc, sc = (jnp.asarray(m) for m in _channel_dft())
    stage1 = jnp.asarray(_stage1_mats(n)).astype(BF16)
    stage2 = jnp.asarray(_stage2_mat()).astype(BF16)
    ctx_dft = jnp.asarray(_dense_dft(nc)).astype(BF16)
    groups_per_half = LANES // FOURIER_GROUP_DIM
    wf_half = jnp.zeros((depth, FOURIER_GROUPS // groups_per_half, LANES, LANES), F32)
    for g in range(FOURIER_GROUPS):
        o = (g % groups_per_half) * FOURIER_GROUP_DIM
        wf_half = wf_half.at[:, g // groups_per_half, o:o + FOURIER_GROUP_DIM,
                             o:o + FOURIER_GROUP_DIM].set(w_fourier[:, g])

    c_rows = jnp.zeros((MOD_ROWS, D_MODEL), F32).at[:b].set(c).at[b].set(c_ctx)
    mod = _modulation(c_rows, w_mod, b_mod)
    g_pre3 = g_pre.reshape(depth, 1, D_MODEL)
    g_post3 = g_post.reshape(depth, 1, D_MODEL)
    conv_b3 = conv_b.reshape(depth, 1, CONV_WIDTH)

    for l in range(depth):
        update_ctx = l < depth - 1
        if update_ctx:
            qc, kc, vc, sgac, ufc, sgfc, tc, bgc = _in_projection(
                ctx, mod, g_pre3, w_in_b, l, rope_tabs=None, ctx_row=b, tm=nc)
        else:
            kc, vc = _in_projection(ctx, mod, g_pre3, w_in_b, l, rope_tabs=None, ctx_row=b, tm=nc,
                                    kv_only=True)
        q, k, v, sga, uf, sgf, t, bg = _in_projection(
            x, mod, g_pre3, w_in_b, l, rope_tabs=rope_tabs, ctx_row=None, tm=TILE_ROWS)
        f = _fourier_mix(uf, sgf, wf_half[l], (cc, sc, stage1, stage2))
        x = _attention_and_mix(x, q, sga, k, v, kc, vc, sink[l], f, t, bg, conv_w, conv_b3, w_out_b,
                               g_post3, mod, l, tq=ATTN_ROWS)
        if update_ctx:
            ac = _context_attention(qc, sgac, kc, vc, sink[l])
            fc = _ctx_fourier_mix(ufc, sgfc, wf_half[l], cc, sc, ctx_dft)
            ctx = _context_mix(ctx, ac, fc, tc, bgc, conv_w, conv_b3, w_out_b, g_post3, mod, l,
                               ctx_row=b)
    return x
```

```python
import functools
import math

import numpy as np
import jax
import jax.numpy as jnp
from jax import lax
from jax.experimental import pallas as pl
from jax.experimental.pallas import tpu as pltpu

D_MODEL = 1024
GRID_W = 64
HEAD_DIM = 64
ATTN_HEADS = 8
KV_HEADS = 2
Q_PER_KV = ATTN_HEADS // KV_HEADS
ATTN_WIDTH = ATTN_HEADS * HEAD_DIM
KV_WIDTH = KV_HEADS * HEAD_DIM
WINDOW = 128
FOURIER_GROUPS = 4
FOURIER_GROUP_DIM = 64
FOURIER_WIDTH = FOURIER_GROUPS * FOURIER_GROUP_DIM
CONV_WIDTH = 256
MIX_WIDTH = ATTN_WIDTH + FOURIER_WIDTH + CONV_WIDTH
PROJ_WIDTH = 2 * ATTN_WIDTH + 2 * KV_WIDTH + 2 * FOURIER_WIDTH + 4 * CONV_WIDTH
ROPE_FREQS = HEAD_DIM // 4
ROPE_BASE = 10000.0
NORM_EPS = 1e-6
LOG2E = math.log2(math.e)

C_Q = 0
C_K = C_Q + ATTN_WIDTH
C_V = C_K + KV_WIDTH
C_GA = C_V + KV_WIDTH
C_UF = C_GA + ATTN_WIDTH
C_GF = C_UF + FOURIER_WIDTH
C_ZC = C_GF + FOURIER_WIDTH
C_BC = C_ZC + CONV_WIDTH
C_CC = C_BC + CONV_WIDTH
C_GC = C_CC + CONV_WIDTH

LANES = 128
SUBLANES = 8
BF16_ROWS = 16
VMEM_LIMIT = 52 * 1024 * 1024

MOD_ROWS = 8
BF16 = jnp.bfloat16
F32 = jnp.float32

FFT_N2 = 64
K1_GROUP = SUBLANES
N2_CHUNK = 16
UF_PITCH = FFT_N2 + SUBLANES
SOFTMAX_ROWS = 32

BLOCK_Q = 128
KEY_SPAN = BLOCK_Q + 2 * WINDOW
HEAD_PAIRS = ATTN_WIDTH // LANES
PAIRS_PER_KV = HEAD_PAIRS // KV_HEADS

TILE_ROWS = 1024
PROJ_SUB_ROWS = 512
ATTN_ROWS = 1024
MIX_ROWS = 256


def _silu(x):
    return x / (1.0 + jnp.exp(-x))


def _dot(a, b):
    return jnp.dot(a, b, preferred_element_type=F32)


def _dot_nt(a, b):
    return lax.dot_general(a, b, (((1,), (1,)), ((), ())), preferred_element_type=F32)


def _cparams(*sem):
    return pltpu.CompilerParams(dimension_semantics=sem, vmem_limit_bytes=VMEM_LIMIT)


def _rope_tables(n):
    t = np.arange(n)
    row = (t // GRID_W).astype(np.float64)
    col = (t % GRID_W).astype(np.float64)
    inv = ROPE_BASE ** (-np.arange(ROPE_FREQS, dtype=np.float64) / ROPE_FREQS)
    ar = row[:, None] * inv
    ac = col[:, None] * inv
    z = np.zeros_like(ar)
    cos_h = np.concatenate([np.cos(ar), np.cos(ar), np.cos(ac), np.cos(ac)], axis=1)
    sin_up = np.concatenate([-np.sin(ar), z, -np.sin(ac), z], axis=1)
    sin_dn = np.concatenate([z, np.sin(ar), z, np.sin(ac)], axis=1)
    rep = LANES // HEAD_DIM
    return tuple(np.tile(a, (1, rep)).astype(np.float32) for a in (cos_h, sin_up, sin_dn))


def _channel_dft():
    c = np.arange(FOURIER_GROUP_DIM)
    ang = 2.0 * np.pi * np.outer(c, c) / FOURIER_GROUP_DIM
    eye = np.eye(LANES // FOURIER_GROUP_DIM)
    return (np.kron(eye, np.cos(ang)).astype(np.float32),
            np.kron(eye, np.sin(ang)).astype(np.float32))


def _stage1_mats(n):
    n1_len = n // FFT_N2
    k1 = np.arange(n1_len)[:, None]
    n1 = np.arange(n1_len)[None, :]
    out = np.empty((FFT_N2, 2 * n1_len, 2 * n1_len), np.float32)
    for n2 in range(FFT_N2):
        ang = 2.0 * np.pi * ((k1 * (FFT_N2 * n1 + n2)) % n) / n
        ce, se = np.cos(ang), np.sin(ang)
        by_part = np.block([[ce, se], [-se, ce]]).reshape(2, n1_len // K1_GROUP, K1_GROUP, 2 * n1_len)
        out[n2] = by_part.transpose(1, 0, 2, 3).reshape(2 * n1_len, 2 * n1_len)
    return out


def _stage2_mat():
    k2 = np.arange(FFT_N2)
    ang = 2.0 * np.pi * np.outer(k2, k2) / FFT_N2
    cs = np.stack([np.cos(ang), np.sin(ang)], axis=-1)
    eye = np.eye(K1_GROUP)
    m = np.einsum('knp,rs->krnps', cs, eye)
    return m.reshape(FFT_N2 * K1_GROUP, FFT_N2 * 2 * K1_GROUP).astype(np.float32)


def _dense_dft(n):
    t = np.arange(n)
    ang = 2.0 * np.pi * (np.outer(t, t) % n) / n
    return np.concatenate([np.cos(ang), np.sin(ang)], axis=1).astype(np.float32)


def _mod_kernel(c_ref, w_ref, b_ref, o_ref):
    o_ref[0] = _dot(_silu(c_ref[...]), w_ref[0]) + b_ref[0]


def _modulation(c_rows, w_mod, b_mod):
    depth = w_mod.shape[0]
    return pl.pallas_call(
        _mod_kernel,
        grid=(depth, 3),
        in_specs=[pl.BlockSpec((MOD_ROWS, D_MODEL), lambda l, j: (0, 0)),
                  pl.BlockSpec((1, D_MODEL, D_MODEL), lambda l, j: (l, 0, j)),
                  pl.BlockSpec((1, 1, D_MODEL), lambda l, j: (l, 0, j))],
        out_specs=pl.BlockSpec((1, MOD_ROWS, D_MODEL), lambda l, j: (l, 0, j)),
        out_shape=jax.ShapeDtypeStruct((depth, MOD_ROWS, 3 * D_MODEL), F32),
        compiler_params=_cparams("arbitrary", "arbitrary"),
        name="modulation",
    )(c_rows, w_mod, b_mod.reshape(depth, 1, 3 * D_MODEL))


def _inproj_kernel(*refs, rope, ctx_row, kv_only, tm):
    x_ref, shift_ref, scale_ref, g_ref, w_ref = refs[:5]
    cos_ref, sup_ref, sdn_ref = refs[5:8] if rope else (None,) * 3
    outs = refs[8:-1] if rope else refs[5:]
    uf_s = refs[-1] if rope else None
    row = pl.program_id(0) if ctx_row is None else ctx_row
    shift = shift_ref[0, pl.ds(row, 1), :]
    gain = g_ref[0] * (1.0 + scale_ref[0, pl.ds(row, 1), :])

    def with_swapped_heads(y):
        return jnp.concatenate([y, pltpu.roll(y, HEAD_DIM, 1)], axis=1).astype(BF16)

    sub = min(tm, PROJ_SUB_ROWS)
    for s in range(tm // sub):
        rows = slice(s * sub, (s + 1) * sub)
        x = x_ref[0, rows]
        r = lax.rsqrt(jnp.mean(x * x, axis=-1, keepdims=True) + NORM_EPS)
        h = ((x * r) * gain + shift).astype(BF16)

        def proj(c0, width):
            return _dot(h, w_ref[0, :, c0:c0 + width])

        def rotate(y):
            if not rope:
                return y
            return (y * cos_ref[rows] + pltpu.roll(y, LANES - ROPE_FREQS, 1) * sup_ref[rows]
                    + pltpu.roll(y, ROPE_FREQS, 1) * sdn_ref[rows])

        kv = proj(C_K, 2 * KV_WIDTH)
        k_out = with_swapped_heads(rotate(kv[:, :KV_WIDTH]))
        v_out = with_swapped_heads(kv[:, KV_WIDTH:])
        if kv_only:
            k_ref, v_ref = outs
            k_ref[0, rows] = k_out
            v_ref[0, rows] = v_out
            continue
        q_ref, k_ref, v_ref, sga_ref, uf_ref, sgf_ref, t_ref, bg_ref = outs
        k_ref[0, rows] = k_out
        v_ref[0, rows] = v_out
        q = proj(C_Q, ATTN_WIDTH)
        for p in range(HEAD_PAIRS):
            sl = slice(p * LANES, (p + 1) * LANES)
            q_ref[0, rows, sl] = (rotate(q[:, sl]) * (HEAD_DIM ** -0.5 * LOG2E)).astype(BF16)
        sga_ref[0, rows] = _silu(proj(C_GA, ATTN_WIDTH)).astype(BF16)
        uf = proj(C_UF, FOURIER_WIDTH)
        if rope:
            per_sub = sub // FFT_N2
            for slab in range(FOURIER_WIDTH // LANES):
                for j in range(per_sub):
                    uf_s[s, slab, j * UF_PITCH:j * UF_PITCH + FFT_N2] = (
                        uf[j * FFT_N2:(j + 1) * FFT_N2, slab * LANES:(slab + 1) * LANES])
            for n2 in range(FFT_N2):
                uf_ref[0, n2, s * per_sub:(s + 1) * per_sub, :] = jnp.concatenate(
                    [uf_s[s, slab, pl.ds(n2, per_sub, stride=UF_PITCH), :]
                     for slab in range(FOURIER_WIDTH // LANES)], axis=1)
        else:
            uf_ref[0, rows] = uf.astype(BF16)
        sgf_ref[0, rows] = _silu(proj(C_GF, FOURIER_WIDTH)).astype(BF16)
        t_ref[0, rows] = (proj(C_CC, CONV_WIDTH) * proj(C_ZC, CONV_WIDTH)).astype(BF16)
        bg_ref[0, rows] = (proj(C_BC, CONV_WIDTH) * _silu(proj(C_GC, CONV_WIDTH))).astype(BF16)


def _in_projection(x, mod, g_pre, w_in, layer, *, rope_tabs, ctx_row, tm, kv_only=False):
    b, n, _ = x.shape
    rope = rope_tabs is not None
    row3 = lambda width: pl.BlockSpec((1, tm, width), lambda bi, i: (bi, i, 0))
    in_specs = [row3(D_MODEL),
                pl.BlockSpec((1, MOD_ROWS, D_MODEL), lambda bi, i: (layer, 0, 0)),
                pl.BlockSpec((1, MOD_ROWS, D_MODEL), lambda bi, i: (layer, 0, 1)),
                pl.BlockSpec((1, 1, D_MODEL), lambda bi, i: (layer, 0, 0)),
                pl.BlockSpec((1, D_MODEL, PROJ_WIDTH), lambda bi, i: (layer, 0, 0))]
    args = [x, mod, mod, g_pre, w_in]
    if rope:
        in_specs += [pl.BlockSpec((tm, LANES), lambda bi, i: (i, 0))] * 3
        args += list(rope_tabs)
    widths = (ATTN_WIDTH, 2 * KV_WIDTH, 2 * KV_WIDTH, ATTN_WIDTH, FOURIER_WIDTH, FOURIER_WIDTH,
              CONV_WIDTH, CONV_WIDTH)
    if kv_only:
        widths = widths[1:3]
    out_specs = [row3(w) for w in widths]
    out_shape = [jax.ShapeDtypeStruct((b, n, w), BF16) for w in widths]
    scratch = []
    if rope:
        sub = min(tm, PROJ_SUB_ROWS)
        assert sub % (FFT_N2 * SUBLANES) == 0
        out_specs[4] = pl.BlockSpec((1, FFT_N2, tm // FFT_N2, FOURIER_WIDTH), lambda bi, i: (bi, 0, i, 0))
        out_shape[4] = jax.ShapeDtypeStruct((b, FFT_N2, n // FFT_N2, FOURIER_WIDTH), F32)
        scratch = [pltpu.VMEM((tm // sub, FOURIER_WIDTH // LANES, sub // FFT_N2 * UF_PITCH, LANES), F32)]
    return pl.pallas_call(
        functools.partial(_inproj_kernel, rope=rope, ctx_row=ctx_row, kv_only=kv_only, tm=tm),
        grid=(b, n // tm),
        in_specs=in_specs,
        out_specs=out_specs,
        out_shape=out_shape,
        scratch_shapes=scratch,
        compiler_params=_cparams("arbitrary", "arbitrary"),
        name="in_projection_rope" if rope else ("in_projection_ctx_kv" if kv_only else "in_projection_ctx"),
    )(*args)


def _lane_half_variants(blk, fill):
    straight, swapped = blk[:, :KV_WIDTH], blk[:, KV_WIDTH:]
    lo = lax.broadcasted_iota(jnp.int32, straight.shape, 1) < HEAD_DIM
    other = jnp.full_like(straight, fill)
    return (jnp.where(lo, straight, other), jnp.where(lo, other, swapped),
            jnp.where(lo, swapped, other), jnp.where(lo, other, straight))


def _merge_head_pair(pv_lo, pv_hi, sink_lo, sink_hi):
    lane_lo = lax.broadcasted_iota(jnp.int32, pv_lo.shape, 1) < HEAD_DIM
    num = jnp.where(lane_lo, pv_lo, pv_hi)
    den = pltpu.roll(jnp.where(lane_lo, pv_hi, pv_lo), HEAD_DIM, 1) + jnp.where(lane_lo, sink_lo, sink_hi)
    return num / den


def _mix_rows(a, f, t, above, below, bg, cw, cb, w, g_post, gate, x):
    n_rows = t.shape[0]
    ridx = lax.broadcasted_iota(jnp.int32, t.shape, 0)
    up = jnp.where(ridx == 0, above, pltpu.roll(t, 1, 0))
    dn = jnp.where(ridx == n_rows - 1, below, pltpu.roll(t, n_rows - 1, 0))
    conv = up * cw[0:1] + t * cw[1:2] + dn * cw[2:3] + cb
    h = jnp.concatenate([a, f, (conv * bg.astype(F32)).astype(BF16)], axis=1)
    y = _dot(h, w)
    r = lax.rsqrt(jnp.mean(y * y, axis=-1, keepdims=True) + NORM_EPS)
    return x + (y * r) * (gate * g_post)


def _attn_kernel(sink_ref, q_ref, g_ref, kp_ref, km_ref, kn_ref, vp_ref, vm_ref, vn_ref, kc_ref, vc_ref,
                 x_ref, f_ref, t_ref, tp_ref, tn_ref, bg_ref, cw_ref, cb_ref, w_ref, gpost_ref, mgate_ref,
                 o_ref, k_s, v_s, kctx_s, vctx_s, bias_s, s_s, p_s, r_s, a_s, *, tq, n_seq, n_ctx):
    i = pl.program_id(1)
    n_blocks = tq // BLOCK_Q
    for off, kref, vref, rows in ((0, kp_ref, vp_ref, WINDOW), (WINDOW, km_ref, vm_ref, tq),
                                  (WINDOW + tq, kn_ref, vn_ref, WINDOW)):
        for idx, kk in enumerate(_lane_half_variants(kref[0], 0.0)):
            k_s[idx, off:off + rows] = kk
        for idx, vv in enumerate(_lane_half_variants(vref[0], 1.0)):
            v_s[idx, off:off + rows] = vv
    for idx, kk in enumerate(_lane_half_variants(kc_ref[0], 0.0)):
        kctx_s[idx] = kk
    for idx, vv in enumerate(_lane_half_variants(vc_ref[0], 1.0)):
        vctx_s[idx] = vv

    ii = lax.broadcasted_iota(jnp.int32, (BLOCK_Q, WINDOW), 0)
    jj = lax.broadcasted_iota(jnp.int32, (BLOCK_Q, WINDOW), 1)
    head_band = jnp.where(jj >= ii, 0.0, -jnp.inf)
    tail_band = jnp.where(jj <= ii, 0.0, -jnp.inf)
    bias_s[0] = head_band
    bias_s[1] = tail_band
    bias_s[2] = jnp.where(i == 0, -jnp.inf, head_band)
    bias_s[3] = jnp.where(i == pl.num_programs(1) - 1, -jnp.inf, tail_band)

    head_w = n_ctx + KEY_SPAN

    def scores(sb):
        par, r0 = sb % 2, sb * BLOCK_Q
        q = q_ref[0, r0:r0 + BLOCK_Q, :]
        for kv in range(KV_HEADS):
            pairs = range(kv * PAIRS_PER_KV, (kv + 1) * PAIRS_PER_KV)
            qg = jnp.concatenate([q[:, p * LANES:(p + 1) * LANES] for p in pairs], axis=0)
            keys = jnp.concatenate([kctx_s[2 * kv], k_s[2 * kv, r0:r0 + KEY_SPAN],
                                    kctx_s[2 * kv + 1], k_s[2 * kv + 1, r0:r0 + KEY_SPAN]], axis=0)
            s_s[par, pairs.start:pairs.stop] = _dot_nt(qg, keys).reshape(PAIRS_PER_KV, BLOCK_Q, 2 * head_w)

    def softmax(sb):
        par = sb % 2
        head_bias = 2 if sb == 0 else 0
        tail_bias = 3 if sb == n_blocks - 1 else 1
        n_cols = head_w // LANES
        for p in range(HEAD_PAIRS):
            for half in range(2):
                h = 2 * p + half
                sink2 = sink_ref[h] * LOG2E
                for rs in range(BLOCK_Q // SOFTMAX_ROWS):
                    rows = slice(rs * SOFTMAX_ROWS, (rs + 1) * SOFTMAX_ROWS)
                    cols = [s_s[par, p, rows, half * head_w + t * LANES:half * head_w + (t + 1) * LANES]
                            for t in range(n_cols)]
                    first_local = n_ctx // LANES
                    cols[first_local] = cols[first_local] + bias_s[head_bias, rows]
                    cols[-1] = cols[-1] + bias_s[tail_bias, rows]
                    m = jnp.maximum(jnp.max(functools.reduce(jnp.maximum, cols), axis=-1, keepdims=True),
                                    sink2)
                    for t, col in enumerate(cols):
                        p_s[par, p, rows, half * head_w + t * LANES:half * head_w + (t + 1) * LANES] = (
                            jnp.exp2((col - m).astype(BF16)))
                    r_s[par, h, rows] = jnp.broadcast_to(jnp.exp2(sink2 - m), (SOFTMAX_ROWS, LANES))

    def weighted_values(sb):
        par, r0 = sb % 2, sb * BLOCK_Q
        for kv in range(KV_HEADS):
            pairs = range(kv * PAIRS_PER_KV, (kv + 1) * PAIRS_PER_KV)
            pv = []
            for half in range(2):
                vals = jnp.concatenate([vctx_s[2 * kv + half], v_s[2 * kv + half, r0:r0 + KEY_SPAN]], axis=0)
                probs = p_s[par, pairs.start:pairs.stop, :, half * head_w:(half + 1) * head_w]
                pv.append(_dot(probs.reshape(PAIRS_PER_KV * BLOCK_Q, head_w), vals))
            for j, p in enumerate(pairs):
                rows = slice(j * BLOCK_Q, (j + 1) * BLOCK_Q)
                out = _merge_head_pair(pv[0][rows], pv[1][rows], r_s[par, 2 * p], r_s[par, 2 * p + 1])
                gate = g_ref[0, r0:r0 + BLOCK_Q, p * LANES:(p + 1) * LANES].astype(F32)
                a_s[r0:r0 + BLOCK_Q, p * LANES:(p + 1) * LANES] = (out * gate).astype(BF16)

    def conv_edge(ref, row, keep):
        return jnp.where(keep, ref[0].astype(F32)[row:row + 1, :], 0.0)

    def mix(r0):
        rows = slice(r0, r0 + MIX_ROWS)
        if r0 == 0:
            above = conv_edge(tp_ref, BF16_ROWS - 1, i > 0)
        else:
            above = t_ref[0, r0 - BF16_ROWS:r0, :].astype(F32)[BF16_ROWS - 1:, :]
        if r0 + MIX_ROWS == tq:
            below = conv_edge(tn_ref, 0, i < pl.num_programs(1) - 1)
        else:
            below = t_ref[0, r0 + MIX_ROWS:r0 + MIX_ROWS + BF16_ROWS, :].astype(F32)[:1, :]
        o_ref[0, rows] = _mix_rows(
            a_s[rows], f_ref[0, rows],
            t_ref[0, rows].astype(F32), above, below, bg_ref[0, rows], cw_ref[0], cb_ref[0],
            w_ref[0], gpost_ref[0], mgate_ref[0, pl.ds(pl.program_id(0), 1), :], x_ref[0, rows])

    per_mix = MIX_ROWS // BLOCK_Q
    scores(0)
    for sb in range(n_blocks):
        if sb + 1 < n_blocks:
            scores(sb + 1)
        softmax(sb)
        weighted_values(sb)
        if (sb + 1) % per_mix == 0:
            mix((sb + 1 - per_mix) * BLOCK_Q)


def _attention_and_mix(x, q, gate, k, v, kc, vc, sink, f, t, bg, conv_w, conv_b, w_out, g_post, mod,
                       layer, *, tq):
    b, n, _ = q.shape
    nc = kc.shape[1]
    per = tq // WINDOW
    last = n // WINDOW - 1
    kvw = 2 * KV_WIDTH
    variants = 2 * KV_HEADS
    main = lambda width: pl.BlockSpec((1, tq, width), lambda bi, i: (bi, i, 0))
    prev = pl.BlockSpec((1, WINDOW, kvw), lambda bi, i: (bi, jnp.maximum(i * per - 1, 0), 0))
    nxt = pl.BlockSpec((1, WINDOW, kvw), lambda bi, i: (bi, jnp.minimum((i + 1) * per, last), 0))
    ctx = pl.BlockSpec((1, nc, kvw), lambda bi, i: (bi, 0, 0))
    lay3 = lambda shape: pl.BlockSpec((1,) + shape, lambda bi, i: (layer, 0, 0))
    t_per = tq // BF16_ROWS
    t_last = n // BF16_ROWS - 1
    t_prev = pl.BlockSpec((1, BF16_ROWS, CONV_WIDTH), lambda bi, i: (bi, jnp.maximum(i * t_per - 1, 0), 0))
    t_next = pl.BlockSpec((1, BF16_ROWS, CONV_WIDTH),
                          lambda bi, i: (bi, jnp.minimum((i + 1) * t_per, t_last), 0))
    span = tq + 2 * WINDOW
    s_cols = 2 * (nc + KEY_SPAN)
    return pl.pallas_call(
        functools.partial(_attn_kernel, tq=tq, n_seq=n, n_ctx=nc),
        grid=(b, n // tq),
        in_specs=[pl.BlockSpec(memory_space=pltpu.SMEM), main(ATTN_WIDTH), main(ATTN_WIDTH),
                  prev, main(kvw), nxt, prev, main(kvw), nxt, ctx, ctx,
                  main(D_MODEL),
                  main(FOURIER_WIDTH), main(CONV_WIDTH), t_prev, t_next, main(CONV_WIDTH),
                  lay3((3, CONV_WIDTH)), lay3((1, CONV_WIDTH)), lay3((MIX_WIDTH, D_MODEL)),
                  lay3((1, D_MODEL)),
                  pl.BlockSpec((1, MOD_ROWS, D_MODEL), lambda bi, i: (layer, 0, 2))],
        out_specs=main(D_MODEL),
        out_shape=jax.ShapeDtypeStruct(x.shape, F32),
        scratch_shapes=[pltpu.VMEM((variants, span, KV_WIDTH), BF16),
                        pltpu.VMEM((variants, span, KV_WIDTH), BF16),
                        pltpu.VMEM((variants, nc, KV_WIDTH), BF16),
                        pltpu.VMEM((variants, nc, KV_WIDTH), BF16),
                        pltpu.VMEM((4, BLOCK_Q, WINDOW), F32),
                        pltpu.VMEM((2, HEAD_PAIRS, BLOCK_Q, s_cols), F32),
                        pltpu.VMEM((2, HEAD_PAIRS, BLOCK_Q, s_cols), BF16),
                        pltpu.VMEM((2, ATTN_HEADS, BLOCK_Q, LANES), F32),
                        pltpu.VMEM((tq, ATTN_WIDTH), BF16)],
        compiler_params=_cparams("arbitrary", "arbitrary"),
        name="attention_and_mix",
    )(sink, q, gate, k, k, k, v, v, v, kc, vc, x, f, t, t, t, bg, conv_w, conv_b, w_out, g_post, mod)


def _ctx_attn_kernel(sink_ref, q_ref, g_ref, kc_ref, vc_ref, o_ref):
    keys = _lane_half_variants(kc_ref[0], 0.0)
    vals = _lane_half_variants(vc_ref[0], 1.0)
    for p in range(HEAD_PAIRS):
        kv = p // PAIRS_PER_KV
        qp = q_ref[0, :, p * LANES:(p + 1) * LANES]
        pv, sink_terms = [], []
        for half in range(2):
            s = _dot_nt(qp, keys[2 * kv + half])
            sink2 = sink_ref[2 * p + half] * LOG2E
            m = jnp.maximum(jnp.max(s, axis=-1, keepdims=True), sink2)
            pv.append(_dot(jnp.exp2(s - m).astype(BF16), vals[2 * kv + half]))
            sink_terms.append(jnp.exp2(sink2 - m))
        out = _merge_head_pair(pv[0], pv[1], sink_terms[0], sink_terms[1])
        gate = g_ref[0, :, p * LANES:(p + 1) * LANES].astype(F32)
        o_ref[0, :, p * LANES:(p + 1) * LANES] = (out * gate).astype(BF16)


def _context_attention(q, gate, kc, vc, sink):
    b, nc, _ = q.shape
    blk = lambda width: pl.BlockSpec((1, nc, width), lambda bi: (bi, 0, 0))
    return pl.pallas_call(
        _ctx_attn_kernel,
        grid=(b,),
        in_specs=[pl.BlockSpec(memory_space=pltpu.SMEM), blk(ATTN_WIDTH), blk(ATTN_WIDTH),
                  blk(2 * KV_WIDTH), blk(2 * KV_WIDTH)],
        out_specs=blk(ATTN_WIDTH),
        out_shape=jax.ShapeDtypeStruct((b, nc, ATTN_WIDTH), BF16),
        compiler_params=_cparams("arbitrary"),
        name="context_attention",
    )(sink, q, gate, kc, vc)


def _channel_mix_matrix(cc_ref, sc_ref, wf_ref, scale):
    wf = wf_ref[0]
    return (jnp.concatenate([_dot(cc_ref[...], wf), -_dot(sc_ref[...], wf)], axis=1) * scale).astype(BF16)


def _fourier_kernel(uf_ref, gate_ref, cc_ref, sc_ref, wf_ref, g_ref, m2_ref, o_ref, mix_s, y_s, *, n_seq):
    n1_len = n_seq // FFT_N2
    chunk = pl.program_id(1)
    halves = FOURIER_WIDTH // LANES

    @pl.when(chunk == 0)
    def _():
        zero = jnp.zeros((LANES, LANES), F32)
        scale = (n_seq * FOURIER_GROUP_DIM) ** -0.5
        re = [_dot(cc_ref[...], wf_ref[h]) * scale for h in range(halves)]
        im = [_dot(sc_ref[...], wf_ref[h]) * -scale for h in range(halves)]
        for h in range(halves):
            row = [re[h] if j == h else zero for j in range(halves)]
            row += [im[h] if j == h else zero for j in range(halves)]
            mix_s[h * LANES:(h + 1) * LANES] = jnp.concatenate(row, axis=1).astype(BF16)

    zs = [_dot(uf_ref[0, u].astype(BF16), mix_s[...]) for u in range(N2_CHUNK)]
    for u, z in enumerate(zs):
        rhs = jnp.concatenate([z[:, :FOURIER_WIDTH], z[:, FOURIER_WIDTH:]], axis=0).astype(BF16)
        y = _dot(g_ref[chunk * N2_CHUNK + u], rhs)
        y_s[chunk * N2_CHUNK + u] = y.astype(BF16).reshape(n1_len // K1_GROUP, 2 * K1_GROUP, FOURIER_WIDTH)

    @pl.when(chunk == pl.num_programs(1) - 1)
    def _():
        per_store = BF16_ROWS // K1_GROUP

        def stage2(t, carry):
            outs = []
            for u in range(per_store):
                rhs = y_s[:, t * per_store + u].reshape(FFT_N2 * 2 * K1_GROUP, FOURIER_WIDTH)
                outs.append(_dot(m2_ref[...], rhs).reshape(FFT_N2, K1_GROUP, FOURIER_WIDTH))
            r0 = pl.multiple_of(t * BF16_ROWS, BF16_ROWS)
            gate = gate_ref[0, :, pl.ds(r0, BF16_ROWS), :].astype(F32)
            o_ref[0, :, pl.ds(r0, BF16_ROWS), :] = (jnp.concatenate(outs, axis=1) * gate).astype(BF16)
            return carry

        lax.fori_loop(0, n1_len // BF16_ROWS, stage2, 0)


def _fourier_mix(uf, gate, wf_half, consts):
    b, _, n1_len, _ = uf.shape
    n = FFT_N2 * n1_len
    cc, sc, g, m2 = consts
    full = lambda shape: pl.BlockSpec(shape, lambda bi, c: (0,) * len(shape))
    whole = pl.BlockSpec((1, FFT_N2, n1_len, FOURIER_WIDTH), lambda bi, c: (bi, 0, 0, 0))
    out = pl.pallas_call(
        functools.partial(_fourier_kernel, n_seq=n),
        grid=(b, FFT_N2 // N2_CHUNK),
        in_specs=[pl.BlockSpec((1, N2_CHUNK, n1_len, FOURIER_WIDTH), lambda bi, c: (bi, c, 0, 0)),
                  whole, full((LANES, LANES)), full((LANES, LANES)), full(wf_half.shape),
                  full(g.shape), full(m2.shape)],
        out_specs=whole,
        out_shape=jax.ShapeDtypeStruct((b, FFT_N2, n1_len, FOURIER_WIDTH), BF16),
        scratch_shapes=[pltpu.VMEM((FOURIER_WIDTH, 2 * FOURIER_WIDTH), BF16),
                        pltpu.VMEM((FFT_N2, n1_len // K1_GROUP, 2 * K1_GROUP, FOURIER_WIDTH), BF16)],
        compiler_params=_cparams("arbitrary", "arbitrary"),
        name="fourier_mix",
    )(uf, gate.reshape(b, FFT_N2, n1_len, FOURIER_WIDTH), cc, sc, wf_half, g, m2)
    return out.reshape(b, n, FOURIER_WIDTH)


def _ctx_fourier_kernel(uf_ref, gate_ref, cc_ref, sc_ref, wf_ref, dft_ref, o_ref, *, n_seq):
    mix = _channel_mix_matrix(cc_ref, sc_ref, wf_ref, (n_seq * FOURIER_GROUP_DIM) ** -0.5)
    z = _dot(uf_ref[0], mix)
    rhs = jnp.concatenate([z[:, :LANES], z[:, LANES:]], axis=0).astype(BF16)
    o_ref[0] = (_dot(dft_ref[...], rhs) * gate_ref[0].astype(F32)).astype(BF16)


def _ctx_fourier_mix(uf, gate, wf_half, cc, sc, dft):
    b, n, _ = uf.shape
    halves = FOURIER_WIDTH // LANES
    full = lambda shape: pl.BlockSpec(shape, lambda bi, hf: (0,) * len(shape))
    half = pl.BlockSpec((1, n, LANES), lambda bi, hf: (bi, 0, hf))
    return pl.pallas_call(
        functools.partial(_ctx_fourier_kernel, n_seq=n),
        grid=(b, halves),
        in_specs=[half, half, full((LANES, LANES)), full((LANES, LANES)),
                  pl.BlockSpec((1, LANES, LANES), lambda bi, hf: (hf, 0, 0)), full(dft.shape)],
        out_specs=half,
        out_shape=jax.ShapeDtypeStruct((b, n, FOURIER_WIDTH), BF16),
        compiler_params=_cparams("arbitrary", "arbitrary"),
        name="context_fourier_mix",
    )(uf, gate, cc, sc, wf_half, dft)


def _ctx_mix_kernel(x_ref, a_ref, f_ref, t_ref, bg_ref, cw_ref, cb_ref, w_ref, g_ref, gate_ref, o_ref,
                    *, ctx_row):
    edge = jnp.zeros((1, CONV_WIDTH), F32)
    o_ref[0] = _mix_rows(a_ref[0], f_ref[0], t_ref[0].astype(F32), edge, edge, bg_ref[0], cw_ref[0],
                         cb_ref[0], w_ref[0], g_ref[0], gate_ref[0, ctx_row:ctx_row + 1, :], x_ref[0])


def _context_mix(x, a, f, t, bg, conv_w, conv_b, w_out, g_post, mod, layer, *, ctx_row):
    b, n, _ = x.shape
    row3 = lambda width: pl.BlockSpec((1, n, width), lambda bi: (bi, 0, 0))
    lay3 = lambda shape: pl.BlockSpec((1,) + shape, lambda bi: (layer, 0, 0))
    return pl.pallas_call(
        functools.partial(_ctx_mix_kernel, ctx_row=ctx_row),
        grid=(b,),
        in_specs=[row3(D_MODEL), row3(ATTN_WIDTH), row3(FOURIER_WIDTH), row3(CONV_WIDTH), row3(CONV_WIDTH), lay3((3, CONV_WIDTH)), lay3((1, CONV_WIDTH)),
                  lay3((MIX_WIDTH, D_MODEL)), lay3((1, D_MODEL)),
                  pl.BlockSpec((1, MOD_ROWS, D_MODEL), lambda bi: (layer, 0, 2))],
        out_specs=row3(D_MODEL),
        out_shape=jax.ShapeDtypeStruct(x.shape, F32),
        compiler_params=_cparams("arbitrary"),
        name="context_mix",
    )(x, a, f, t, bg, conv_w, conv_b, w_out, g_post, mod)


def kernel(x, c, ctx, c_ctx, w_mod, b_mod, g_pre, g_post, w_in, w_out, sink, w_fourier, conv_w, conv_b):
    depth = w_mod.shape[0]
    b, n, _ = x.shape
    nc = ctx.shape[1]
    assert b + 1 <= MOD_ROWS and n % GRID_W == 0 and n % ATTN_ROWS == 0 and n % TILE_ROWS == 0
    assert (n // FFT_N2) % BF16_ROWS == 0 and FFT_N2 % N2_CHUNK == 0

    w_in_b = w_in.astype(BF16)
    w_out_b = w_out.astype(BF16)

    rope_tabs = tuple(jnp.asarray(t) for t in _rope_tables(n))
    cc, sc = (jnp.asarray(m) for m in _channel_dft())
    stage1 = jnp.asarray(_stage1_mats(n)).astype(BF16)
    stage2 = jnp.asarray(_stage2_mat()).astype(BF16)
    ctx_dft = jnp.asarray(_dense_dft(nc)).astype(BF16)
    groups_per_half = LANES // FOURIER_GROUP_DIM
    wf_half = jnp.zeros((depth, FOURIER_GROUPS // groups_per_half, LANES, LANES), F32)
    for g in range(FOURIER_GROUPS):
        o = (g % groups_per_half) * FOURIER_GROUP_DIM
        wf_half = wf_half.at[:, g // groups_per_half, o:o + FOURIER_GROUP_DIM,
                             o:o + FOURIER_GROUP_DIM].set(w_fourier[:, g])

    c_rows = jnp.zeros((MOD_ROWS, D_MODEL), F32).at[:b].set(c).at[b].set(c_ctx)
    mod = _modulation(c_rows, w_mod, b_mod)
    g_pre3 = g_pre.reshape(depth, 1, D_MODEL)
    g_post3 = g_post.reshape(depth, 1, D_MODEL)
    conv_b3 = conv_b.reshape(depth, 1, CONV_WIDTH)

    for l in range(depth):
        update_ctx = l < depth - 1
        if update_ctx:
            qc, kc, vc, sgac, ufc, sgfc, tc, bgc = _in_projection(
                ctx, mod, g_pre3, w_in_b, l, rope_tabs=None, ctx_row=b, tm=nc)
        else:
            kc, vc = _in_projection(ctx, mod, g_pre3, w_in_b, l, rope_tabs=None, ctx_row=b, tm=nc,
                                    kv_only=True)
        q, k, v, sga, uf, sgf, t, bg = _in_projection(
            x, mod, g_pre3, w_in_b, l, rope_tabs=rope_tabs, ctx_row=None, tm=TILE_ROWS)
        f = _fourier_mix(uf, sgf, wf_half[l], (cc, sc, stage1, stage2))
        x = _attention_and_mix(x, q, sga, k, v, kc, vc, sink[l], f, t, bg, conv_w, conv_b3, w_out_b,
                               g_post3, mod, l, tq=ATTN_ROWS)
        if update_ctx:
            ac = _context_attention(qc, sgac, kc, vc, sink[l])
            fc = _ctx_fourier_mix(ufc, sgfc, wf_half[l], cc, sc, ctx_dft)
            ctx = _context_mix(ctx, ac, fc, tc, bgc, conv_w, conv_b3, w_out_b, g_post3, mod, l,
                               ctx_row=b)
    return x
```

```python
import functools
import math

import numpy as np
import jax
import jax.numpy as jnp
from jax import lax
from jax.experimental import pallas as pl
from jax.experimental.pallas import tpu as pltpu

D_MODEL = 1024
GRID_W = 64
HEAD_DIM = 64
ATTN_HEADS = 8
KV_HEADS = 2
Q_PER_KV = ATTN_HEADS // KV_HEADS
ATTN_WIDTH = ATTN_HEADS * HEAD_DIM
KV_WIDTH = KV_HEADS * HEAD_DIM
WINDOW = 128
FOURIER_GROUPS = 4
FOURIER_GROUP_DIM = 64
FOURIER_WIDTH = FOURIER_GROUPS * FOURIER_GROUP_DIM
CONV_WIDTH = 256
MIX_WIDTH = ATTN_WIDTH + FOURIER_WIDTH + CONV_WIDTH
PROJ_WIDTH = 2 * ATTN_WIDTH + 2 * KV_WIDTH + 2 * FOURIER_WIDTH + 4 * CONV_WIDTH
ROPE_FREQS = HEAD_DIM // 4
ROPE_BASE = 10000.0
NORM_EPS = 1e-6
LOG2E = math.log2(math.e)

C_Q = 0
C_K = C_Q + ATTN_WIDTH
C_V = C_K + KV_WIDTH
C_GA = C_V + KV_WIDTH
C_UF = C_GA + ATTN_WIDTH
C_GF = C_UF + FOURIER_WIDTH
C_ZC = C_GF + FOURIER_WIDTH
C_BC = C_ZC + CONV_WIDTH
C_CC = C_BC + CONV_WIDTH
C_GC = C_CC + CONV_WIDTH

LANES = 128
SUBLANES = 8
BF16_ROWS = 16
VMEM_LIMIT = 52 * 1024 * 1024

MOD_ROWS = 8
BF16 = jnp.bfloat16
F32 = jnp.float32

FFT_N2 = 64
K1_GROUP = SUBLANES
N2_CHUNK = 16
UF_PITCH = FFT_N2 + SUBLANES
SOFTMAX_ROWS = 32

BLOCK_Q = 128
KEY_SPAN = BLOCK_Q + 2 * WINDOW
HEAD_PAIRS = ATTN_WIDTH // LANES
PAIRS_PER_KV = HEAD_PAIRS // KV_HEADS

TILE_ROWS = 1024
PROJ_SUB_ROWS = 512
ATTN_ROWS = 1024
SCORE_BUFFERS = 2
MIX_ROWS = 512


def _silu(x):
    return x / (1.0 + jnp.exp(-x))


def _dot(a, b):
    return jnp.dot(a, b, preferred_element_type=F32)


def _dot_nt(a, b):
    return lax.dot_general(a, b, (((1,), (1,)), ((), ())), preferred_element_type=F32)


def _cparams(*sem):
    return pltpu.CompilerParams(dimension_semantics=sem, vmem_limit_bytes=VMEM_LIMIT)


def _rope_tables(n):
    t = np.arange(n)
    row = (t // GRID_W).astype(np.float64)
    col = (t % GRID_W).astype(np.float64)
    inv = ROPE_BASE ** (-np.arange(ROPE_FREQS, dtype=np.float64) / ROPE_FREQS)
    ar = row[:, None] * inv
    ac = col[:, None] * inv
    z = np.zeros_like(ar)
    cos_h = np.concatenate([np.cos(ar), np.cos(ar), np.cos(ac), np.cos(ac)], axis=1)
    sin_up = np.concatenate([-np.sin(ar), z, -np.sin(ac), z], axis=1)
    sin_dn = np.concatenate([z, np.sin(ar), z, np.sin(ac)], axis=1)
    rep = LANES // HEAD_DIM
    return tuple(np.tile(a, (1, rep)).astype(np.float32) for a in (cos_h, sin_up, sin_dn))


def _channel_dft():
    c = np.arange(FOURIER_GROUP_DIM)
    ang = 2.0 * np.pi * np.outer(c, c) / FOURIER_GROUP_DIM
    eye = np.eye(LANES // FOURIER_GROUP_DIM)
    return (np.kron(eye, np.cos(ang)).astype(np.float32),
            np.kron(eye, np.sin(ang)).astype(np.float32))


def _stage1_mats(n):
    n1_len = n // FFT_N2
    k1 = np.arange(n1_len)[:, None]
    n1 = np.arange(n1_len)[None, :]
    out = np.empty((FFT_N2, 2 * n1_len, 2 * n1_len), np.float32)
    for n2 in range(FFT_N2):
        ang = 2.0 * np.pi * ((k1 * (FFT_N2 * n1 + n2)) % n) / n
        ce, se = np.cos(ang), np.sin(ang)
        by_part = np.block([[ce, se], [-se, ce]]).reshape(2, n1_len // K1_GROUP, K1_GROUP, 2 * n1_len)
        out[n2] = by_part.transpose(1, 0, 2, 3).reshape(2 * n1_len, 2 * n1_len)
    return out


def _stage2_mat():
    k2 = np.arange(FFT_N2)
    ang = 2.0 * np.pi * np.outer(k2, k2) / FFT_N2
    cs = np.stack([np.cos(ang), np.sin(ang)], axis=-1)
    eye = np.eye(K1_GROUP)
    m = np.einsum('knp,rs->krnps', cs, eye)
    return m.reshape(FFT_N2 * K1_GROUP, FFT_N2 * 2 * K1_GROUP).astype(np.float32)


def _dense_dft(n):
    t = np.arange(n)
    ang = 2.0 * np.pi * (np.outer(t, t) % n) / n
    return np.concatenate([np.cos(ang), np.sin(ang)], axis=1).astype(np.float32)


def _mod_kernel(c_ref, w_ref, b_ref, o_ref):
    o_ref[0] = _dot(_silu(c_ref[...]), w_ref[0]) + b_ref[0]


def _modulation(c_rows, w_mod, b_mod):
    depth = w_mod.shape[0]
    return pl.pallas_call(
        _mod_kernel,
        grid=(depth, 3),
        in_specs=[pl.BlockSpec((MOD_ROWS, D_MODEL), lambda l, j: (0, 0)),
                  pl.BlockSpec((1, D_MODEL, D_MODEL), lambda l, j: (l, 0, j)),
                  pl.BlockSpec((1, 1, D_MODEL), lambda l, j: (l, 0, j))],
        out_specs=pl.BlockSpec((1, MOD_ROWS, D_MODEL), lambda l, j: (l, 0, j)),
        out_shape=jax.ShapeDtypeStruct((depth, MOD_ROWS, 3 * D_MODEL), F32),
        compiler_params=_cparams("arbitrary", "arbitrary"),
        name="modulation",
    )(c_rows, w_mod, b_mod.reshape(depth, 1, 3 * D_MODEL))


def _inproj_kernel(*refs, rope, ctx_row, kv_only, tm):
    x_ref, shift_ref, scale_ref, g_ref, w_ref = refs[:5]
    cos_ref, sup_ref, sdn_ref = refs[5:8] if rope else (None,) * 3
    outs = refs[8:-1] if rope else refs[5:]
    uf_s = refs[-1] if rope else None
    row = pl.program_id(0) if ctx_row is None else ctx_row
    shift = shift_ref[0, pl.ds(row, 1), :]
    gain = g_ref[0] * (1.0 + scale_ref[0, pl.ds(row, 1), :])

    def with_swapped_heads(y):
        return jnp.concatenate([y, pltpu.roll(y, HEAD_DIM, 1)], axis=1).astype(BF16)

    sub = min(tm, PROJ_SUB_ROWS)
    for s in range(tm // sub):
        rows = slice(s * sub, (s + 1) * sub)
        x = x_ref[0, rows]
        r = lax.rsqrt(jnp.mean(x * x, axis=-1, keepdims=True) + NORM_EPS)
        h = ((x * r) * gain + shift).astype(BF16)

        def proj(c0, width):
            return _dot(h, w_ref[0, :, c0:c0 + width])

        def rotate(y):
            if not rope:
                return y
            return (y * cos_ref[rows] + pltpu.roll(y, LANES - ROPE_FREQS, 1) * sup_ref[rows]
                    + pltpu.roll(y, ROPE_FREQS, 1) * sdn_ref[rows])

        kv = proj(C_K, 2 * KV_WIDTH)
        k_out = with_swapped_heads(rotate(kv[:, :KV_WIDTH]))
        v_out = with_swapped_heads(kv[:, KV_WIDTH:])
        if kv_only:
            k_ref, v_ref = outs
            k_ref[0, rows] = k_out
            v_ref[0, rows] = v_out
            continue
        q_ref, k_ref, v_ref, sga_ref, uf_ref, sgf_ref, t_ref, bg_ref = outs
        k_ref[0, rows] = k_out
        v_ref[0, rows] = v_out
        q = proj(C_Q, ATTN_WIDTH)
        for p in range(HEAD_PAIRS):
            sl = slice(p * LANES, (p + 1) * LANES)
            q_ref[0, rows, sl] = (rotate(q[:, sl]) * (HEAD_DIM ** -0.5 * LOG2E)).astype(BF16)
        sga_ref[0, rows] = _silu(proj(C_GA, ATTN_WIDTH)).astype(BF16)
        uf = proj(C_UF, FOURIER_WIDTH)
        if rope:
            per_sub = sub // FFT_N2
            for slab in range(FOURIER_WIDTH // LANES):
                for j in range(per_sub):
                    uf_s[s, slab, j * UF_PITCH:j * UF_PITCH + FFT_N2] = (
                        uf[j * FFT_N2:(j + 1) * FFT_N2, slab * LANES:(slab + 1) * LANES])
            for n2 in range(FFT_N2):
                uf_ref[0, n2, s * per_sub:(s + 1) * per_sub, :] = jnp.concatenate(
                    [uf_s[s, slab, pl.ds(n2, per_sub, stride=UF_PITCH), :]
                     for slab in range(FOURIER_WIDTH // LANES)], axis=1)
        else:
            uf_ref[0, rows] = uf.astype(BF16)
        sgf_ref[0, rows] = _silu(proj(C_GF, FOURIER_WIDTH)).astype(BF16)
        t_ref[0, rows] = (proj(C_CC, CONV_WIDTH) * proj(C_ZC, CONV_WIDTH)).astype(BF16)
        bg_ref[0, rows] = (proj(C_BC, CONV_WIDTH) * _silu(proj(C_GC, CONV_WIDTH))).astype(BF16)


def _in_projection(x, mod, g_pre, w_in, layer, *, rope_tabs, ctx_row, tm, kv_only=False):
    b, n, _ = x.shape
    rope = rope_tabs is not None
    row3 = lambda width: pl.BlockSpec((1, tm, width), lambda bi, i: (bi, i, 0))
    in_specs = [row3(D_MODEL),
                pl.BlockSpec((1, MOD_ROWS, D_MODEL), lambda bi, i: (layer, 0, 0)),
                pl.BlockSpec((1, MOD_ROWS, D_MODEL), lambda bi, i: (layer, 0, 1)),
                pl.BlockSpec((1, 1, D_MODEL), lambda bi, i: (layer, 0, 0)),
                pl.BlockSpec((1, D_MODEL, PROJ_WIDTH), lambda bi, i: (layer, 0, 0))]
    args = [x, mod, mod, g_pre, w_in]
    if rope:
        in_specs += [pl.BlockSpec((tm, LANES), lambda bi, i: (i, 0))] * 3
        args += list(rope_tabs)
    widths = (ATTN_WIDTH, 2 * KV_WIDTH, 2 * KV_WIDTH, ATTN_WIDTH, FOURIER_WIDTH, FOURIER_WIDTH,
              CONV_WIDTH, CONV_WIDTH)
    if kv_only:
        widths = widths[1:3]
    out_specs = [row3(w) for w in widths]
    out_shape = [jax.ShapeDtypeStruct((b, n, w), BF16) for w in widths]
    scratch = []
    if rope:
        sub = min(tm, PROJ_SUB_ROWS)
        assert sub % (FFT_N2 * SUBLANES) == 0
        out_specs[4] = pl.BlockSpec((1, FFT_N2, tm // FFT_N2, FOURIER_WIDTH), lambda bi, i: (bi, 0, i, 0))
        out_shape[4] = jax.ShapeDtypeStruct((b, FFT_N2, n // FFT_N2, FOURIER_WIDTH), F32)
        scratch = [pltpu.VMEM((tm // sub, FOURIER_WIDTH // LANES, sub // FFT_N2 * UF_PITCH, LANES), F32)]
    return pl.pallas_call(
        functools.partial(_inproj_kernel, rope=rope, ctx_row=ctx_row, kv_only=kv_only, tm=tm),
        grid=(b, n // tm),
        in_specs=in_specs,
        out_specs=out_specs,
        out_shape=out_shape,
        scratch_shapes=scratch,
        compiler_params=_cparams("arbitrary", "arbitrary"),
        name="in_projection_rope" if rope else ("in_projection_ctx_kv" if kv_only else "in_projection_ctx"),
    )(*args)


def _lane_half_variants(blk, fill):
    straight, swapped = blk[:, :KV_WIDTH], blk[:, KV_WIDTH:]
    lo = lax.broadcasted_iota(jnp.int32, straight.shape, 1) < HEAD_DIM
    other = jnp.full_like(straight, fill)
    return (jnp.where(lo, straight, other), jnp.where(lo, other, swapped),
            jnp.where(lo, swapped, other), jnp.where(lo, other, straight))


def _merge_head_pair(pv_lo, pv_hi, sink_lo, sink_hi):
    lane_lo = lax.broadcasted_iota(jnp.int32, pv_lo.shape, 1) < HEAD_DIM
    num = jnp.where(lane_lo, pv_lo, pv_hi)
    den = pltpu.roll(jnp.where(lane_lo, pv_hi, pv_lo), HEAD_DIM, 1) + jnp.where(lane_lo, sink_lo, sink_hi)
    return num / den


def _mix_rows(a, f, t, above, below, bg, cw, cb, w, g_post, gate, x):
    n_rows = t.shape[0]
    ridx = lax.broadcasted_iota(jnp.int32, t.shape, 0)
    up = jnp.where(ridx == 0, above, pltpu.roll(t, 1, 0))
    dn = jnp.where(ridx == n_rows - 1, below, pltpu.roll(t, n_rows - 1, 0))
    conv = up * cw[0:1] + t * cw[1:2] + dn * cw[2:3] + cb
    h = jnp.concatenate([a, f, (conv * bg.astype(F32)).astype(BF16)], axis=1)
    y = _dot(h, w)
    r = lax.rsqrt(jnp.mean(y * y, axis=-1, keepdims=True) + NORM_EPS)
    return x + (y * r) * (gate * g_post)


def _attn_kernel(sink_ref, q_ref, g_ref, kp_ref, km_ref, kn_ref, vp_ref, vm_ref, vn_ref, kc_ref, vc_ref,
                 x_ref, f_ref, t_ref, tp_ref, tn_ref, bg_ref, cw_ref, cb_ref, w_ref, gpost_ref, mgate_ref,
                 o_ref, k_s, v_s, kctx_s, vctx_s, bias_s, s_s, p_s, r_s, a_s, *, tq, n_seq, n_ctx):
    i = pl.program_id(1)
    n_blocks = tq // BLOCK_Q
    for off, kref, vref, rows in ((0, kp_ref, vp_ref, WINDOW), (WINDOW, km_ref, vm_ref, tq),
                                  (WINDOW + tq, kn_ref, vn_ref, WINDOW)):
        for idx, kk in enumerate(_lane_half_variants(kref[0], 0.0)):
            k_s[idx, off:off + rows] = kk
        for idx, vv in enumerate(_lane_half_variants(vref[0], 1.0)):
            v_s[idx, off:off + rows] = vv
    for idx, kk in enumerate(_lane_half_variants(kc_ref[0], 0.0)):
        kctx_s[idx] = kk
    for idx, vv in enumerate(_lane_half_variants(vc_ref[0], 1.0)):
        vctx_s[idx] = vv

    ii = lax.broadcasted_iota(jnp.int32, (BLOCK_Q, WINDOW), 0)
    jj = lax.broadcasted_iota(jnp.int32, (BLOCK_Q, WINDOW), 1)
    head_band = jnp.where(jj >= ii, 0.0, -jnp.inf)
    tail_band = jnp.where(jj <= ii, 0.0, -jnp.inf)
    bias_s[0] = head_band
    bias_s[1] = tail_band
    bias_s[2] = jnp.where(i == 0, -jnp.inf, head_band)
    bias_s[3] = jnp.where(i == pl.num_programs(1) - 1, -jnp.inf, tail_band)

    head_w = n_ctx + KEY_SPAN

    def scores(sb):
        par, r0 = sb % SCORE_BUFFERS, sb * BLOCK_Q
        q = q_ref[0, r0:r0 + BLOCK_Q, :]
        for kv in range(KV_HEADS):
            pairs = range(kv * PAIRS_PER_KV, (kv + 1) * PAIRS_PER_KV)
            qg = jnp.concatenate([q[:, p * LANES:(p + 1) * LANES] for p in pairs], axis=0)
            keys = jnp.concatenate([kctx_s[2 * kv], k_s[2 * kv, r0:r0 + KEY_SPAN],
                                    kctx_s[2 * kv + 1], k_s[2 * kv + 1, r0:r0 + KEY_SPAN]], axis=0)
            s_s[par, pairs.start:pairs.stop] = _dot_nt(qg, keys).reshape(PAIRS_PER_KV, BLOCK_Q, 2 * head_w)

    def softmax(sb):
        par, spar = sb % 2, sb % SCORE_BUFFERS
        head_bias = 2 if sb == 0 else 0
        tail_bias = 3 if sb == n_blocks - 1 else 1
        n_cols = head_w // LANES
        for p in range(HEAD_PAIRS):
            for half in range(2):
                h = 2 * p + half
                sink2 = sink_ref[h] * LOG2E
                for rs in range(BLOCK_Q // SOFTMAX_ROWS):
                    rows = slice(rs * SOFTMAX_ROWS, (rs + 1) * SOFTMAX_ROWS)
                    cols = [s_s[spar, p, rows, half * head_w + t * LANES:half * head_w + (t + 1) * LANES]
                            for t in range(n_cols)]
                    first_local = n_ctx // LANES
                    cols[first_local] = cols[first_local] + bias_s[head_bias, rows]
                    cols[-1] = cols[-1] + bias_s[tail_bias, rows]
                    m = jnp.maximum(jnp.max(functools.reduce(jnp.maximum, cols), axis=-1, keepdims=True),
                                    sink2)
                    for t, col in enumerate(cols):
                        p_s[par, p, rows, half * head_w + t * LANES:half * head_w + (t + 1) * LANES] = (
                            jnp.exp2(col - m).astype(BF16))
                    r_s[par, h, rows] = jnp.broadcast_to(jnp.exp2(sink2 - m), (SOFTMAX_ROWS, LANES))

    def weighted_values(sb):
        par, r0 = sb % 2, sb * BLOCK_Q
        for kv in range(KV_HEADS):
            pairs = range(kv * PAIRS_PER_KV, (kv + 1) * PAIRS_PER_KV)
            pv = []
            for half in range(2):
                vals = jnp.concatenate([vctx_s[2 * kv + half], v_s[2 * kv + half, r0:r0 + KEY_SPAN]], axis=0)
                probs = p_s[par, pairs.start:pairs.stop, :, half * head_w:(half + 1) * head_w]
                pv.append(_dot(probs.reshape(PAIRS_PER_KV * BLOCK_Q, head_w), vals))
            for j, p in enumerate(pairs):
                rows = slice(j * BLOCK_Q, (j + 1) * BLOCK_Q)
                out = _merge_head_pair(pv[0][rows], pv[1][rows], r_s[par, 2 * p], r_s[par, 2 * p + 1])
                gate = g_ref[0, r0:r0 + BLOCK_Q, p * LANES:(p + 1) * LANES].astype(F32)
                a_s[r0:r0 + BLOCK_Q, p * LANES:(p + 1) * LANES] = (out * gate).astype(BF16)

    def conv_edge(ref, row, keep):
        return jnp.where(keep, ref[0].astype(F32)[row:row + 1, :], 0.0)

    def mix(r0):
        rows = slice(r0, r0 + MIX_ROWS)
        if r0 == 0:
            above = conv_edge(tp_ref, BF16_ROWS - 1, i > 0)
        else:
            above = t_ref[0, r0 - BF16_ROWS:r0, :].astype(F32)[BF16_ROWS - 1:, :]
        if r0 + MIX_ROWS == tq:
            below = conv_edge(tn_ref, 0, i < pl.num_programs(1) - 1)
        else:
            below = t_ref[0, r0 + MIX_ROWS:r0 + MIX_ROWS + BF16_ROWS, :].astype(F32)[:1, :]
        o_ref[0, rows] = _mix_rows(
            a_s[rows], f_ref[0, rows],
            t_ref[0, rows].astype(F32), above, below, bg_ref[0, rows], cw_ref[0], cb_ref[0],
            w_ref[0], gpost_ref[0], mgate_ref[0, pl.ds(pl.program_id(0), 1), :], x_ref[0, rows])

    per_mix = MIX_ROWS // BLOCK_Q
    ahead = SCORE_BUFFERS - 1
    for sb in range(min(ahead, n_blocks)):
        scores(sb)
    for sb in range(n_blocks):
        if sb + ahead < n_blocks:
            scores(sb + ahead)
        softmax(sb)
        weighted_values(sb)
        if (sb + 1) % per_mix == 0:
            mix((sb + 1 - per_mix) * BLOCK_Q)


def _attention_and_mix(x, q, gate, k, v, kc, vc, sink, f, t, bg, conv_w, conv_b, w_out, g_post, mod,
                       layer, *, tq):
    b, n, _ = q.shape
    nc = kc.shape[1]
    per = tq // WINDOW
    last = n // WINDOW - 1
    kvw = 2 * KV_WIDTH
    variants = 2 * KV_HEADS
    main = lambda width: pl.BlockSpec((1, tq, width), lambda bi, i: (bi, i, 0))
    prev = pl.BlockSpec((1, WINDOW, kvw), lambda bi, i: (bi, jnp.maximum(i * per - 1, 0), 0))
    nxt = pl.BlockSpec((1, WINDOW, kvw), lambda bi, i: (bi, jnp.minimum((i + 1) * per, last), 0))
    ctx = pl.BlockSpec((1, nc, kvw), lambda bi, i: (bi, 0, 0))
    lay3 = lambda shape: pl.BlockSpec((1,) + shape, lambda bi, i: (layer, 0, 0))
    t_per = tq // BF16_ROWS
    t_last = n // BF16_ROWS - 1
    t_prev = pl.BlockSpec((1, BF16_ROWS, CONV_WIDTH), lambda bi, i: (bi, jnp.maximum(i * t_per - 1, 0), 0))
    t_next = pl.BlockSpec((1, BF16_ROWS, CONV_WIDTH),
                          lambda bi, i: (bi, jnp.minimum((i + 1) * t_per, t_last), 0))
    span = tq + 2 * WINDOW
    s_cols = 2 * (nc + KEY_SPAN)
    return pl.pallas_call(
        functools.partial(_attn_kernel, tq=tq, n_seq=n, n_ctx=nc),
        grid=(b, n // tq),
        in_specs=[pl.BlockSpec(memory_space=pltpu.SMEM), main(ATTN_WIDTH), main(ATTN_WIDTH),
                  prev, main(kvw), nxt, prev, main(kvw), nxt, ctx, ctx,
                  main(D_MODEL),
                  main(FOURIER_WIDTH), main(CONV_WIDTH), t_prev, t_next, main(CONV_WIDTH),
                  lay3((3, CONV_WIDTH)), lay3((1, CONV_WIDTH)), lay3((MIX_WIDTH, D_MODEL)),
                  lay3((1, D_MODEL)),
                  pl.BlockSpec((1, MOD_ROWS, D_MODEL), lambda bi, i: (layer, 0, 2))],
        out_specs=main(D_MODEL),
        out_shape=jax.ShapeDtypeStruct(x.shape, F32),
        scratch_shapes=[pltpu.VMEM((variants, span, KV_WIDTH), BF16),
                        pltpu.VMEM((variants, span, KV_WIDTH), BF16),
                        pltpu.VMEM((variants, nc, KV_WIDTH), BF16),
                        pltpu.VMEM((variants, nc, KV_WIDTH), BF16),
                        pltpu.VMEM((4, BLOCK_Q, WINDOW), F32),
                        pltpu.VMEM((SCORE_BUFFERS, HEAD_PAIRS, BLOCK_Q, s_cols), F32),
                        pltpu.VMEM((2, HEAD_PAIRS, BLOCK_Q, s_cols), BF16),
                        pltpu.VMEM((2, ATTN_HEADS, BLOCK_Q, LANES), F32),
                        pltpu.VMEM((tq, ATTN_WIDTH), BF16)],
        compiler_params=_cparams("arbitrary", "arbitrary"),
        name="attention_and_mix",
    )(sink, q, gate, k, k, k, v, v, v, kc, vc, x, f, t, t, t, bg, conv_w, conv_b, w_out, g_post, mod)


def _ctx_attn_kernel(sink_ref, q_ref, g_ref, kc_ref, vc_ref, o_ref):
    keys = _lane_half_variants(kc_ref[0], 0.0)
    vals = _lane_half_variants(vc_ref[0], 1.0)
    for p in range(HEAD_PAIRS):
        kv = p // PAIRS_PER_KV
        qp = q_ref[0, :, p * LANES:(p + 1) * LANES]
        pv, sink_terms = [], []
        for half in range(2):
            s = _dot_nt(qp, keys[2 * kv + half])
            sink2 = sink_ref[2 * p + half] * LOG2E
            m = jnp.maximum(jnp.max(s, axis=-1, keepdims=True), sink2)
            pv.append(_dot(jnp.exp2(s - m).astype(BF16), vals[2 * kv + half]))
            sink_terms.append(jnp.exp2(sink2 - m))
        out = _merge_head_pair(pv[0], pv[1], sink_terms[0], sink_terms[1])
        gate = g_ref[0, :, p * LANES:(p + 1) * LANES].astype(F32)
        o_ref[0, :, p * LANES:(p + 1) * LANES] = (out * gate).astype(BF16)


def _context_attention(q, gate, kc, vc, sink):
    b, nc, _ = q.shape
    blk = lambda width: pl.BlockSpec((1, nc, width), lambda bi: (bi, 0, 0))
    return pl.pallas_call(
        _ctx_attn_kernel,
        grid=(b,),
        in_specs=[pl.BlockSpec(memory_space=pltpu.SMEM), blk(ATTN_WIDTH), blk(ATTN_WIDTH),
                  blk(2 * KV_WIDTH), blk(2 * KV_WIDTH)],
        out_specs=blk(ATTN_WIDTH),
        out_shape=jax.ShapeDtypeStruct((b, nc, ATTN_WIDTH), BF16),
        compiler_params=_cparams("arbitrary"),
        name="context_attention",
    )(sink, q, gate, kc, vc)


def _channel_mix_matrix(cc_ref, sc_ref, wf_ref, scale):
    wf = wf_ref[0]
    return (jnp.concatenate([_dot(cc_ref[...], wf), -_dot(sc_ref[...], wf)], axis=1) * scale).astype(BF16)


def _fourier_kernel(uf_ref, gate_ref, cc_ref, sc_ref, wf_ref, g_ref, m2_ref, o_ref, mix_s, y_s, *, n_seq):
    n1_len = n_seq // FFT_N2
    chunk = pl.program_id(1)
    halves = FOURIER_WIDTH // LANES

    @pl.when(chunk == 0)
    def _():
        zero = jnp.zeros((LANES, LANES), F32)
        scale = (n_seq * FOURIER_GROUP_DIM) ** -0.5
        re = [_dot(cc_ref[...], wf_ref[h]) * scale for h in range(halves)]
        im = [_dot(sc_ref[...], wf_ref[h]) * -scale for h in range(halves)]
        for h in range(halves):
            row = [re[h] if j == h else zero for j in range(halves)]
            row += [im[h] if j == h else zero for j in range(halves)]
            mix_s[h * LANES:(h + 1) * LANES] = jnp.concatenate(row, axis=1).astype(BF16)

    zs = [_dot(uf_ref[0, u].astype(BF16), mix_s[...]) for u in range(N2_CHUNK)]
    for u, z in enumerate(zs):
        rhs = jnp.concatenate([z[:, :FOURIER_WIDTH], z[:, FOURIER_WIDTH:]], axis=0).astype(BF16)
        y = _dot(g_ref[chunk * N2_CHUNK + u], rhs)
        y_s[chunk * N2_CHUNK + u] = y.astype(BF16).reshape(n1_len // K1_GROUP, 2 * K1_GROUP, FOURIER_WIDTH)

    @pl.when(chunk == pl.num_programs(1) - 1)
    def _():
        per_store = BF16_ROWS // K1_GROUP

        def stage2(t, carry):
            outs = []
            for u in range(per_store):
                rhs = y_s[:, t * per_store + u].reshape(FFT_N2 * 2 * K1_GROUP, FOURIER_WIDTH)
                outs.append(_dot(m2_ref[...], rhs).reshape(FFT_N2, K1_GROUP, FOURIER_WIDTH))
            r0 = pl.multiple_of(t * BF16_ROWS, BF16_ROWS)
            gate = gate_ref[0, :, pl.ds(r0, BF16_ROWS), :].astype(F32)
            o_ref[0, :, pl.ds(r0, BF16_ROWS), :] = (jnp.concatenate(outs, axis=1) * gate).astype(BF16)
            return carry

        lax.fori_loop(0, n1_len // BF16_ROWS, stage2, 0)


def _fourier_mix(uf, gate, wf_half, consts):
    b, _, n1_len, _ = uf.shape
    n = FFT_N2 * n1_len
    cc, sc, g, m2 = consts
    full = lambda shape: pl.BlockSpec(shape, lambda bi, c: (0,) * len(shape))
    whole = pl.BlockSpec((1, FFT_N2, n1_len, FOURIER_WIDTH), lambda bi, c: (bi, 0, 0, 0))
    out = pl.pallas_call(
        functools.partial(_fourier_kernel, n_seq=n),
        grid=(b, FFT_N2 // N2_CHUNK),
        in_specs=[pl.BlockSpec((1, N2_CHUNK, n1_len, FOURIER_WIDTH), lambda bi, c: (bi, c, 0, 0)),
                  whole, full((LANES, LANES)), full((LANES, LANES)), full(wf_half.shape),
                  full(g.shape), full(m2.shape)],
        out_specs=whole,
        out_shape=jax.ShapeDtypeStruct((b, FFT_N2, n1_len, FOURIER_WIDTH), BF16),
        scratch_shapes=[pltpu.VMEM((FOURIER_WIDTH, 2 * FOURIER_WIDTH), BF16),
                        pltpu.VMEM((FFT_N2, n1_len // K1_GROUP, 2 * K1_GROUP, FOURIER_WIDTH), BF16)],
        compiler_params=_cparams("arbitrary", "arbitrary"),
        name="fourier_mix",
    )(uf, gate.reshape(b, FFT_N2, n1_len, FOURIER_WIDTH), cc, sc, wf_half, g, m2)
    return out.reshape(b, n, FOURIER_WIDTH)


def _ctx_fourier_kernel(uf_ref, gate_ref, cc_ref, sc_ref, wf_ref, dft_ref, o_ref, *, n_seq):
    mix = _channel_mix_matrix(cc_ref, sc_ref, wf_ref, (n_seq * FOURIER_GROUP_DIM) ** -0.5)
    z = _dot(uf_ref[0], mix)
    rhs = jnp.concatenate([z[:, :LANES], z[:, LANES:]], axis=0).astype(BF16)
    o_ref[0] = (_dot(dft_ref[...], rhs) * gate_ref[0].astype(F32)).astype(BF16)


def _ctx_fourier_mix(uf, gate, wf_half, cc, sc, dft):
    b, n, _ = uf.shape
    halves = FOURIER_WIDTH // LANES
    full = lambda shape: pl.BlockSpec(shape, lambda bi, hf: (0,) * len(shape))
    half = pl.BlockSpec((1, n, LANES), lambda bi, hf: (bi, 0, hf))
    return pl.pallas_call(
        functools.partial(_ctx_fourier_kernel, n_seq=n),
        grid=(b, halves),
        in_specs=[half, half, full((LANES, LANES)), full((LANES, LANES)),
                  pl.BlockSpec((1, LANES, LANES), lambda bi, hf: (hf, 0, 0)), full(dft.shape)],
        out_specs=half,
        out_shape=jax.ShapeDtypeStruct((b, n, FOURIER_WIDTH), BF16),
        compiler_params=_cparams("arbitrary", "arbitrary"),
        name="context_fourier_mix",
    )(uf, gate, cc, sc, wf_half, dft)


def _ctx_mix_kernel(x_ref, a_ref, f_ref, t_ref, bg_ref, cw_ref, cb_ref, w_ref, g_ref, gate_ref, o_ref,
                    *, ctx_row):
    edge = jnp.zeros((1, CONV_WIDTH), F32)
    o_ref[0] = _mix_rows(a_ref[0], f_ref[0], t_ref[0].astype(F32), edge, edge, bg_ref[0], cw_ref[0],
                         cb_ref[0], w_ref[0], g_ref[0], gate_ref[0, ctx_row:ctx_row + 1, :], x_ref[0])


def _context_mix(x, a, f, t, bg, conv_w, conv_b, w_out, g_post, mod, layer, *, ctx_row):
    b, n, _ = x.shape
    row3 = lambda width: pl.BlockSpec((1, n, width), lambda bi: (bi, 0, 0))
    lay3 = lambda shape: pl.BlockSpec((1,) + shape, lambda bi: (layer, 0, 0))
    return pl.pallas_call(
        functools.partial(_ctx_mix_kernel, ctx_row=ctx_row),
        grid=(b,),
        in_specs=[row3(D_MODEL), row3(ATTN_WIDTH), row3(FOURIER_WIDTH), row3(CONV_WIDTH), row3(CONV_WIDTH), lay3((3, CONV_WIDTH)), lay3((1, CONV_WIDTH)),
                  lay3((MIX_WIDTH, D_MODEL)), lay3((1, D_MODEL)),
                  pl.BlockSpec((1, MOD_ROWS, D_MODEL), lambda bi: (layer, 0, 2))],
        out_specs=row3(D_MODEL),
        out_shape=jax.ShapeDtypeStruct(x.shape, F32),
        compiler_params=_cparams("arbitrary"),
        name="context_mix",
    )(x, a, f, t, bg, conv_w, conv_b, w_out, g_post, mod)


def kernel(x, c, ctx, c_ctx, w_mod, b_mod, g_pre, g_post, w_in, w_out, sink, w_fourier, conv_w, conv_b):
    depth = w_mod.shape[0]
    b, n, _ = x.shape
    nc = ctx.shape[1]
    assert b + 1 <= MOD_ROWS and n % GRID_W == 0 and n % ATTN_ROWS == 0 and n % TILE_ROWS == 0
    assert (n // FFT_N2) % BF16_ROWS == 0 and FFT_N2 % N2_CHUNK == 0

    w_in_b = w_in.astype(BF16)
    w_out_b = w_out.astype(BF16)

    rope_tabs = tuple(jnp.asarray(t) for t in _rope_tables(n))
    cc, sc = (jnp.asarray(m) for m in _channel_dft())
    stage1 = jnp.asarray(_stage1_mats(n)).astype(BF16)
    stage2 = jnp.asarray(_stage2_mat()).astype(BF16)
    ctx_dft = jnp.asarray(_dense_dft(nc)).astype(BF16)
    groups_per_half = LANES // FOURIER_GROUP_DIM
    wf_half = jnp.zeros((depth, FOURIER_GROUPS // groups_per_half, LANES, LANES), F32)
    for g in range(FOURIER_GROUPS):
        o = (g % groups_per_half) * FOURIER_GROUP_DIM
        wf_half = wf_half.at[:, g // groups_per_half, o:o + FOURIER_GROUP_DIM,
                             o:o + FOURIER_GROUP_DIM].set(w_fourier[:, g])

    c_rows = jnp.zeros((MOD_ROWS, D_MODEL), F32).at[:b].set(c).at[b].set(c_ctx)
    mod = _modulation(c_rows, w_mod, b_mod)
    g_pre3 = g_pre.reshape(depth, 1, D_MODEL)
    g_post3 = g_post.reshape(depth, 1, D_MODEL)
    conv_b3 = conv_b.reshape(depth, 1, CONV_WIDTH)

    for l in range(depth):
        update_ctx = l < depth - 1
        if update_ctx:
            qc, kc, vc, sgac, ufc, sgfc, tc, bgc = _in_projection(
                ctx, mod, g_pre3, w_in_b, l, rope_tabs=None, ctx_row=b, tm=nc)
        else:
            kc, vc = _in_projection(ctx, mod, g_pre3, w_in_b, l, rope_tabs=None, ctx_row=b, tm=nc,
                                    kv_only=True)
        q, k, v, sga, uf, sgf, t, bg = _in_projection(
            x, mod, g_pre3, w_in_b, l, rope_tabs=rope_tabs, ctx_row=None, tm=TILE_ROWS)
        f = _fourier_mix(uf, sgf, wf_half[l], (cc, sc, stage1, stage2))
        x = _attention_and_mix(x, q, sga, k, v, kc, vc, sink[l], f, t, bg, conv_w, conv_b3, w_out_b,
                               g_post3, mod, l, tq=ATTN_ROWS)
        if update_ctx:
            ac = _context_attention(qc, sgac, kc, vc, sink[l])
            fc = _ctx_fourier_mix(ufc, sgfc, wf_half[l], cc, sc, ctx_dft)
            ctx = _context_mix(ctx, ac, fc, tc, bgc, conv_w, conv_b3, w_out_b, g_post3, mod, l,
                               ctx_row=b)
    return x
```

```python
import functools
import math

import numpy as np
import jax
import jax.numpy as jnp
from jax import lax
from jax.experimental import pallas as pl
from jax.experimental.pallas import tpu as pltpu

D_MODEL = 1024
GRID_W = 64
HEAD_DIM = 64
ATTN_HEADS = 8
KV_HEADS = 2
Q_PER_KV = ATTN_HEADS // KV_HEADS
ATTN_WIDTH = ATTN_HEADS * HEAD_DIM
KV_WIDTH = KV_HEADS * HEAD_DIM
WINDOW = 128
FOURIER_GROUPS = 4
FOURIER_GROUP_DIM = 64
FOURIER_WIDTH = FOURIER_GROUPS * FOURIER_GROUP_DIM
CONV_WIDTH = 256
MIX_WIDTH = ATTN_WIDTH + FOURIER_WIDTH + CONV_WIDTH
PROJ_WIDTH = 2 * ATTN_WIDTH + 2 * KV_WIDTH + 2 * FOURIER_WIDTH + 4 * CONV_WIDTH
ROPE_FREQS = HEAD_DIM // 4
ROPE_BASE = 10000.0
NORM_EPS = 1e-6
LOG2E = math.log2(math.e)

C_Q = 0
C_K = C_Q + ATTN_WIDTH
C_V = C_K + KV_WIDTH
C_GA = C_V + KV_WIDTH
C_UF = C_GA + ATTN_WIDTH
C_GF = C_UF + FOURIER_WIDTH
C_ZC = C_GF + FOURIER_WIDTH
C_BC = C_ZC + CONV_WIDTH
C_CC = C_BC + CONV_WIDTH
C_GC = C_CC + CONV_WIDTH

LANES = 128
SUBLANES = 8
BF16_ROWS = 16
VMEM_LIMIT = 52 * 1024 * 1024

MOD_ROWS = 8
BF16 = jnp.bfloat16
F32 = jnp.float32

FFT_N2 = 64
K1_GROUP = SUBLANES
N2_CHUNK = 16
UF_PITCH = FFT_N2 + SUBLANES
SOFTMAX_ROWS = 32

BLOCK_Q = 128
KEY_SPAN = BLOCK_Q + 2 * WINDOW
HEAD_PAIRS = ATTN_WIDTH // LANES
PAIRS_PER_KV = HEAD_PAIRS // KV_HEADS

TILE_ROWS = 1024
PROJ_SUB_ROWS = 512
ATTN_ROWS = 1024
SCORE_BUFFERS = 2
MIX_ROWS = 1024


def _silu(x):
    return x / (1.0 + jnp.exp(-x))


def _dot(a, b):
    return jnp.dot(a, b, preferred_element_type=F32)


def _dot_nt(a, b):
    return lax.dot_general(a, b, (((1,), (1,)), ((), ())), preferred_element_type=F32)


def _cparams(*sem):
    return pltpu.CompilerParams(dimension_semantics=sem, vmem_limit_bytes=VMEM_LIMIT)


def _rope_tables(n):
    t = np.arange(n)
    row = (t // GRID_W).astype(np.float64)
    col = (t % GRID_W).astype(np.float64)
    inv = ROPE_BASE ** (-np.arange(ROPE_FREQS, dtype=np.float64) / ROPE_FREQS)
    ar = row[:, None] * inv
    ac = col[:, None] * inv
    z = np.zeros_like(ar)
    cos_h = np.concatenate([np.cos(ar), np.cos(ar), np.cos(ac), np.cos(ac)], axis=1)
    sin_up = np.concatenate([-np.sin(ar), z, -np.sin(ac), z], axis=1)
    sin_dn = np.concatenate([z, np.sin(ar), z, np.sin(ac)], axis=1)
    rep = LANES // HEAD_DIM
    return tuple(np.tile(a, (1, rep)).astype(np.float32) for a in (cos_h, sin_up, sin_dn))


def _channel_dft():
    c = np.arange(FOURIER_GROUP_DIM)
    ang = 2.0 * np.pi * np.outer(c, c) / FOURIER_GROUP_DIM
    eye = np.eye(LANES // FOURIER_GROUP_DIM)
    return (np.kron(eye, np.cos(ang)).astype(np.float32),
            np.kron(eye, np.sin(ang)).astype(np.float32))


def _stage1_mats(n):
    n1_len = n // FFT_N2
    k1 = np.arange(n1_len)[:, None]
    n1 = np.arange(n1_len)[None, :]
    out = np.empty((FFT_N2, 2 * n1_len, 2 * n1_len), np.float32)
    for n2 in range(FFT_N2):
        ang = 2.0 * np.pi * ((k1 * (FFT_N2 * n1 + n2)) % n) / n
        ce, se = np.cos(ang), np.sin(ang)
        by_part = np.block([[ce, se], [-se, ce]]).reshape(2, n1_len // K1_GROUP, K1_GROUP, 2 * n1_len)
        out[n2] = by_part.transpose(1, 0, 2, 3).reshape(2 * n1_len, 2 * n1_len)
    return out


def _stage2_mat():
    k2 = np.arange(FFT_N2)
    ang = 2.0 * np.pi * np.outer(k2, k2) / FFT_N2
    cs = np.stack([np.cos(ang), np.sin(ang)], axis=-1)
    eye = np.eye(K1_GROUP)
    m = np.einsum('knp,rs->krnps', cs, eye)
    return m.reshape(FFT_N2 * K1_GROUP, FFT_N2 * 2 * K1_GROUP).astype(np.float32)


def _dense_dft(n):
    t = np.arange(n)
    ang = 2.0 * np.pi * (np.outer(t, t) % n) / n
    return np.concatenate([np.cos(ang), np.sin(ang)], axis=1).astype(np.float32)


def _mod_kernel(c_ref, w_ref, b_ref, o_ref):
    o_ref[0] = _dot(_silu(c_ref[...]), w_ref[0]) + b_ref[0]


def _modulation(c_rows, w_mod, b_mod):
    depth = w_mod.shape[0]
    return pl.pallas_call(
        _mod_kernel,
        grid=(depth, 3),
        in_specs=[pl.BlockSpec((MOD_ROWS, D_MODEL), lambda l, j: (0, 0)),
                  pl.BlockSpec((1, D_MODEL, D_MODEL), lambda l, j: (l, 0, j)),
                  pl.BlockSpec((1, 1, D_MODEL), lambda l, j: (l, 0, j))],
        out_specs=pl.BlockSpec((1, MOD_ROWS, D_MODEL), lambda l, j: (l, 0, j)),
        out_shape=jax.ShapeDtypeStruct((depth, MOD_ROWS, 3 * D_MODEL), F32),
        compiler_params=_cparams("arbitrary", "arbitrary"),
        name="modulation",
    )(c_rows, w_mod, b_mod.reshape(depth, 1, 3 * D_MODEL))


def _inproj_kernel(*refs, rope, ctx_row, kv_only, tm):
    x_ref, shift_ref, scale_ref, g_ref, w_ref = refs[:5]
    cos_ref, sup_ref, sdn_ref = refs[5:8] if rope else (None,) * 3
    outs = refs[8:-1] if rope else refs[5:]
    uf_s = refs[-1] if rope else None
    row = pl.program_id(0) if ctx_row is None else ctx_row
    shift = shift_ref[0, pl.ds(row, 1), :]
    gain = g_ref[0] * (1.0 + scale_ref[0, pl.ds(row, 1), :])

    def with_swapped_heads(y):
        return jnp.concatenate([y, pltpu.roll(y, HEAD_DIM, 1)], axis=1).astype(BF16)

    sub = min(tm, PROJ_SUB_ROWS)
    for s in range(tm // sub):
        rows = slice(s * sub, (s + 1) * sub)
        x = x_ref[0, rows]
        r = lax.rsqrt(jnp.mean(x * x, axis=-1, keepdims=True) + NORM_EPS)
        h = ((x * r) * gain + shift).astype(BF16)

        def proj(c0, width):
            return _dot(h, w_ref[0, :, c0:c0 + width])

        def rotate(y):
            if not rope:
                return y
            return (y * cos_ref[rows] + pltpu.roll(y, LANES - ROPE_FREQS, 1) * sup_ref[rows]
                    + pltpu.roll(y, ROPE_FREQS, 1) * sdn_ref[rows])

        kv = proj(C_K, 2 * KV_WIDTH)
        k_out = with_swapped_heads(rotate(kv[:, :KV_WIDTH]))
        v_out = with_swapped_heads(kv[:, KV_WIDTH:])
        if kv_only:
            k_ref, v_ref = outs
            k_ref[0, rows] = k_out
            v_ref[0, rows] = v_out
            continue
        q_ref, k_ref, v_ref, sga_ref, uf_ref, sgf_ref, t_ref, bg_ref = outs
        k_ref[0, rows] = k_out
        v_ref[0, rows] = v_out
        q = proj(C_Q, ATTN_WIDTH)
        for p in range(HEAD_PAIRS):
            sl = slice(p * LANES, (p + 1) * LANES)
            q_ref[0, rows, sl] = (rotate(q[:, sl]) * (HEAD_DIM ** -0.5 * LOG2E)).astype(BF16)
        sga_ref[0, rows] = _silu(proj(C_GA, ATTN_WIDTH)).astype(BF16)
        uf = proj(C_UF, FOURIER_WIDTH)
        if rope:
            per_sub = sub // FFT_N2
            for slab in range(FOURIER_WIDTH // LANES):
                for j in range(per_sub):
                    uf_s[s, slab, j * UF_PITCH:j * UF_PITCH + FFT_N2] = (
                        uf[j * FFT_N2:(j + 1) * FFT_N2, slab * LANES:(slab + 1) * LANES])
            for n2 in range(FFT_N2):
                uf_ref[0, n2, s * per_sub:(s + 1) * per_sub, :] = jnp.concatenate(
                    [uf_s[s, slab, pl.ds(n2, per_sub, stride=UF_PITCH), :]
                     for slab in range(FOURIER_WIDTH // LANES)], axis=1)
        else:
            uf_ref[0, rows] = uf.astype(BF16)
        sgf_ref[0, rows] = _silu(proj(C_GF, FOURIER_WIDTH)).astype(BF16)
        t_ref[0, rows] = (proj(C_CC, CONV_WIDTH) * proj(C_ZC, CONV_WIDTH)).astype(BF16)
        bg_ref[0, rows] = (proj(C_BC, CONV_WIDTH) * _silu(proj(C_GC, CONV_WIDTH))).astype(BF16)


def _in_projection(x, mod, g_pre, w_in, layer, *, rope_tabs, ctx_row, tm, kv_only=False):
    b, n, _ = x.shape
    rope = rope_tabs is not None
    row3 = lambda width: pl.BlockSpec((1, tm, width), lambda bi, i: (bi, i, 0))
    in_specs = [row3(D_MODEL),
                pl.BlockSpec((1, MOD_ROWS, D_MODEL), lambda bi, i: (layer, 0, 0)),
                pl.BlockSpec((1, MOD_ROWS, D_MODEL), lambda bi, i: (layer, 0, 1)),
                pl.BlockSpec((1, 1, D_MODEL), lambda bi, i: (layer, 0, 0)),
                pl.BlockSpec((1, D_MODEL, PROJ_WIDTH), lambda bi, i: (layer, 0, 0))]
    args = [x, mod, mod, g_pre, w_in]
    if rope:
        in_specs += [pl.BlockSpec((tm, LANES), lambda bi, i: (i, 0))] * 3
        args += list(rope_tabs)
    widths = (ATTN_WIDTH, 2 * KV_WIDTH, 2 * KV_WIDTH, ATTN_WIDTH, FOURIER_WIDTH, FOURIER_WIDTH,
              CONV_WIDTH, CONV_WIDTH)
    if kv_only:
        widths = widths[1:3]
    out_specs = [row3(w) for w in widths]
    out_shape = [jax.ShapeDtypeStruct((b, n, w), BF16) for w in widths]
    scratch = []
    if rope:
        sub = min(tm, PROJ_SUB_ROWS)
        assert sub % (FFT_N2 * SUBLANES) == 0
        out_specs[4] = pl.BlockSpec((1, FFT_N2, tm // FFT_N2, FOURIER_WIDTH), lambda bi, i: (bi, 0, i, 0))
        out_shape[4] = jax.ShapeDtypeStruct((b, FFT_N2, n // FFT_N2, FOURIER_WIDTH), F32)
        scratch = [pltpu.VMEM((tm // sub, FOURIER_WIDTH // LANES, sub // FFT_N2 * UF_PITCH, LANES), F32)]
    return pl.pallas_call(
        functools.partial(_inproj_kernel, rope=rope, ctx_row=ctx_row, kv_only=kv_only, tm=tm),
        grid=(b, n // tm),
        in_specs=in_specs,
        out_specs=out_specs,
        out_shape=out_shape,
        scratch_shapes=scratch,
        compiler_params=_cparams("arbitrary", "arbitrary"),
        name="in_projection_rope" if rope else ("in_projection_ctx_kv" if kv_only else "in_projection_ctx"),
    )(*args)


def _lane_half_variants(blk, fill):
    straight, swapped = blk[:, :KV_WIDTH], blk[:, KV_WIDTH:]
    lo = lax.broadcasted_iota(jnp.int32, straight.shape, 1) < HEAD_DIM
    other = jnp.full_like(straight, fill)
    return (jnp.where(lo, straight, other), jnp.where(lo, other, swapped),
            jnp.where(lo, swapped, other), jnp.where(lo, other, straight))


def _merge_head_pair(pv_lo, pv_hi, sink_lo, sink_hi):
    lane_lo = lax.broadcasted_iota(jnp.int32, pv_lo.shape, 1) < HEAD_DIM
    num = jnp.where(lane_lo, pv_lo, pv_hi)
    den = pltpu.roll(jnp.where(lane_lo, pv_hi, pv_lo), HEAD_DIM, 1) + jnp.where(lane_lo, sink_lo, sink_hi)
    return num / den


def _mix_rows(a, f, t, above, below, bg, cw, cb, w, g_post, gate, x):
    n_rows = t.shape[0]
    ridx = lax.broadcasted_iota(jnp.int32, t.shape, 0)
    up = jnp.where(ridx == 0, above, pltpu.roll(t, 1, 0))
    dn = jnp.where(ridx == n_rows - 1, below, pltpu.roll(t, n_rows - 1, 0))
    conv = up * cw[0:1] + t * cw[1:2] + dn * cw[2:3] + cb
    h = jnp.concatenate([a, f, (conv * bg.astype(F32)).astype(BF16)], axis=1)
    y = _dot(h, w)
    r = lax.rsqrt(jnp.mean(y * y, axis=-1, keepdims=True) + NORM_EPS)
    return x + (y * r) * (gate * g_post)


def _attn_kernel(sink_ref, q_ref, g_ref, kp_ref, km_ref, kn_ref, vp_ref, vm_ref, vn_ref, kc_ref, vc_ref,
                 x_ref, f_ref, t_ref, tp_ref, tn_ref, bg_ref, cw_ref, cb_ref, w_ref, gpost_ref, mgate_ref,
                 o_ref, k_s, v_s, kctx_s, vctx_s, bias_s, s_s, p_s, r_s, a_s, *, tq, n_seq, n_ctx):
    i = pl.program_id(1)
    n_blocks = tq // BLOCK_Q
    for off, kref, vref, rows in ((0, kp_ref, vp_ref, WINDOW), (WINDOW, km_ref, vm_ref, tq),
                                  (WINDOW + tq, kn_ref, vn_ref, WINDOW)):
        for idx, kk in enumerate(_lane_half_variants(kref[0], 0.0)):
            k_s[idx, off:off + rows] = kk
        for idx, vv in enumerate(_lane_half_variants(vref[0], 1.0)):
            v_s[idx, off:off + rows] = vv
    for idx, kk in enumerate(_lane_half_variants(kc_ref[0], 0.0)):
        kctx_s[idx] = kk
    for idx, vv in enumerate(_lane_half_variants(vc_ref[0], 1.0)):
        vctx_s[idx] = vv

    ii = lax.broadcasted_iota(jnp.int32, (BLOCK_Q, WINDOW), 0)
    jj = lax.broadcasted_iota(jnp.int32, (BLOCK_Q, WINDOW), 1)
    head_band = jnp.where(jj >= ii, 0.0, -jnp.inf)
    tail_band = jnp.where(jj <= ii, 0.0, -jnp.inf)
    bias_s[0] = head_band
    bias_s[1] = tail_band
    bias_s[2] = jnp.where(i == 0, -jnp.inf, head_band)
    bias_s[3] = jnp.where(i == pl.num_programs(1) - 1, -jnp.inf, tail_band)

    head_w = n_ctx + KEY_SPAN

    def scores(sb):
        par, r0 = sb % SCORE_BUFFERS, sb * BLOCK_Q
        q = q_ref[0, r0:r0 + BLOCK_Q, :]
        for kv in range(KV_HEADS):
            pairs = range(kv * PAIRS_PER_KV, (kv + 1) * PAIRS_PER_KV)
            qg = jnp.concatenate([q[:, p * LANES:(p + 1) * LANES] for p in pairs], axis=0)
            keys = jnp.concatenate([kctx_s[2 * kv], k_s[2 * kv, r0:r0 + KEY_SPAN],
                                    kctx_s[2 * kv + 1], k_s[2 * kv + 1, r0:r0 + KEY_SPAN]], axis=0)
            s_s[par, pairs.start:pairs.stop] = _dot_nt(qg, keys).reshape(PAIRS_PER_KV, BLOCK_Q, 2 * head_w)

    def softmax(sb):
        par, spar = sb % 2, sb % SCORE_BUFFERS
        head_bias = 2 if sb == 0 else 0
        tail_bias = 3 if sb == n_blocks - 1 else 1
        n_cols = head_w // LANES
        for p in range(HEAD_PAIRS):
            for half in range(2):
                h = 2 * p + half
                sink2 = sink_ref[h] * LOG2E
                for rs in range(BLOCK_Q // SOFTMAX_ROWS):
                    rows = slice(rs * SOFTMAX_ROWS, (rs + 1) * SOFTMAX_ROWS)
                    cols = [s_s[spar, p, rows, half * head_w + t * LANES:half * head_w + (t + 1) * LANES]
                            for t in range(n_cols)]
                    first_local = n_ctx // LANES
                    cols[first_local] = cols[first_local] + bias_s[head_bias, rows]
                    cols[-1] = cols[-1] + bias_s[tail_bias, rows]
                    m = jnp.maximum(jnp.max(functools.reduce(jnp.maximum, cols), axis=-1, keepdims=True),
                                    sink2)
                    for t, col in enumerate(cols):
                        p_s[par, p, rows, half * head_w + t * LANES:half * head_w + (t + 1) * LANES] = (
                            jnp.exp2(col - m).astype(BF16))
                    r_s[par, h, rows] = jnp.broadcast_to(jnp.exp2(sink2 - m), (SOFTMAX_ROWS, LANES))

    def weighted_values(sb):
        par, r0 = sb % 2, sb * BLOCK_Q
        for kv in range(KV_HEADS):
            pairs = range(kv * PAIRS_PER_KV, (kv + 1) * PAIRS_PER_KV)
            pv = []
            for half in range(2):
                vals = jnp.concatenate([vctx_s[2 * kv + half], v_s[2 * kv + half, r0:r0 + KEY_SPAN]], axis=0)
                probs = p_s[par, pairs.start:pairs.stop, :, half * head_w:(half + 1) * head_w]
                pv.append(_dot(probs.reshape(PAIRS_PER_KV * BLOCK_Q, head_w), vals))
            for j, p in enumerate(pairs):
                rows = slice(j * BLOCK_Q, (j + 1) * BLOCK_Q)
                out = _merge_head_pair(pv[0][rows], pv[1][rows], r_s[par, 2 * p], r_s[par, 2 * p + 1])
                gate = g_ref[0, r0:r0 + BLOCK_Q, p * LANES:(p + 1) * LANES].astype(F32)
                a_s[r0:r0 + BLOCK_Q, p * LANES:(p + 1) * LANES] = (out * gate).astype(BF16)

    def conv_edge(ref, row, keep):
        return jnp.where(keep, ref[0].astype(F32)[row:row + 1, :], 0.0)

    def mix(r0):
        rows = slice(r0, r0 + MIX_ROWS)
        if r0 == 0:
            above = conv_edge(tp_ref, BF16_ROWS - 1, i > 0)
        else:
            above = t_ref[0, r0 - BF16_ROWS:r0, :].astype(F32)[BF16_ROWS - 1:, :]
        if r0 + MIX_ROWS == tq:
            below = conv_edge(tn_ref, 0, i < pl.num_programs(1) - 1)
        else:
            below = t_ref[0, r0 + MIX_ROWS:r0 + MIX_ROWS + BF16_ROWS, :].astype(F32)[:1, :]
        o_ref[0, rows] = _mix_rows(
            a_s[rows], f_ref[0, rows],
            t_ref[0, rows].astype(F32), above, below, bg_ref[0, rows], cw_ref[0], cb_ref[0],
            w_ref[0], gpost_ref[0], mgate_ref[0, pl.ds(pl.program_id(0), 1), :], x_ref[0, rows])

    per_mix = MIX_ROWS // BLOCK_Q
    ahead = SCORE_BUFFERS - 1
    for sb in range(min(ahead, n_blocks)):
        scores(sb)
    for sb in range(n_blocks):
        if sb + ahead < n_blocks:
            scores(sb + ahead)
        softmax(sb)
        weighted_values(sb)
        if (sb + 1) % per_mix == 0:
            mix((sb + 1 - per_mix) * BLOCK_Q)


def _attention_and_mix(x, q, gate, k, v, kc, vc, sink, f, t, bg, conv_w, conv_b, w_out, g_post, mod,
                       layer, *, tq):
    b, n, _ = q.shape
    nc = kc.shape[1]
    per = tq // WINDOW
    last = n // WINDOW - 1
    kvw = 2 * KV_WIDTH
    variants = 2 * KV_HEADS
    main = lambda width: pl.BlockSpec((1, tq, width), lambda bi, i: (bi, i, 0))
    prev = pl.BlockSpec((1, WINDOW, kvw), lambda bi, i: (bi, jnp.maximum(i * per - 1, 0), 0))
    nxt = pl.BlockSpec((1, WINDOW, kvw), lambda bi, i: (bi, jnp.minimum((i + 1) * per, last), 0))
    ctx = pl.BlockSpec((1, nc, kvw), lambda bi, i: (bi, 0, 0))
    lay3 = lambda shape: pl.BlockSpec((1,) + shape, lambda bi, i: (layer, 0, 0))
    t_per = tq // BF16_ROWS
    t_last = n // BF16_ROWS - 1
    t_prev = pl.BlockSpec((1, BF16_ROWS, CONV_WIDTH), lambda bi, i: (bi, jnp.maximum(i * t_per - 1, 0), 0))
    t_next = pl.BlockSpec((1, BF16_ROWS, CONV_WIDTH),
                          lambda bi, i: (bi, jnp.minimum((i + 1) * t_per, t_last), 0))
    span = tq + 2 * WINDOW
    s_cols = 2 * (nc + KEY_SPAN)
    return pl.pallas_call(
        functools.partial(_attn_kernel, tq=tq, n_seq=n, n_ctx=nc),
        grid=(b, n // tq),
        in_specs=[pl.BlockSpec(memory_space=pltpu.SMEM), main(ATTN_WIDTH), main(ATTN_WIDTH),
                  prev, main(kvw), nxt, prev, main(kvw), nxt, ctx, ctx,
                  main(D_MODEL),
                  main(FOURIER_WIDTH), main(CONV_WIDTH), t_prev, t_next, main(CONV_WIDTH),
                  lay3((3, CONV_WIDTH)), lay3((1, CONV_WIDTH)), lay3((MIX_WIDTH, D_MODEL)),
                  lay3((1, D_MODEL)),
                  pl.BlockSpec((1, MOD_ROWS, D_MODEL), lambda bi, i: (layer, 0, 2))],
        out_specs=main(D_MODEL),
        out_shape=jax.ShapeDtypeStruct(x.shape, F32),
        scratch_shapes=[pltpu.VMEM((variants, span, KV_WIDTH), BF16),
                        pltpu.VMEM((variants, span, KV_WIDTH), BF16),
                        pltpu.VMEM((variants, nc, KV_WIDTH), BF16),
                        pltpu.VMEM((variants, nc, KV_WIDTH), BF16),
                        pltpu.VMEM((4, BLOCK_Q, WINDOW), F32),
                        pltpu.VMEM((SCORE_BUFFERS, HEAD_PAIRS, BLOCK_Q, s_cols), F32),
                        pltpu.VMEM((2, HEAD_PAIRS, BLOCK_Q, s_cols), BF16),
                        pltpu.VMEM((2, ATTN_HEADS, BLOCK_Q, LANES), F32),
                        pltpu.VMEM((tq, ATTN_WIDTH), BF16)],
        compiler_params=_cparams("arbitrary", "arbitrary"),
        name="attention_and_mix",
    )(sink, q, gate, k, k, k, v, v, v, kc, vc, x, f, t, t, t, bg, conv_w, conv_b, w_out, g_post, mod)


def _ctx_attn_kernel(sink_ref, q_ref, g_ref, kc_ref, vc_ref, o_ref):
    keys = _lane_half_variants(kc_ref[0], 0.0)
    vals = _lane_half_variants(vc_ref[0], 1.0)
    for p in range(HEAD_PAIRS):
        kv = p // PAIRS_PER_KV
        qp = q_ref[0, :, p * LANES:(p + 1) * LANES]
        pv, sink_terms = [], []
        for half in range(2):
            s = _dot_nt(qp, keys[2 * kv + half])
            sink2 = sink_ref[2 * p + half] * LOG2E
            m = jnp.maximum(jnp.max(s, axis=-1, keepdims=True), sink2)
            pv.append(_dot(jnp.exp2(s - m).astype(BF16), vals[2 * kv + half]))
            sink_terms.append(jnp.exp2(sink2 - m))
        out = _merge_head_pair(pv[0], pv[1], sink_terms[0], sink_terms[1])
        gate = g_ref[0, :, p * LANES:(p + 1) * LANES].astype(F32)
        o_ref[0, :, p * LANES:(p + 1) * LANES] = (out * gate).astype(BF16)


def _context_attention(q, gate, kc, vc, sink):
    b, nc, _ = q.shape
    blk = lambda width: pl.BlockSpec((1, nc, width), lambda bi: (bi, 0, 0))
    return pl.pallas_call(
        _ctx_attn_kernel,
        grid=(b,),
        in_specs=[pl.BlockSpec(memory_space=pltpu.SMEM), blk(ATTN_WIDTH), blk(ATTN_WIDTH),
                  blk(2 * KV_WIDTH), blk(2 * KV_WIDTH)],
        out_specs=blk(ATTN_WIDTH),
        out_shape=jax.ShapeDtypeStruct((b, nc, ATTN_WIDTH), BF16),
        compiler_params=_cparams("arbitrary"),
        name="context_attention",
    )(sink, q, gate, kc, vc)


def _channel_mix_matrix(cc_ref, sc_ref, wf_ref, scale):
    wf = wf_ref[0]
    return (jnp.concatenate([_dot(cc_ref[...], wf), -_dot(sc_ref[...], wf)], axis=1) * scale).astype(BF16)


def _fourier_kernel(uf_ref, gate_ref, cc_ref, sc_ref, wf_ref, g_ref, m2_ref, o_ref, mix_s, y_s, *, n_seq):
    n1_len = n_seq // FFT_N2
    chunk = pl.program_id(1)
    halves = FOURIER_WIDTH // LANES

    @pl.when(chunk == 0)
    def _():
        zero = jnp.zeros((LANES, LANES), F32)
        scale = (n_seq * FOURIER_GROUP_DIM) ** -0.5
        re = [_dot(cc_ref[...], wf_ref[h]) * scale for h in range(halves)]
        im = [_dot(sc_ref[...], wf_ref[h]) * -scale for h in range(halves)]
        for h in range(halves):
            row = [re[h] if j == h else zero for j in range(halves)]
            row += [im[h] if j == h else zero for j in range(halves)]
            mix_s[h * LANES:(h + 1) * LANES] = jnp.concatenate(row, axis=1).astype(BF16)

    zs = [_dot(uf_ref[0, u].astype(BF16), mix_s[...]) for u in range(N2_CHUNK)]
    for u, z in enumerate(zs):
        rhs = jnp.concatenate([z[:, :FOURIER_WIDTH], z[:, FOURIER_WIDTH:]], axis=0).astype(BF16)
        y = _dot(g_ref[chunk * N2_CHUNK + u], rhs)
        y_s[chunk * N2_CHUNK + u] = y.astype(BF16).reshape(n1_len // K1_GROUP, 2 * K1_GROUP, FOURIER_WIDTH)

    @pl.when(chunk == pl.num_programs(1) - 1)
    def _():
        per_store = BF16_ROWS // K1_GROUP

        def stage2(t, carry):
            outs = []
            for u in range(per_store):
                rhs = y_s[:, t * per_store + u].reshape(FFT_N2 * 2 * K1_GROUP, FOURIER_WIDTH)
                outs.append(_dot(m2_ref[...], rhs).reshape(FFT_N2, K1_GROUP, FOURIER_WIDTH))
            r0 = pl.multiple_of(t * BF16_ROWS, BF16_ROWS)
            gate = gate_ref[0, :, pl.ds(r0, BF16_ROWS), :].astype(F32)
            o_ref[0, :, pl.ds(r0, BF16_ROWS), :] = (jnp.concatenate(outs, axis=1) * gate).astype(BF16)
            return carry

        lax.fori_loop(0, n1_len // BF16_ROWS, stage2, 0)


def _fourier_mix(uf, gate, wf_half, consts):
    b, _, n1_len, _ = uf.shape
    n = FFT_N2 * n1_len
    cc, sc, g, m2 = consts
    full = lambda shape: pl.BlockSpec(shape, lambda bi, c: (0,) * len(shape))
    whole = pl.BlockSpec((1, FFT_N2, n1_len, FOURIER_WIDTH), lambda bi, c: (bi, 0, 0, 0))
    out = pl.pallas_call(
        functools.partial(_fourier_kernel, n_seq=n),
        grid=(b, FFT_N2 // N2_CHUNK),
        in_specs=[pl.BlockSpec((1, N2_CHUNK, n1_len, FOURIER_WIDTH), lambda bi, c: (bi, c, 0, 0)),
                  whole, full((LANES, LANES)), full((LANES, LANES)), full(wf_half.shape),
                  full(g.shape), full(m2.shape)],
        out_specs=whole,
        out_shape=jax.ShapeDtypeStruct((b, FFT_N2, n1_len, FOURIER_WIDTH), BF16),
        scratch_shapes=[pltpu.VMEM((FOURIER_WIDTH, 2 * FOURIER_WIDTH), BF16),
                        pltpu.VMEM((FFT_N2, n1_len // K1_GROUP, 2 * K1_GROUP, FOURIER_WIDTH), BF16)],
        compiler_params=_cparams("arbitrary", "arbitrary"),
        name="fourier_mix",
    )(uf, gate.reshape(b, FFT_N2, n1_len, FOURIER_WIDTH), cc, sc, wf_half, g, m2)
    return out.reshape(b, n, FOURIER_WIDTH)


def _ctx_fourier_kernel(uf_ref, gate_ref, cc_ref, sc_ref, wf_ref, dft_ref, o_ref, *, n_seq):
    mix = _channel_mix_matrix(cc_ref, sc_ref, wf_ref, (n_seq * FOURIER_GROUP_DIM) ** -0.5)
    z = _dot(uf_ref[0], mix)
    rhs = jnp.concatenate([z[:, :LANES], z[:, LANES:]], axis=0).astype(BF16)
    o_ref[0] = (_dot(dft_ref[...], rhs) * gate_ref[0].astype(F32)).astype(BF16)


def _ctx_fourier_mix(uf, gate, wf_half, cc, sc, dft):
    b, n, _ = uf.shape
    halves = FOURIER_WIDTH // LANES
    full = lambda shape: pl.BlockSpec(shape, lambda bi, hf: (0,) * len(shape))
    half = pl.BlockSpec((1, n, LANES), lambda bi, hf: (bi, 0, hf))
    return pl.pallas_call(
        functools.partial(_ctx_fourier_kernel, n_seq=n),
        grid=(b, halves),
        in_specs=[half, half, full((LANES, LANES)), full((LANES, LANES)),
                  pl.BlockSpec((1, LANES, LANES), lambda bi, hf: (hf, 0, 0)), full(dft.shape)],
        out_specs=half,
        out_shape=jax.ShapeDtypeStruct((b, n, FOURIER_WIDTH), BF16),
        compiler_params=_cparams("arbitrary", "arbitrary"),
        name="context_fourier_mix",
    )(uf, gate, cc, sc, wf_half, dft)


def _ctx_mix_kernel(x_ref, a_ref, f_ref, t_ref, bg_ref, cw_ref, cb_ref, w_ref, g_ref, gate_ref, o_ref,
                    *, ctx_row):
    edge = jnp.zeros((1, CONV_WIDTH), F32)
    o_ref[0] = _mix_rows(a_ref[0], f_ref[0], t_ref[0].astype(F32), edge, edge, bg_ref[0], cw_ref[0],
                         cb_ref[0], w_ref[0], g_ref[0], gate_ref[0, ctx_row:ctx_row + 1, :], x_ref[0])


def _context_mix(x, a, f, t, bg, conv_w, conv_b, w_out, g_post, mod, layer, *, ctx_row):
    b, n, _ = x.shape
    row3 = lambda width: pl.BlockSpec((1, n, width), lambda bi: (bi, 0, 0))
    lay3 = lambda shape: pl.BlockSpec((1,) + shape, lambda bi: (layer, 0, 0))
    return pl.pallas_call(
        functools.partial(_ctx_mix_kernel, ctx_row=ctx_row),
        grid=(b,),
        in_specs=[row3(D_MODEL), row3(ATTN_WIDTH), row3(FOURIER_WIDTH), row3(CONV_WIDTH), row3(CONV_WIDTH), lay3((3, CONV_WIDTH)), lay3((1, CONV_WIDTH)),
                  lay3((MIX_WIDTH, D_MODEL)), lay3((1, D_MODEL)),
                  pl.BlockSpec((1, MOD_ROWS, D_MODEL), lambda bi: (layer, 0, 2))],
        out_specs=row3(D_MODEL),
        out_shape=jax.ShapeDtypeStruct(x.shape, F32),
        compiler_params=_cparams("arbitrary"),
        name="context_mix",
    )(x, a, f, t, bg, conv_w, conv_b, w_out, g_post, mod)


def kernel(x, c, ctx, c_ctx, w_mod, b_mod, g_pre, g_post, w_in, w_out, sink, w_fourier, conv_w, conv_b):
    depth = w_mod.shape[0]
    b, n, _ = x.shape
    nc = ctx.shape[1]
    assert b + 1 <= MOD_ROWS and n % GRID_W == 0 and n % ATTN_ROWS == 0 and n % TILE_ROWS == 0
    assert (n // FFT_N2) % BF16_ROWS == 0 and FFT_N2 % N2_CHUNK == 0

    w_in_b = w_in.astype(BF16)
    w_out_b = w_out.astype(BF16)

    rope_tabs = tuple(jnp.asarray(t) for t in _rope_tables(n))
    cc, sc = (jnp.asarray(m) for m in _channel_dft())
    stage1 = jnp.asarray(_stage1_mats(n)).astype(BF16)
    stage2 = jnp.asarray(_stage2_mat()).astype(BF16)
    ctx_dft = jnp.asarray(_dense_dft(nc)).astype(BF16)
    groups_per_half = LANES // FOURIER_GROUP_DIM
    wf_half = jnp.zeros((depth, FOURIER_GROUPS // groups_per_half, LANES, LANES), F32)
    for g in range(FOURIER_GROUPS):
        o = (g % groups_per_half) * FOURIER_GROUP_DIM
        wf_half = wf_half.at[:, g // groups_per_half, o:o + FOURIER_GROUP_DIM,
                             o:o + FOURIER_GROUP_DIM].set(w_fourier[:, g])

    c_rows = jnp.zeros((MOD_ROWS, D_MODEL), F32).at[:b].set(c).at[b].set(c_ctx)
    mod = _modulation(c_rows, w_mod, b_mod)
    g_pre3 = g_pre.reshape(depth, 1, D_MODEL)
    g_post3 = g_post.reshape(depth, 1, D_MODEL)
    conv_b3 = conv_b.reshape(depth, 1, CONV_WIDTH)

    for l in range(depth):
        update_ctx = l < depth - 1
        if update_ctx:
            qc, kc, vc, sgac, ufc, sgfc, tc, bgc = _in_projection(
                ctx, mod, g_pre3, w_in_b, l, rope_tabs=None, ctx_row=b, tm=nc)
        else:
            kc, vc = _in_projection(ctx, mod, g_pre3, w_in_b, l, rope_tabs=None, ctx_row=b, tm=nc,
                                    kv_only=True)
        q, k, v, sga, uf, sgf, t, bg = _in_projection(
            x, mod, g_pre3, w_in_b, l, rope_tabs=rope_tabs, ctx_row=None, tm=TILE_ROWS)
        f = _fourier_mix(uf, sgf, wf_half[l], (cc, sc, stage1, stage2))
        x = _attention_and_mix(x, q, sga, k, v, kc, vc, sink[l], f, t, bg, conv_w, conv_b3, w_out_b,
                               g_post3, mod, l, tq=ATTN_ROWS)
        if update_ctx:
            ac = _context_attention(qc, sgac, kc, vc, sink[l])
            fc = _ctx_fourier_mix(ufc, sgfc, wf_half[l], cc, sc, ctx_dft)
            ctx = _context_mix(ctx, ac, fc, tc, bgc, conv_w, conv_b3, w_out_b, g_post3, mod, l,
                               ctx_row=b)
    return x
```

```python
import functools
import math

import numpy as np
import jax
import jax.numpy as jnp
from jax import lax
from jax.experimental import pallas as pl
from jax.experimental.pallas import tpu as pltpu

D_MODEL = 1024
GRID_W = 64
HEAD_DIM = 64
ATTN_HEADS = 8
KV_HEADS = 2
Q_PER_KV = ATTN_HEADS // KV_HEADS
ATTN_WIDTH = ATTN_HEADS * HEAD_DIM
KV_WIDTH = KV_HEADS * HEAD_DIM
WINDOW = 128
FOURIER_GROUPS = 4
FOURIER_GROUP_DIM = 64
FOURIER_WIDTH = FOURIER_GROUPS * FOURIER_GROUP_DIM
CONV_WIDTH = 256
MIX_WIDTH = ATTN_WIDTH + FOURIER_WIDTH + CONV_WIDTH
PROJ_WIDTH = 2 * ATTN_WIDTH + 2 * KV_WIDTH + 2 * FOURIER_WIDTH + 4 * CONV_WIDTH
ROPE_FREQS = HEAD_DIM // 4
ROPE_BASE = 10000.0
NORM_EPS = 1e-6
LOG2E = math.log2(math.e)

C_Q = 0
C_K = C_Q + ATTN_WIDTH
C_V = C_K + KV_WIDTH
C_GA = C_V + KV_WIDTH
C_UF = C_GA + ATTN_WIDTH
C_GF = C_UF + FOURIER_WIDTH
C_ZC = C_GF + FOURIER_WIDTH
C_BC = C_ZC + CONV_WIDTH
C_CC = C_BC + CONV_WIDTH
C_GC = C_CC + CONV_WIDTH

LANES = 128
SUBLANES = 8
BF16_ROWS = 16
VMEM_LIMIT = 52 * 1024 * 1024

MOD_ROWS = 8
BF16 = jnp.bfloat16
F32 = jnp.float32

FFT_N2 = 64
K1_GROUP = SUBLANES
N2_CHUNK = 16
UF_PITCH = FFT_N2 + SUBLANES
SOFTMAX_ROWS = 32

BLOCK_Q = 128
KEY_SPAN = BLOCK_Q + 2 * WINDOW
HEAD_PAIRS = ATTN_WIDTH // LANES
PAIRS_PER_KV = HEAD_PAIRS // KV_HEADS

TILE_ROWS = 1024
PROJ_SUB_ROWS = 512
ATTN_ROWS = 1024
SCORE_BUFFERS = 2
MIX_ROWS = 512


def _silu(x):
    return x / (1.0 + jnp.exp(-x))


def _dot(a, b):
    return jnp.dot(a, b, preferred_element_type=F32)


def _dot_nt(a, b):
    return lax.dot_general(a, b, (((1,), (1,)), ((), ())), preferred_element_type=F32)


def _cparams(*sem):
    return pltpu.CompilerParams(dimension_semantics=sem, vmem_limit_bytes=VMEM_LIMIT)


def _rope_tables(n):
    t = np.arange(n)
    row = (t // GRID_W).astype(np.float64)
    col = (t % GRID_W).astype(np.float64)
    inv = ROPE_BASE ** (-np.arange(ROPE_FREQS, dtype=np.float64) / ROPE_FREQS)
    ar = row[:, None] * inv
    ac = col[:, None] * inv
    z = np.zeros_like(ar)
    cos_h = np.concatenate([np.cos(ar), np.cos(ar), np.cos(ac), np.cos(ac)], axis=1)
    sin_up = np.concatenate([-np.sin(ar), z, -np.sin(ac), z], axis=1)
    sin_dn = np.concatenate([z, np.sin(ar), z, np.sin(ac)], axis=1)
    rep = LANES // HEAD_DIM
    return tuple(np.tile(a, (1, rep)).astype(np.float32) for a in (cos_h, sin_up, sin_dn))


def _channel_dft():
    c = np.arange(FOURIER_GROUP_DIM)
    ang = 2.0 * np.pi * np.outer(c, c) / FOURIER_GROUP_DIM
    eye = np.eye(LANES // FOURIER_GROUP_DIM)
    return (np.kron(eye, np.cos(ang)).astype(np.float32),
            np.kron(eye, np.sin(ang)).astype(np.float32))


def _stage1_mats(n):
    n1_len = n // FFT_N2
    k1 = np.arange(n1_len)[:, None]
    n1 = np.arange(n1_len)[None, :]
    out = np.empty((FFT_N2, 2 * n1_len, 2 * n1_len), np.float32)
    for n2 in range(FFT_N2):
        ang = 2.0 * np.pi * ((k1 * (FFT_N2 * n1 + n2)) % n) / n
        ce, se = np.cos(ang), np.sin(ang)
        by_part = np.block([[ce, se], [-se, ce]]).reshape(2, n1_len // K1_GROUP, K1_GROUP, 2 * n1_len)
        out[n2] = by_part.transpose(1, 0, 2, 3).reshape(2 * n1_len, 2 * n1_len)
    return out


def _stage2_mat():
    k2 = np.arange(FFT_N2)
    ang = 2.0 * np.pi * np.outer(k2, k2) / FFT_N2
    cs = np.stack([np.cos(ang), np.sin(ang)], axis=-1)
    eye = np.eye(K1_GROUP)
    m = np.einsum('knp,rs->krnps', cs, eye)
    return m.reshape(FFT_N2 * K1_GROUP, FFT_N2 * 2 * K1_GROUP).astype(np.float32)


def _dense_dft(n):
    t = np.arange(n)
    ang = 2.0 * np.pi * (np.outer(t, t) % n) / n
    return np.concatenate([np.cos(ang), np.sin(ang)], axis=1).astype(np.float32)


def _mod_kernel(c_ref, w_ref, b_ref, o_ref):
    o_ref[0] = _dot(_silu(c_ref[...]), w_ref[0]) + b_ref[0]


def _modulation(c_rows, w_mod, b_mod):
    depth = w_mod.shape[0]
    return pl.pallas_call(
        _mod_kernel,
        grid=(depth, 3),
        in_specs=[pl.BlockSpec((MOD_ROWS, D_MODEL), lambda l, j: (0, 0)),
                  pl.BlockSpec((1, D_MODEL, D_MODEL), lambda l, j: (l, 0, j)),
                  pl.BlockSpec((1, 1, D_MODEL), lambda l, j: (l, 0, j))],
        out_specs=pl.BlockSpec((1, MOD_ROWS, D_MODEL), lambda l, j: (l, 0, j)),
        out_shape=jax.ShapeDtypeStruct((depth, MOD_ROWS, 3 * D_MODEL), F32),
        compiler_params=_cparams("arbitrary", "arbitrary"),
        name="modulation",
    )(c_rows, w_mod, b_mod.reshape(depth, 1, 3 * D_MODEL))


def _inproj_kernel(*refs, rope, ctx_row, kv_only, tm):
    x_ref, shift_ref, scale_ref, g_ref, w_ref = refs[:5]
    cos_ref, sup_ref, sdn_ref = refs[5:8] if rope else (None,) * 3
    outs = refs[8:-1] if rope else refs[5:]
    uf_s = refs[-1] if rope else None
    row = pl.program_id(0) if ctx_row is None else ctx_row
    shift = shift_ref[0, pl.ds(row, 1), :]
    gain = g_ref[0] * (1.0 + scale_ref[0, pl.ds(row, 1), :])

    def with_swapped_heads(y):
        return jnp.concatenate([y, pltpu.roll(y, HEAD_DIM, 1)], axis=1).astype(BF16)

    sub = min(tm, PROJ_SUB_ROWS)
    for s in range(tm // sub):
        rows = slice(s * sub, (s + 1) * sub)
        x = x_ref[0, rows]
        r = lax.rsqrt(jnp.mean(x * x, axis=-1, keepdims=True) + NORM_EPS)
        h = ((x * r) * gain + shift).astype(BF16)

        def proj(c0, width):
            return _dot(h, w_ref[0, :, c0:c0 + width])

        def rotate(y):
            if not rope:
                return y
            return (y * cos_ref[rows] + pltpu.roll(y, LANES - ROPE_FREQS, 1) * sup_ref[rows]
                    + pltpu.roll(y, ROPE_FREQS, 1) * sdn_ref[rows])

        kv = proj(C_K, 2 * KV_WIDTH)
        k_out = with_swapped_heads(rotate(kv[:, :KV_WIDTH]))
        v_out = with_swapped_heads(kv[:, KV_WIDTH:])
        if kv_only:
            k_ref, v_ref = outs
            k_ref[0, rows] = k_out
            v_ref[0, rows] = v_out
            continue
        q_ref, k_ref, v_ref, sga_ref, uf_ref, sgf_ref, t_ref, bg_ref = outs
        k_ref[0, rows] = k_out
        if rope:
            v_both = jnp.concatenate([kv[:, KV_WIDTH:], pltpu.roll(kv[:, KV_WIDTH:], HEAD_DIM, 1)], axis=1)
            v_ref[0, :, rows] = v_both.T.astype(BF16)
        else:
            v_ref[0, rows] = v_out
        q = proj(C_Q, ATTN_WIDTH)
        for p in range(HEAD_PAIRS):
            sl = slice(p * LANES, (p + 1) * LANES)
            q_ref[0, rows, sl] = (rotate(q[:, sl]) * (HEAD_DIM ** -0.5 * LOG2E)).astype(BF16)
        sga_ref[0, rows] = _silu(proj(C_GA, ATTN_WIDTH)).astype(BF16)
        uf = proj(C_UF, FOURIER_WIDTH)
        if rope:
            per_sub = sub // FFT_N2
            for slab in range(FOURIER_WIDTH // LANES):
                for j in range(per_sub):
                    uf_s[s, slab, j * UF_PITCH:j * UF_PITCH + FFT_N2] = (
                        uf[j * FFT_N2:(j + 1) * FFT_N2, slab * LANES:(slab + 1) * LANES])
            for n2 in range(FFT_N2):
                uf_ref[0, n2, s * per_sub:(s + 1) * per_sub, :] = jnp.concatenate(
                    [uf_s[s, slab, pl.ds(n2, per_sub, stride=UF_PITCH), :]
                     for slab in range(FOURIER_WIDTH // LANES)], axis=1)
        else:
            uf_ref[0, rows] = uf.astype(BF16)
        sgf_ref[0, rows] = _silu(proj(C_GF, FOURIER_WIDTH)).astype(BF16)
        t_ref[0, rows] = (proj(C_CC, CONV_WIDTH) * proj(C_ZC, CONV_WIDTH)).astype(BF16)
        bg_ref[0, rows] = (proj(C_BC, CONV_WIDTH) * _silu(proj(C_GC, CONV_WIDTH))).astype(BF16)


def _in_projection(x, mod, g_pre, w_in, layer, *, rope_tabs, ctx_row, tm, kv_only=False):
    b, n, _ = x.shape
    rope = rope_tabs is not None
    row3 = lambda width: pl.BlockSpec((1, tm, width), lambda bi, i: (bi, i, 0))
    in_specs = [row3(D_MODEL),
                pl.BlockSpec((1, MOD_ROWS, D_MODEL), lambda bi, i: (layer, 0, 0)),
                pl.BlockSpec((1, MOD_ROWS, D_MODEL), lambda bi, i: (layer, 0, 1)),
                pl.BlockSpec((1, 1, D_MODEL), lambda bi, i: (layer, 0, 0)),
                pl.BlockSpec((1, D_MODEL, PROJ_WIDTH), lambda bi, i: (layer, 0, 0))]
    args = [x, mod, mod, g_pre, w_in]
    if rope:
        in_specs += [pl.BlockSpec((tm, LANES), lambda bi, i: (i, 0))] * 3
        args += list(rope_tabs)
    widths = (ATTN_WIDTH, 2 * KV_WIDTH, 2 * KV_WIDTH, ATTN_WIDTH, FOURIER_WIDTH, FOURIER_WIDTH,
              CONV_WIDTH, CONV_WIDTH)
    if kv_only:
        widths = widths[1:3]
    out_specs = [row3(w) for w in widths]
    out_shape = [jax.ShapeDtypeStruct((b, n, w), BF16) for w in widths]
    scratch = []
    if rope:
        sub = min(tm, PROJ_SUB_ROWS)
        assert sub % (FFT_N2 * SUBLANES) == 0
        out_specs[4] = pl.BlockSpec((1, FFT_N2, tm // FFT_N2, FOURIER_WIDTH), lambda bi, i: (bi, 0, i, 0))
        out_shape[4] = jax.ShapeDtypeStruct((b, FFT_N2, n // FFT_N2, FOURIER_WIDTH), F32)
        out_specs[2] = pl.BlockSpec((1, 2 * KV_WIDTH, tm), lambda bi, i: (bi, 0, i))
        out_shape[2] = jax.ShapeDtypeStruct((b, 2 * KV_WIDTH, n), BF16)
        scratch = [pltpu.VMEM((tm // sub, FOURIER_WIDTH // LANES, sub // FFT_N2 * UF_PITCH, LANES), F32)]
    return pl.pallas_call(
        functools.partial(_inproj_kernel, rope=rope, ctx_row=ctx_row, kv_only=kv_only, tm=tm),
        grid=(b, n // tm),
        in_specs=in_specs,
        out_specs=out_specs,
        out_shape=out_shape,
        scratch_shapes=scratch,
        compiler_params=_cparams("arbitrary", "arbitrary"),
        name="in_projection_rope" if rope else ("in_projection_ctx_kv" if kv_only else "in_projection_ctx"),
    )(*args)


def _lane_half_variants(blk, fill):
    straight, swapped = blk[:, :KV_WIDTH], blk[:, KV_WIDTH:]
    lo = lax.broadcasted_iota(jnp.int32, straight.shape, 1) < HEAD_DIM
    other = jnp.full_like(straight, fill)
    return (jnp.where(lo, straight, other), jnp.where(lo, other, swapped),
            jnp.where(lo, swapped, other), jnp.where(lo, other, straight))


def _row_half_variants(blk_t, fill):
    straight, swapped = blk_t[:KV_WIDTH], blk_t[KV_WIDTH:]
    lo = lax.broadcasted_iota(jnp.int32, straight.shape, 0) < HEAD_DIM
    other = jnp.full_like(straight, fill)
    return (jnp.where(lo, straight, other), jnp.where(lo, other, swapped),
            jnp.where(lo, swapped, other), jnp.where(lo, other, straight))


def _merge_head_pair(pv_lo, pv_hi, sink_lo, sink_hi):
    lane_lo = lax.broadcasted_iota(jnp.int32, pv_lo.shape, 1) < HEAD_DIM
    num = jnp.where(lane_lo, pv_lo, pv_hi)
    den = pltpu.roll(jnp.where(lane_lo, pv_hi, pv_lo), HEAD_DIM, 1) + jnp.where(lane_lo, sink_lo, sink_hi)
    return num / den


def _mix_rows(a, f, t, above, below, bg, cw, cb, w, g_post, gate, x):
    n_rows = t.shape[0]
    ridx = lax.broadcasted_iota(jnp.int32, t.shape, 0)
    up = jnp.where(ridx == 0, above, pltpu.roll(t, 1, 0))
    dn = jnp.where(ridx == n_rows - 1, below, pltpu.roll(t, n_rows - 1, 0))
    conv = up * cw[0:1] + t * cw[1:2] + dn * cw[2:3] + cb
    h = jnp.concatenate([a, f, (conv * bg.astype(F32)).astype(BF16)], axis=1)
    y = _dot(h, w)
    r = lax.rsqrt(jnp.mean(y * y, axis=-1, keepdims=True) + NORM_EPS)
    return x + (y * r) * (gate * g_post)


def _attn_kernel(sink_ref, q_ref, g_ref, kp_ref, km_ref, kn_ref, vp_ref, vm_ref, vn_ref, kc_ref, vc_ref,
                 x_ref, f_ref, t_ref, tp_ref, tn_ref, bg_ref, cw_ref, cb_ref, w_ref, gpost_ref, mgate_ref,
                 o_ref, k_s, v_s, kctx_s, vctx_s, bias_s, s_s, p_s, r_s, a_s, *, tq, n_seq, n_ctx):
    i = pl.program_id(1)
    n_blocks = tq // BLOCK_Q
    for off, kref, vref, rows in ((0, kp_ref, vp_ref, WINDOW), (WINDOW, km_ref, vm_ref, tq),
                                  (WINDOW + tq, kn_ref, vn_ref, WINDOW)):
        for idx, kk in enumerate(_lane_half_variants(kref[0], 0.0)):
            k_s[idx, off:off + rows] = kk
        for idx, vv in enumerate(_row_half_variants(vref[0], 1.0)):
            v_s[idx, :, off:off + rows] = vv
    for idx, kk in enumerate(_lane_half_variants(kc_ref[0], 0.0)):
        kctx_s[idx] = kk
    for idx, vv in enumerate(_row_half_variants(vc_ref[0].astype(F32).T.astype(BF16), 1.0)):
        vctx_s[idx] = vv

    jj = lax.broadcasted_iota(jnp.int32, (WINDOW, BLOCK_Q), 0)
    ii = lax.broadcasted_iota(jnp.int32, (WINDOW, BLOCK_Q), 1)
    head_band = jnp.where(jj >= ii, 0.0, -jnp.inf)
    tail_band = jnp.where(jj <= ii, 0.0, -jnp.inf)
    bias_s[0] = head_band
    bias_s[1] = tail_band
    bias_s[2] = jnp.where(i == 0, -jnp.inf, head_band)
    bias_s[3] = jnp.where(i == pl.num_programs(1) - 1, -jnp.inf, tail_band)

    head_w = n_ctx + KEY_SPAN

    def scores(sb):
        par, r0 = sb % SCORE_BUFFERS, sb * BLOCK_Q
        q = q_ref[0, r0:r0 + BLOCK_Q, :]
        for kv in range(KV_HEADS):
            pairs = range(kv * PAIRS_PER_KV, (kv + 1) * PAIRS_PER_KV)
            qg = jnp.concatenate([q[:, p * LANES:(p + 1) * LANES] for p in pairs], axis=0)
            keys = jnp.concatenate([kctx_s[2 * kv], k_s[2 * kv, r0:r0 + KEY_SPAN],
                                    kctx_s[2 * kv + 1], k_s[2 * kv + 1, r0:r0 + KEY_SPAN]], axis=0)
            s_s[par, kv] = _dot_nt(keys, qg)

    def softmax(sb):
        par, spar = sb % 2, sb % SCORE_BUFFERS
        head_bias = 2 if sb == 0 else 0
        tail_bias = 3 if sb == n_blocks - 1 else 1
        n_blk = head_w // LANES
        first_local = n_ctx // LANES
        for p in range(HEAD_PAIRS):
            kv, lanes = p // PAIRS_PER_KV, slice((p % PAIRS_PER_KV) * BLOCK_Q, (p % PAIRS_PER_KV + 1) * BLOCK_Q)
            for half in range(2):
                h = 2 * p + half
                sink2 = sink_ref[h] * LOG2E

                def key_block(t):
                    rows = slice(half * head_w + t * LANES, half * head_w + (t + 1) * LANES)
                    blk = s_s[spar, kv, rows, lanes]
                    if t == first_local:
                        blk = blk + bias_s[head_bias]
                    elif t == n_blk - 1:
                        blk = blk + bias_s[tail_bias]
                    return rows, blk

                mx = functools.reduce(jnp.maximum, [key_block(t)[1] for t in range(n_blk)])
                m = jnp.maximum(jnp.max(mx, axis=0, keepdims=True), sink2)
                for t in range(n_blk):
                    rows, blk = key_block(t)
                    p_s[par, kv, rows, lanes] = jnp.exp2(blk - m).astype(BF16)
                r_s[par, h] = jnp.broadcast_to(jnp.exp2(sink2 - m), (SUBLANES, BLOCK_Q))

    def weighted_values(sb):
        par, r0 = sb % 2, sb * BLOCK_Q
        for kv in range(KV_HEADS):
            pv = []
            for half in range(2):
                vals_t = jnp.concatenate([vctx_s[2 * kv + half], v_s[2 * kv + half, :, r0:r0 + KEY_SPAN]],
                                         axis=1)
                pv.append(_dot(vals_t, p_s[par, kv, half * head_w:(half + 1) * head_w, :]))
            for j in range(PAIRS_PER_KV):
                p = kv * PAIRS_PER_KV + j
                lanes = slice(j * BLOCK_Q, (j + 1) * BLOCK_Q)
                lo, hi = pv[0][:, lanes], pv[1][:, lanes]
                num = jnp.concatenate([lo[:HEAD_DIM], hi[HEAD_DIM:]], axis=0)
                den = jnp.concatenate([lo[HEAD_DIM:], hi[:HEAD_DIM]], axis=0)
                sink_t = jnp.concatenate([jnp.broadcast_to(r_s[par, 2 * p + half, 0:1], (HEAD_DIM, BLOCK_Q))
                                          for half in range(2)], axis=0)
                out = (num / (den + sink_t)).T
                gate = g_ref[0, r0:r0 + BLOCK_Q, p * LANES:(p + 1) * LANES].astype(F32)
                a_s[r0:r0 + BLOCK_Q, p * LANES:(p + 1) * LANES] = (out * gate).astype(BF16)

    def conv_edge(ref, row, keep):
        return jnp.where(keep, ref[0].astype(F32)[row:row + 1, :], 0.0)

    def mix(r0):
        rows = slice(r0, r0 + MIX_ROWS)
        if r0 == 0:
            above = conv_edge(tp_ref, BF16_ROWS - 1, i > 0)
        else:
            above = t_ref[0, r0 - BF16_ROWS:r0, :].astype(F32)[BF16_ROWS - 1:, :]
        if r0 + MIX_ROWS == tq:
            below = conv_edge(tn_ref, 0, i < pl.num_programs(1) - 1)
        else:
            below = t_ref[0, r0 + MIX_ROWS:r0 + MIX_ROWS + BF16_ROWS, :].astype(F32)[:1, :]
        o_ref[0, rows] = _mix_rows(
            a_s[rows], f_ref[0, rows],
            t_ref[0, rows].astype(F32), above, below, bg_ref[0, rows], cw_ref[0], cb_ref[0],
            w_ref[0], gpost_ref[0], mgate_ref[0, pl.ds(pl.program_id(0), 1), :], x_ref[0, rows])

    per_mix = MIX_ROWS // BLOCK_Q
    ahead = SCORE_BUFFERS - 1
    for sb in range(min(ahead, n_blocks)):
        scores(sb)
    for sb in range(n_blocks):
        if sb + ahead < n_blocks:
            scores(sb + ahead)
        softmax(sb)
        weighted_values(sb)
        if (sb + 1) % per_mix == 0:
            mix((sb + 1 - per_mix) * BLOCK_Q)


def _attention_and_mix(x, q, gate, k, v, kc, vc, sink, f, t, bg, conv_w, conv_b, w_out, g_post, mod,
                       layer, *, tq):
    b, n, _ = q.shape
    nc = kc.shape[1]
    per = tq // WINDOW
    last = n // WINDOW - 1
    kvw = 2 * KV_WIDTH
    variants = 2 * KV_HEADS
    main = lambda width: pl.BlockSpec((1, tq, width), lambda bi, i: (bi, i, 0))
    prev = pl.BlockSpec((1, WINDOW, kvw), lambda bi, i: (bi, jnp.maximum(i * per - 1, 0), 0))
    nxt = pl.BlockSpec((1, WINDOW, kvw), lambda bi, i: (bi, jnp.minimum((i + 1) * per, last), 0))
    ctx = pl.BlockSpec((1, nc, kvw), lambda bi, i: (bi, 0, 0))
    lay3 = lambda shape: pl.BlockSpec((1,) + shape, lambda bi, i: (layer, 0, 0))
    t_per = tq // BF16_ROWS
    t_last = n // BF16_ROWS - 1
    t_prev = pl.BlockSpec((1, BF16_ROWS, CONV_WIDTH), lambda bi, i: (bi, jnp.maximum(i * t_per - 1, 0), 0))
    t_next = pl.BlockSpec((1, BF16_ROWS, CONV_WIDTH),
                          lambda bi, i: (bi, jnp.minimum((i + 1) * t_per, t_last), 0))
    v_prev = pl.BlockSpec((1, kvw, WINDOW), lambda bi, i: (bi, 0, jnp.maximum(i * per - 1, 0)))
    v_main = pl.BlockSpec((1, kvw, tq), lambda bi, i: (bi, 0, i))
    v_next = pl.BlockSpec((1, kvw, WINDOW), lambda bi, i: (bi, 0, jnp.minimum((i + 1) * per, last)))
    span = tq + 2 * WINDOW
    s_rows = 2 * (nc + KEY_SPAN)
    group_q = PAIRS_PER_KV * BLOCK_Q
    return pl.pallas_call(
        functools.partial(_attn_kernel, tq=tq, n_seq=n, n_ctx=nc),
        grid=(b, n // tq),
        in_specs=[pl.BlockSpec(memory_space=pltpu.SMEM), main(ATTN_WIDTH), main(ATTN_WIDTH),
                  prev, main(kvw), nxt, v_prev, v_main, v_next, ctx, ctx,
                  main(D_MODEL),
                  main(FOURIER_WIDTH), main(CONV_WIDTH), t_prev, t_next, main(CONV_WIDTH),
                  lay3((3, CONV_WIDTH)), lay3((1, CONV_WIDTH)), lay3((MIX_WIDTH, D_MODEL)),
                  lay3((1, D_MODEL)),
                  pl.BlockSpec((1, MOD_ROWS, D_MODEL), lambda bi, i: (layer, 0, 2))],
        out_specs=main(D_MODEL),
        out_shape=jax.ShapeDtypeStruct(x.shape, F32),
        scratch_shapes=[pltpu.VMEM((variants, span, KV_WIDTH), BF16),
                        pltpu.VMEM((variants, KV_WIDTH, span), BF16),
                        pltpu.VMEM((variants, nc, KV_WIDTH), BF16),
                        pltpu.VMEM((variants, KV_WIDTH, nc), BF16),
                        pltpu.VMEM((4, WINDOW, BLOCK_Q), F32),
                        pltpu.VMEM((SCORE_BUFFERS, KV_HEADS, s_rows, group_q), F32),
                        pltpu.VMEM((2, KV_HEADS, s_rows, group_q), BF16),
                        pltpu.VMEM((2, ATTN_HEADS, SUBLANES, BLOCK_Q), F32),
                        pltpu.VMEM((tq, ATTN_WIDTH), BF16)],
        compiler_params=_cparams("arbitrary", "arbitrary"),
        name="attention_and_mix",
    )(sink, q, gate, k, k, k, v, v, v, kc, vc, x, f, t, t, t, bg, conv_w, conv_b, w_out, g_post, mod)


def _ctx_attn_kernel(sink_ref, q_ref, g_ref, kc_ref, vc_ref, o_ref):
    keys = _lane_half_variants(kc_ref[0], 0.0)
    vals = _lane_half_variants(vc_ref[0], 1.0)
    for p in range(HEAD_PAIRS):
        kv = p // PAIRS_PER_KV
        qp = q_ref[0, :, p * LANES:(p + 1) * LANES]
        pv, sink_terms = [], []
        for half in range(2):
            s = _dot_nt(qp, keys[2 * kv + half])
            sink2 = sink_ref[2 * p + half] * LOG2E
            m = jnp.maximum(jnp.max(s, axis=-1, keepdims=True), sink2)
            pv.append(_dot(jnp.exp2(s - m).astype(BF16), vals[2 * kv + half]))
            sink_terms.append(jnp.exp2(sink2 - m))
        out = _merge_head_pair(pv[0], pv[1], sink_terms[0], sink_terms[1])
        gate = g_ref[0, :, p * LANES:(p + 1) * LANES].astype(F32)
        o_ref[0, :, p * LANES:(p + 1) * LANES] = (out * gate).astype(BF16)


def _context_attention(q, gate, kc, vc, sink):
    b, nc, _ = q.shape
    blk = lambda width: pl.BlockSpec((1, nc, width), lambda bi: (bi, 0, 0))
    return pl.pallas_call(
        _ctx_attn_kernel,
        grid=(b,),
        in_specs=[pl.BlockSpec(memory_space=pltpu.SMEM), blk(ATTN_WIDTH), blk(ATTN_WIDTH),
                  blk(2 * KV_WIDTH), blk(2 * KV_WIDTH)],
        out_specs=blk(ATTN_WIDTH),
        out_shape=jax.ShapeDtypeStruct((b, nc, ATTN_WIDTH), BF16),
        compiler_params=_cparams("arbitrary"),
        name="context_attention",
    )(sink, q, gate, kc, vc)


def _channel_mix_matrix(cc_ref, sc_ref, wf_ref, scale):
    wf = wf_ref[0]
    return (jnp.concatenate([_dot(cc_ref[...], wf), -_dot(sc_ref[...], wf)], axis=1) * scale).astype(BF16)


def _fourier_kernel(uf_ref, gate_ref, cc_ref, sc_ref, wf_ref, g_ref, m2_ref, o_ref, mix_s, y_s, *, n_seq):
    n1_len = n_seq // FFT_N2
    chunk = pl.program_id(1)
    halves = FOURIER_WIDTH // LANES

    @pl.when(chunk == 0)
    def _():
        zero = jnp.zeros((LANES, LANES), F32)
        scale = (n_seq * FOURIER_GROUP_DIM) ** -0.5
        re = [_dot(cc_ref[...], wf_ref[h]) * scale for h in range(halves)]
        im = [_dot(sc_ref[...], wf_ref[h]) * -scale for h in range(halves)]
        for h in range(halves):
            row = [re[h] if j == h else zero for j in range(halves)]
            row += [im[h] if j == h else zero for j in range(halves)]
            mix_s[h * LANES:(h + 1) * LANES] = jnp.concatenate(row, axis=1).astype(BF16)

    zs = [_dot(uf_ref[0, u].astype(BF16), mix_s[...]) for u in range(N2_CHUNK)]
    for u, z in enumerate(zs):
        rhs = jnp.concatenate([z[:, :FOURIER_WIDTH], z[:, FOURIER_WIDTH:]], axis=0).astype(BF16)
        y = _dot(g_ref[chunk * N2_CHUNK + u], rhs)
        y_s[chunk * N2_CHUNK + u] = y.astype(BF16).reshape(n1_len // K1_GROUP, 2 * K1_GROUP, FOURIER_WIDTH)

    @pl.when(chunk == pl.num_programs(1) - 1)
    def _():
        per_store = BF16_ROWS // K1_GROUP

        def stage2(t, carry):
            outs = []
            for u in range(per_store):
                rhs = y_s[:, t * per_store + u].reshape(FFT_N2 * 2 * K1_GROUP, FOURIER_WIDTH)
                outs.append(_dot(m2_ref[...], rhs).reshape(FFT_N2, K1_GROUP, FOURIER_WIDTH))
            r0 = pl.multiple_of(t * BF16_ROWS, BF16_ROWS)
            gate = gate_ref[0, :, pl.ds(r0, BF16_ROWS), :].astype(F32)
            o_ref[0, :, pl.ds(r0, BF16_ROWS), :] = (jnp.concatenate(outs, axis=1) * gate).astype(BF16)
            return carry

        lax.fori_loop(0, n1_len // BF16_ROWS, stage2, 0)


def _fourier_mix(uf, gate, wf_half, consts):
    b, _, n1_len, _ = uf.shape
    n = FFT_N2 * n1_len
    cc, sc, g, m2 = consts
    full = lambda shape: pl.BlockSpec(shape, lambda bi, c: (0,) * len(shape))
    whole = pl.BlockSpec((1, FFT_N2, n1_len, FOURIER_WIDTH), lambda bi, c: (bi, 0, 0, 0))
    out = pl.pallas_call(
        functools.partial(_fourier_kernel, n_seq=n),
        grid=(b, FFT_N2 // N2_CHUNK),
        in_specs=[pl.BlockSpec((1, N2_CHUNK, n1_len, FOURIER_WIDTH), lambda bi, c: (bi, c, 0, 0)),
                  whole, full((LANES, LANES)), full((LANES, LANES)), full(wf_half.shape),
                  full(g.shape), full(m2.shape)],
        out_specs=whole,
        out_shape=jax.ShapeDtypeStruct((b, FFT_N2, n1_len, FOURIER_WIDTH), BF16),
        scratch_shapes=[pltpu.VMEM((FOURIER_WIDTH, 2 * FOURIER_WIDTH), BF16),
                        pltpu.VMEM((FFT_N2, n1_len // K1_GROUP, 2 * K1_GROUP, FOURIER_WIDTH), BF16)],
        compiler_params=_cparams("arbitrary", "arbitrary"),
        name="fourier_mix",
    )(uf, gate.reshape(b, FFT_N2, n1_len, FOURIER_WIDTH), cc, sc, wf_half, g, m2)
    return out.reshape(b, n, FOURIER_WIDTH)


def _ctx_fourier_kernel(uf_ref, gate_ref, cc_ref, sc_ref, wf_ref, dft_ref, o_ref, *, n_seq):
    mix = _channel_mix_matrix(cc_ref, sc_ref, wf_ref, (n_seq * FOURIER_GROUP_DIM) ** -0.5)
    z = _dot(uf_ref[0], mix)
    rhs = jnp.concatenate([z[:, :LANES], z[:, LANES:]], axis=0).astype(BF16)
    o_ref[0] = (_dot(dft_ref[...], rhs) * gate_ref[0].astype(F32)).astype(BF16)


def _ctx_fourier_mix(uf, gate, wf_half, cc, sc, dft):
    b, n, _ = uf.shape
    halves = FOURIER_WIDTH // LANES
    full = lambda shape: pl.BlockSpec(shape, lambda bi, hf: (0,) * len(shape))
    half = pl.BlockSpec((1, n, LANES), lambda bi, hf: (bi, 0, hf))
    return pl.pallas_call(
        functools.partial(_ctx_fourier_kernel, n_seq=n),
        grid=(b, halves),
        in_specs=[half, half, full((LANES, LANES)), full((LANES, LANES)),
                  pl.BlockSpec((1, LANES, LANES), lambda bi, hf: (hf, 0, 0)), full(dft.shape)],
        out_specs=half,
        out_shape=jax.ShapeDtypeStruct((b, n, FOURIER_WIDTH), BF16),
        compiler_params=_cparams("arbitrary", "arbitrary"),
        name="context_fourier_mix",
    )(uf, gate, cc, sc, wf_half, dft)


def _ctx_mix_kernel(x_ref, a_ref, f_ref, t_ref, bg_ref, cw_ref, cb_ref, w_ref, g_ref, gate_ref, o_ref,
                    *, ctx_row):
    edge = jnp.zeros((1, CONV_WIDTH), F32)
    o_ref[0] = _mix_rows(a_ref[0], f_ref[0], t_ref[0].astype(F32), edge, edge, bg_ref[0], cw_ref[0],
                         cb_ref[0], w_ref[0], g_ref[0], gate_ref[0, ctx_row:ctx_row + 1, :], x_ref[0])


def _context_mix(x, a, f, t, bg, conv_w, conv_b, w_out, g_post, mod, layer, *, ctx_row):
    b, n, _ = x.shape
    row3 = lambda width: pl.BlockSpec((1, n, width), lambda bi: (bi, 0, 0))
    lay3 = lambda shape: pl.BlockSpec((1,) + shape, lambda bi: (layer, 0, 0))
    return pl.pallas_call(
        functools.partial(_ctx_mix_kernel, ctx_row=ctx_row),
        grid=(b,),
        in_specs=[row3(D_MODEL), row3(ATTN_WIDTH), row3(FOURIER_WIDTH), row3(CONV_WIDTH), row3(CONV_WIDTH), lay3((3, CONV_WIDTH)), lay3((1, CONV_WIDTH)),
                  lay3((MIX_WIDTH, D_MODEL)), lay3((1, D_MODEL)),
                  pl.BlockSpec((1, MOD_ROWS, D_MODEL), lambda bi: (layer, 0, 2))],
        out_specs=row3(D_MODEL),
        out_shape=jax.ShapeDtypeStruct(x.shape, F32),
        compiler_params=_cparams("arbitrary"),
        name="context_mix",
    )(x, a, f, t, bg, conv_w, conv_b, w_out, g_post, mod)


def kernel(x, c, ctx, c_ctx, w_mod, b_mod, g_pre, g_post, w_in, w_out, sink, w_fourier, conv_w, conv_b):
    depth = w_mod.shape[0]
    b, n, _ = x.shape
    nc = ctx.shape[1]
    assert b + 1 <= MOD_ROWS and n % GRID_W == 0 and n % ATTN_ROWS == 0 and n % TILE_ROWS == 0
    assert (n // FFT_N2) % BF16_ROWS == 0 and FFT_N2 % N2_CHUNK == 0

    w_in_b = w_in.astype(BF16)
    w_out_b = w_out.astype(BF16)

    rope_tabs = tuple(jnp.asarray(t) for t in _rope_tables(n))
    cc, sc = (jnp.asarray(m) for m in _channel_dft())
    stage1 = jnp.asarray(_stage1_mats(n)).astype(BF16)
    stage2 = jnp.asarray(_stage2_mat()).astype(BF16)
    ctx_dft = jnp.asarray(_dense_dft(nc)).astype(BF16)
    groups_per_half = LANES // FOURIER_GROUP_DIM
    wf_half = jnp.zeros((depth, FOURIER_GROUPS // groups_per_half, LANES, LANES), F32)
    for g in range(FOURIER_GROUPS):
        o = (g % groups_per_half) * FOURIER_GROUP_DIM
        wf_half = wf_half.at[:, g // groups_per_half, o:o + FOURIER_GROUP_DIM,
                             o:o + FOURIER_GROUP_DIM].set(w_fourier[:, g])

    c_rows = jnp.zeros((MOD_ROWS, D_MODEL), F32).at[:b].set(c).at[b].set(c_ctx)
    mod = _modulation(c_rows, w_mod, b_mod)
    g_pre3 = g_pre.reshape(depth, 1, D_MODEL)
    g_post3 = g_post.reshape(depth, 1, D_MODEL)
    conv_b3 = conv_b.reshape(depth, 1, CONV_WIDTH)

    for l in range(depth):
        update_ctx = l < depth - 1
        if update_ctx:
            qc, kc, vc, sgac, ufc, sgfc, tc, bgc = _in_projection(
                ctx, mod, g_pre3, w_in_b, l, rope_tabs=None, ctx_row=b, tm=nc)
        else:
            kc, vc = _in_projection(ctx, mod, g_pre3, w_in_b, l, rope_tabs=None, ctx_row=b, tm=nc,
                                    kv_only=True)
        q, k, v, sga, uf, sgf, t, bg = _in_projection(
            x, mod, g_pre3, w_in_b, l, rope_tabs=rope_tabs, ctx_row=None, tm=TILE_ROWS)
        f = _fourier_mix(uf, sgf, wf_half[l], (cc, sc, stage1, stage2))
        x = _attention_and_mix(x, q, sga, k, v, kc, vc, sink[l], f, t, bg, conv_w, conv_b3, w_out_b,
                               g_post3, mod, l, tq=ATTN_ROWS)
        if update_ctx:
            ac = _context_attention(qc, sgac, kc, vc, sink[l])
            fc = _ctx_fourier_mix(ufc, sgfc, wf_half[l], cc, sc, ctx_dft)
            ctx = _context_mix(ctx, ac, fc, tc, bgc, conv_w, conv_b3, w_out_b, g_post3, mod, l,
                               ctx_row=b)
    return x
```

```python
import functools
import math

import numpy as np
import jax
import jax.numpy as jnp
from jax import lax
from jax.experimental import pallas as pl
from jax.experimental.pallas import tpu as pltpu

D_MODEL = 1024
GRID_W = 64
HEAD_DIM = 64
ATTN_HEADS = 8
KV_HEADS = 2
Q_PER_KV = ATTN_HEADS // KV_HEADS
ATTN_WIDTH = ATTN_HEADS * HEAD_DIM
KV_WIDTH = KV_HEADS * HEAD_DIM
WINDOW = 128
FOURIER_GROUPS = 4
FOURIER_GROUP_DIM = 64
FOURIER_WIDTH = FOURIER_GROUPS * FOURIER_GROUP_DIM
CONV_WIDTH = 256
MIX_WIDTH = ATTN_WIDTH + FOURIER_WIDTH + CONV_WIDTH
PROJ_WIDTH = 2 * ATTN_WIDTH + 2 * KV_WIDTH + 2 * FOURIER_WIDTH + 4 * CONV_WIDTH
ROPE_FREQS = HEAD_DIM // 4
ROPE_BASE = 10000.0
NORM_EPS = 1e-6
LOG2E = math.log2(math.e)

C_Q = 0
C_K = C_Q + ATTN_WIDTH
C_V = C_K + KV_WIDTH
C_GA = C_V + KV_WIDTH
C_UF = C_GA + ATTN_WIDTH
C_GF = C_UF + FOURIER_WIDTH
C_ZC = C_GF + FOURIER_WIDTH
C_BC = C_ZC + CONV_WIDTH
C_CC = C_BC + CONV_WIDTH
C_GC = C_CC + CONV_WIDTH

LANES = 128
SUBLANES = 8
BF16_ROWS = 16
VMEM_LIMIT = 52 * 1024 * 1024

MOD_ROWS = 8
BF16 = jnp.bfloat16
F32 = jnp.float32

FFT_N2 = 64
K1_GROUP = SUBLANES
N2_CHUNK = 32
UF_PITCH = FFT_N2 + SUBLANES
SOFTMAX_ROWS = 32

BLOCK_Q = 128
KEY_SPAN = BLOCK_Q + 2 * WINDOW
HEAD_PAIRS = ATTN_WIDTH // LANES
PAIRS_PER_KV = HEAD_PAIRS // KV_HEADS

TILE_ROWS = 1024
PROJ_SUB_ROWS = 512
ATTN_ROWS = 1024
SCORE_BUFFERS = 2
MIX_ROWS = 512


def _silu(x):
    return x / (1.0 + jnp.exp(-x))


def _dot(a, b):
    return jnp.dot(a, b, preferred_element_type=F32)


def _dot_nt(a, b):
    return lax.dot_general(a, b, (((1,), (1,)), ((), ())), preferred_element_type=F32)


def _cparams(*sem):
    return pltpu.CompilerParams(dimension_semantics=sem, vmem_limit_bytes=VMEM_LIMIT)


def _rope_tables(n):
    t = np.arange(n)
    row = (t // GRID_W).astype(np.float64)
    col = (t % GRID_W).astype(np.float64)
    inv = ROPE_BASE ** (-np.arange(ROPE_FREQS, dtype=np.float64) / ROPE_FREQS)
    ar = row[:, None] * inv
    ac = col[:, None] * inv
    z = np.zeros_like(ar)
    cos_h = np.concatenate([np.cos(ar), np.cos(ar), np.cos(ac), np.cos(ac)], axis=1)
    sin_up = np.concatenate([-np.sin(ar), z, -np.sin(ac), z], axis=1)
    sin_dn = np.concatenate([z, np.sin(ar), z, np.sin(ac)], axis=1)
    rep = LANES // HEAD_DIM
    return tuple(np.tile(a, (1, rep)).astype(np.float32) for a in (cos_h, sin_up, sin_dn))


def _channel_dft():
    c = np.arange(FOURIER_GROUP_DIM)
    ang = 2.0 * np.pi * np.outer(c, c) / FOURIER_GROUP_DIM
    eye = np.eye(LANES // FOURIER_GROUP_DIM)
    return (np.kron(eye, np.cos(ang)).astype(np.float32),
            np.kron(eye, np.sin(ang)).astype(np.float32))


def _stage1_mats(n):
    n1_len = n // FFT_N2
    k1 = np.arange(n1_len)[:, None]
    n1 = np.arange(n1_len)[None, :]
    out = np.empty((FFT_N2, 2 * n1_len, 2 * n1_len), np.float32)
    for n2 in range(FFT_N2):
        ang = 2.0 * np.pi * ((k1 * (FFT_N2 * n1 + n2)) % n) / n
        ce, se = np.cos(ang), np.sin(ang)
        by_part = np.block([[ce, se], [-se, ce]]).reshape(2, n1_len // K1_GROUP, K1_GROUP, 2 * n1_len)
        out[n2] = by_part.transpose(1, 0, 2, 3).reshape(2 * n1_len, 2 * n1_len)
    return out


def _stage2_mat():
    k2 = np.arange(FFT_N2)
    ang = 2.0 * np.pi * np.outer(k2, k2) / FFT_N2
    cs = np.stack([np.cos(ang), np.sin(ang)], axis=-1)
    eye = np.eye(K1_GROUP)
    m = np.einsum('knp,rs->krnps', cs, eye)
    return m.reshape(FFT_N2 * K1_GROUP, FFT_N2 * 2 * K1_GROUP).astype(np.float32)


def _dense_dft(n):
    t = np.arange(n)
    ang = 2.0 * np.pi * (np.outer(t, t) % n) / n
    return np.concatenate([np.cos(ang), np.sin(ang)], axis=1).astype(np.float32)


def _mod_kernel(c_ref, w_ref, b_ref, o_ref):
    o_ref[0] = _dot(_silu(c_ref[...]), w_ref[0]) + b_ref[0]


def _modulation(c_rows, w_mod, b_mod):
    depth = w_mod.shape[0]
    return pl.pallas_call(
        _mod_kernel,
        grid=(depth, 3),
        in_specs=[pl.BlockSpec((MOD_ROWS, D_MODEL), lambda l, j: (0, 0)),
                  pl.BlockSpec((1, D_MODEL, D_MODEL), lambda l, j: (l, 0, j)),
                  pl.BlockSpec((1, 1, D_MODEL), lambda l, j: (l, 0, j))],
        out_specs=pl.BlockSpec((1, MOD_ROWS, D_MODEL), lambda l, j: (l, 0, j)),
        out_shape=jax.ShapeDtypeStruct((depth, MOD_ROWS, 3 * D_MODEL), F32),
        compiler_params=_cparams("arbitrary", "arbitrary"),
        name="modulation",
    )(c_rows, w_mod, b_mod.reshape(depth, 1, 3 * D_MODEL))


def _inproj_kernel(*refs, rope, ctx_row, kv_only, tm):
    x_ref, shift_ref, scale_ref, g_ref, w_ref = refs[:5]
    cos_ref, sup_ref, sdn_ref = refs[5:8] if rope else (None,) * 3
    outs = refs[8:-1] if rope else refs[5:]
    uf_s = refs[-1] if rope else None
    row = pl.program_id(0) if ctx_row is None else ctx_row
    shift = shift_ref[0, pl.ds(row, 1), :]
    gain = g_ref[0] * (1.0 + scale_ref[0, pl.ds(row, 1), :])

    def with_swapped_heads(y):
        return jnp.concatenate([y, pltpu.roll(y, HEAD_DIM, 1)], axis=1).astype(BF16)

    sub = min(tm, PROJ_SUB_ROWS)
    for s in range(tm // sub):
        rows = slice(s * sub, (s + 1) * sub)
        x = x_ref[0, rows]
        r = lax.rsqrt(jnp.mean(x * x, axis=-1, keepdims=True) + NORM_EPS)
        h = ((x * r) * gain + shift).astype(BF16)

        def proj(c0, width):
            return _dot(h, w_ref[0, :, c0:c0 + width])

        def rotate(y):
            if not rope:
                return y
            return (y * cos_ref[rows] + pltpu.roll(y, LANES - ROPE_FREQS, 1) * sup_ref[rows]
                    + pltpu.roll(y, ROPE_FREQS, 1) * sdn_ref[rows])

        kv = proj(C_K, 2 * KV_WIDTH)
        k_out = with_swapped_heads(rotate(kv[:, :KV_WIDTH]))
        v_out = with_swapped_heads(kv[:, KV_WIDTH:])
        if kv_only:
            k_ref, v_ref = outs
            k_ref[0, rows] = k_out
            v_ref[0, rows] = v_out
            continue
        q_ref, k_ref, v_ref, sga_ref, uf_ref, sgf_ref, t_ref, bg_ref = outs
        k_ref[0, rows] = k_out
        if rope:
            v_both = jnp.concatenate([kv[:, KV_WIDTH:], pltpu.roll(kv[:, KV_WIDTH:], HEAD_DIM, 1)], axis=1)
            v_ref[0, :, rows] = v_both.T.astype(BF16)
        else:
            v_ref[0, rows] = v_out
        q = proj(C_Q, ATTN_WIDTH)
        for p in range(HEAD_PAIRS):
            sl = slice(p * LANES, (p + 1) * LANES)
            q_ref[0, rows, sl] = (rotate(q[:, sl]) * (HEAD_DIM ** -0.5 * LOG2E)).astype(BF16)
        sga_ref[0, rows] = _silu(proj(C_GA, ATTN_WIDTH)).astype(BF16)
        uf = proj(C_UF, FOURIER_WIDTH)
        if rope:
            per_sub = sub // FFT_N2
            for slab in range(FOURIER_WIDTH // LANES):
                for j in range(per_sub):
                    uf_s[s, slab, j * UF_PITCH:j * UF_PITCH + FFT_N2] = (
                        uf[j * FFT_N2:(j + 1) * FFT_N2, slab * LANES:(slab + 1) * LANES])
            for n2 in range(FFT_N2):
                uf_ref[0, n2, s * per_sub:(s + 1) * per_sub, :] = jnp.concatenate(
                    [uf_s[s, slab, pl.ds(n2, per_sub, stride=UF_PITCH), :]
                     for slab in range(FOURIER_WIDTH // LANES)], axis=1)
        else:
            uf_ref[0, rows] = uf.astype(BF16)
        sgf_ref[0, rows] = _silu(proj(C_GF, FOURIER_WIDTH)).astype(BF16)
        t_ref[0, rows] = (proj(C_CC, CONV_WIDTH) * proj(C_ZC, CONV_WIDTH)).astype(BF16)
        bg_ref[0, rows] = (proj(C_BC, CONV_WIDTH) * _silu(proj(C_GC, CONV_WIDTH))).astype(BF16)


def _in_projection(x, mod, g_pre, w_in, layer, *, rope_tabs, ctx_row, tm, kv_only=False):
    b, n, _ = x.shape
    rope = rope_tabs is not None
    row3 = lambda width: pl.BlockSpec((1, tm, width), lambda bi, i: (bi, i, 0))
    in_specs = [row3(D_MODEL),
                pl.BlockSpec((1, MOD_ROWS, D_MODEL), lambda bi, i: (layer, 0, 0)),
                pl.BlockSpec((1, MOD_ROWS, D_MODEL), lambda bi, i: (layer, 0, 1)),
                pl.BlockSpec((1, 1, D_MODEL), lambda bi, i: (layer, 0, 0)),
                pl.BlockSpec((1, D_MODEL, PROJ_WIDTH), lambda bi, i: (layer, 0, 0))]
    args = [x, mod, mod, g_pre, w_in]
    if rope:
        in_specs += [pl.BlockSpec((tm, LANES), lambda bi, i: (i, 0))] * 3
        args += list(rope_tabs)
    widths = (ATTN_WIDTH, 2 * KV_WIDTH, 2 * KV_WIDTH, ATTN_WIDTH, FOURIER_WIDTH, FOURIER_WIDTH,
              CONV_WIDTH, CONV_WIDTH)
    if kv_only:
        widths = widths[1:3]
    out_specs = [row3(w) for w in widths]
    out_shape = [jax.ShapeDtypeStruct((b, n, w), BF16) for w in widths]
    scratch = []
    if rope:
        sub = min(tm, PROJ_SUB_ROWS)
        assert sub % (FFT_N2 * SUBLANES) == 0
        out_specs[4] = pl.BlockSpec((1, FFT_N2, tm // FFT_N2, FOURIER_WIDTH), lambda bi, i: (bi, 0, i, 0))
        out_shape[4] = jax.ShapeDtypeStruct((b, FFT_N2, n // FFT_N2, FOURIER_WIDTH), F32)
        out_specs[2] = pl.BlockSpec((1, 2 * KV_WIDTH, tm), lambda bi, i: (bi, 0, i))
        out_shape[2] = jax.ShapeDtypeStruct((b, 2 * KV_WIDTH, n), BF16)
        scratch = [pltpu.VMEM((tm // sub, FOURIER_WIDTH // LANES, sub // FFT_N2 * UF_PITCH, LANES), F32)]
    return pl.pallas_call(
        functools.partial(_inproj_kernel, rope=rope, ctx_row=ctx_row, kv_only=kv_only, tm=tm),
        grid=(b, n // tm),
        in_specs=in_specs,
        out_specs=out_specs,
        out_shape=out_shape,
        scratch_shapes=scratch,
        compiler_params=_cparams("arbitrary", "arbitrary"),
        name="in_projection_rope" if rope else ("in_projection_ctx_kv" if kv_only else "in_projection_ctx"),
    )(*args)


def _lane_half_variants(blk, fill):
    straight, swapped = blk[:, :KV_WIDTH], blk[:, KV_WIDTH:]
    lo = lax.broadcasted_iota(jnp.int32, straight.shape, 1) < HEAD_DIM
    other = jnp.full_like(straight, fill)
    return (jnp.where(lo, straight, other), jnp.where(lo, other, swapped),
            jnp.where(lo, swapped, other), jnp.where(lo, other, straight))


def _row_half_variants(blk_t, fill):
    straight, swapped = blk_t[:KV_WIDTH], blk_t[KV_WIDTH:]
    lo = lax.broadcasted_iota(jnp.int32, straight.shape, 0) < HEAD_DIM
    other = jnp.full_like(straight, fill)
    return (jnp.where(lo, straight, other), jnp.where(lo, other, swapped),
            jnp.where(lo, swapped, other), jnp.where(lo, other, straight))


def _merge_head_pair(pv_lo, pv_hi, sink_lo, sink_hi):
    lane_lo = lax.broadcasted_iota(jnp.int32, pv_lo.shape, 1) < HEAD_DIM
    num = jnp.where(lane_lo, pv_lo, pv_hi)
    den = pltpu.roll(jnp.where(lane_lo, pv_hi, pv_lo), HEAD_DIM, 1) + jnp.where(lane_lo, sink_lo, sink_hi)
    return num / den


def _conv_rows(t, above, below, bg, cw, cb):
    n_rows = t.shape[0]
    ridx = lax.broadcasted_iota(jnp.int32, t.shape, 0)
    up = jnp.where(ridx == 0, above, pltpu.roll(t, 1, 0))
    dn = jnp.where(ridx == n_rows - 1, below, pltpu.roll(t, n_rows - 1, 0))
    conv = up * cw[0:1] + t * cw[1:2] + dn * cw[2:3] + cb
    return (conv * bg.astype(F32)).astype(BF16)


def _mix_rows(a, f, s, w, g_post, gate, x):
    h = jnp.concatenate([a, f, s], axis=1)
    y = _dot(h, w)
    r = lax.rsqrt(jnp.mean(y * y, axis=-1, keepdims=True) + NORM_EPS)
    return x + (y * r) * (gate * g_post)


def _attn_kernel(sink_ref, q_ref, g_ref, kp_ref, km_ref, kn_ref, vp_ref, vm_ref, vn_ref, kc_ref, vc_ref,
                 x_ref, f_ref, t_ref, tp_ref, tn_ref, bg_ref, cw_ref, cb_ref, w_ref, gpost_ref, mgate_ref,
                 o_ref, k_s, v_s, kctx_s, vctx_s, bias_s, s_s, p_s, r_s, a_s, c_s, *, tq, n_seq, n_ctx):
    i = pl.program_id(1)
    n_blocks = tq // BLOCK_Q
    for off, kref, vref, rows in ((0, kp_ref, vp_ref, WINDOW), (WINDOW, km_ref, vm_ref, tq),
                                  (WINDOW + tq, kn_ref, vn_ref, WINDOW)):
        for idx, kk in enumerate(_lane_half_variants(kref[0], 0.0)):
            k_s[idx, off:off + rows] = kk
        for idx, vv in enumerate(_row_half_variants(vref[0], 1.0)):
            v_s[idx, :, off:off + rows] = vv
    for idx, kk in enumerate(_lane_half_variants(kc_ref[0], 0.0)):
        kctx_s[idx] = kk
    for idx, vv in enumerate(_row_half_variants(vc_ref[0].astype(F32).T.astype(BF16), 1.0)):
        vctx_s[idx] = vv

    jj = lax.broadcasted_iota(jnp.int32, (WINDOW, BLOCK_Q), 0)
    ii = lax.broadcasted_iota(jnp.int32, (WINDOW, BLOCK_Q), 1)
    head_band = jnp.where(jj >= ii, 0.0, -jnp.inf)
    tail_band = jnp.where(jj <= ii, 0.0, -jnp.inf)
    bias_s[0] = head_band
    bias_s[1] = tail_band
    bias_s[2] = jnp.where(i == 0, -jnp.inf, head_band)
    bias_s[3] = jnp.where(i == pl.num_programs(1) - 1, -jnp.inf, tail_band)

    head_w = n_ctx + KEY_SPAN

    def scores(sb):
        par, r0 = sb % SCORE_BUFFERS, sb * BLOCK_Q
        q = q_ref[0, r0:r0 + BLOCK_Q, :]
        for kv in range(KV_HEADS):
            pairs = range(kv * PAIRS_PER_KV, (kv + 1) * PAIRS_PER_KV)
            qg = jnp.concatenate([q[:, p * LANES:(p + 1) * LANES] for p in pairs], axis=0)
            keys = jnp.concatenate([kctx_s[2 * kv], k_s[2 * kv, r0:r0 + KEY_SPAN],
                                    kctx_s[2 * kv + 1], k_s[2 * kv + 1, r0:r0 + KEY_SPAN]], axis=0)
            s_s[par, kv] = _dot_nt(keys, qg)

    def softmax(sb):
        par, spar = sb % 2, sb % SCORE_BUFFERS
        head_bias = 2 if sb == 0 else 0
        tail_bias = 3 if sb == n_blocks - 1 else 1
        n_blk = head_w // LANES
        first_local = n_ctx // LANES
        for p in range(HEAD_PAIRS):
            kv, lanes = p // PAIRS_PER_KV, slice((p % PAIRS_PER_KV) * BLOCK_Q, (p % PAIRS_PER_KV + 1) * BLOCK_Q)
            for half in range(2):
                h = 2 * p + half
                sink2 = sink_ref[h] * LOG2E

                def key_block(t):
                    rows = slice(half * head_w + t * LANES, half * head_w + (t + 1) * LANES)
                    blk = s_s[spar, kv, rows, lanes]
                    if t == first_local:
                        blk = blk + bias_s[head_bias]
                    elif t == n_blk - 1:
                        blk = blk + bias_s[tail_bias]
                    return rows, blk

                mx = functools.reduce(jnp.maximum, [key_block(t)[1] for t in range(n_blk)])
                m = jnp.maximum(jnp.max(mx, axis=0, keepdims=True), sink2)
                for t in range(n_blk):
                    rows, blk = key_block(t)
                    p_s[par, kv, rows, lanes] = jnp.exp2(blk - m).astype(BF16)
                r_s[par, h] = jnp.broadcast_to(jnp.exp2(sink2 - m), (SUBLANES, BLOCK_Q))

    def weighted_values(sb):
        par, r0 = sb % 2, sb * BLOCK_Q
        for kv in range(KV_HEADS):
            pv = []
            for half in range(2):
                vals_t = jnp.concatenate([vctx_s[2 * kv + half], v_s[2 * kv + half, :, r0:r0 + KEY_SPAN]],
                                         axis=1)
                pv.append(_dot(vals_t, p_s[par, kv, half * head_w:(half + 1) * head_w, :]))
            for j in range(PAIRS_PER_KV):
                p = kv * PAIRS_PER_KV + j
                lanes = slice(j * BLOCK_Q, (j + 1) * BLOCK_Q)
                lo, hi = pv[0][:, lanes], pv[1][:, lanes]
                num = jnp.concatenate([lo[:HEAD_DIM], hi[HEAD_DIM:]], axis=0)
                den = jnp.concatenate([lo[HEAD_DIM:], hi[:HEAD_DIM]], axis=0)
                sink_t = jnp.concatenate([jnp.broadcast_to(r_s[par, 2 * p + half, 0:1], (HEAD_DIM, BLOCK_Q))
                                          for half in range(2)], axis=0)
                out = (num / (den + sink_t)).T
                gate = g_ref[0, r0:r0 + BLOCK_Q, p * LANES:(p + 1) * LANES].astype(F32)
                a_s[r0:r0 + BLOCK_Q, p * LANES:(p + 1) * LANES] = (out * gate).astype(BF16)

    def conv_edge(ref, row, keep):
        return jnp.where(keep, ref[0].astype(F32)[row:row + 1, :], 0.0)

    for r0 in range(0, tq, MIX_ROWS):
        rows = slice(r0, r0 + MIX_ROWS)
        if r0 == 0:
            above = conv_edge(tp_ref, BF16_ROWS - 1, i > 0)
        else:
            above = t_ref[0, r0 - BF16_ROWS:r0, :].astype(F32)[BF16_ROWS - 1:, :]
        if r0 + MIX_ROWS == tq:
            below = conv_edge(tn_ref, 0, i < pl.num_programs(1) - 1)
        else:
            below = t_ref[0, r0 + MIX_ROWS:r0 + MIX_ROWS + BF16_ROWS, :].astype(F32)[:1, :]
        c_s[rows] = _conv_rows(t_ref[0, rows].astype(F32), above, below, bg_ref[0, rows], cw_ref[0],
                               cb_ref[0])

    def mix(r0):
        rows = slice(r0, r0 + MIX_ROWS)
        o_ref[0, rows] = _mix_rows(a_s[rows], f_ref[0, rows], c_s[rows], w_ref[0], gpost_ref[0],
                                   mgate_ref[0, pl.ds(pl.program_id(0), 1), :], x_ref[0, rows])

    per_mix = MIX_ROWS // BLOCK_Q
    ahead = SCORE_BUFFERS - 1
    for sb in range(min(ahead, n_blocks)):
        scores(sb)
    for sb in range(n_blocks):
        if sb + ahead < n_blocks:
            scores(sb + ahead)
        softmax(sb)
        weighted_values(sb)
        if (sb + 1) % per_mix == 0:
            mix((sb + 1 - per_mix) * BLOCK_Q)


def _attention_and_mix(x, q, gate, k, v, kc, vc, sink, f, t, bg, conv_w, conv_b, w_out, g_post, mod,
                       layer, *, tq):
    b, n, _ = q.shape
    nc = kc.shape[1]
    per = tq // WINDOW
    last = n // WINDOW - 1
    kvw = 2 * KV_WIDTH
    variants = 2 * KV_HEADS
    main = lambda width: pl.BlockSpec((1, tq, width), lambda bi, i: (bi, i, 0))
    prev = pl.BlockSpec((1, WINDOW, kvw), lambda bi, i: (bi, jnp.maximum(i * per - 1, 0), 0))
    nxt = pl.BlockSpec((1, WINDOW, kvw), lambda bi, i: (bi, jnp.minimum((i + 1) * per, last), 0))
    ctx = pl.BlockSpec((1, nc, kvw), lambda bi, i: (bi, 0, 0))
    lay3 = lambda shape: pl.BlockSpec((1,) + shape, lambda bi, i: (layer, 0, 0))
    t_per = tq // BF16_ROWS
    t_last = n // BF16_ROWS - 1
    t_prev = pl.BlockSpec((1, BF16_ROWS, CONV_WIDTH), lambda bi, i: (bi, jnp.maximum(i * t_per - 1, 0), 0))
    t_next = pl.BlockSpec((1, BF16_ROWS, CONV_WIDTH),
                          lambda bi, i: (bi, jnp.minimum((i + 1) * t_per, t_last), 0))
    v_prev = pl.BlockSpec((1, kvw, WINDOW), lambda bi, i: (bi, 0, jnp.maximum(i * per - 1, 0)))
    v_main = pl.BlockSpec((1, kvw, tq), lambda bi, i: (bi, 0, i))
    v_next = pl.BlockSpec((1, kvw, WINDOW), lambda bi, i: (bi, 0, jnp.minimum((i + 1) * per, last)))
    span = tq + 2 * WINDOW
    s_rows = 2 * (nc + KEY_SPAN)
    group_q = PAIRS_PER_KV * BLOCK_Q
    return pl.pallas_call(
        functools.partial(_attn_kernel, tq=tq, n_seq=n, n_ctx=nc),
        grid=(b, n // tq),
        in_specs=[pl.BlockSpec(memory_space=pltpu.SMEM), main(ATTN_WIDTH), main(ATTN_WIDTH),
                  prev, main(kvw), nxt, v_prev, v_main, v_next, ctx, ctx,
                  main(D_MODEL),
                  main(FOURIER_WIDTH), main(CONV_WIDTH), t_prev, t_next, main(CONV_WIDTH),
                  lay3((3, CONV_WIDTH)), lay3((1, CONV_WIDTH)), lay3((MIX_WIDTH, D_MODEL)),
                  lay3((1, D_MODEL)),
                  pl.BlockSpec((1, MOD_ROWS, D_MODEL), lambda bi, i: (layer, 0, 2))],
        out_specs=main(D_MODEL),
        out_shape=jax.ShapeDtypeStruct(x.shape, F32),
        scratch_shapes=[pltpu.VMEM((variants, span, KV_WIDTH), BF16),
                        pltpu.VMEM((variants, KV_WIDTH, span), BF16),
                        pltpu.VMEM((variants, nc, KV_WIDTH), BF16),
                        pltpu.VMEM((variants, KV_WIDTH, nc), BF16),
                        pltpu.VMEM((4, WINDOW, BLOCK_Q), F32),
                        pltpu.VMEM((SCORE_BUFFERS, KV_HEADS, s_rows, group_q), F32),
                        pltpu.VMEM((2, KV_HEADS, s_rows, group_q), BF16),
                        pltpu.VMEM((2, ATTN_HEADS, SUBLANES, BLOCK_Q), F32),
                        pltpu.VMEM((tq, ATTN_WIDTH), BF16),
                        pltpu.VMEM((tq, CONV_WIDTH), BF16)],
        compiler_params=_cparams("arbitrary", "arbitrary"),
        name="attention_and_mix",
    )(sink, q, gate, k, k, k, v, v, v, kc, vc, x, f, t, t, t, bg, conv_w, conv_b, w_out, g_post, mod)


def _ctx_attn_kernel(sink_ref, q_ref, g_ref, kc_ref, vc_ref, o_ref):
    keys = _lane_half_variants(kc_ref[0], 0.0)
    vals = _lane_half_variants(vc_ref[0], 1.0)
    for p in range(HEAD_PAIRS):
        kv = p // PAIRS_PER_KV
        qp = q_ref[0, :, p * LANES:(p + 1) * LANES]
        pv, sink_terms = [], []
        for half in range(2):
            s = _dot_nt(qp, keys[2 * kv + half])
            sink2 = sink_ref[2 * p + half] * LOG2E
            m = jnp.maximum(jnp.max(s, axis=-1, keepdims=True), sink2)
            pv.append(_dot(jnp.exp2(s - m).astype(BF16), vals[2 * kv + half]))
            sink_terms.append(jnp.exp2(sink2 - m))
        out = _merge_head_pair(pv[0], pv[1], sink_terms[0], sink_terms[1])
        gate = g_ref[0, :, p * LANES:(p + 1) * LANES].astype(F32)
        o_ref[0, :, p * LANES:(p + 1) * LANES] = (out * gate).astype(BF16)


def _context_attention(q, gate, kc, vc, sink):
    b, nc, _ = q.shape
    blk = lambda width: pl.BlockSpec((1, nc, width), lambda bi: (bi, 0, 0))
    return pl.pallas_call(
        _ctx_attn_kernel,
        grid=(b,),
        in_specs=[pl.BlockSpec(memory_space=pltpu.SMEM), blk(ATTN_WIDTH), blk(ATTN_WIDTH),
                  blk(2 * KV_WIDTH), blk(2 * KV_WIDTH)],
        out_specs=blk(ATTN_WIDTH),
        out_shape=jax.ShapeDtypeStruct((b, nc, ATTN_WIDTH), BF16),
        compiler_params=_cparams("arbitrary"),
        name="context_attention",
    )(sink, q, gate, kc, vc)


def _channel_mix_matrix(cc_ref, sc_ref, wf_ref, scale):
    wf = wf_ref[0]
    return (jnp.concatenate([_dot(cc_ref[...], wf), -_dot(sc_ref[...], wf)], axis=1) * scale).astype(BF16)


def _fourier_kernel(uf_ref, gate_ref, cc_ref, sc_ref, wf_ref, g_ref, m2_ref, o_ref, mix_s, y_s, *, n_seq):
    n1_len = n_seq // FFT_N2
    chunk = pl.program_id(1)
    halves = FOURIER_WIDTH // LANES

    @pl.when(chunk == 0)
    def _():
        zero = jnp.zeros((LANES, LANES), F32)
        scale = (n_seq * FOURIER_GROUP_DIM) ** -0.5
        re = [_dot(cc_ref[...], wf_ref[h]) * scale for h in range(halves)]
        im = [_dot(sc_ref[...], wf_ref[h]) * -scale for h in range(halves)]
        for h in range(halves):
            row = [re[h] if j == h else zero for j in range(halves)]
            row += [im[h] if j == h else zero for j in range(halves)]
            mix_s[h * LANES:(h + 1) * LANES] = jnp.concatenate(row, axis=1).astype(BF16)

    zs = [_dot(uf_ref[0, u].astype(BF16), mix_s[...]) for u in range(N2_CHUNK)]
    for u, z in enumerate(zs):
        rhs = jnp.concatenate([z[:, :FOURIER_WIDTH], z[:, FOURIER_WIDTH:]], axis=0).astype(BF16)
        y = _dot(g_ref[chunk * N2_CHUNK + u], rhs)
        y_s[chunk * N2_CHUNK + u] = y.astype(BF16).reshape(n1_len // K1_GROUP, 2 * K1_GROUP, FOURIER_WIDTH)

    @pl.when(chunk == pl.num_programs(1) - 1)
    def _():
        per_store = BF16_ROWS // K1_GROUP

        def stage2(t, carry):
            outs = []
            for u in range(per_store):
                rhs = y_s[:, t * per_store + u].reshape(FFT_N2 * 2 * K1_GROUP, FOURIER_WIDTH)
                outs.append(_dot(m2_ref[...], rhs).reshape(FFT_N2, K1_GROUP, FOURIER_WIDTH))
            r0 = pl.multiple_of(t * BF16_ROWS, BF16_ROWS)
            gate = gate_ref[0, :, pl.ds(r0, BF16_ROWS), :].astype(F32)
            o_ref[0, :, pl.ds(r0, BF16_ROWS), :] = (jnp.concatenate(outs, axis=1) * gate).astype(BF16)
            return carry

        lax.fori_loop(0, n1_len // BF16_ROWS, stage2, 0)


def _fourier_mix(uf, gate, wf_half, consts):
    b, _, n1_len, _ = uf.shape
    n = FFT_N2 * n1_len
    cc, sc, g, m2 = consts
    full = lambda shape: pl.BlockSpec(shape, lambda bi, c: (0,) * len(shape))
    whole = pl.BlockSpec((1, FFT_N2, n1_len, FOURIER_WIDTH), lambda bi, c: (bi, 0, 0, 0))
    out = pl.pallas_call(
        functools.partial(_fourier_kernel, n_seq=n),
        grid=(b, FFT_N2 // N2_CHUNK),
        in_specs=[pl.BlockSpec((1, N2_CHUNK, n1_len, FOURIER_WIDTH), lambda bi, c: (bi, c, 0, 0)),
                  whole, full((LANES, LANES)), full((LANES, LANES)), full(wf_half.shape),
                  full(g.shape), full(m2.shape)],
        out_specs=whole,
        out_shape=jax.ShapeDtypeStruct((b, FFT_N2, n1_len, FOURIER_WIDTH), BF16),
        scratch_shapes=[pltpu.VMEM((FOURIER_WIDTH, 2 * FOURIER_WIDTH), BF16),
                        pltpu.VMEM((FFT_N2, n1_len // K1_GROUP, 2 * K1_GROUP, FOURIER_WIDTH), BF16)],
        compiler_params=_cparams("arbitrary", "arbitrary"),
        name="fourier_mix",
    )(uf, gate.reshape(b, FFT_N2, n1_len, FOURIER_WIDTH), cc, sc, wf_half, g, m2)
    return out.reshape(b, n, FOURIER_WIDTH)


def _ctx_fourier_kernel(uf_ref, gate_ref, cc_ref, sc_ref, wf_ref, dft_ref, o_ref, *, n_seq):
    mix = _channel_mix_matrix(cc_ref, sc_ref, wf_ref, (n_seq * FOURIER_GROUP_DIM) ** -0.5)
    z = _dot(uf_ref[0], mix)
    rhs = jnp.concatenate([z[:, :LANES], z[:, LANES:]], axis=0).astype(BF16)
    o_ref[0] = (_dot(dft_ref[...], rhs) * gate_ref[0].astype(F32)).astype(BF16)


def _ctx_fourier_mix(uf, gate, wf_half, cc, sc, dft):
    b, n, _ = uf.shape
    halves = FOURIER_WIDTH // LANES
    full = lambda shape: pl.BlockSpec(shape, lambda bi, hf: (0,) * len(shape))
    half = pl.BlockSpec((1, n, LANES), lambda bi, hf: (bi, 0, hf))
    return pl.pallas_call(
        functools.partial(_ctx_fourier_kernel, n_seq=n),
        grid=(b, halves),
        in_specs=[half, half, full((LANES, LANES)), full((LANES, LANES)),
                  pl.BlockSpec((1, LANES, LANES), lambda bi, hf: (hf, 0, 0)), full(dft.shape)],
        out_specs=half,
        out_shape=jax.ShapeDtypeStruct((b, n, FOURIER_WIDTH), BF16),
        compiler_params=_cparams("arbitrary", "arbitrary"),
        name="context_fourier_mix",
    )(uf, gate, cc, sc, wf_half, dft)


def _ctx_mix_kernel(x_ref, a_ref, f_ref, t_ref, bg_ref, cw_ref, cb_ref, w_ref, g_ref, gate_ref, o_ref,
                    *, ctx_row):
    edge = jnp.zeros((1, CONV_WIDTH), F32)
    conv = _conv_rows(t_ref[0].astype(F32), edge, edge, bg_ref[0], cw_ref[0], cb_ref[0])
    o_ref[0] = _mix_rows(a_ref[0], f_ref[0], conv, w_ref[0], g_ref[0], gate_ref[0, ctx_row:ctx_row + 1, :],
                         x_ref[0])


def _context_mix(x, a, f, t, bg, conv_w, conv_b, w_out, g_post, mod, layer, *, ctx_row):
    b, n, _ = x.shape
    row3 = lambda width: pl.BlockSpec((1, n, width), lambda bi: (bi, 0, 0))
    lay3 = lambda shape: pl.BlockSpec((1,) + shape, lambda bi: (layer, 0, 0))
    return pl.pallas_call(
        functools.partial(_ctx_mix_kernel, ctx_row=ctx_row),
        grid=(b,),
        in_specs=[row3(D_MODEL), row3(ATTN_WIDTH), row3(FOURIER_WIDTH), row3(CONV_WIDTH), row3(CONV_WIDTH), lay3((3, CONV_WIDTH)), lay3((1, CONV_WIDTH)),
                  lay3((MIX_WIDTH, D_MODEL)), lay3((1, D_MODEL)),
                  pl.BlockSpec((1, MOD_ROWS, D_MODEL), lambda bi: (layer, 0, 2))],
        out_specs=row3(D_MODEL),
        out_shape=jax.ShapeDtypeStruct(x.shape, F32),
        compiler_params=_cparams("arbitrary"),
        name="context_mix",
    )(x, a, f, t, bg, conv_w, conv_b, w_out, g_post, mod)


def kernel(x, c, ctx, c_ctx, w_mod, b_mod, g_pre, g_post, w_in, w_out, sink, w_fourier, conv_w, conv_b):
    depth = w_mod.shape[0]
    b, n, _ = x.shape
    nc = ctx.shape[1]
    assert b + 1 <= MOD_ROWS and n % GRID_W == 0 and n % ATTN_ROWS == 0 and n % TILE_ROWS == 0
    assert (n // FFT_N2) % BF16_ROWS == 0 and FFT_N2 % N2_CHUNK == 0

    w_in_b = w_in.astype(BF16)
    w_out_b = w_out.astype(BF16)

    rope_tabs = tuple(jnp.asarray(t) for t in _rope_tables(n))
    cc, sc = (jnp.asarray(m) for m in _channel_dft())
    stage1 = jnp.asarray(_stage1_mats(n)).astype(BF16)
    stage2 = jnp.asarray(_stage2_mat()).astype(BF16)
    ctx_dft = jnp.asarray(_dense_dft(nc)).astype(BF16)
    groups_per_half = LANES // FOURIER_GROUP_DIM
    wf_half = jnp.zeros((depth, FOURIER_GROUPS // groups_per_half, LANES, LANES), F32)
    for g in range(FOURIER_GROUPS):
        o = (g % groups_per_half) * FOURIER_GROUP_DIM
        wf_half = wf_half.at[:, g // groups_per_half, o:o + FOURIER_GROUP_DIM,
                             o:o + FOURIER_GROUP_DIM].set(w_fourier[:, g])

    c_rows = jnp.zeros((MOD_ROWS, D_MODEL), F32).at[:b].set(c).at[b].set(c_ctx)
    mod = _modulation(c_rows, w_mod, b_mod)
    g_pre3 = g_pre.reshape(depth, 1, D_MODEL)
    g_post3 = g_post.reshape(depth, 1, D_MODEL)
    conv_b3 = conv_b.reshape(depth, 1, CONV_WIDTH)

    for l in range(depth):
        update_ctx = l < depth - 1
        ctx_rows = ctx.reshape(1, b * nc, D_MODEL)
        ctx_proj = _in_projection(ctx_rows, mod, g_pre3, w_in_b, l, rope_tabs=None, ctx_row=b,
                                  tm=b * nc, kv_only=not update_ctx)
        ctx_proj = [y.reshape(b, nc, y.shape[-1]) for y in ctx_proj]
        if update_ctx:
            qc, kc, vc, sgac, ufc, sgfc, tc, bgc = ctx_proj
        else:
            kc, vc = ctx_proj
        q, k, v, sga, uf, sgf, t, bg = _in_projection(
            x, mod, g_pre3, w_in_b, l, rope_tabs=rope_tabs, ctx_row=None, tm=TILE_ROWS)
        f = _fourier_mix(uf, sgf, wf_half[l], (cc, sc, stage1, stage2))
        x = _attention_and_mix(x, q, sga, k, v, kc, vc, sink[l], f, t, bg, conv_w, conv_b3, w_out_b,
                               g_post3, mod, l, tq=ATTN_ROWS)
        if update_ctx:
            ac = _context_attention(qc, sgac, kc, vc, sink[l])
            fc = _ctx_fourier_mix(ufc, sgfc, wf_half[l], cc, sc, ctx_dft)
            ctx = _context_mix(ctx, ac, fc, tc, bgc, conv_w, conv_b3, w_out_b, g_post3, mod, l,
                               ctx_row=b)
    return x
```

```python
import functools
import math

import numpy as np
import jax
import jax.numpy as jnp
from jax import lax
from jax.experimental import pallas as pl
from jax.experimental.pallas import tpu as pltpu

D_MODEL = 1024
GRID_W = 64
HEAD_DIM = 64
ATTN_HEADS = 8
KV_HEADS = 2
ATTN_WIDTH = ATTN_HEADS * HEAD_DIM
KV_WIDTH = KV_HEADS * HEAD_DIM
WINDOW = 128
FOURIER_GROUPS = 4
FOURIER_GROUP_DIM = 64
FOURIER_WIDTH = FOURIER_GROUPS * FOURIER_GROUP_DIM
CONV_WIDTH = 256
MIX_WIDTH = ATTN_WIDTH + FOURIER_WIDTH + CONV_WIDTH
PROJ_WIDTH = 2 * ATTN_WIDTH + 2 * KV_WIDTH + 2 * FOURIER_WIDTH + 4 * CONV_WIDTH
ROPE_FREQS = HEAD_DIM // 4
ROPE_BASE = 10000.0
NORM_EPS = 1e-6
LOG2E = math.log2(math.e)

C_Q = 0
C_K = C_Q + ATTN_WIDTH
C_V = C_K + KV_WIDTH
C_GA = C_V + KV_WIDTH
C_UF = C_GA + ATTN_WIDTH
C_GF = C_UF + FOURIER_WIDTH
C_ZC = C_GF + FOURIER_WIDTH
C_BC = C_ZC + CONV_WIDTH
C_CC = C_BC + CONV_WIDTH
C_GC = C_CC + CONV_WIDTH

LANES = 128
SUBLANES = 8
BF16_ROWS = 16
VMEM_LIMIT = 52 * 1024 * 1024

MOD_ROWS = 8
BF16 = jnp.bfloat16
F32 = jnp.float32

FFT_N2 = 64
K1_GROUP = SUBLANES
N2_CHUNK = 32
UF_PITCH = FFT_N2 + SUBLANES

BLOCK_Q = 128
KEY_SPAN = BLOCK_Q + 2 * WINDOW
HEAD_PAIRS = ATTN_WIDTH // LANES
PAIRS_PER_KV = HEAD_PAIRS // KV_HEADS

TILE_ROWS = 1024
PROJ_SUB_ROWS = 512
ATTN_ROWS = 1024
BLOCK_BUFFERS = 2
MIX_ROWS = 512


def _silu(x):
    return x / (1.0 + jnp.exp(-x))


def _dot(a, b):
    return jnp.dot(a, b, preferred_element_type=F32)


def _dot_nt(a, b):
    return lax.dot_general(a, b, (((1,), (1,)), ((), ())), preferred_element_type=F32)


def _cparams(*sem):
    return pltpu.CompilerParams(dimension_semantics=sem, vmem_limit_bytes=VMEM_LIMIT)


def _rope_tables(n):
    t = np.arange(n)
    row = (t // GRID_W).astype(np.float64)
    col = (t % GRID_W).astype(np.float64)
    inv = ROPE_BASE ** (-np.arange(ROPE_FREQS, dtype=np.float64) / ROPE_FREQS)
    ar = row[:, None] * inv
    ac = col[:, None] * inv
    z = np.zeros_like(ar)
    cos_h = np.concatenate([np.cos(ar), np.cos(ar), np.cos(ac), np.cos(ac)], axis=1)
    sin_up = np.concatenate([-np.sin(ar), z, -np.sin(ac), z], axis=1)
    sin_dn = np.concatenate([z, np.sin(ar), z, np.sin(ac)], axis=1)
    rep = LANES // HEAD_DIM
    return tuple(np.tile(a, (1, rep)).astype(np.float32) for a in (cos_h, sin_up, sin_dn))


def _channel_dft():
    c = np.arange(FOURIER_GROUP_DIM)
    ang = 2.0 * np.pi * np.outer(c, c) / FOURIER_GROUP_DIM
    eye = np.eye(LANES // FOURIER_GROUP_DIM)
    return (np.kron(eye, np.cos(ang)).astype(np.float32),
            np.kron(eye, np.sin(ang)).astype(np.float32))


def _stage1_mats(n):
    n1_len = n // FFT_N2
    k1 = np.arange(n1_len)[:, None]
    n1 = np.arange(n1_len)[None, :]
    out = np.empty((FFT_N2, 2 * n1_len, 2 * n1_len), np.float32)
    for n2 in range(FFT_N2):
        ang = 2.0 * np.pi * ((k1 * (FFT_N2 * n1 + n2)) % n) / n
        ce, se = np.cos(ang), np.sin(ang)
        by_part = np.block([[ce, se], [-se, ce]]).reshape(2, n1_len // K1_GROUP, K1_GROUP, 2 * n1_len)
        out[n2] = by_part.transpose(1, 0, 2, 3).reshape(2 * n1_len, 2 * n1_len)
    return out


def _stage2_mat():
    k2 = np.arange(FFT_N2)
    ang = 2.0 * np.pi * np.outer(k2, k2) / FFT_N2
    cs = np.stack([np.cos(ang), np.sin(ang)], axis=-1)
    eye = np.eye(K1_GROUP)
    m = np.einsum('knp,rs->krnps', cs, eye)
    return m.reshape(FFT_N2 * K1_GROUP, FFT_N2 * 2 * K1_GROUP).astype(np.float32)


def _dense_dft(n):
    t = np.arange(n)
    ang = 2.0 * np.pi * (np.outer(t, t) % n) / n
    return np.concatenate([np.cos(ang), np.sin(ang)], axis=1).astype(np.float32)


def _mod_kernel(c_ref, w_ref, b_ref, o_ref):
    o_ref[0] = _dot(_silu(c_ref[...]), w_ref[0]) + b_ref[0]


def _modulation(c_rows, w_mod, b_mod):
    depth = w_mod.shape[0]
    return pl.pallas_call(
        _mod_kernel,
        grid=(depth, 3),
        in_specs=[pl.BlockSpec((MOD_ROWS, D_MODEL), lambda l, j: (0, 0)),
                  pl.BlockSpec((1, D_MODEL, D_MODEL), lambda l, j: (l, 0, j)),
                  pl.BlockSpec((1, 1, D_MODEL), lambda l, j: (l, 0, j))],
        out_specs=pl.BlockSpec((1, MOD_ROWS, D_MODEL), lambda l, j: (l, 0, j)),
        out_shape=jax.ShapeDtypeStruct((depth, MOD_ROWS, 3 * D_MODEL), F32),
        compiler_params=_cparams("arbitrary", "arbitrary"),
        name="modulation",
    )(c_rows, w_mod, b_mod.reshape(depth, 1, 3 * D_MODEL))


def _inproj_kernel(*refs, rope, ctx_row, kv_only, tm):
    x_ref, shift_ref, scale_ref, g_ref, w_ref = refs[:5]
    cos_ref, sup_ref, sdn_ref = refs[5:8] if rope else (None,) * 3
    outs = refs[8:-1] if rope else refs[5:]
    uf_s = refs[-1] if rope else None
    row = pl.program_id(0) if ctx_row is None else ctx_row
    shift = shift_ref[0, pl.ds(row, 1), :]
    gain = g_ref[0] * (1.0 + scale_ref[0, pl.ds(row, 1), :])

    def with_swapped_heads(y):
        return jnp.concatenate([y, pltpu.roll(y, HEAD_DIM, 1)], axis=1).astype(BF16)

    sub = min(tm, PROJ_SUB_ROWS)
    for s in range(tm // sub):
        rows = slice(s * sub, (s + 1) * sub)
        x = x_ref[0, rows]
        r = lax.rsqrt(jnp.mean(x * x, axis=-1, keepdims=True) + NORM_EPS)
        h = ((x * r) * gain + shift).astype(BF16)

        def proj(c0, width):
            return _dot(h, w_ref[0, :, c0:c0 + width])

        def rotate(y):
            if not rope:
                return y
            return (y * cos_ref[rows] + pltpu.roll(y, LANES - ROPE_FREQS, 1) * sup_ref[rows]
                    + pltpu.roll(y, ROPE_FREQS, 1) * sdn_ref[rows])

        kv = proj(C_K, 2 * KV_WIDTH)
        k_out = with_swapped_heads(rotate(kv[:, :KV_WIDTH]))
        v_out = with_swapped_heads(kv[:, KV_WIDTH:])
        if kv_only:
            k_ref, v_ref = outs
            k_ref[0, rows] = k_out
            v_ref[0, rows] = v_out
            continue
        q_ref, k_ref, v_ref, sga_ref, uf_ref, sgf_ref, t_ref, bg_ref = outs
        k_ref[0, rows] = k_out
        if rope:
            v_both = jnp.concatenate([kv[:, KV_WIDTH:], pltpu.roll(kv[:, KV_WIDTH:], HEAD_DIM, 1)], axis=1)
            v_ref[0, :, rows] = v_both.T.astype(BF16)
        else:
            v_ref[0, rows] = v_out
        q = proj(C_Q, ATTN_WIDTH)
        for p in range(HEAD_PAIRS):
            sl = slice(p * LANES, (p + 1) * LANES)
            q_ref[0, rows, sl] = (rotate(q[:, sl]) * (HEAD_DIM ** -0.5 * LOG2E)).astype(BF16)
        sga_ref[0, rows] = _silu(proj(C_GA, ATTN_WIDTH)).astype(BF16)
        uf = proj(C_UF, FOURIER_WIDTH)
        if rope:
            per_sub = sub // FFT_N2
            for slab in range(FOURIER_WIDTH // LANES):
                for j in range(per_sub):
                    uf_s[s, slab, j * UF_PITCH:j * UF_PITCH + FFT_N2] = (
                        uf[j * FFT_N2:(j + 1) * FFT_N2, slab * LANES:(slab + 1) * LANES])
            for n2 in range(FFT_N2):
                uf_ref[0, n2, s * per_sub:(s + 1) * per_sub, :] = jnp.concatenate(
                    [uf_s[s, slab, pl.ds(n2, per_sub, stride=UF_PITCH), :]
                     for slab in range(FOURIER_WIDTH // LANES)], axis=1)
        else:
            uf_ref[0, rows] = uf.astype(BF16)
        sgf_ref[0, rows] = _silu(proj(C_GF, FOURIER_WIDTH)).astype(BF16)
        t_ref[0, rows] = (proj(C_CC, CONV_WIDTH) * proj(C_ZC, CONV_WIDTH)).astype(BF16)
        bg_ref[0, rows] = (proj(C_BC, CONV_WIDTH) * _silu(proj(C_GC, CONV_WIDTH))).astype(BF16)


def _in_projection(x, mod, g_pre, w_in, layer, *, rope_tabs, ctx_row, tm, kv_only=False):
    b, n, _ = x.shape
    rope = rope_tabs is not None
    row3 = lambda width: pl.BlockSpec((1, tm, width), lambda bi, i: (bi, i, 0))
    in_specs = [row3(D_MODEL),
                pl.BlockSpec((1, MOD_ROWS, D_MODEL), lambda bi, i: (layer, 0, 0)),
                pl.BlockSpec((1, MOD_ROWS, D_MODEL), lambda bi, i: (layer, 0, 1)),
                pl.BlockSpec((1, 1, D_MODEL), lambda bi, i: (layer, 0, 0)),
                pl.BlockSpec((1, D_MODEL, PROJ_WIDTH), lambda bi, i: (layer, 0, 0))]
    args = [x, mod, mod, g_pre, w_in]
    if rope:
        in_specs += [pl.BlockSpec((tm, LANES), lambda bi, i: (i, 0))] * 3
        args += list(rope_tabs)
    widths = (ATTN_WIDTH, 2 * KV_WIDTH, 2 * KV_WIDTH, ATTN_WIDTH, FOURIER_WIDTH, FOURIER_WIDTH,
              CONV_WIDTH, CONV_WIDTH)
    if kv_only:
        widths = widths[1:3]
    out_specs = [row3(w) for w in widths]
    out_shape = [jax.ShapeDtypeStruct((b, n, w), BF16) for w in widths]
    scratch = []
    if rope:
        sub = min(tm, PROJ_SUB_ROWS)
        assert sub % (FFT_N2 * SUBLANES) == 0
        out_specs[4] = pl.BlockSpec((1, FFT_N2, tm // FFT_N2, FOURIER_WIDTH), lambda bi, i: (bi, 0, i, 0))
        out_shape[4] = jax.ShapeDtypeStruct((b, FFT_N2, n // FFT_N2, FOURIER_WIDTH), F32)
        out_specs[2] = pl.BlockSpec((1, 2 * KV_WIDTH, tm), lambda bi, i: (bi, 0, i))
        out_shape[2] = jax.ShapeDtypeStruct((b, 2 * KV_WIDTH, n), BF16)
        scratch = [pltpu.VMEM((tm // sub, FOURIER_WIDTH // LANES, sub // FFT_N2 * UF_PITCH, LANES), F32)]
    return pl.pallas_call(
        functools.partial(_inproj_kernel, rope=rope, ctx_row=ctx_row, kv_only=kv_only, tm=tm),
        grid=(b, n // tm),
        in_specs=in_specs,
        out_specs=out_specs,
        out_shape=out_shape,
        scratch_shapes=scratch,
        compiler_params=_cparams("arbitrary", "arbitrary"),
        name="in_projection_rope" if rope else ("in_projection_ctx_kv" if kv_only else "in_projection_ctx"),
    )(*args)


def _lane_half_variants(blk, fill):
    straight, swapped = blk[:, :KV_WIDTH], blk[:, KV_WIDTH:]
    lo = lax.broadcasted_iota(jnp.int32, straight.shape, 1) < HEAD_DIM
    other = jnp.full_like(straight, fill)
    return (jnp.where(lo, straight, other), jnp.where(lo, other, swapped),
            jnp.where(lo, swapped, other), jnp.where(lo, other, straight))


def _row_half_variants(blk_t, fill):
    straight, swapped = blk_t[:KV_WIDTH], blk_t[KV_WIDTH:]
    lo = lax.broadcasted_iota(jnp.int32, straight.shape, 0) < HEAD_DIM
    other = jnp.full_like(straight, fill)
    return (jnp.where(lo, straight, other), jnp.where(lo, other, swapped),
            jnp.where(lo, swapped, other), jnp.where(lo, other, straight))


def _merge_head_pair(pv_lo, pv_hi, sink_lo, sink_hi):
    lane_lo = lax.broadcasted_iota(jnp.int32, pv_lo.shape, 1) < HEAD_DIM
    num = jnp.where(lane_lo, pv_lo, pv_hi)
    den = pltpu.roll(jnp.where(lane_lo, pv_hi, pv_lo), HEAD_DIM, 1) + jnp.where(lane_lo, sink_lo, sink_hi)
    return num / den


def _conv_rows(t, above, below, bg, cw, cb):
    n_rows = t.shape[0]
    ridx = lax.broadcasted_iota(jnp.int32, t.shape, 0)
    up = jnp.where(ridx == 0, above, pltpu.roll(t, 1, 0))
    dn = jnp.where(ridx == n_rows - 1, below, pltpu.roll(t, n_rows - 1, 0))
    conv = up * cw[0:1] + t * cw[1:2] + dn * cw[2:3] + cb
    return (conv * bg.astype(F32)).astype(BF16)


def _mix_rows(a, f, s, w, g_post, gate, x):
    h = jnp.concatenate([a, f, s], axis=1)
    y = _dot(h, w)
    r = lax.rsqrt(jnp.mean(y * y, axis=-1, keepdims=True) + NORM_EPS)
    return x + (y * r) * (gate * g_post)


def _attn_kernel(sink_ref, q_ref, g_ref, kp_ref, km_ref, kn_ref, vp_ref, vm_ref, vn_ref, kc_ref, vc_ref,
                 x_ref, f_ref, t_ref, tp_ref, tn_ref, bg_ref, cw_ref, cb_ref, w_ref, gpost_ref, mgate_ref,
                 o_ref, k_s, v_s, kctx_s, vctx_s, bias_s, s_s, p_s, r_s, a_s, c_s, *, tq, n_seq, n_ctx):
    i = pl.program_id(1)
    n_blocks = tq // BLOCK_Q
    for off, kref, vref, rows in ((0, kp_ref, vp_ref, WINDOW), (WINDOW, km_ref, vm_ref, tq),
                                  (WINDOW + tq, kn_ref, vn_ref, WINDOW)):
        for idx, kk in enumerate(_lane_half_variants(kref[0], 0.0)):
            k_s[idx, off:off + rows] = kk
        for idx, vv in enumerate(_row_half_variants(vref[0], 1.0)):
            v_s[idx, :, off:off + rows] = vv
    for idx, kk in enumerate(_lane_half_variants(kc_ref[0], 0.0)):
        kctx_s[idx] = kk
    for idx, vv in enumerate(_row_half_variants(vc_ref[0].astype(F32).T.astype(BF16), 1.0)):
        vctx_s[idx] = vv

    jj = lax.broadcasted_iota(jnp.int32, (WINDOW, BLOCK_Q), 0)
    ii = lax.broadcasted_iota(jnp.int32, (WINDOW, BLOCK_Q), 1)
    head_band = jnp.where(jj >= ii, 0.0, -jnp.inf)
    tail_band = jnp.where(jj <= ii, 0.0, -jnp.inf)
    bias_s[0] = head_band
    bias_s[1] = tail_band
    bias_s[2] = jnp.where(i == 0, -jnp.inf, head_band)
    bias_s[3] = jnp.where(i == pl.num_programs(1) - 1, -jnp.inf, tail_band)

    head_w = n_ctx + KEY_SPAN

    def scores(sb):
        par, r0 = sb % BLOCK_BUFFERS, sb * BLOCK_Q
        q = q_ref[0, r0:r0 + BLOCK_Q, :]
        for kv in range(KV_HEADS):
            pairs = range(kv * PAIRS_PER_KV, (kv + 1) * PAIRS_PER_KV)
            qg = jnp.concatenate([q[:, p * LANES:(p + 1) * LANES] for p in pairs], axis=0)
            keys = jnp.concatenate([kctx_s[2 * kv], k_s[2 * kv, r0:r0 + KEY_SPAN],
                                    kctx_s[2 * kv + 1], k_s[2 * kv + 1, r0:r0 + KEY_SPAN]], axis=0)
            s_s[par, kv] = _dot_nt(keys, qg)

    def softmax(sb):
        par = sb % BLOCK_BUFFERS
        head_bias = 2 if sb == 0 else 0
        tail_bias = 3 if sb == n_blocks - 1 else 1
        n_blk = head_w // LANES
        first_local = n_ctx // LANES
        for p in range(HEAD_PAIRS):
            kv, lanes = p // PAIRS_PER_KV, slice((p % PAIRS_PER_KV) * BLOCK_Q, (p % PAIRS_PER_KV + 1) * BLOCK_Q)
            for half in range(2):
                h = 2 * p + half
                sink2 = sink_ref[h] * LOG2E

                def key_block(t):
                    rows = slice(half * head_w + t * LANES, half * head_w + (t + 1) * LANES)
                    blk = s_s[par, kv, rows, lanes]
                    if t == first_local:
                        blk = blk + bias_s[head_bias]
                    elif t == n_blk - 1:
                        blk = blk + bias_s[tail_bias]
                    return rows, blk

                mx = functools.reduce(jnp.maximum, [key_block(t)[1] for t in range(n_blk)])
                m = jnp.maximum(jnp.max(mx, axis=0, keepdims=True), sink2)
                for t in range(n_blk):
                    rows, blk = key_block(t)
                    p_s[par, kv, rows, lanes] = jnp.exp2(blk - m).astype(BF16)
                r_s[par, h] = jnp.broadcast_to(jnp.exp2(sink2 - m), (SUBLANES, BLOCK_Q))

    def weighted_values(sb):
        par, r0 = sb % BLOCK_BUFFERS, sb * BLOCK_Q
        for kv in range(KV_HEADS):
            pv = []
            for half in range(2):
                vals_t = jnp.concatenate([vctx_s[2 * kv + half], v_s[2 * kv + half, :, r0:r0 + KEY_SPAN]],
                                         axis=1)
                pv.append(_dot(vals_t, p_s[par, kv, half * head_w:(half + 1) * head_w, :]))
            for j in range(PAIRS_PER_KV):
                p = kv * PAIRS_PER_KV + j
                lanes = slice(j * BLOCK_Q, (j + 1) * BLOCK_Q)
                lo, hi = pv[0][:, lanes], pv[1][:, lanes]
                num = jnp.concatenate([lo[:HEAD_DIM], hi[HEAD_DIM:]], axis=0)
                den = jnp.concatenate([lo[HEAD_DIM:], hi[:HEAD_DIM]], axis=0)
                sink_t = jnp.concatenate([jnp.broadcast_to(r_s[par, 2 * p + half, 0:1], (HEAD_DIM, BLOCK_Q))
                                          for half in range(2)], axis=0)
                out = (num / (den + sink_t)).T
                gate = g_ref[0, r0:r0 + BLOCK_Q, p * LANES:(p + 1) * LANES].astype(F32)
                a_s[r0:r0 + BLOCK_Q, p * LANES:(p + 1) * LANES] = (out * gate).astype(BF16)

    def conv_edge(ref, row, keep):
        return jnp.where(keep, ref[0].astype(F32)[row:row + 1, :], 0.0)

    for r0 in range(0, tq, MIX_ROWS):
        rows = slice(r0, r0 + MIX_ROWS)
        if r0 == 0:
            above = conv_edge(tp_ref, BF16_ROWS - 1, i > 0)
        else:
            above = t_ref[0, r0 - BF16_ROWS:r0, :].astype(F32)[BF16_ROWS - 1:, :]
        if r0 + MIX_ROWS == tq:
            below = conv_edge(tn_ref, 0, i < pl.num_programs(1) - 1)
        else:
            below = t_ref[0, r0 + MIX_ROWS:r0 + MIX_ROWS + BF16_ROWS, :].astype(F32)[:1, :]
        c_s[rows] = _conv_rows(t_ref[0, rows].astype(F32), above, below, bg_ref[0, rows], cw_ref[0],
                               cb_ref[0])

    def mix(r0):
        rows = slice(r0, r0 + MIX_ROWS)
        o_ref[0, rows] = _mix_rows(a_s[rows], f_ref[0, rows], c_s[rows], w_ref[0], gpost_ref[0],
                                   mgate_ref[0, pl.ds(pl.program_id(0), 1), :], x_ref[0, rows])

    per_mix = MIX_ROWS // BLOCK_Q
    scores(0)
    for sb in range(n_blocks):
        if sb + 1 < n_blocks:
            scores(sb + 1)
        softmax(sb)
        weighted_values(sb)
        if (sb + 1) % per_mix == 0:
            mix((sb + 1 - per_mix) * BLOCK_Q)


def _attention_and_mix(x, q, gate, k, v, kc, vc, sink, f, t, bg, conv_w, conv_b, w_out, g_post, mod,
                       layer, *, tq):
    b, n, _ = q.shape
    nc = kc.shape[1]
    per = tq // WINDOW
    last = n // WINDOW - 1
    kvw = 2 * KV_WIDTH
    variants = 2 * KV_HEADS
    main = lambda width: pl.BlockSpec((1, tq, width), lambda bi, i: (bi, i, 0))
    prev = pl.BlockSpec((1, WINDOW, kvw), lambda bi, i: (bi, jnp.maximum(i * per - 1, 0), 0))
    nxt = pl.BlockSpec((1, WINDOW, kvw), lambda bi, i: (bi, jnp.minimum((i + 1) * per, last), 0))
    ctx = pl.BlockSpec((1, nc, kvw), lambda bi, i: (bi, 0, 0))
    lay3 = lambda shape: pl.BlockSpec((1,) + shape, lambda bi, i: (layer, 0, 0))
    t_per = tq // BF16_ROWS
    t_last = n // BF16_ROWS - 1
    t_prev = pl.BlockSpec((1, BF16_ROWS, CONV_WIDTH), lambda bi, i: (bi, jnp.maximum(i * t_per - 1, 0), 0))
    t_next = pl.BlockSpec((1, BF16_ROWS, CONV_WIDTH),
                          lambda bi, i: (bi, jnp.minimum((i + 1) * t_per, t_last), 0))
    v_prev = pl.BlockSpec((1, kvw, WINDOW), lambda bi, i: (bi, 0, jnp.maximum(i * per - 1, 0)))
    v_main = pl.BlockSpec((1, kvw, tq), lambda bi, i: (bi, 0, i))
    v_next = pl.BlockSpec((1, kvw, WINDOW), lambda bi, i: (bi, 0, jnp.minimum((i + 1) * per, last)))
    span = tq + 2 * WINDOW
    s_rows = 2 * (nc + KEY_SPAN)
    group_q = PAIRS_PER_KV * BLOCK_Q
    return pl.pallas_call(
        functools.partial(_attn_kernel, tq=tq, n_seq=n, n_ctx=nc),
        grid=(b, n // tq),
        in_specs=[pl.BlockSpec(memory_space=pltpu.SMEM), main(ATTN_WIDTH), main(ATTN_WIDTH),
                  prev, main(kvw), nxt, v_prev, v_main, v_next, ctx, ctx,
                  main(D_MODEL),
                  main(FOURIER_WIDTH), main(CONV_WIDTH), t_prev, t_next, main(CONV_WIDTH),
                  lay3((3, CONV_WIDTH)), lay3((1, CONV_WIDTH)), lay3((MIX_WIDTH, D_MODEL)),
                  lay3((1, D_MODEL)),
                  pl.BlockSpec((1, MOD_ROWS, D_MODEL), lambda bi, i: (layer, 0, 2))],
        out_specs=main(D_MODEL),
        out_shape=jax.ShapeDtypeStruct(x.shape, F32),
        scratch_shapes=[pltpu.VMEM((variants, span, KV_WIDTH), BF16),
                        pltpu.VMEM((variants, KV_WIDTH, span), BF16),
                        pltpu.VMEM((variants, nc, KV_WIDTH), BF16),
                        pltpu.VMEM((variants, KV_WIDTH, nc), BF16),
                        pltpu.VMEM((4, WINDOW, BLOCK_Q), F32),
                        pltpu.VMEM((BLOCK_BUFFERS, KV_HEADS, s_rows, group_q), F32),
                        pltpu.VMEM((BLOCK_BUFFERS, KV_HEADS, s_rows, group_q), BF16),
                        pltpu.VMEM((BLOCK_BUFFERS, ATTN_HEADS, SUBLANES, BLOCK_Q), F32),
                        pltpu.VMEM((tq, ATTN_WIDTH), BF16),
                        pltpu.VMEM((tq, CONV_WIDTH), BF16)],
        compiler_params=_cparams("arbitrary", "arbitrary"),
        name="attention_and_mix",
    )(sink, q, gate, k, k, k, v, v, v, kc, vc, x, f, t, t, t, bg, conv_w, conv_b, w_out, g_post, mod)


def _ctx_attn_kernel(sink_ref, q_ref, g_ref, kc_ref, vc_ref, o_ref):
    keys = _lane_half_variants(kc_ref[0], 0.0)
    vals = _lane_half_variants(vc_ref[0], 1.0)
    for p in range(HEAD_PAIRS):
        kv = p // PAIRS_PER_KV
        qp = q_ref[0, :, p * LANES:(p + 1) * LANES]
        pv, sink_terms = [], []
        for half in range(2):
            s = _dot_nt(qp, keys[2 * kv + half])
            sink2 = sink_ref[2 * p + half] * LOG2E
            m = jnp.maximum(jnp.max(s, axis=-1, keepdims=True), sink2)
            pv.append(_dot(jnp.exp2(s - m).astype(BF16), vals[2 * kv + half]))
            sink_terms.append(jnp.exp2(sink2 - m))
        out = _merge_head_pair(pv[0], pv[1], sink_terms[0], sink_terms[1])
        gate = g_ref[0, :, p * LANES:(p + 1) * LANES].astype(F32)
        o_ref[0, :, p * LANES:(p + 1) * LANES] = (out * gate).astype(BF16)


def _context_attention(q, gate, kc, vc, sink):
    b, nc, _ = q.shape
    blk = lambda width: pl.BlockSpec((1, nc, width), lambda bi: (bi, 0, 0))
    return pl.pallas_call(
        _ctx_attn_kernel,
        grid=(b,),
        in_specs=[pl.BlockSpec(memory_space=pltpu.SMEM), blk(ATTN_WIDTH), blk(ATTN_WIDTH),
                  blk(2 * KV_WIDTH), blk(2 * KV_WIDTH)],
        out_specs=blk(ATTN_WIDTH),
        out_shape=jax.ShapeDtypeStruct((b, nc, ATTN_WIDTH), BF16),
        compiler_params=_cparams("arbitrary"),
        name="context_attention",
    )(sink, q, gate, kc, vc)


def _channel_mix_matrix(cc_ref, sc_ref, wf_ref, scale):
    wf = wf_ref[0]
    return (jnp.concatenate([_dot(cc_ref[...], wf), -_dot(sc_ref[...], wf)], axis=1) * scale).astype(BF16)


def _fourier_kernel(uf_ref, gate_ref, cc_ref, sc_ref, wf_ref, g_ref, m2_ref, o_ref, mix_s, y_s, *, n_seq):
    n1_len = n_seq // FFT_N2
    chunk = pl.program_id(1)
    halves = FOURIER_WIDTH // LANES

    @pl.when(chunk == 0)
    def _():
        zero = jnp.zeros((LANES, LANES), F32)
        scale = (n_seq * FOURIER_GROUP_DIM) ** -0.5
        re = [_dot(cc_ref[...], wf_ref[h]) * scale for h in range(halves)]
        im = [_dot(sc_ref[...], wf_ref[h]) * -scale for h in range(halves)]
        for h in range(halves):
            row = [re[h] if j == h else zero for j in range(halves)]
            row += [im[h] if j == h else zero for j in range(halves)]
            mix_s[h * LANES:(h + 1) * LANES] = jnp.concatenate(row, axis=1).astype(BF16)

    zs = [_dot(uf_ref[0, u].astype(BF16), mix_s[...]) for u in range(N2_CHUNK)]
    for u, z in enumerate(zs):
        rhs = jnp.concatenate([z[:, :FOURIER_WIDTH], z[:, FOURIER_WIDTH:]], axis=0).astype(BF16)
        y = _dot(g_ref[chunk * N2_CHUNK + u], rhs)
        y_s[chunk * N2_CHUNK + u] = y.astype(BF16).reshape(n1_len // K1_GROUP, 2 * K1_GROUP, FOURIER_WIDTH)

    @pl.when(chunk == pl.num_programs(1) - 1)
    def _():
        per_store = BF16_ROWS // K1_GROUP

        def stage2(t, carry):
            outs = []
            for u in range(per_store):
                rhs = y_s[:, t * per_store + u].reshape(FFT_N2 * 2 * K1_GROUP, FOURIER_WIDTH)
                outs.append(_dot(m2_ref[...], rhs).reshape(FFT_N2, K1_GROUP, FOURIER_WIDTH))
            r0 = pl.multiple_of(t * BF16_ROWS, BF16_ROWS)
            gate = gate_ref[0, :, pl.ds(r0, BF16_ROWS), :].astype(F32)
            o_ref[0, :, pl.ds(r0, BF16_ROWS), :] = (jnp.concatenate(outs, axis=1) * gate).astype(BF16)
            return carry

        lax.fori_loop(0, n1_len // BF16_ROWS, stage2, 0)


def _fourier_mix(uf, gate, wf_half, consts):
    b, _, n1_len, _ = uf.shape
    n = FFT_N2 * n1_len
    cc, sc, g, m2 = consts
    full = lambda shape: pl.BlockSpec(shape, lambda bi, c: (0,) * len(shape))
    whole = pl.BlockSpec((1, FFT_N2, n1_len, FOURIER_WIDTH), lambda bi, c: (bi, 0, 0, 0))
    out = pl.pallas_call(
        functools.partial(_fourier_kernel, n_seq=n),
        grid=(b, FFT_N2 // N2_CHUNK),
        in_specs=[pl.BlockSpec((1, N2_CHUNK, n1_len, FOURIER_WIDTH), lambda bi, c: (bi, c, 0, 0)),
                  whole, full((LANES, LANES)), full((LANES, LANES)), full(wf_half.shape),
                  full(g.shape), full(m2.shape)],
        out_specs=whole,
        out_shape=jax.ShapeDtypeStruct((b, FFT_N2, n1_len, FOURIER_WIDTH), BF16),
        scratch_shapes=[pltpu.VMEM((FOURIER_WIDTH, 2 * FOURIER_WIDTH), BF16),
                        pltpu.VMEM((FFT_N2, n1_len // K1_GROUP, 2 * K1_GROUP, FOURIER_WIDTH), BF16)],
        compiler_params=_cparams("arbitrary", "arbitrary"),
        name="fourier_mix",
    )(uf, gate.reshape(b, FFT_N2, n1_len, FOURIER_WIDTH), cc, sc, wf_half, g, m2)
    return out.reshape(b, n, FOURIER_WIDTH)


def _ctx_fourier_kernel(uf_ref, gate_ref, cc_ref, sc_ref, wf_ref, dft_ref, o_ref, *, n_seq):
    mix = _channel_mix_matrix(cc_ref, sc_ref, wf_ref, (n_seq * FOURIER_GROUP_DIM) ** -0.5)
    z = _dot(uf_ref[0], mix)
    rhs = jnp.concatenate([z[:, :LANES], z[:, LANES:]], axis=0).astype(BF16)
    o_ref[0] = (_dot(dft_ref[...], rhs) * gate_ref[0].astype(F32)).astype(BF16)


def _ctx_fourier_mix(uf, gate, wf_half, cc, sc, dft):
    b, n, _ = uf.shape
    halves = FOURIER_WIDTH // LANES
    full = lambda shape: pl.BlockSpec(shape, lambda bi, hf: (0,) * len(shape))
    half = pl.BlockSpec((1, n, LANES), lambda bi, hf: (bi, 0, hf))
    return pl.pallas_call(
        functools.partial(_ctx_fourier_kernel, n_seq=n),
        grid=(b, halves),
        in_specs=[half, half, full((LANES, LANES)), full((LANES, LANES)),
                  pl.BlockSpec((1, LANES, LANES), lambda bi, hf: (hf, 0, 0)), full(dft.shape)],
        out_specs=half,
        out_shape=jax.ShapeDtypeStruct((b, n, FOURIER_WIDTH), BF16),
        compiler_params=_cparams("arbitrary", "arbitrary"),
        name="context_fourier_mix",
    )(uf, gate, cc, sc, wf_half, dft)


def _ctx_mix_kernel(x_ref, a_ref, f_ref, t_ref, bg_ref, cw_ref, cb_ref, w_ref, g_ref, gate_ref, o_ref,
                    *, ctx_row):
    edge = jnp.zeros((1, CONV_WIDTH), F32)
    conv = _conv_rows(t_ref[0].astype(F32), edge, edge, bg_ref[0], cw_ref[0], cb_ref[0])
    o_ref[0] = _mix_rows(a_ref[0], f_ref[0], conv, w_ref[0], g_ref[0], gate_ref[0, ctx_row:ctx_row + 1, :],
                         x_ref[0])


def _context_mix(x, a, f, t, bg, conv_w, conv_b, w_out, g_post, mod, layer, *, ctx_row):
    b, n, _ = x.shape
    row3 = lambda width: pl.BlockSpec((1, n, width), lambda bi: (bi, 0, 0))
    lay3 = lambda shape: pl.BlockSpec((1,) + shape, lambda bi: (layer, 0, 0))
    return pl.pallas_call(
        functools.partial(_ctx_mix_kernel, ctx_row=ctx_row),
        grid=(b,),
        in_specs=[row3(D_MODEL), row3(ATTN_WIDTH), row3(FOURIER_WIDTH), row3(CONV_WIDTH), row3(CONV_WIDTH), lay3((3, CONV_WIDTH)), lay3((1, CONV_WIDTH)),
                  lay3((MIX_WIDTH, D_MODEL)), lay3((1, D_MODEL)),
                  pl.BlockSpec((1, MOD_ROWS, D_MODEL), lambda bi: (layer, 0, 2))],
        out_specs=row3(D_MODEL),
        out_shape=jax.ShapeDtypeStruct(x.shape, F32),
        compiler_params=_cparams("arbitrary"),
        name="context_mix",
    )(x, a, f, t, bg, conv_w, conv_b, w_out, g_post, mod)


def kernel(x, c, ctx, c_ctx, w_mod, b_mod, g_pre, g_post, w_in, w_out, sink, w_fourier, conv_w, conv_b):
    depth = w_mod.shape[0]
    b, n, _ = x.shape
    nc = ctx.shape[1]
    assert b + 1 <= MOD_ROWS and n % GRID_W == 0 and n % ATTN_ROWS == 0 and n % TILE_ROWS == 0
    assert (n // FFT_N2) % BF16_ROWS == 0 and FFT_N2 % N2_CHUNK == 0

    w_in_b = w_in.astype(BF16)
    w_out_b = w_out.astype(BF16)

    rope_tabs = tuple(jnp.asarray(t) for t in _rope_tables(n))
    cc, sc = (jnp.asarray(m) for m in _channel_dft())
    stage1 = jnp.asarray(_stage1_mats(n)).astype(BF16)
    stage2 = jnp.asarray(_stage2_mat()).astype(BF16)
    ctx_dft = jnp.asarray(_dense_dft(nc)).astype(BF16)
    groups_per_half = LANES // FOURIER_GROUP_DIM
    wf_half = jnp.zeros((depth, FOURIER_GROUPS // groups_per_half, LANES, LANES), F32)
    for g in range(FOURIER_GROUPS):
        o = (g % groups_per_half) * FOURIER_GROUP_DIM
        wf_half = wf_half.at[:, g // groups_per_half, o:o + FOURIER_GROUP_DIM,
                             o:o + FOURIER_GROUP_DIM].set(w_fourier[:, g])

    c_rows = jnp.zeros((MOD_ROWS, D_MODEL), F32).at[:b].set(c).at[b].set(c_ctx)
    mod = _modulation(c_rows, w_mod, b_mod)
    g_pre3 = g_pre.reshape(depth, 1, D_MODEL)
    g_post3 = g_post.reshape(depth, 1, D_MODEL)
    conv_b3 = conv_b.reshape(depth, 1, CONV_WIDTH)

    for l in range(depth):
        update_ctx = l < depth - 1
        ctx_rows = ctx.reshape(1, b * nc, D_MODEL)
        ctx_proj = _in_projection(ctx_rows, mod, g_pre3, w_in_b, l, rope_tabs=None, ctx_row=b,
                                  tm=b * nc, kv_only=not update_ctx)
        ctx_proj = [y.reshape(b, nc, y.shape[-1]) for y in ctx_proj]
        if update_ctx:
            qc, kc, vc, sgac, ufc, sgfc, tc, bgc = ctx_proj
        else:
            kc, vc = ctx_proj
        q, k, v, sga, uf, sgf, t, bg = _in_projection(
            x, mod, g_pre3, w_in_b, l, rope_tabs=rope_tabs, ctx_row=None, tm=TILE_ROWS)
        f = _fourier_mix(uf, sgf, wf_half[l], (cc, sc, stage1, stage2))
        x = _attention_and_mix(x, q, sga, k, v, kc, vc, sink[l], f, t, bg, conv_w, conv_b3, w_out_b,
                               g_post3, mod, l, tq=ATTN_ROWS)
        if update_ctx:
            ac = _context_attention(qc, sgac, kc, vc, sink[l])
            fc = _ctx_fourier_mix(ufc, sgfc, wf_half[l], cc, sc, ctx_dft)
            ctx = _context_mix(ctx, ac, fc, tc, bgc, conv_w, conv_b3, w_out_b, g_post3, mod, l,
                               ctx_row=b)
    return x
```

```python
import functools
import math

import numpy as np
import jax
import jax.numpy as jnp
from jax import lax
from jax.experimental import pallas as pl
from jax.experimental.pallas import tpu as pltpu

D_MODEL = 1024
GRID_W = 64
HEAD_DIM = 64
ATTN_HEADS = 8
KV_HEADS = 2
ATTN_WIDTH = ATTN_HEADS * HEAD_DIM
KV_WIDTH = KV_HEADS * HEAD_DIM
WINDOW = 128
FOURIER_GROUPS = 4
FOURIER_GROUP_DIM = 64
FOURIER_WIDTH = FOURIER_GROUPS * FOURIER_GROUP_DIM
CONV_WIDTH = 256
MIX_WIDTH = ATTN_WIDTH + FOURIER_WIDTH + CONV_WIDTH
PROJ_WIDTH = 2 * ATTN_WIDTH + 2 * KV_WIDTH + 2 * FOURIER_WIDTH + 4 * CONV_WIDTH
ROPE_FREQS = HEAD_DIM // 4
ROPE_BASE = 10000.0
NORM_EPS = 1e-6
LOG2E = math.log2(math.e)

C_Q = 0
C_K = C_Q + ATTN_WIDTH
C_V = C_K + KV_WIDTH
C_GA = C_V + KV_WIDTH
C_UF = C_GA + ATTN_WIDTH
C_GF = C_UF + FOURIER_WIDTH
C_ZC = C_GF + FOURIER_WIDTH
C_BC = C_ZC + CONV_WIDTH
C_CC = C_BC + CONV_WIDTH
C_GC = C_CC + CONV_WIDTH

LANES = 128
SUBLANES = 8
BF16_ROWS = 16
VMEM_LIMIT = 52 * 1024 * 1024

MOD_ROWS = 8
BF16 = jnp.bfloat16
F32 = jnp.float32

FFT_N2 = 64
K1_GROUP = SUBLANES
N2_CHUNK = 32
UF_PITCH = FFT_N2 + SUBLANES

BLOCK_Q = 128
KEY_SPAN = BLOCK_Q + 2 * WINDOW
HEAD_PAIRS = ATTN_WIDTH // LANES
PAIRS_PER_KV = HEAD_PAIRS // KV_HEADS

TILE_ROWS = 1024
PROJ_SUB_ROWS = 512
ATTN_ROWS = 1024
BLOCK_BUFFERS = 2
MIX_ROWS = 512


def _silu(x):
    return x / (1.0 + jnp.exp(-x))


def _dot(a, b):
    return jnp.dot(a, b, preferred_element_type=F32)


def _dot_nt(a, b):
    return lax.dot_general(a, b, (((1,), (1,)), ((), ())), preferred_element_type=F32)


def _cparams(*sem):
    return pltpu.CompilerParams(dimension_semantics=sem, vmem_limit_bytes=VMEM_LIMIT)


def _rope_tables(n):
    t = np.arange(n)
    row = (t // GRID_W).astype(np.float64)
    col = (t % GRID_W).astype(np.float64)
    inv = ROPE_BASE ** (-np.arange(ROPE_FREQS, dtype=np.float64) / ROPE_FREQS)
    ar = row[:, None] * inv
    ac = col[:, None] * inv
    z = np.zeros_like(ar)
    cos_h = np.concatenate([np.cos(ar), np.cos(ar), np.cos(ac), np.cos(ac)], axis=1)
    sin_up = np.concatenate([-np.sin(ar), z, -np.sin(ac), z], axis=1)
    sin_dn = np.concatenate([z, np.sin(ar), z, np.sin(ac)], axis=1)
    rep = LANES // HEAD_DIM
    return tuple(np.tile(a, (1, rep)).astype(np.float32) for a in (cos_h, sin_up, sin_dn))


def _channel_dft():
    c = np.arange(FOURIER_GROUP_DIM)
    ang = 2.0 * np.pi * np.outer(c, c) / FOURIER_GROUP_DIM
    eye = np.eye(LANES // FOURIER_GROUP_DIM)
    return (np.kron(eye, np.cos(ang)).astype(np.float32),
            np.kron(eye, np.sin(ang)).astype(np.float32))


def _stage1_mats(n):
    n1_len = n // FFT_N2
    k1 = np.arange(n1_len)[:, None]
    n1 = np.arange(n1_len)[None, :]
    out = np.empty((FFT_N2, 2 * n1_len, 2 * n1_len), np.float32)
    for n2 in range(FFT_N2):
        ang = 2.0 * np.pi * ((k1 * (FFT_N2 * n1 + n2)) % n) / n
        ce, se = np.cos(ang), np.sin(ang)
        by_part = np.block([[ce, se], [-se, ce]]).reshape(2, n1_len // K1_GROUP, K1_GROUP, 2 * n1_len)
        out[n2] = by_part.transpose(1, 0, 2, 3).reshape(2 * n1_len, 2 * n1_len)
    return out


def _stage2_mat():
    k2 = np.arange(FFT_N2)
    ang = 2.0 * np.pi * np.outer(k2, k2) / FFT_N2
    cs = np.stack([np.cos(ang), np.sin(ang)], axis=-1)
    eye = np.eye(K1_GROUP)
    m = np.einsum('knp,rs->krnps', cs, eye)
    return m.reshape(FFT_N2 * K1_GROUP, FFT_N2 * 2 * K1_GROUP).astype(np.float32)


def _dense_dft(n):
    t = np.arange(n)
    ang = 2.0 * np.pi * (np.outer(t, t) % n) / n
    return np.concatenate([np.cos(ang), np.sin(ang)], axis=1).astype(np.float32)


def _mod_kernel(c_ref, w_ref, b_ref, o_ref):
    o_ref[0] = _dot(_silu(c_ref[...]), w_ref[0]) + b_ref[0]


def _modulation(c_rows, w_mod, b_mod):
    depth = w_mod.shape[0]
    return pl.pallas_call(
        _mod_kernel,
        grid=(depth, 3),
        in_specs=[pl.BlockSpec((MOD_ROWS, D_MODEL), lambda l, j: (0, 0)),
                  pl.BlockSpec((1, D_MODEL, D_MODEL), lambda l, j: (l, 0, j)),
                  pl.BlockSpec((1, 1, D_MODEL), lambda l, j: (l, 0, j))],
        out_specs=pl.BlockSpec((1, MOD_ROWS, D_MODEL), lambda l, j: (l, 0, j)),
        out_shape=jax.ShapeDtypeStruct((depth, MOD_ROWS, 3 * D_MODEL), F32),
        compiler_params=_cparams("arbitrary", "arbitrary"),
        name="modulation",
    )(c_rows, w_mod, b_mod.reshape(depth, 1, 3 * D_MODEL))


def _inproj_kernel(*refs, rope, ctx_row, kv_only, tm):
    x_ref, shift_ref, scale_ref, g_ref, w_ref = refs[:5]
    cos_ref, sup_ref, sdn_ref = refs[5:8] if rope else (None,) * 3
    outs = refs[8:-1] if rope else refs[5:]
    uf_s = refs[-1] if rope else None
    row = pl.program_id(0) if ctx_row is None else ctx_row
    shift = shift_ref[0, pl.ds(row, 1), :]
    gain = g_ref[0] * (1.0 + scale_ref[0, pl.ds(row, 1), :])

    def with_swapped_heads(y):
        return jnp.concatenate([y, pltpu.roll(y, HEAD_DIM, 1)], axis=1).astype(BF16)

    sub = min(tm, PROJ_SUB_ROWS)
    for s in range(tm // sub):
        rows = slice(s * sub, (s + 1) * sub)
        x = x_ref[0, rows]
        r = lax.rsqrt(jnp.mean(x * x, axis=-1, keepdims=True) + NORM_EPS)
        h = ((x * r) * gain + shift).astype(BF16)

        def proj(c0, width):
            return _dot(h, w_ref[0, :, c0:c0 + width])

        def rotate(y):
            if not rope:
                return y
            return (y * cos_ref[rows] + pltpu.roll(y, LANES - ROPE_FREQS, 1) * sup_ref[rows]
                    + pltpu.roll(y, ROPE_FREQS, 1) * sdn_ref[rows])

        kv = proj(C_K, 2 * KV_WIDTH)
        k_out = with_swapped_heads(rotate(kv[:, :KV_WIDTH]))
        v_out = with_swapped_heads(kv[:, KV_WIDTH:])
        if kv_only:
            k_ref, v_ref = outs
            k_ref[0, rows] = k_out
            v_ref[0, rows] = v_out
            continue
        q_ref, k_ref, v_ref, sga_ref, uf_ref, sgf_ref, t_ref, bg_ref = outs
        k_ref[0, rows] = k_out
        if rope:
            v_both = jnp.concatenate([kv[:, KV_WIDTH:], pltpu.roll(kv[:, KV_WIDTH:], HEAD_DIM, 1)], axis=1)
            v_ref[0, :, rows] = v_both.T.astype(BF16)
        else:
            v_ref[0, rows] = v_out
        q = proj(C_Q, ATTN_WIDTH)
        for p in range(HEAD_PAIRS):
            sl = slice(p * LANES, (p + 1) * LANES)
            q_ref[0, rows, sl] = (rotate(q[:, sl]) * (HEAD_DIM ** -0.5 * LOG2E)).astype(BF16)
        sga_ref[0, rows] = _silu(proj(C_GA, ATTN_WIDTH)).astype(BF16)
        uf = proj(C_UF, FOURIER_WIDTH)
        if rope:
            per_sub = sub // FFT_N2
            for slab in range(FOURIER_WIDTH // LANES):
                for j in range(per_sub):
                    uf_s[s, slab, j * UF_PITCH:j * UF_PITCH + FFT_N2] = (
                        uf[j * FFT_N2:(j + 1) * FFT_N2, slab * LANES:(slab + 1) * LANES])
            for n2 in range(FFT_N2):
                uf_ref[0, n2, s * per_sub:(s + 1) * per_sub, :] = jnp.concatenate(
                    [uf_s[s, slab, pl.ds(n2, per_sub, stride=UF_PITCH), :]
                     for slab in range(FOURIER_WIDTH // LANES)], axis=1)
        else:
            uf_ref[0, rows] = uf.astype(BF16)
        sgf_ref[0, rows] = _silu(proj(C_GF, FOURIER_WIDTH)).astype(BF16)
        t_ref[0, rows] = (proj(C_CC, CONV_WIDTH) * proj(C_ZC, CONV_WIDTH)).astype(BF16)
        bg_ref[0, rows] = (proj(C_BC, CONV_WIDTH) * _silu(proj(C_GC, CONV_WIDTH))).astype(BF16)


def _in_projection(x, mod, g_pre, w_in, layer, *, rope_tabs, ctx_row, tm, kv_only=False):
    b, n, _ = x.shape
    rope = rope_tabs is not None
    row3 = lambda width: pl.BlockSpec((1, tm, width), lambda bi, i: (bi, i, 0))
    in_specs = [row3(D_MODEL),
                pl.BlockSpec((1, MOD_ROWS, D_MODEL), lambda bi, i: (layer, 0, 0)),
                pl.BlockSpec((1, MOD_ROWS, D_MODEL), lambda bi, i: (layer, 0, 1)),
                pl.BlockSpec((1, 1, D_MODEL), lambda bi, i: (layer, 0, 0)),
                pl.BlockSpec((1, D_MODEL, PROJ_WIDTH), lambda bi, i: (layer, 0, 0))]
    args = [x, mod, mod, g_pre, w_in]
    if rope:
        in_specs += [pl.BlockSpec((tm, LANES), lambda bi, i: (i, 0))] * 3
        args += list(rope_tabs)
    widths = (ATTN_WIDTH, 2 * KV_WIDTH, 2 * KV_WIDTH, ATTN_WIDTH, FOURIER_WIDTH, FOURIER_WIDTH,
              CONV_WIDTH, CONV_WIDTH)
    if kv_only:
        widths = widths[1:3]
    out_specs = [row3(w) for w in widths]
    out_shape = [jax.ShapeDtypeStruct((b, n, w), BF16) for w in widths]
    scratch = []
    if rope:
        sub = min(tm, PROJ_SUB_ROWS)
        assert sub % (FFT_N2 * SUBLANES) == 0
        out_specs[4] = pl.BlockSpec((1, FFT_N2, tm // FFT_N2, FOURIER_WIDTH), lambda bi, i: (bi, 0, i, 0))
        out_shape[4] = jax.ShapeDtypeStruct((b, FFT_N2, n // FFT_N2, FOURIER_WIDTH), F32)
        out_specs[2] = pl.BlockSpec((1, 2 * KV_WIDTH, tm), lambda bi, i: (bi, 0, i))
        out_shape[2] = jax.ShapeDtypeStruct((b, 2 * KV_WIDTH, n), BF16)
        scratch = [pltpu.VMEM((tm // sub, FOURIER_WIDTH // LANES, sub // FFT_N2 * UF_PITCH, LANES), F32)]
    return pl.pallas_call(
        functools.partial(_inproj_kernel, rope=rope, ctx_row=ctx_row, kv_only=kv_only, tm=tm),
        grid=(b, n // tm),
        in_specs=in_specs,
        out_specs=out_specs,
        out_shape=out_shape,
        scratch_shapes=scratch,
        compiler_params=_cparams("arbitrary", "arbitrary"),
        name="in_projection_rope" if rope else ("in_projection_ctx_kv" if kv_only else "in_projection_ctx"),
    )(*args)


def _lane_half_variants(blk, fill):
    straight, swapped = blk[:, :KV_WIDTH], blk[:, KV_WIDTH:]
    lo = lax.broadcasted_iota(jnp.int32, straight.shape, 1) < HEAD_DIM
    other = jnp.full_like(straight, fill)
    return (jnp.where(lo, straight, other), jnp.where(lo, other, swapped),
            jnp.where(lo, swapped, other), jnp.where(lo, other, straight))


def _row_half_variants(blk_t, fill):
    straight, swapped = blk_t[:KV_WIDTH], blk_t[KV_WIDTH:]
    lo = lax.broadcasted_iota(jnp.int32, straight.shape, 0) < HEAD_DIM
    other = jnp.full_like(straight, fill)
    return (jnp.where(lo, straight, other), jnp.where(lo, other, swapped),
            jnp.where(lo, swapped, other), jnp.where(lo, other, straight))


def _merge_head_pair(pv_lo, pv_hi, sink_lo, sink_hi):
    lane_lo = lax.broadcasted_iota(jnp.int32, pv_lo.shape, 1) < HEAD_DIM
    num = jnp.where(lane_lo, pv_lo, pv_hi)
    den = pltpu.roll(jnp.where(lane_lo, pv_hi, pv_lo), HEAD_DIM, 1) + jnp.where(lane_lo, sink_lo, sink_hi)
    return num / den


def _conv_rows(t, above, below, bg, cw, cb):
    n_rows = t.shape[0]
    ridx = lax.broadcasted_iota(jnp.int32, t.shape, 0)
    up = jnp.where(ridx == 0, above, pltpu.roll(t, 1, 0))
    dn = jnp.where(ridx == n_rows - 1, below, pltpu.roll(t, n_rows - 1, 0))
    conv = up * cw[0:1] + t * cw[1:2] + dn * cw[2:3] + cb
    return (conv * bg.astype(F32)).astype(BF16)


def _mix_rows(a, f, s, w, g_post, gate, x):
    h = jnp.concatenate([a, f, s], axis=1)
    y = _dot(h, w)
    r = lax.rsqrt(jnp.mean(y * y, axis=-1, keepdims=True) + NORM_EPS)
    return x + (y * r) * (gate * g_post)


def _attn_kernel(sink_ref, q_ref, g_ref, kp_ref, km_ref, kn_ref, vp_ref, vm_ref, vn_ref, kc_ref, vc_ref,
                 x_ref, f_ref, t_ref, tp_ref, tn_ref, bg_ref, cw_ref, cb_ref, w_ref, gpost_ref, mgate_ref,
                 o_ref, k_s, v_s, kctx_s, vctx_s, bias_s, s_s, p_s, r_s, a_s, c_s, *, tq, n_seq, n_ctx):
    i = pl.program_id(1)
    n_blocks = tq // BLOCK_Q
    for off, kref, vref, rows in ((0, kp_ref, vp_ref, WINDOW), (WINDOW, km_ref, vm_ref, tq),
                                  (WINDOW + tq, kn_ref, vn_ref, WINDOW)):
        for idx, kk in enumerate(_lane_half_variants(kref[0], 0.0)):
            k_s[idx, off:off + rows] = kk
        for idx, vv in enumerate(_row_half_variants(vref[0], 1.0)):
            v_s[idx, :, off:off + rows] = vv
    for idx, kk in enumerate(_lane_half_variants(kc_ref[0], 0.0)):
        kctx_s[idx] = kk
    for idx, vv in enumerate(_row_half_variants(vc_ref[0].astype(F32).T.astype(BF16), 1.0)):
        vctx_s[idx] = vv

    jj = lax.broadcasted_iota(jnp.int32, (WINDOW, BLOCK_Q), 0)
    ii = lax.broadcasted_iota(jnp.int32, (WINDOW, BLOCK_Q), 1)
    head_band = jnp.where(jj >= ii, 0.0, -jnp.inf)
    tail_band = jnp.where(jj <= ii, 0.0, -jnp.inf)
    bias_s[0] = head_band
    bias_s[1] = tail_band
    bias_s[2] = jnp.where(i == 0, -jnp.inf, head_band)
    bias_s[3] = jnp.where(i == pl.num_programs(1) - 1, -jnp.inf, tail_band)

    head_w = n_ctx + KEY_SPAN

    def scores(sb, kvs):
        par, r0 = sb % BLOCK_BUFFERS, sb * BLOCK_Q
        q = q_ref[0, r0:r0 + BLOCK_Q, :]
        for kv in kvs:
            pairs = range(kv * PAIRS_PER_KV, (kv + 1) * PAIRS_PER_KV)
            qg = jnp.concatenate([q[:, p * LANES:(p + 1) * LANES] for p in pairs], axis=0)
            keys = jnp.concatenate([kctx_s[2 * kv], k_s[2 * kv, r0:r0 + KEY_SPAN],
                                    kctx_s[2 * kv + 1], k_s[2 * kv + 1, r0:r0 + KEY_SPAN]], axis=0)
            s_s[par, kv] = _dot_nt(keys, qg)

    def softmax(sb, kvs):
        par = sb % BLOCK_BUFFERS
        head_bias = 2 if sb == 0 else 0
        tail_bias = 3 if sb == n_blocks - 1 else 1
        n_blk = head_w // LANES
        first_local = n_ctx // LANES
        for p in [p for p in range(HEAD_PAIRS) if p // PAIRS_PER_KV in kvs]:
            kv, lanes = p // PAIRS_PER_KV, slice((p % PAIRS_PER_KV) * BLOCK_Q, (p % PAIRS_PER_KV + 1) * BLOCK_Q)
            for half in range(2):
                h = 2 * p + half
                sink2 = sink_ref[h] * LOG2E

                def key_block(t):
                    rows = slice(half * head_w + t * LANES, half * head_w + (t + 1) * LANES)
                    blk = s_s[par, kv, rows, lanes]
                    if t == first_local:
                        blk = blk + bias_s[head_bias]
                    elif t == n_blk - 1:
                        blk = blk + bias_s[tail_bias]
                    return rows, blk

                mx = functools.reduce(jnp.maximum, [key_block(t)[1] for t in range(n_blk)])
                m = jnp.maximum(jnp.max(mx, axis=0, keepdims=True), sink2)
                for t in range(n_blk):
                    rows, blk = key_block(t)
                    p_s[par, kv, rows, lanes] = jnp.exp2(blk - m).astype(BF16)
                r_s[par, h] = jnp.broadcast_to(jnp.exp2(sink2 - m), (SUBLANES, BLOCK_Q))

    def weighted_values(sb, kvs):
        par, r0 = sb % BLOCK_BUFFERS, sb * BLOCK_Q
        for kv in kvs:
            pv = []
            for half in range(2):
                vals_t = jnp.concatenate([vctx_s[2 * kv + half], v_s[2 * kv + half, :, r0:r0 + KEY_SPAN]],
                                         axis=1)
                pv.append(_dot(vals_t, p_s[par, kv, half * head_w:(half + 1) * head_w, :]))
            for j in range(PAIRS_PER_KV):
                p = kv * PAIRS_PER_KV + j
                lanes = slice(j * BLOCK_Q, (j + 1) * BLOCK_Q)
                lo, hi = pv[0][:, lanes], pv[1][:, lanes]
                num = jnp.concatenate([lo[:HEAD_DIM], hi[HEAD_DIM:]], axis=0)
                den = jnp.concatenate([lo[HEAD_DIM:], hi[:HEAD_DIM]], axis=0)
                sink_t = jnp.concatenate([jnp.broadcast_to(r_s[par, 2 * p + half, 0:1], (HEAD_DIM, BLOCK_Q))
                                          for half in range(2)], axis=0)
                out = (num / (den + sink_t)).T
                gate = g_ref[0, r0:r0 + BLOCK_Q, p * LANES:(p + 1) * LANES].astype(F32)
                a_s[r0:r0 + BLOCK_Q, p * LANES:(p + 1) * LANES] = (out * gate).astype(BF16)

    def conv_edge(ref, row, keep):
        return jnp.where(keep, ref[0].astype(F32)[row:row + 1, :], 0.0)

    for r0 in range(0, tq, MIX_ROWS):
        rows = slice(r0, r0 + MIX_ROWS)
        if r0 == 0:
            above = conv_edge(tp_ref, BF16_ROWS - 1, i > 0)
        else:
            above = t_ref[0, r0 - BF16_ROWS:r0, :].astype(F32)[BF16_ROWS - 1:, :]
        if r0 + MIX_ROWS == tq:
            below = conv_edge(tn_ref, 0, i < pl.num_programs(1) - 1)
        else:
            below = t_ref[0, r0 + MIX_ROWS:r0 + MIX_ROWS + BF16_ROWS, :].astype(F32)[:1, :]
        c_s[rows] = _conv_rows(t_ref[0, rows].astype(F32), above, below, bg_ref[0, rows], cw_ref[0],
                               cb_ref[0])

    def mix(r0):
        rows = slice(r0, r0 + MIX_ROWS)
        o_ref[0, rows] = _mix_rows(a_s[rows], f_ref[0, rows], c_s[rows], w_ref[0], gpost_ref[0],
                                   mgate_ref[0, pl.ds(pl.program_id(0), 1), :], x_ref[0, rows])

    per_mix = MIX_ROWS // BLOCK_Q
    scores(0, range(KV_HEADS))
    for sb in range(n_blocks):
        for kv in range(KV_HEADS):
            if sb + 1 < n_blocks:
                scores(sb + 1, (kv,))
            softmax(sb, (kv,))
            weighted_values(sb, (kv,))
        if (sb + 1) % per_mix == 0:
            mix((sb + 1 - per_mix) * BLOCK_Q)


def _attention_and_mix(x, q, gate, k, v, kc, vc, sink, f, t, bg, conv_w, conv_b, w_out, g_post, mod,
                       layer, *, tq):
    b, n, _ = q.shape
    nc = kc.shape[1]
    per = tq // WINDOW
    last = n // WINDOW - 1
    kvw = 2 * KV_WIDTH
    variants = 2 * KV_HEADS
    main = lambda width: pl.BlockSpec((1, tq, width), lambda bi, i: (bi, i, 0))
    prev = pl.BlockSpec((1, WINDOW, kvw), lambda bi, i: (bi, jnp.maximum(i * per - 1, 0), 0))
    nxt = pl.BlockSpec((1, WINDOW, kvw), lambda bi, i: (bi, jnp.minimum((i + 1) * per, last), 0))
    ctx = pl.BlockSpec((1, nc, kvw), lambda bi, i: (bi, 0, 0))
    lay3 = lambda shape: pl.BlockSpec((1,) + shape, lambda bi, i: (layer, 0, 0))
    t_per = tq // BF16_ROWS
    t_last = n // BF16_ROWS - 1
    t_prev = pl.BlockSpec((1, BF16_ROWS, CONV_WIDTH), lambda bi, i: (bi, jnp.maximum(i * t_per - 1, 0), 0))
    t_next = pl.BlockSpec((1, BF16_ROWS, CONV_WIDTH),
                          lambda bi, i: (bi, jnp.minimum((i + 1) * t_per, t_last), 0))
    v_prev = pl.BlockSpec((1, kvw, WINDOW), lambda bi, i: (bi, 0, jnp.maximum(i * per - 1, 0)))
    v_main = pl.BlockSpec((1, kvw, tq), lambda bi, i: (bi, 0, i))
    v_next = pl.BlockSpec((1, kvw, WINDOW), lambda bi, i: (bi, 0, jnp.minimum((i + 1) * per, last)))
    span = tq + 2 * WINDOW
    s_rows = 2 * (nc + KEY_SPAN)
    group_q = PAIRS_PER_KV * BLOCK_Q
    return pl.pallas_call(
        functools.partial(_attn_kernel, tq=tq, n_seq=n, n_ctx=nc),
        grid=(b, n // tq),
        in_specs=[pl.BlockSpec(memory_space=pltpu.SMEM), main(ATTN_WIDTH), main(ATTN_WIDTH),
                  prev, main(kvw), nxt, v_prev, v_main, v_next, ctx, ctx,
                  main(D_MODEL),
                  main(FOURIER_WIDTH), main(CONV_WIDTH), t_prev, t_next, main(CONV_WIDTH),
                  lay3((3, CONV_WIDTH)), lay3((1, CONV_WIDTH)), lay3((MIX_WIDTH, D_MODEL)),
                  lay3((1, D_MODEL)),
                  pl.BlockSpec((1, MOD_ROWS, D_MODEL), lambda bi, i: (layer, 0, 2))],
        out_specs=main(D_MODEL),
        out_shape=jax.ShapeDtypeStruct(x.shape, F32),
        scratch_shapes=[pltpu.VMEM((variants, span, KV_WIDTH), BF16),
                        pltpu.VMEM((variants, KV_WIDTH, span), BF16),
                        pltpu.VMEM((variants, nc, KV_WIDTH), BF16),
                        pltpu.VMEM((variants, KV_WIDTH, nc), BF16),
                        pltpu.VMEM((4, WINDOW, BLOCK_Q), F32),
                        pltpu.VMEM((BLOCK_BUFFERS, KV_HEADS, s_rows, group_q), F32),
                        pltpu.VMEM((BLOCK_BUFFERS, KV_HEADS, s_rows, group_q), BF16),
                        pltpu.VMEM((BLOCK_BUFFERS, ATTN_HEADS, SUBLANES, BLOCK_Q), F32),
                        pltpu.VMEM((tq, ATTN_WIDTH), BF16),
                        pltpu.VMEM((tq, CONV_WIDTH), BF16)],
        compiler_params=_cparams("arbitrary", "arbitrary"),
        name="attention_and_mix",
    )(sink, q, gate, k, k, k, v, v, v, kc, vc, x, f, t, t, t, bg, conv_w, conv_b, w_out, g_post, mod)


def _ctx_attn_kernel(sink_ref, q_ref, g_ref, kc_ref, vc_ref, o_ref):
    keys = _lane_half_variants(kc_ref[0], 0.0)
    vals = _lane_half_variants(vc_ref[0], 1.0)
    for p in range(HEAD_PAIRS):
        kv = p // PAIRS_PER_KV
        qp = q_ref[0, :, p * LANES:(p + 1) * LANES]
        pv, sink_terms = [], []
        for half in range(2):
            s = _dot_nt(qp, keys[2 * kv + half])
            sink2 = sink_ref[2 * p + half] * LOG2E
            m = jnp.maximum(jnp.max(s, axis=-1, keepdims=True), sink2)
            pv.append(_dot(jnp.exp2(s - m).astype(BF16), vals[2 * kv + half]))
            sink_terms.append(jnp.exp2(sink2 - m))
        out = _merge_head_pair(pv[0], pv[1], sink_terms[0], sink_terms[1])
        gate = g_ref[0, :, p * LANES:(p + 1) * LANES].astype(F32)
        o_ref[0, :, p * LANES:(p + 1) * LANES] = (out * gate).astype(BF16)


def _context_attention(q, gate, kc, vc, sink):
    b, nc, _ = q.shape
    blk = lambda width: pl.BlockSpec((1, nc, width), lambda bi: (bi, 0, 0))
    return pl.pallas_call(
        _ctx_attn_kernel,
        grid=(b,),
        in_specs=[pl.BlockSpec(memory_space=pltpu.SMEM), blk(ATTN_WIDTH), blk(ATTN_WIDTH),
                  blk(2 * KV_WIDTH), blk(2 * KV_WIDTH)],
        out_specs=blk(ATTN_WIDTH),
        out_shape=jax.ShapeDtypeStruct((b, nc, ATTN_WIDTH), BF16),
        compiler_params=_cparams("arbitrary"),
        name="context_attention",
    )(sink, q, gate, kc, vc)


def _channel_mix_matrix(cc_ref, sc_ref, wf_ref, scale):
    wf = wf_ref[0]
    return (jnp.concatenate([_dot(cc_ref[...], wf), -_dot(sc_ref[...], wf)], axis=1) * scale).astype(BF16)


def _fourier_kernel(uf_ref, gate_ref, cc_ref, sc_ref, wf_ref, g_ref, m2_ref, o_ref, mix_s, y_s, *, n_seq):
    n1_len = n_seq // FFT_N2
    chunk = pl.program_id(1)
    halves = FOURIER_WIDTH // LANES

    @pl.when(chunk == 0)
    def _():
        zero = jnp.zeros((LANES, LANES), F32)
        scale = (n_seq * FOURIER_GROUP_DIM) ** -0.5
        re = [_dot(cc_ref[...], wf_ref[h]) * scale for h in range(halves)]
        im = [_dot(sc_ref[...], wf_ref[h]) * -scale for h in range(halves)]
        for h in range(halves):
            row = [re[h] if j == h else zero for j in range(halves)]
            row += [im[h] if j == h else zero for j in range(halves)]
            mix_s[h * LANES:(h + 1) * LANES] = jnp.concatenate(row, axis=1).astype(BF16)

    zs = [_dot(uf_ref[0, u].astype(BF16), mix_s[...]) for u in range(N2_CHUNK)]
    for u, z in enumerate(zs):
        rhs = jnp.concatenate([z[:, :FOURIER_WIDTH], z[:, FOURIER_WIDTH:]], axis=0).astype(BF16)
        y = _dot(g_ref[chunk * N2_CHUNK + u], rhs)
        y_s[chunk * N2_CHUNK + u] = y.astype(BF16).reshape(n1_len // K1_GROUP, 2 * K1_GROUP, FOURIER_WIDTH)

    @pl.when(chunk == pl.num_programs(1) - 1)
    def _():
        per_store = BF16_ROWS // K1_GROUP

        def stage2(t, carry):
            outs = []
            for u in range(per_store):
                rhs = y_s[:, t * per_store + u].reshape(FFT_N2 * 2 * K1_GROUP, FOURIER_WIDTH)
                outs.append(_dot(m2_ref[...], rhs).reshape(FFT_N2, K1_GROUP, FOURIER_WIDTH))
            r0 = pl.multiple_of(t * BF16_ROWS, BF16_ROWS)
            gate = gate_ref[0, :, pl.ds(r0, BF16_ROWS), :].astype(F32)
            o_ref[0, :, pl.ds(r0, BF16_ROWS), :] = (jnp.concatenate(outs, axis=1) * gate).astype(BF16)
            return carry

        lax.fori_loop(0, n1_len // BF16_ROWS, stage2, 0)


def _fourier_mix(uf, gate, wf_half, consts):
    b, _, n1_len, _ = uf.shape
    n = FFT_N2 * n1_len
    cc, sc, g, m2 = consts
    full = lambda shape: pl.BlockSpec(shape, lambda bi, c: (0,) * len(shape))
    whole = pl.BlockSpec((1, FFT_N2, n1_len, FOURIER_WIDTH), lambda bi, c: (bi, 0, 0, 0))
    out = pl.pallas_call(
        functools.partial(_fourier_kernel, n_seq=n),
        grid=(b, FFT_N2 // N2_CHUNK),
        in_specs=[pl.BlockSpec((1, N2_CHUNK, n1_len, FOURIER_WIDTH), lambda bi, c: (bi, c, 0, 0)),
                  whole, full((LANES, LANES)), full((LANES, LANES)), full(wf_half.shape),
                  full(g.shape), full(m2.shape)],
        out_specs=whole,
        out_shape=jax.ShapeDtypeStruct((b, FFT_N2, n1_len, FOURIER_WIDTH), BF16),
        scratch_shapes=[pltpu.VMEM((FOURIER_WIDTH, 2 * FOURIER_WIDTH), BF16),
                        pltpu.VMEM((FFT_N2, n1_len // K1_GROUP, 2 * K1_GROUP, FOURIER_WIDTH), BF16)],
        compiler_params=_cparams("arbitrary", "arbitrary"),
        name="fourier_mix",
    )(uf, gate.reshape(b, FFT_N2, n1_len, FOURIER_WIDTH), cc, sc, wf_half, g, m2)
    return out.reshape(b, n, FOURIER_WIDTH)


def _ctx_fourier_kernel(uf_ref, gate_ref, cc_ref, sc_ref, wf_ref, dft_ref, o_ref, *, n_seq):
    mix = _channel_mix_matrix(cc_ref, sc_ref, wf_ref, (n_seq * FOURIER_GROUP_DIM) ** -0.5)
    z = _dot(uf_ref[0], mix)
    rhs = jnp.concatenate([z[:, :LANES], z[:, LANES:]], axis=0).astype(BF16)
    o_ref[0] = (_dot(dft_ref[...], rhs) * gate_ref[0].astype(F32)).astype(BF16)


def _ctx_fourier_mix(uf, gate, wf_half, cc, sc, dft):
    b, n, _ = uf.shape
    halves = FOURIER_WIDTH // LANES
    full = lambda shape: pl.BlockSpec(shape, lambda bi, hf: (0,) * len(shape))
    half = pl.BlockSpec((1, n, LANES), lambda bi, hf: (bi, 0, hf))
    return pl.pallas_call(
        functools.partial(_ctx_fourier_kernel, n_seq=n),
        grid=(b, halves),
        in_specs=[half, half, full((LANES, LANES)), full((LANES, LANES)),
                  pl.BlockSpec((1, LANES, LANES), lambda bi, hf: (hf, 0, 0)), full(dft.shape)],
        out_specs=half,
        out_shape=jax.ShapeDtypeStruct((b, n, FOURIER_WIDTH), BF16),
        compiler_params=_cparams("arbitrary", "arbitrary"),
        name="context_fourier_mix",
    )(uf, gate, cc, sc, wf_half, dft)


def _ctx_mix_kernel(x_ref, a_ref, f_ref, t_ref, bg_ref, cw_ref, cb_ref, w_ref, g_ref, gate_ref, o_ref,
                    *, ctx_row):
    edge = jnp.zeros((1, CONV_WIDTH), F32)
    conv = _conv_rows(t_ref[0].astype(F32), edge, edge, bg_ref[0], cw_ref[0], cb_ref[0])
    o_ref[0] = _mix_rows(a_ref[0], f_ref[0], conv, w_ref[0], g_ref[0], gate_ref[0, ctx_row:ctx_row + 1, :],
                         x_ref[0])


def _context_mix(x, a, f, t, bg, conv_w, conv_b, w_out, g_post, mod, layer, *, ctx_row):
    b, n, _ = x.shape
    row3 = lambda width: pl.BlockSpec((1, n, width), lambda bi: (bi, 0, 0))
    lay3 = lambda shape: pl.BlockSpec((1,) + shape, lambda bi: (layer, 0, 0))
    return pl.pallas_call(
        functools.partial(_ctx_mix_kernel, ctx_row=ctx_row),
        grid=(b,),
        in_specs=[row3(D_MODEL), row3(ATTN_WIDTH), row3(FOURIER_WIDTH), row3(CONV_WIDTH), row3(CONV_WIDTH), lay3((3, CONV_WIDTH)), lay3((1, CONV_WIDTH)),
                  lay3((MIX_WIDTH, D_MODEL)), lay3((1, D_MODEL)),
                  pl.BlockSpec((1, MOD_ROWS, D_MODEL), lambda bi: (layer, 0, 2))],
        out_specs=row3(D_MODEL),
        out_shape=jax.ShapeDtypeStruct(x.shape, F32),
        compiler_params=_cparams("arbitrary"),
        name="context_mix",
    )(x, a, f, t, bg, conv_w, conv_b, w_out, g_post, mod)


def kernel(x, c, ctx, c_ctx, w_mod, b_mod, g_pre, g_post, w_in, w_out, sink, w_fourier, conv_w, conv_b):
    depth = w_mod.shape[0]
    b, n, _ = x.shape
    nc = ctx.shape[1]
    assert b + 1 <= MOD_ROWS and n % GRID_W == 0 and n % ATTN_ROWS == 0 and n % TILE_ROWS == 0
    assert (n // FFT_N2) % BF16_ROWS == 0 and FFT_N2 % N2_CHUNK == 0

    w_in_b = w_in.astype(BF16)
    w_out_b = w_out.astype(BF16)

    rope_tabs = tuple(jnp.asarray(t) for t in _rope_tables(n))
    cc, sc = (jnp.asarray(m) for m in _channel_dft())
    stage1 = jnp.asarray(_stage1_mats(n)).astype(BF16)
    stage2 = jnp.asarray(_stage2_mat()).astype(BF16)
    ctx_dft = jnp.asarray(_dense_dft(nc)).astype(BF16)
    groups_per_half = LANES // FOURIER_GROUP_DIM
    wf_half = jnp.zeros((depth, FOURIER_GROUPS // groups_per_half, LANES, LANES), F32)
    for g in range(FOURIER_GROUPS):
        o = (g % groups_per_half) * FOURIER_GROUP_DIM
        wf_half = wf_half.at[:, g // groups_per_half, o:o + FOURIER_GROUP_DIM,
                             o:o + FOURIER_GROUP_DIM].set(w_fourier[:, g])

    c_rows = jnp.zeros((MOD_ROWS, D_MODEL), F32).at[:b].set(c).at[b].set(c_ctx)
    mod = _modulation(c_rows, w_mod, b_mod)
    g_pre3 = g_pre.reshape(depth, 1, D_MODEL)
    g_post3 = g_post.reshape(depth, 1, D_MODEL)
    conv_b3 = conv_b.reshape(depth, 1, CONV_WIDTH)

    for l in range(depth):
        update_ctx = l < depth - 1
        ctx_rows = ctx.reshape(1, b * nc, D_MODEL)
        ctx_proj = _in_projection(ctx_rows, mod, g_pre3, w_in_b, l, rope_tabs=None, ctx_row=b,
                                  tm=b * nc, kv_only=not update_ctx)
        ctx_proj = [y.reshape(b, nc, y.shape[-1]) for y in ctx_proj]
        if update_ctx:
            qc, kc, vc, sgac, ufc, sgfc, tc, bgc = ctx_proj
        else:
            kc, vc = ctx_proj
        q, k, v, sga, uf, sgf, t, bg = _in_projection(
            x, mod, g_pre3, w_in_b, l, rope_tabs=rope_tabs, ctx_row=None, tm=TILE_ROWS)
        f = _fourier_mix(uf, sgf, wf_half[l], (cc, sc, stage1, stage2))
        x = _attention_and_mix(x, q, sga, k, v, kc, vc, sink[l], f, t, bg, conv_w, conv_b3, w_out_b,
                               g_post3, mod, l, tq=ATTN_ROWS)
        if update_ctx:
            ac = _context_attention(qc, sgac, kc, vc, sink[l])
            fc = _ctx_fourier_mix(ufc, sgfc, wf_half[l], cc, sc, ctx_dft)
            ctx = _context_mix(ctx, ac, fc, tc, bgc, conv_w, conv_b3, w_out_b, g_post3, mod, l,
                               ctx_row=b)
    return x
```

```python
import functools
import math

import numpy as np
import jax
import jax.numpy as jnp
from jax import lax
from jax.experimental import pallas as pl
from jax.experimental.pallas import tpu as pltpu

D_MODEL = 1024
GRID_W = 64
HEAD_DIM = 64
ATTN_HEADS = 8
KV_HEADS = 2
ATTN_WIDTH = ATTN_HEADS * HEAD_DIM
KV_WIDTH = KV_HEADS * HEAD_DIM
WINDOW = 128
FOURIER_GROUPS = 4
FOURIER_GROUP_DIM = 64
FOURIER_WIDTH = FOURIER_GROUPS * FOURIER_GROUP_DIM
CONV_WIDTH = 256
MIX_WIDTH = ATTN_WIDTH + FOURIER_WIDTH + CONV_WIDTH
PROJ_WIDTH = 2 * ATTN_WIDTH + 2 * KV_WIDTH + 2 * FOURIER_WIDTH + 4 * CONV_WIDTH
ROPE_FREQS = HEAD_DIM // 4
ROPE_BASE = 10000.0
NORM_EPS = 1e-6
LOG2E = math.log2(math.e)

C_Q = 0
C_K = C_Q + ATTN_WIDTH
C_V = C_K + KV_WIDTH
C_GA = C_V + KV_WIDTH
C_UF = C_GA + ATTN_WIDTH
C_GF = C_UF + FOURIER_WIDTH
C_ZC = C_GF + FOURIER_WIDTH
C_BC = C_ZC + CONV_WIDTH
C_CC = C_BC + CONV_WIDTH
C_GC = C_CC + CONV_WIDTH

LANES = 128
SUBLANES = 8
BF16_ROWS = 16
VMEM_LIMIT = 52 * 1024 * 1024

MOD_ROWS = 8
BF16 = jnp.bfloat16
F32 = jnp.float32

FFT_N2 = 64
K1_GROUP = SUBLANES
N2_CHUNK = 32
UF_PITCH = FFT_N2 + SUBLANES

BLOCK_Q = 128
KEY_SPAN = BLOCK_Q + 2 * WINDOW
HEAD_PAIRS = ATTN_WIDTH // LANES
PAIRS_PER_KV = HEAD_PAIRS // KV_HEADS

TILE_ROWS = 1024
PROJ_SUB_ROWS = 512
ATTN_ROWS = 1024
BLOCK_BUFFERS = 2
MIX_ROWS = 512


def _silu(x):
    return x / (1.0 + jnp.exp(-x))


def _dot(a, b):
    return jnp.dot(a, b, preferred_element_type=F32)


def _dot_nt(a, b):
    return lax.dot_general(a, b, (((1,), (1,)), ((), ())), preferred_element_type=F32)


def _cparams(*sem):
    return pltpu.CompilerParams(dimension_semantics=sem, vmem_limit_bytes=VMEM_LIMIT)


def _rope_tables(n):
    t = np.arange(n)
    row = (t // GRID_W).astype(np.float64)
    col = (t % GRID_W).astype(np.float64)
    inv = ROPE_BASE ** (-np.arange(ROPE_FREQS, dtype=np.float64) / ROPE_FREQS)
    ar = row[:, None] * inv
    ac = col[:, None] * inv
    z = np.zeros_like(ar)
    cos_h = np.concatenate([np.cos(ar), np.cos(ar), np.cos(ac), np.cos(ac)], axis=1)
    sin_up = np.concatenate([-np.sin(ar), z, -np.sin(ac), z], axis=1)
    sin_dn = np.concatenate([z, np.sin(ar), z, np.sin(ac)], axis=1)
    rep = LANES // HEAD_DIM
    return tuple(np.tile(a, (1, rep)).astype(np.float32) for a in (cos_h, sin_up, sin_dn))


def _channel_dft():
    c = np.arange(FOURIER_GROUP_DIM)
    ang = 2.0 * np.pi * np.outer(c, c) / FOURIER_GROUP_DIM
    eye = np.eye(LANES // FOURIER_GROUP_DIM)
    return (np.kron(eye, np.cos(ang)).astype(np.float32),
            np.kron(eye, np.sin(ang)).astype(np.float32))


def _stage1_mats(n):
    n1_len = n // FFT_N2
    k1 = np.arange(n1_len)[:, None]
    n1 = np.arange(n1_len)[None, :]
    out = np.empty((FFT_N2, 2 * n1_len, 2 * n1_len), np.float32)
    for n2 in range(FFT_N2):
        ang = 2.0 * np.pi * ((k1 * (FFT_N2 * n1 + n2)) % n) / n
        ce, se = np.cos(ang), np.sin(ang)
        by_part = np.block([[ce, se], [-se, ce]]).reshape(2, n1_len // K1_GROUP, K1_GROUP, 2 * n1_len)
        out[n2] = by_part.transpose(1, 0, 2, 3).reshape(2 * n1_len, 2 * n1_len)
    return out


def _stage2_mat():
    k2 = np.arange(FFT_N2)
    ang = 2.0 * np.pi * np.outer(k2, k2) / FFT_N2
    cs = np.stack([np.cos(ang), np.sin(ang)], axis=-1)
    eye = np.eye(K1_GROUP)
    m = np.einsum('knp,rs->krnps', cs, eye)
    return m.reshape(FFT_N2 * K1_GROUP, FFT_N2 * 2 * K1_GROUP).astype(np.float32)


def _dense_dft(n):
    t = np.arange(n)
    ang = 2.0 * np.pi * (np.outer(t, t) % n) / n
    return np.concatenate([np.cos(ang), np.sin(ang)], axis=1).astype(np.float32)


def _mod_kernel(c_ref, w_ref, b_ref, o_ref):
    o_ref[0] = _dot(_silu(c_ref[...]), w_ref[0]) + b_ref[0]


def _modulation(c_rows, w_mod, b_mod):
    depth = w_mod.shape[0]
    return pl.pallas_call(
        _mod_kernel,
        grid=(depth, 3),
        in_specs=[pl.BlockSpec((MOD_ROWS, D_MODEL), lambda l, j: (0, 0)),
                  pl.BlockSpec((1, D_MODEL, D_MODEL), lambda l, j: (l, 0, j)),
                  pl.BlockSpec((1, 1, D_MODEL), lambda l, j: (l, 0, j))],
        out_specs=pl.BlockSpec((1, MOD_ROWS, D_MODEL), lambda l, j: (l, 0, j)),
        out_shape=jax.ShapeDtypeStruct((depth, MOD_ROWS, 3 * D_MODEL), F32),
        compiler_params=_cparams("arbitrary", "arbitrary"),
        name="modulation",
    )(c_rows, w_mod, b_mod.reshape(depth, 1, 3 * D_MODEL))


def _inproj_kernel(*refs, rope, ctx_row, kv_only, tm):
    x_ref, shift_ref, scale_ref, g_ref, w_ref = refs[:5]
    cos_ref, sup_ref, sdn_ref = refs[5:8] if rope else (None,) * 3
    outs = refs[8:-1] if rope else refs[5:]
    uf_s = refs[-1] if rope else None
    row = pl.program_id(0) if ctx_row is None else ctx_row
    shift = shift_ref[0, pl.ds(row, 1), :]
    gain = g_ref[0] * (1.0 + scale_ref[0, pl.ds(row, 1), :])

    def with_swapped_heads(y):
        return jnp.concatenate([y, pltpu.roll(y, HEAD_DIM, 1)], axis=1).astype(BF16)

    sub = min(tm, PROJ_SUB_ROWS)
    for s in range(tm // sub):
        rows = slice(s * sub, (s + 1) * sub)
        x = x_ref[0, rows]
        r = lax.rsqrt(jnp.mean(x * x, axis=-1, keepdims=True) + NORM_EPS)
        h = ((x * r) * gain + shift).astype(BF16)

        def proj(c0, width):
            return _dot(h, w_ref[0, :, c0:c0 + width])

        def rotate(y):
            if not rope:
                return y
            return (y * cos_ref[rows] + pltpu.roll(y, LANES - ROPE_FREQS, 1) * sup_ref[rows]
                    + pltpu.roll(y, ROPE_FREQS, 1) * sdn_ref[rows])

        kv = proj(C_K, 2 * KV_WIDTH)
        k_out = with_swapped_heads(rotate(kv[:, :KV_WIDTH]))
        v_out = with_swapped_heads(kv[:, KV_WIDTH:])
        if kv_only:
            k_ref, v_ref = outs
            k_ref[0, rows] = k_out
            v_ref[0, rows] = v_out
            continue
        q_ref, k_ref, v_ref, sga_ref, uf_ref, sgf_ref, t_ref, bg_ref = outs
        k_ref[0, rows] = k_out
        if rope:
            v_both = jnp.concatenate([kv[:, KV_WIDTH:], pltpu.roll(kv[:, KV_WIDTH:], HEAD_DIM, 1)], axis=1)
            v_ref[0, :, rows] = v_both.T.astype(BF16)
        else:
            v_ref[0, rows] = v_out
        q = proj(C_Q, ATTN_WIDTH)
        for p in range(HEAD_PAIRS):
            sl = slice(p * LANES, (p + 1) * LANES)
            q_ref[0, rows, sl] = (rotate(q[:, sl]) * (HEAD_DIM ** -0.5 * LOG2E)).astype(BF16)
        sga_ref[0, rows] = _silu(proj(C_GA, ATTN_WIDTH)).astype(BF16)
        uf = proj(C_UF, FOURIER_WIDTH)
        if rope:
            per_sub = sub // FFT_N2
            for slab in range(FOURIER_WIDTH // LANES):
                for j in range(per_sub):
                    uf_s[s, slab, j * UF_PITCH:j * UF_PITCH + FFT_N2] = (
                        uf[j * FFT_N2:(j + 1) * FFT_N2, slab * LANES:(slab + 1) * LANES])
            for n2 in range(FFT_N2):
                uf_ref[0, n2, s * per_sub:(s + 1) * per_sub, :] = jnp.concatenate(
                    [uf_s[s, slab, pl.ds(n2, per_sub, stride=UF_PITCH), :]
                     for slab in range(FOURIER_WIDTH // LANES)], axis=1)
        else:
            uf_ref[0, rows] = uf.astype(BF16)
        sgf_ref[0, rows] = _silu(proj(C_GF, FOURIER_WIDTH)).astype(BF16)
        t_ref[0, rows] = (proj(C_CC, CONV_WIDTH) * proj(C_ZC, CONV_WIDTH)).astype(BF16)
        bg_ref[0, rows] = (proj(C_BC, CONV_WIDTH) * _silu(proj(C_GC, CONV_WIDTH))).astype(BF16)


def _in_projection(x, mod, g_pre, w_in, layer, *, rope_tabs, ctx_row, tm, kv_only=False):
    b, n, _ = x.shape
    rope = rope_tabs is not None
    row3 = lambda width: pl.BlockSpec((1, tm, width), lambda bi, i: (bi, i, 0))
    in_specs = [row3(D_MODEL),
                pl.BlockSpec((1, MOD_ROWS, D_MODEL), lambda bi, i: (layer, 0, 0)),
                pl.BlockSpec((1, MOD_ROWS, D_MODEL), lambda bi, i: (layer, 0, 1)),
                pl.BlockSpec((1, 1, D_MODEL), lambda bi, i: (layer, 0, 0)),
                pl.BlockSpec((1, D_MODEL, PROJ_WIDTH), lambda bi, i: (layer, 0, 0))]
    args = [x, mod, mod, g_pre, w_in]
    if rope:
        in_specs += [pl.BlockSpec((tm, LANES), lambda bi, i: (i, 0))] * 3
        args += list(rope_tabs)
    widths = (ATTN_WIDTH, 2 * KV_WIDTH, 2 * KV_WIDTH, ATTN_WIDTH, FOURIER_WIDTH, FOURIER_WIDTH,
              CONV_WIDTH, CONV_WIDTH)
    if kv_only:
        widths = widths[1:3]
    out_specs = [row3(w) for w in widths]
    out_shape = [jax.ShapeDtypeStruct((b, n, w), BF16) for w in widths]
    scratch = []
    if rope:
        sub = min(tm, PROJ_SUB_ROWS)
        assert sub % (FFT_N2 * SUBLANES) == 0
        out_specs[4] = pl.BlockSpec((1, FFT_N2, tm // FFT_N2, FOURIER_WIDTH), lambda bi, i: (bi, 0, i, 0))
        out_shape[4] = jax.ShapeDtypeStruct((b, FFT_N2, n // FFT_N2, FOURIER_WIDTH), F32)
        out_specs[2] = pl.BlockSpec((1, 2 * KV_WIDTH, tm), lambda bi, i: (bi, 0, i))
        out_shape[2] = jax.ShapeDtypeStruct((b, 2 * KV_WIDTH, n), BF16)
        scratch = [pltpu.VMEM((tm // sub, FOURIER_WIDTH // LANES, sub // FFT_N2 * UF_PITCH, LANES), F32)]
    return pl.pallas_call(
        functools.partial(_inproj_kernel, rope=rope, ctx_row=ctx_row, kv_only=kv_only, tm=tm),
        grid=(b, n // tm),
        in_specs=in_specs,
        out_specs=out_specs,
        out_shape=out_shape,
        scratch_shapes=scratch,
        compiler_params=_cparams("arbitrary", "arbitrary"),
        name="in_projection_rope" if rope else ("in_projection_ctx_kv" if kv_only else "in_projection_ctx"),
    )(*args)


def _lane_half_variants(blk, fill):
    straight, swapped = blk[:, :KV_WIDTH], blk[:, KV_WIDTH:]
    lo = lax.broadcasted_iota(jnp.int32, straight.shape, 1) < HEAD_DIM
    other = jnp.full_like(straight, fill)
    return (jnp.where(lo, straight, other), jnp.where(lo, other, swapped),
            jnp.where(lo, swapped, other), jnp.where(lo, other, straight))


def _row_half_variants(blk_t, fill):
    straight, swapped = blk_t[:KV_WIDTH], blk_t[KV_WIDTH:]
    lo = lax.broadcasted_iota(jnp.int32, straight.shape, 0) < HEAD_DIM
    other = jnp.full_like(straight, fill)
    return (jnp.where(lo, straight, other), jnp.where(lo, other, swapped),
            jnp.where(lo, swapped, other), jnp.where(lo, other, straight))


def _merge_head_pair(pv_lo, pv_hi, sink_lo, sink_hi):
    lane_lo = lax.broadcasted_iota(jnp.int32, pv_lo.shape, 1) < HEAD_DIM
    num = jnp.where(lane_lo, pv_lo, pv_hi)
    den = pltpu.roll(jnp.where(lane_lo, pv_hi, pv_lo), HEAD_DIM, 1) + jnp.where(lane_lo, sink_lo, sink_hi)
    return num / den


def _conv_rows(t, above, below, bg, cw, cb):
    n_rows = t.shape[0]
    ridx = lax.broadcasted_iota(jnp.int32, t.shape, 0)
    up = jnp.where(ridx == 0, above, pltpu.roll(t, 1, 0))
    dn = jnp.where(ridx == n_rows - 1, below, pltpu.roll(t, n_rows - 1, 0))
    conv = up * cw[0:1] + t * cw[1:2] + dn * cw[2:3] + cb
    return (conv * bg.astype(F32)).astype(BF16)


def _mix_rows(a, f, s, w, g_post, gate, x):
    h = jnp.concatenate([a, f, s], axis=1)
    y = _dot(h, w)
    r = lax.rsqrt(jnp.mean(y * y, axis=-1, keepdims=True) + NORM_EPS)
    return x + (y * r) * (gate * g_post)


def _attn_kernel(sink_ref, q_ref, g_ref, kp_ref, km_ref, kn_ref, vp_ref, vm_ref, vn_ref, kc_ref, vc_ref,
                 x_ref, f_ref, t_ref, tp_ref, tn_ref, bg_ref, cw_ref, cb_ref, w_ref, gpost_ref, mgate_ref,
                 o_ref, k_s, v_s, kctx_s, vctx_s, bias_s, s_s, p_s, r_s, pv_s, a_s, c_s, *, tq, n_seq, n_ctx):
    i = pl.program_id(1)
    n_blocks = tq // BLOCK_Q
    for off, kref, vref, rows in ((0, kp_ref, vp_ref, WINDOW), (WINDOW, km_ref, vm_ref, tq),
                                  (WINDOW + tq, kn_ref, vn_ref, WINDOW)):
        for idx, kk in enumerate(_lane_half_variants(kref[0], 0.0)):
            k_s[idx, off:off + rows] = kk
        for idx, vv in enumerate(_row_half_variants(vref[0], 1.0)):
            v_s[idx, :, off:off + rows] = vv
    for idx, kk in enumerate(_lane_half_variants(kc_ref[0], 0.0)):
        kctx_s[idx] = kk
    for idx, vv in enumerate(_row_half_variants(vc_ref[0].astype(F32).T.astype(BF16), 1.0)):
        vctx_s[idx] = vv

    jj = lax.broadcasted_iota(jnp.int32, (WINDOW, BLOCK_Q), 0)
    ii = lax.broadcasted_iota(jnp.int32, (WINDOW, BLOCK_Q), 1)
    head_band = jnp.where(jj >= ii, 0.0, -jnp.inf)
    tail_band = jnp.where(jj <= ii, 0.0, -jnp.inf)
    bias_s[0] = head_band
    bias_s[1] = tail_band
    bias_s[2] = jnp.where(i == 0, -jnp.inf, head_band)
    bias_s[3] = jnp.where(i == pl.num_programs(1) - 1, -jnp.inf, tail_band)

    head_w = n_ctx + KEY_SPAN


    def scores(sb, units):
        par, r0 = sb % BLOCK_BUFFERS, sb * BLOCK_Q
        q = q_ref[0, r0:r0 + BLOCK_Q, :]
        for kv, half in units:
            pairs = range(kv * PAIRS_PER_KV, (kv + 1) * PAIRS_PER_KV)
            qg = jnp.concatenate([q[:, p * LANES:(p + 1) * LANES] for p in pairs], axis=0)
            keys = jnp.concatenate([kctx_s[2 * kv + half], k_s[2 * kv + half, r0:r0 + KEY_SPAN]], axis=0)
            s_s[par, kv, half * head_w:(half + 1) * head_w, :] = _dot_nt(keys, qg)

    def softmax(sb, units):
        par = sb % BLOCK_BUFFERS
        head_bias = 2 if sb == 0 else 0
        tail_bias = 3 if sb == n_blocks - 1 else 1
        n_blk = head_w // LANES
        first_local = n_ctx // LANES
        for kv, half in units:
            for j in range(PAIRS_PER_KV):
                p, lanes = kv * PAIRS_PER_KV + j, slice(j * BLOCK_Q, (j + 1) * BLOCK_Q)
                h = 2 * p + half
                sink2 = sink_ref[h] * LOG2E

                def key_block(t):
                    rows = slice(half * head_w + t * LANES, half * head_w + (t + 1) * LANES)
                    blk = s_s[par, kv, rows, lanes]
                    if t == first_local:
                        blk = blk + bias_s[head_bias]
                    elif t == n_blk - 1:
                        blk = blk + bias_s[tail_bias]
                    return rows, blk

                mx = functools.reduce(jnp.maximum, [key_block(t)[1] for t in range(n_blk)])
                m = jnp.maximum(jnp.max(mx, axis=0, keepdims=True), sink2)
                for t in range(n_blk):
                    rows, blk = key_block(t)
                    p_s[par, kv, rows, lanes] = jnp.exp2(blk - m).astype(BF16)
                r_s[par, h] = jnp.broadcast_to(jnp.exp2(sink2 - m), (SUBLANES, BLOCK_Q))

    def weighted_values(sb, units):
        par, r0 = sb % BLOCK_BUFFERS, sb * BLOCK_Q
        for kv, half in units:
            vals_t = jnp.concatenate([vctx_s[2 * kv + half], v_s[2 * kv + half, :, r0:r0 + KEY_SPAN]], axis=1)
            pv_s[kv, half] = _dot(vals_t, p_s[par, kv, half * head_w:(half + 1) * head_w, :])
            if half == 0:
                continue
            for j in range(PAIRS_PER_KV):
                p = kv * PAIRS_PER_KV + j
                lanes = slice(j * BLOCK_Q, (j + 1) * BLOCK_Q)
                lo, hi = pv_s[kv, 0, :, lanes], pv_s[kv, 1, :, lanes]
                num = jnp.concatenate([lo[:HEAD_DIM], hi[HEAD_DIM:]], axis=0)
                den = jnp.concatenate([lo[HEAD_DIM:], hi[:HEAD_DIM]], axis=0)
                sink_t = jnp.concatenate([jnp.broadcast_to(r_s[par, 2 * p + half, 0:1], (HEAD_DIM, BLOCK_Q))
                                          for half in range(2)], axis=0)
                out = (num / (den + sink_t)).T
                gate = g_ref[0, r0:r0 + BLOCK_Q, p * LANES:(p + 1) * LANES].astype(F32)
                a_s[r0:r0 + BLOCK_Q, p * LANES:(p + 1) * LANES] = (out * gate).astype(BF16)

    def conv_edge(ref, row, keep):
        return jnp.where(keep, ref[0].astype(F32)[row:row + 1, :], 0.0)

    for r0 in range(0, tq, MIX_ROWS):
        rows = slice(r0, r0 + MIX_ROWS)
        if r0 == 0:
            above = conv_edge(tp_ref, BF16_ROWS - 1, i > 0)
        else:
            above = t_ref[0, r0 - BF16_ROWS:r0, :].astype(F32)[BF16_ROWS - 1:, :]
        if r0 + MIX_ROWS == tq:
            below = conv_edge(tn_ref, 0, i < pl.num_programs(1) - 1)
        else:
            below = t_ref[0, r0 + MIX_ROWS:r0 + MIX_ROWS + BF16_ROWS, :].astype(F32)[:1, :]
        c_s[rows] = _conv_rows(t_ref[0, rows].astype(F32), above, below, bg_ref[0, rows], cw_ref[0],
                               cb_ref[0])

    def mix(r0):
        rows = slice(r0, r0 + MIX_ROWS)
        o_ref[0, rows] = _mix_rows(a_s[rows], f_ref[0, rows], c_s[rows], w_ref[0], gpost_ref[0],
                                   mgate_ref[0, pl.ds(pl.program_id(0), 1), :], x_ref[0, rows])

    per_mix = MIX_ROWS // BLOCK_Q
    units = [(kv, half) for kv in range(KV_HEADS) for half in range(2)]
    scores(0, units)
    for sb in range(n_blocks):
        for unit in units:
            if sb + 1 < n_blocks:
                scores(sb + 1, (unit,))
            softmax(sb, (unit,))
            weighted_values(sb, (unit,))
        if (sb + 1) % per_mix == 0:
            mix((sb + 1 - per_mix) * BLOCK_Q)


def _attention_and_mix(x, q, gate, k, v, kc, vc, sink, f, t, bg, conv_w, conv_b, w_out, g_post, mod,
                       layer, *, tq):
    b, n, _ = q.shape
    nc = kc.shape[1]
    per = tq // WINDOW
    last = n // WINDOW - 1
    kvw = 2 * KV_WIDTH
    variants = 2 * KV_HEADS
    main = lambda width: pl.BlockSpec((1, tq, width), lambda bi, i: (bi, i, 0))
    prev = pl.BlockSpec((1, WINDOW, kvw), lambda bi, i: (bi, jnp.maximum(i * per - 1, 0), 0))
    nxt = pl.BlockSpec((1, WINDOW, kvw), lambda bi, i: (bi, jnp.minimum((i + 1) * per, last), 0))
    ctx = pl.BlockSpec((1, nc, kvw), lambda bi, i: (bi, 0, 0))
    lay3 = lambda shape: pl.BlockSpec((1,) + shape, lambda bi, i: (layer, 0, 0))
    t_per = tq // BF16_ROWS
    t_last = n // BF16_ROWS - 1
    t_prev = pl.BlockSpec((1, BF16_ROWS, CONV_WIDTH), lambda bi, i: (bi, jnp.maximum(i * t_per - 1, 0), 0))
    t_next = pl.BlockSpec((1, BF16_ROWS, CONV_WIDTH),
                          lambda bi, i: (bi, jnp.minimum((i + 1) * t_per, t_last), 0))
    v_prev = pl.BlockSpec((1, kvw, WINDOW), lambda bi, i: (bi, 0, jnp.maximum(i * per - 1, 0)))
    v_main = pl.BlockSpec((1, kvw, tq), lambda bi, i: (bi, 0, i))
    v_next = pl.BlockSpec((1, kvw, WINDOW), lambda bi, i: (bi, 0, jnp.minimum((i + 1) * per, last)))
    span = tq + 2 * WINDOW
    s_rows = 2 * (nc + KEY_SPAN)
    group_q = PAIRS_PER_KV * BLOCK_Q
    return pl.pallas_call(
        functools.partial(_attn_kernel, tq=tq, n_seq=n, n_ctx=nc),
        grid=(b, n // tq),
        in_specs=[pl.BlockSpec(memory_space=pltpu.SMEM), main(ATTN_WIDTH), main(ATTN_WIDTH),
                  prev, main(kvw), nxt, v_prev, v_main, v_next, ctx, ctx,
                  main(D_MODEL),
                  main(FOURIER_WIDTH), main(CONV_WIDTH), t_prev, t_next, main(CONV_WIDTH),
                  lay3((3, CONV_WIDTH)), lay3((1, CONV_WIDTH)), lay3((MIX_WIDTH, D_MODEL)),
                  lay3((1, D_MODEL)),
                  pl.BlockSpec((1, MOD_ROWS, D_MODEL), lambda bi, i: (layer, 0, 2))],
        out_specs=main(D_MODEL),
        out_shape=jax.ShapeDtypeStruct(x.shape, F32),
        scratch_shapes=[pltpu.VMEM((variants, span, KV_WIDTH), BF16),
                        pltpu.VMEM((variants, KV_WIDTH, span), BF16),
                        pltpu.VMEM((variants, nc, KV_WIDTH), BF16),
                        pltpu.VMEM((variants, KV_WIDTH, nc), BF16),
                        pltpu.VMEM((4, WINDOW, BLOCK_Q), F32),
                        pltpu.VMEM((BLOCK_BUFFERS, KV_HEADS, s_rows, group_q), F32),
                        pltpu.VMEM((BLOCK_BUFFERS, KV_HEADS, s_rows, group_q), BF16),
                        pltpu.VMEM((BLOCK_BUFFERS, ATTN_HEADS, SUBLANES, BLOCK_Q), F32),
                        pltpu.VMEM((KV_HEADS, 2, KV_WIDTH, group_q), F32),
                        pltpu.VMEM((tq, ATTN_WIDTH), BF16),
                        pltpu.VMEM((tq, CONV_WIDTH), BF16)],
        compiler_params=_cparams("arbitrary", "arbitrary"),
        name="attention_and_mix",
    )(sink, q, gate, k, k, k, v, v, v, kc, vc, x, f, t, t, t, bg, conv_w, conv_b, w_out, g_post, mod)


def _ctx_attn_kernel(sink_ref, q_ref, g_ref, kc_ref, vc_ref, o_ref):
    keys = _lane_half_variants(kc_ref[0], 0.0)
    vals = _lane_half_variants(vc_ref[0], 1.0)
    for p in range(HEAD_PAIRS):
        kv = p // PAIRS_PER_KV
        qp = q_ref[0, :, p * LANES:(p + 1) * LANES]
        pv, sink_terms = [], []
        for half in range(2):
            s = _dot_nt(qp, keys[2 * kv + half])
            sink2 = sink_ref[2 * p + half] * LOG2E
            m = jnp.maximum(jnp.max(s, axis=-1, keepdims=True), sink2)
            pv.append(_dot(jnp.exp2(s - m).astype(BF16), vals[2 * kv + half]))
            sink_terms.append(jnp.exp2(sink2 - m))
        out = _merge_head_pair(pv[0], pv[1], sink_terms[0], sink_terms[1])
        gate = g_ref[0, :, p * LANES:(p + 1) * LANES].astype(F32)
        o_ref[0, :, p * LANES:(p + 1) * LANES] = (out * gate).astype(BF16)


def _context_attention(q, gate, kc, vc, sink):
    b, nc, _ = q.shape
    blk = lambda width: pl.BlockSpec((1, nc, width), lambda bi: (bi, 0, 0))
    return pl.pallas_call(
        _ctx_attn_kernel,
        grid=(b,),
        in_specs=[pl.BlockSpec(memory_space=pltpu.SMEM), blk(ATTN_WIDTH), blk(ATTN_WIDTH),
                  blk(2 * KV_WIDTH), blk(2 * KV_WIDTH)],
        out_specs=blk(ATTN_WIDTH),
        out_shape=jax.ShapeDtypeStruct((b, nc, ATTN_WIDTH), BF16),
        compiler_params=_cparams("arbitrary"),
        name="context_attention",
    )(sink, q, gate, kc, vc)


def _channel_mix_matrix(cc_ref, sc_ref, wf_ref, scale):
    wf = wf_ref[0]
    return (jnp.concatenate([_dot(cc_ref[...], wf), -_dot(sc_ref[...], wf)], axis=1) * scale).astype(BF16)


def _fourier_kernel(uf_ref, gate_ref, cc_ref, sc_ref, wf_ref, g_ref, m2_ref, o_ref, mix_s, y_s, *, n_seq):
    n1_len = n_seq // FFT_N2
    chunk = pl.program_id(1)
    halves = FOURIER_WIDTH // LANES

    @pl.when(chunk == 0)
    def _():
        zero = jnp.zeros((LANES, LANES), F32)
        scale = (n_seq * FOURIER_GROUP_DIM) ** -0.5
        re = [_dot(cc_ref[...], wf_ref[h]) * scale for h in range(halves)]
        im = [_dot(sc_ref[...], wf_ref[h]) * -scale for h in range(halves)]
        for h in range(halves):
            row = [re[h] if j == h else zero for j in range(halves)]
            row += [im[h] if j == h else zero for j in range(halves)]
            mix_s[h * LANES:(h + 1) * LANES] = jnp.concatenate(row, axis=1).astype(BF16)

    zs = [_dot(uf_ref[0, u].astype(BF16), mix_s[...]) for u in range(N2_CHUNK)]
    for u, z in enumerate(zs):
        rhs = jnp.concatenate([z[:, :FOURIER_WIDTH], z[:, FOURIER_WIDTH:]], axis=0).astype(BF16)
        y = _dot(g_ref[chunk * N2_CHUNK + u], rhs)
        y_s[chunk * N2_CHUNK + u] = y.astype(BF16).reshape(n1_len // K1_GROUP, 2 * K1_GROUP, FOURIER_WIDTH)

    @pl.when(chunk == pl.num_programs(1) - 1)
    def _():
        per_store = BF16_ROWS // K1_GROUP

        def stage2(t, carry):
            outs = []
            for u in range(per_store):
                rhs = y_s[:, t * per_store + u].reshape(FFT_N2 * 2 * K1_GROUP, FOURIER_WIDTH)
                outs.append(_dot(m2_ref[...], rhs).reshape(FFT_N2, K1_GROUP, FOURIER_WIDTH))
            r0 = pl.multiple_of(t * BF16_ROWS, BF16_ROWS)
            gate = gate_ref[0, :, pl.ds(r0, BF16_ROWS), :].astype(F32)
            o_ref[0, :, pl.ds(r0, BF16_ROWS), :] = (jnp.concatenate(outs, axis=1) * gate).astype(BF16)
            return carry

        lax.fori_loop(0, n1_len // BF16_ROWS, stage2, 0)


def _fourier_mix(uf, gate, wf_half, consts):
    b, _, n1_len, _ = uf.shape
    n = FFT_N2 * n1_len
    cc, sc, g, m2 = consts
    full = lambda shape: pl.BlockSpec(shape, lambda bi, c: (0,) * len(shape))
    whole = pl.BlockSpec((1, FFT_N2, n1_len, FOURIER_WIDTH), lambda bi, c: (bi, 0, 0, 0))
    out = pl.pallas_call(
        functools.partial(_fourier_kernel, n_seq=n),
        grid=(b, FFT_N2 // N2_CHUNK),
        in_specs=[pl.BlockSpec((1, N2_CHUNK, n1_len, FOURIER_WIDTH), lambda bi, c: (bi, c, 0, 0)),
                  whole, full((LANES, LANES)), full((LANES, LANES)), full(wf_half.shape),
                  full(g.shape), full(m2.shape)],
        out_specs=whole,
        out_shape=jax.ShapeDtypeStruct((b, FFT_N2, n1_len, FOURIER_WIDTH), BF16),
        scratch_shapes=[pltpu.VMEM((FOURIER_WIDTH, 2 * FOURIER_WIDTH), BF16),
                        pltpu.VMEM((FFT_N2, n1_len // K1_GROUP, 2 * K1_GROUP, FOURIER_WIDTH), BF16)],
        compiler_params=_cparams("arbitrary", "arbitrary"),
        name="fourier_mix",
    )(uf, gate.reshape(b, FFT_N2, n1_len, FOURIER_WIDTH), cc, sc, wf_half, g, m2)
    return out.reshape(b, n, FOURIER_WIDTH)


def _ctx_fourier_kernel(uf_ref, gate_ref, cc_ref, sc_ref, wf_ref, dft_ref, o_ref, *, n_seq):
    mix = _channel_mix_matrix(cc_ref, sc_ref, wf_ref, (n_seq * FOURIER_GROUP_DIM) ** -0.5)
    z = _dot(uf_ref[0], mix)
    rhs = jnp.concatenate([z[:, :LANES], z[:, LANES:]], axis=0).astype(BF16)
    o_ref[0] = (_dot(dft_ref[...], rhs) * gate_ref[0].astype(F32)).astype(BF16)


def _ctx_fourier_mix(uf, gate, wf_half, cc, sc, dft):
    b, n, _ = uf.shape
    halves = FOURIER_WIDTH // LANES
    full = lambda shape: pl.BlockSpec(shape, lambda bi, hf: (0,) * len(shape))
    half = pl.BlockSpec((1, n, LANES), lambda bi, hf: (bi, 0, hf))
    return pl.pallas_call(
        functools.partial(_ctx_fourier_kernel, n_seq=n),
        grid=(b, halves),
        in_specs=[half, half, full((LANES, LANES)), full((LANES, LANES)),
                  pl.BlockSpec((1, LANES, LANES), lambda bi, hf: (hf, 0, 0)), full(dft.shape)],
        out_specs=half,
        out_shape=jax.ShapeDtypeStruct((b, n, FOURIER_WIDTH), BF16),
        compiler_params=_cparams("arbitrary", "arbitrary"),
        name="context_fourier_mix",
    )(uf, gate, cc, sc, wf_half, dft)


def _ctx_mix_kernel(x_ref, a_ref, f_ref, t_ref, bg_ref, cw_ref, cb_ref, w_ref, g_ref, gate_ref, o_ref,
                    *, ctx_row):
    edge = jnp.zeros((1, CONV_WIDTH), F32)
    conv = _conv_rows(t_ref[0].astype(F32), edge, edge, bg_ref[0], cw_ref[0], cb_ref[0])
    o_ref[0] = _mix_rows(a_ref[0], f_ref[0], conv, w_ref[0], g_ref[0], gate_ref[0, ctx_row:ctx_row + 1, :],
                         x_ref[0])


def _context_mix(x, a, f, t, bg, conv_w, conv_b, w_out, g_post, mod, layer, *, ctx_row):
    b, n, _ = x.shape
    row3 = lambda width: pl.BlockSpec((1, n, width), lambda bi: (bi, 0, 0))
    lay3 = lambda shape: pl.BlockSpec((1,) + shape, lambda bi: (layer, 0, 0))
    return pl.pallas_call(
        functools.partial(_ctx_mix_kernel, ctx_row=ctx_row),
        grid=(b,),
        in_specs=[row3(D_MODEL), row3(ATTN_WIDTH), row3(FOURIER_WIDTH), row3(CONV_WIDTH), row3(CONV_WIDTH), lay3((3, CONV_WIDTH)), lay3((1, CONV_WIDTH)),
                  lay3((MIX_WIDTH, D_MODEL)), lay3((1, D_MODEL)),
                  pl.BlockSpec((1, MOD_ROWS, D_MODEL), lambda bi: (layer, 0, 2))],
        out_specs=row3(D_MODEL),
        out_shape=jax.ShapeDtypeStruct(x.shape, F32),
        compiler_params=_cparams("arbitrary"),
        name="context_mix",
    )(x, a, f, t, bg, conv_w, conv_b, w_out, g_post, mod)


def kernel(x, c, ctx, c_ctx, w_mod, b_mod, g_pre, g_post, w_in, w_out, sink, w_fourier, conv_w, conv_b):
    depth = w_mod.shape[0]
    b, n, _ = x.shape
    nc = ctx.shape[1]
    assert b + 1 <= MOD_ROWS and n % GRID_W == 0 and n % ATTN_ROWS == 0 and n % TILE_ROWS == 0
    assert (n // FFT_N2) % BF16_ROWS == 0 and FFT_N2 % N2_CHUNK == 0

    w_in_b = w_in.astype(BF16)
    w_out_b = w_out.astype(BF16)

    rope_tabs = tuple(jnp.asarray(t) for t in _rope_tables(n))
    cc, sc = (jnp.asarray(m) for m in _channel_dft())
    stage1 = jnp.asarray(_stage1_mats(n)).astype(BF16)
    stage2 = jnp.asarray(_stage2_mat()).astype(BF16)
    ctx_dft = jnp.asarray(_dense_dft(nc)).astype(BF16)
    groups_per_half = LANES // FOURIER_GROUP_DIM
    wf_half = jnp.zeros((depth, FOURIER_GROUPS // groups_per_half, LANES, LANES), F32)
    for g in range(FOURIER_GROUPS):
        o = (g % groups_per_half) * FOURIER_GROUP_DIM
        wf_half = wf_half.at[:, g // groups_per_half, o:o + FOURIER_GROUP_DIM,
                             o:o + FOURIER_GROUP_DIM].set(w_fourier[:, g])

    c_rows = jnp.zeros((MOD_ROWS, D_MODEL), F32).at[:b].set(c).at[b].set(c_ctx)
    mod = _modulation(c_rows, w_mod, b_mod)
    g_pre3 = g_pre.reshape(depth, 1, D_MODEL)
    g_post3 = g_post.reshape(depth, 1, D_MODEL)
    conv_b3 = conv_b.reshape(depth, 1, CONV_WIDTH)

    for l in range(depth):
        update_ctx = l < depth - 1
        ctx_rows = ctx.reshape(1, b * nc, D_MODEL)
        ctx_proj = _in_projection(ctx_rows, mod, g_pre3, w_in_b, l, rope_tabs=None, ctx_row=b,
                                  tm=b * nc, kv_only=not update_ctx)
        ctx_proj = [y.reshape(b, nc, y.shape[-1]) for y in ctx_proj]
        if update_ctx:
            qc, kc, vc, sgac, ufc, sgfc, tc, bgc = ctx_proj
        else:
            kc, vc = ctx_proj
        q, k, v, sga, uf, sgf, t, bg = _in_projection(
            x, mod, g_pre3, w_in_b, l, rope_tabs=rope_tabs, ctx_row=None, tm=TILE_ROWS)
        f = _fourier_mix(uf, sgf, wf_half[l], (cc, sc, stage1, stage2))
        x = _attention_and_mix(x, q, sga, k, v, kc, vc, sink[l], f, t, bg, conv_w, conv_b3, w_out_b,
                               g_post3, mod, l, tq=ATTN_ROWS)
        if update_ctx:
            ac = _context_attention(qc, sgac, kc, vc, sink[l])
            fc = _ctx_fourier_mix(ufc, sgfc, wf_half[l], cc, sc, ctx_dft)
            ctx = _context_mix(ctx, ac, fc, tc, bgc, conv_w, conv_b3, w_out_b, g_post3, mod, l,
                               ctx_row=b)
    return x
```

```python
import functools
import math

import numpy as np
import jax
import jax.numpy as jnp
from jax import lax
from jax.experimental import pallas as pl
from jax.experimental.pallas import tpu as pltpu

D_MODEL = 1024
GRID_W = 64
HEAD_DIM = 64
ATTN_HEADS = 8
KV_HEADS = 2
ATTN_WIDTH = ATTN_HEADS * HEAD_DIM
KV_WIDTH = KV_HEADS * HEAD_DIM
WINDOW = 128
FOURIER_GROUPS = 4
FOURIER_GROUP_DIM = 64
FOURIER_WIDTH = FOURIER_GROUPS * FOURIER_GROUP_DIM
CONV_WIDTH = 256
MIX_WIDTH = ATTN_WIDTH + FOURIER_WIDTH + CONV_WIDTH
PROJ_WIDTH = 2 * ATTN_WIDTH + 2 * KV_WIDTH + 2 * FOURIER_WIDTH + 4 * CONV_WIDTH
ROPE_FREQS = HEAD_DIM // 4
ROPE_BASE = 10000.0
NORM_EPS = 1e-6
LOG2E = math.log2(math.e)

C_Q = 0
C_K = C_Q + ATTN_WIDTH
C_V = C_K + KV_WIDTH
C_GA = C_V + KV_WIDTH
C_UF = C_GA + ATTN_WIDTH
C_GF = C_UF + FOURIER_WIDTH
C_ZC = C_GF + FOURIER_WIDTH
C_BC = C_ZC + CONV_WIDTH
C_CC = C_BC + CONV_WIDTH
C_GC = C_CC + CONV_WIDTH

LANES = 128
SUBLANES = 8
BF16_ROWS = 16
VMEM_LIMIT = 52 * 1024 * 1024

MOD_ROWS = 8
BF16 = jnp.bfloat16
F32 = jnp.float32

FFT_N2 = 64
K1_GROUP = SUBLANES
N2_CHUNK = 32
UF_PITCH = FFT_N2 + SUBLANES

BLOCK_Q = 128
KEY_SPAN = BLOCK_Q + 2 * WINDOW
HEAD_PAIRS = ATTN_WIDTH // LANES
PAIRS_PER_KV = HEAD_PAIRS // KV_HEADS

TILE_ROWS = 1024
PROJ_SUB_ROWS = 512
ATTN_ROWS = 1024
BLOCK_BUFFERS = 2
MIX_ROWS = 512


def _silu(x):
    return x / (1.0 + jnp.exp(-x))


def _dot(a, b):
    return jnp.dot(a, b, preferred_element_type=F32)


def _dot_nt(a, b):
    return lax.dot_general(a, b, (((1,), (1,)), ((), ())), preferred_element_type=F32)


def _cparams(*sem):
    return pltpu.CompilerParams(dimension_semantics=sem, vmem_limit_bytes=VMEM_LIMIT)


def _rope_tables(n):
    t = np.arange(n)
    row = (t // GRID_W).astype(np.float64)
    col = (t % GRID_W).astype(np.float64)
    inv = ROPE_BASE ** (-np.arange(ROPE_FREQS, dtype=np.float64) / ROPE_FREQS)
    ar = row[:, None] * inv
    ac = col[:, None] * inv
    z = np.zeros_like(ar)
    cos_h = np.concatenate([np.cos(ar), np.cos(ar), np.cos(ac), np.cos(ac)], axis=1)
    sin_up = np.concatenate([-np.sin(ar), z, -np.sin(ac), z], axis=1)
    sin_dn = np.concatenate([z, np.sin(ar), z, np.sin(ac)], axis=1)
    rep = LANES // HEAD_DIM
    return tuple(np.tile(a, (1, rep)).astype(np.float32) for a in (cos_h, sin_up, sin_dn))


def _channel_dft():
    c = np.arange(FOURIER_GROUP_DIM)
    ang = 2.0 * np.pi * np.outer(c, c) / FOURIER_GROUP_DIM
    eye = np.eye(LANES // FOURIER_GROUP_DIM)
    return (np.kron(eye, np.cos(ang)).astype(np.float32),
            np.kron(eye, np.sin(ang)).astype(np.float32))


def _stage1_mats(n):
    n1_len = n // FFT_N2
    k1 = np.arange(n1_len)[:, None]
    n1 = np.arange(n1_len)[None, :]
    out = np.empty((FFT_N2, 2 * n1_len, 2 * n1_len), np.float32)
    for n2 in range(FFT_N2):
        ang = 2.0 * np.pi * ((k1 * (FFT_N2 * n1 + n2)) % n) / n
        ce, se = np.cos(ang), np.sin(ang)
        by_part = np.block([[ce, se], [-se, ce]]).reshape(2, n1_len // K1_GROUP, K1_GROUP, 2 * n1_len)
        out[n2] = by_part.transpose(1, 0, 2, 3).reshape(2 * n1_len, 2 * n1_len)
    return out


def _stage2_mat():
    k2 = np.arange(FFT_N2)
    ang = 2.0 * np.pi * np.outer(k2, k2) / FFT_N2
    cs = np.stack([np.cos(ang), np.sin(ang)], axis=-1)
    eye = np.eye(K1_GROUP)
    m = np.einsum('knp,rs->krnps', cs, eye)
    return m.reshape(FFT_N2 * K1_GROUP, FFT_N2 * 2 * K1_GROUP).astype(np.float32)


def _dense_dft(n):
    t = np.arange(n)
    ang = 2.0 * np.pi * (np.outer(t, t) % n) / n
    return np.concatenate([np.cos(ang), np.sin(ang)], axis=1).astype(np.float32)


def _mod_kernel(c_ref, w_ref, b_ref, o_ref):
    o_ref[0] = _dot(_silu(c_ref[...]), w_ref[0]) + b_ref[0]


def _modulation(c_rows, w_mod, b_mod):
    depth = w_mod.shape[0]
    return pl.pallas_call(
        _mod_kernel,
        grid=(depth, 3),
        in_specs=[pl.BlockSpec((MOD_ROWS, D_MODEL), lambda l, j: (0, 0)),
                  pl.BlockSpec((1, D_MODEL, D_MODEL), lambda l, j: (l, 0, j)),
                  pl.BlockSpec((1, 1, D_MODEL), lambda l, j: (l, 0, j))],
        out_specs=pl.BlockSpec((1, MOD_ROWS, D_MODEL), lambda l, j: (l, 0, j)),
        out_shape=jax.ShapeDtypeStruct((depth, MOD_ROWS, 3 * D_MODEL), F32),
        compiler_params=_cparams("arbitrary", "arbitrary"),
        name="modulation",
    )(c_rows, w_mod, b_mod.reshape(depth, 1, 3 * D_MODEL))


def _inproj_kernel(*refs, rope, ctx_row, kv_only, tm):
    x_ref, shift_ref, scale_ref, g_ref, w_ref = refs[:5]
    cos_ref, sup_ref, sdn_ref = refs[5:8] if rope else (None,) * 3
    outs = refs[8:-1] if rope else refs[5:]
    uf_s = refs[-1] if rope else None
    row = pl.program_id(0) if ctx_row is None else ctx_row
    shift = shift_ref[0, pl.ds(row, 1), :]
    gain = g_ref[0] * (1.0 + scale_ref[0, pl.ds(row, 1), :])

    def with_swapped_heads(y):
        return jnp.concatenate([y, pltpu.roll(y, HEAD_DIM, 1)], axis=1).astype(BF16)

    sub = min(tm, PROJ_SUB_ROWS)
    for s in range(tm // sub):
        rows = slice(s * sub, (s + 1) * sub)
        x = x_ref[0, rows]
        r = lax.rsqrt(jnp.mean(x * x, axis=-1, keepdims=True) + NORM_EPS)
        h = ((x * r) * gain + shift).astype(BF16)

        def proj(c0, width):
            return _dot(h, w_ref[0, :, c0:c0 + width])

        def rotate(y):
            if not rope:
                return y
            return (y * cos_ref[rows] + pltpu.roll(y, LANES - ROPE_FREQS, 1) * sup_ref[rows]
                    + pltpu.roll(y, ROPE_FREQS, 1) * sdn_ref[rows])

        kv = proj(C_K, 2 * KV_WIDTH)
        k_out = with_swapped_heads(rotate(kv[:, :KV_WIDTH]))
        v_out = with_swapped_heads(kv[:, KV_WIDTH:])
        if kv_only:
            k_ref, v_ref = outs
            k_ref[0, rows] = k_out
            v_ref[0, rows] = v_out
            continue
        q_ref, k_ref, v_ref, sga_ref, uf_ref, sgf_ref, t_ref, bg_ref = outs
        k_ref[0, rows] = k_out
        if rope:
            v_both = jnp.concatenate([kv[:, KV_WIDTH:], pltpu.roll(kv[:, KV_WIDTH:], HEAD_DIM, 1)], axis=1)
            v_ref[0, :, rows] = v_both.T.astype(BF16)
        else:
            v_ref[0, rows] = v_out
        q = proj(C_Q, ATTN_WIDTH)
        for p in range(HEAD_PAIRS):
            sl = slice(p * LANES, (p + 1) * LANES)
            q_ref[0, rows, sl] = (rotate(q[:, sl]) * (HEAD_DIM ** -0.5 * LOG2E)).astype(BF16)
        sga_ref[0, rows] = _silu(proj(C_GA, ATTN_WIDTH)).astype(BF16)
        uf = proj(C_UF, FOURIER_WIDTH)
        if rope:
            per_sub = sub // FFT_N2
            for slab in range(FOURIER_WIDTH // LANES):
                for j in range(per_sub):
                    uf_s[s, slab, j * UF_PITCH:j * UF_PITCH + FFT_N2] = (
                        uf[j * FFT_N2:(j + 1) * FFT_N2, slab * LANES:(slab + 1) * LANES])
            for n2 in range(FFT_N2):
                uf_ref[0, n2, s * per_sub:(s + 1) * per_sub, :] = jnp.concatenate(
                    [uf_s[s, slab, pl.ds(n2, per_sub, stride=UF_PITCH), :]
                     for slab in range(FOURIER_WIDTH // LANES)], axis=1)
        else:
            uf_ref[0, rows] = uf.astype(BF16)
        sgf_ref[0, rows] = _silu(proj(C_GF, FOURIER_WIDTH)).astype(BF16)
        t_ref[0, rows] = (proj(C_CC, CONV_WIDTH) * proj(C_ZC, CONV_WIDTH)).astype(BF16)
        bg_ref[0, rows] = (proj(C_BC, CONV_WIDTH) * _silu(proj(C_GC, CONV_WIDTH))).astype(BF16)


def _in_projection(x, mod, g_pre, w_in, layer, *, rope_tabs, ctx_row, tm, kv_only=False):
    b, n, _ = x.shape
    rope = rope_tabs is not None
    row3 = lambda width: pl.BlockSpec((1, tm, width), lambda bi, i: (bi, i, 0))
    in_specs = [row3(D_MODEL),
                pl.BlockSpec((1, MOD_ROWS, D_MODEL), lambda bi, i: (layer, 0, 0)),
                pl.BlockSpec((1, MOD_ROWS, D_MODEL), lambda bi, i: (layer, 0, 1)),
                pl.BlockSpec((1, 1, D_MODEL), lambda bi, i: (layer, 0, 0)),
                pl.BlockSpec((1, D_MODEL, PROJ_WIDTH), lambda bi, i: (layer, 0, 0))]
    args = [x, mod, mod, g_pre, w_in]
    if rope:
        in_specs += [pl.BlockSpec((tm, LANES), lambda bi, i: (i, 0))] * 3
        args += list(rope_tabs)
    widths = (ATTN_WIDTH, 2 * KV_WIDTH, 2 * KV_WIDTH, ATTN_WIDTH, FOURIER_WIDTH, FOURIER_WIDTH,
              CONV_WIDTH, CONV_WIDTH)
    if kv_only:
        widths = widths[1:3]
    out_specs = [row3(w) for w in widths]
    out_shape = [jax.ShapeDtypeStruct((b, n, w), BF16) for w in widths]
    scratch = []
    if rope:
        sub = min(tm, PROJ_SUB_ROWS)
        assert sub % (FFT_N2 * SUBLANES) == 0
        out_specs[4] = pl.BlockSpec((1, FFT_N2, tm // FFT_N2, FOURIER_WIDTH), lambda bi, i: (bi, 0, i, 0))
        out_shape[4] = jax.ShapeDtypeStruct((b, FFT_N2, n // FFT_N2, FOURIER_WIDTH), F32)
        out_specs[2] = pl.BlockSpec((1, 2 * KV_WIDTH, tm), lambda bi, i: (bi, 0, i))
        out_shape[2] = jax.ShapeDtypeStruct((b, 2 * KV_WIDTH, n), BF16)
        scratch = [pltpu.VMEM((tm // sub, FOURIER_WIDTH // LANES, sub // FFT_N2 * UF_PITCH, LANES), F32)]
    return pl.pallas_call(
        functools.partial(_inproj_kernel, rope=rope, ctx_row=ctx_row, kv_only=kv_only, tm=tm),
        grid=(b, n // tm),
        in_specs=in_specs,
        out_specs=out_specs,
        out_shape=out_shape,
        scratch_shapes=scratch,
        compiler_params=_cparams("arbitrary", "arbitrary"),
        name="in_projection_rope" if rope else ("in_projection_ctx_kv" if kv_only else "in_projection_ctx"),
    )(*args)


def _lane_half_variants(blk, fill):
    straight, swapped = blk[:, :KV_WIDTH], blk[:, KV_WIDTH:]
    lo = lax.broadcasted_iota(jnp.int32, straight.shape, 1) < HEAD_DIM
    other = jnp.full_like(straight, fill)
    return (jnp.where(lo, straight, other), jnp.where(lo, other, swapped),
            jnp.where(lo, swapped, other), jnp.where(lo, other, straight))


def _row_half_variants(blk_t, fill):
    straight, swapped = blk_t[:KV_WIDTH], blk_t[KV_WIDTH:]
    lo = lax.broadcasted_iota(jnp.int32, straight.shape, 0) < HEAD_DIM
    other = jnp.full_like(straight, fill)
    return (jnp.where(lo, straight, other), jnp.where(lo, other, swapped),
            jnp.where(lo, swapped, other), jnp.where(lo, other, straight))


def _merge_head_pair(pv_lo, pv_hi, sink_lo, sink_hi):
    lane_lo = lax.broadcasted_iota(jnp.int32, pv_lo.shape, 1) < HEAD_DIM
    num = jnp.where(lane_lo, pv_lo, pv_hi)
    den = pltpu.roll(jnp.where(lane_lo, pv_hi, pv_lo), HEAD_DIM, 1) + jnp.where(lane_lo, sink_lo, sink_hi)
    return num / den


def _conv_rows(t, above, below, bg, cw, cb):
    n_rows = t.shape[0]
    ridx = lax.broadcasted_iota(jnp.int32, t.shape, 0)
    up = jnp.where(ridx == 0, above, pltpu.roll(t, 1, 0))
    dn = jnp.where(ridx == n_rows - 1, below, pltpu.roll(t, n_rows - 1, 0))
    conv = up * cw[0:1] + t * cw[1:2] + dn * cw[2:3] + cb
    return (conv * bg.astype(F32)).astype(BF16)


def _mix_rows(a, f, s, w, g_post, gate, x):
    h = jnp.concatenate([a, f, s], axis=1)
    y = _dot(h, w)
    r = lax.rsqrt(jnp.mean(y * y, axis=-1, keepdims=True) + NORM_EPS)
    return x + (y * r) * (gate * g_post)


def _attn_kernel(sink_ref, q_ref, g_ref, kp_ref, km_ref, kn_ref, vp_ref, vm_ref, vn_ref, kc_ref, vc_ref,
                 x_ref, f_ref, t_ref, tp_ref, tn_ref, bg_ref, cw_ref, cb_ref, w_ref, gpost_ref, mgate_ref,
                 o_ref, k_s, v_s, kctx_s, vctx_s, bias_s, s_s, p_s, r_s, a_s, c_s, *, tq, n_seq, n_ctx):
    i = pl.program_id(1)
    n_blocks = tq // BLOCK_Q
    for off, kref, vref, rows in ((0, kp_ref, vp_ref, WINDOW), (WINDOW, km_ref, vm_ref, tq),
                                  (WINDOW + tq, kn_ref, vn_ref, WINDOW)):
        for idx, kk in enumerate(_lane_half_variants(kref[0], 0.0)):
            k_s[idx, off:off + rows] = kk
        for idx, vv in enumerate(_row_half_variants(vref[0], 1.0)):
            v_s[idx, :, off:off + rows] = vv
    for idx, kk in enumerate(_lane_half_variants(kc_ref[0], 0.0)):
        kctx_s[idx] = kk
    for idx, vv in enumerate(_row_half_variants(vc_ref[0].astype(F32).T.astype(BF16), 1.0)):
        vctx_s[idx] = vv

    jj = lax.broadcasted_iota(jnp.int32, (WINDOW, BLOCK_Q), 0)
    ii = lax.broadcasted_iota(jnp.int32, (WINDOW, BLOCK_Q), 1)
    head_band = jnp.where(jj >= ii, 0.0, -jnp.inf)
    tail_band = jnp.where(jj <= ii, 0.0, -jnp.inf)
    bias_s[0] = head_band
    bias_s[1] = tail_band
    bias_s[2] = jnp.where(i == 0, -jnp.inf, head_band)
    bias_s[3] = jnp.where(i == pl.num_programs(1) - 1, -jnp.inf, tail_band)

    head_w = n_ctx + KEY_SPAN

    def scores(sb, kvs):
        par, r0 = sb % BLOCK_BUFFERS, sb * BLOCK_Q
        q = q_ref[0, r0:r0 + BLOCK_Q, :]
        for kv in kvs:
            pairs = range(kv * PAIRS_PER_KV, (kv + 1) * PAIRS_PER_KV)
            qg = jnp.concatenate([q[:, p * LANES:(p + 1) * LANES] for p in pairs], axis=0)
            keys = jnp.concatenate([kctx_s[2 * kv], k_s[2 * kv, r0:r0 + KEY_SPAN],
                                    kctx_s[2 * kv + 1], k_s[2 * kv + 1, r0:r0 + KEY_SPAN]], axis=0)
            s_s[par, kv] = _dot_nt(keys, qg)

    def softmax(sb, kvs):
        par = sb % BLOCK_BUFFERS
        head_bias = 2 if sb == 0 else 0
        tail_bias = 3 if sb == n_blocks - 1 else 1
        n_blk = head_w // LANES
        first_local = n_ctx // LANES
        for p in [p for p in range(HEAD_PAIRS) if p // PAIRS_PER_KV in kvs]:
            kv, lanes = p // PAIRS_PER_KV, slice((p % PAIRS_PER_KV) * BLOCK_Q, (p % PAIRS_PER_KV + 1) * BLOCK_Q)
            for half in range(2):
                h = 2 * p + half
                sink2 = sink_ref[h] * LOG2E

                def key_block(t):
                    rows = slice(half * head_w + t * LANES, half * head_w + (t + 1) * LANES)
                    blk = s_s[par, kv, rows, lanes]
                    if t == first_local:
                        blk = blk + bias_s[head_bias]
                    elif t == n_blk - 1:
                        blk = blk + bias_s[tail_bias]
                    return rows, blk

                mx = functools.reduce(jnp.maximum, [key_block(t)[1] for t in range(n_blk)])
                m = jnp.maximum(jnp.max(mx, axis=0, keepdims=True), sink2)
                for t in range(n_blk):
                    rows, blk = key_block(t)
                    p_s[par, kv, rows, lanes] = jnp.exp2(blk - m).astype(BF16)
                r_s[par, h] = jnp.broadcast_to(jnp.exp2(sink2 - m), (SUBLANES, BLOCK_Q))

    def weighted_values(sb, kvs):
        par, r0 = sb % BLOCK_BUFFERS, sb * BLOCK_Q
        for kv in kvs:
            pv = []
            for half in range(2):
                vals_t = jnp.concatenate([vctx_s[2 * kv + half], v_s[2 * kv + half, :, r0:r0 + KEY_SPAN]],
                                         axis=1)
                pv.append(_dot(vals_t, p_s[par, kv, half * head_w:(half + 1) * head_w, :]))
            for j in range(PAIRS_PER_KV):
                p = kv * PAIRS_PER_KV + j
                lanes = slice(j * BLOCK_Q, (j + 1) * BLOCK_Q)
                lo, hi = pv[0][:, lanes], pv[1][:, lanes]
                num = jnp.concatenate([lo[:HEAD_DIM], hi[HEAD_DIM:]], axis=0)
                den = jnp.concatenate([lo[HEAD_DIM:], hi[:HEAD_DIM]], axis=0)
                sink_t = jnp.concatenate([jnp.broadcast_to(r_s[par, 2 * p + half, 0:1], (HEAD_DIM, BLOCK_Q))
                                          for half in range(2)], axis=0)
                out = (num / (den + sink_t)).T
                gate = g_ref[0, r0:r0 + BLOCK_Q, p * LANES:(p + 1) * LANES].astype(F32)
                a_s[r0:r0 + BLOCK_Q, p * LANES:(p + 1) * LANES] = (out * gate).astype(BF16)

    def conv_edge(ref, row, keep):
        return jnp.where(keep, ref[0].astype(F32)[row:row + 1, :], 0.0)

    for r0 in range(0, tq, MIX_ROWS):
        rows = slice(r0, r0 + MIX_ROWS)
        if r0 == 0:
            above = conv_edge(tp_ref, BF16_ROWS - 1, i > 0)
        else:
            above = t_ref[0, r0 - BF16_ROWS:r0, :].astype(F32)[BF16_ROWS - 1:, :]
        if r0 + MIX_ROWS == tq:
            below = conv_edge(tn_ref, 0, i < pl.num_programs(1) - 1)
        else:
            below = t_ref[0, r0 + MIX_ROWS:r0 + MIX_ROWS + BF16_ROWS, :].astype(F32)[:1, :]
        c_s[rows] = _conv_rows(t_ref[0, rows].astype(F32), above, below, bg_ref[0, rows], cw_ref[0],
                               cb_ref[0])

    def mix(r0):
        rows = slice(r0, r0 + MIX_ROWS)
        o_ref[0, rows] = _mix_rows(a_s[rows], f_ref[0, rows], c_s[rows], w_ref[0], gpost_ref[0],
                                   mgate_ref[0, pl.ds(pl.program_id(0), 1), :], x_ref[0, rows])

    per_mix = MIX_ROWS // BLOCK_Q
    scores(0, range(KV_HEADS))
    for sb in range(n_blocks):
        for kv in range(KV_HEADS):
            if sb + 1 < n_blocks:
                scores(sb + 1, (kv,))
            softmax(sb, (kv,))
            weighted_values(sb, (kv,))
        if (sb + 1) % per_mix == 0:
            mix((sb + 1 - per_mix) * BLOCK_Q)


def _attention_and_mix(x, q, gate, k, v, kc, vc, sink, f, t, bg, conv_w, conv_b, w_out, g_post, mod,
                       layer, *, tq):
    b, n, _ = q.shape
    nc = kc.shape[1]
    per = tq // WINDOW
    last = n // WINDOW - 1
    kvw = 2 * KV_WIDTH
    variants = 2 * KV_HEADS
    main = lambda width: pl.BlockSpec((1, tq, width), lambda bi, i: (bi, i, 0))
    prev = pl.BlockSpec((1, WINDOW, kvw), lambda bi, i: (bi, jnp.maximum(i * per - 1, 0), 0))
    nxt = pl.BlockSpec((1, WINDOW, kvw), lambda bi, i: (bi, jnp.minimum((i + 1) * per, last), 0))
    ctx = pl.BlockSpec((1, nc, kvw), lambda bi, i: (bi, 0, 0))
    lay3 = lambda shape: pl.BlockSpec((1,) + shape, lambda bi, i: (layer, 0, 0))
    t_per = tq // BF16_ROWS
    t_last = n // BF16_ROWS - 1
    t_prev = pl.BlockSpec((1, BF16_ROWS, CONV_WIDTH), lambda bi, i: (bi, jnp.maximum(i * t_per - 1, 0), 0))
    t_next = pl.BlockSpec((1, BF16_ROWS, CONV_WIDTH),
                          lambda bi, i: (bi, jnp.minimum((i + 1) * t_per, t_last), 0))
    v_prev = pl.BlockSpec((1, kvw, WINDOW), lambda bi, i: (bi, 0, jnp.maximum(i * per - 1, 0)))
    v_main = pl.BlockSpec((1, kvw, tq), lambda bi, i: (bi, 0, i))
    v_next = pl.BlockSpec((1, kvw, WINDOW), lambda bi, i: (bi, 0, jnp.minimum((i + 1) * per, last)))
    span = tq + 2 * WINDOW
    s_rows = 2 * (nc + KEY_SPAN)
    group_q = PAIRS_PER_KV * BLOCK_Q
    return pl.pallas_call(
        functools.partial(_attn_kernel, tq=tq, n_seq=n, n_ctx=nc),
        grid=(b, n // tq),
        in_specs=[pl.BlockSpec(memory_space=pltpu.SMEM), main(ATTN_WIDTH), main(ATTN_WIDTH),
                  prev, main(kvw), nxt, v_prev, v_main, v_next, ctx, ctx,
                  main(D_MODEL),
                  main(FOURIER_WIDTH), main(CONV_WIDTH), t_prev, t_next, main(CONV_WIDTH),
                  lay3((3, CONV_WIDTH)), lay3((1, CONV_WIDTH)), lay3((MIX_WIDTH, D_MODEL)),
                  lay3((1, D_MODEL)),
                  pl.BlockSpec((1, MOD_ROWS, D_MODEL), lambda bi, i: (layer, 0, 2))],
        out_specs=main(D_MODEL),
        out_shape=jax.ShapeDtypeStruct(x.shape, F32),
        scratch_shapes=[pltpu.VMEM((variants, span, KV_WIDTH), BF16),
                        pltpu.VMEM((variants, KV_WIDTH, span), BF16),
                        pltpu.VMEM((variants, nc, KV_WIDTH), BF16),
                        pltpu.VMEM((variants, KV_WIDTH, nc), BF16),
                        pltpu.VMEM((4, WINDOW, BLOCK_Q), F32),
                        pltpu.VMEM((BLOCK_BUFFERS, KV_HEADS, s_rows, group_q), F32),
                        pltpu.VMEM((BLOCK_BUFFERS, KV_HEADS, s_rows, group_q), BF16),
                        pltpu.VMEM((BLOCK_BUFFERS, ATTN_HEADS, SUBLANES, BLOCK_Q), F32),
                        pltpu.VMEM((tq, ATTN_WIDTH), BF16),
                        pltpu.VMEM((tq, CONV_WIDTH), BF16)],
        compiler_params=_cparams("arbitrary", "arbitrary"),
        name="attention_and_mix",
    )(sink, q, gate, k, k, k, v, v, v, kc, vc, x, f, t, t, t, bg, conv_w, conv_b, w_out, g_post, mod)


def _ctx_attn_kernel(sink_ref, q_ref, g_ref, kc_ref, vc_ref, o_ref):
    keys = _lane_half_variants(kc_ref[0], 0.0)
    vals = _lane_half_variants(vc_ref[0], 1.0)
    for p in range(HEAD_PAIRS):
        kv = p // PAIRS_PER_KV
        qp = q_ref[0, :, p * LANES:(p + 1) * LANES]
        pv, sink_terms = [], []
        for half in range(2):
            s = _dot_nt(qp, keys[2 * kv + half])
            sink2 = sink_ref[2 * p + half] * LOG2E
            m = jnp.maximum(jnp.max(s, axis=-1, keepdims=True), sink2)
            pv.append(_dot(jnp.exp2(s - m).astype(BF16), vals[2 * kv + half]))
            sink_terms.append(jnp.exp2(sink2 - m))
        out = _merge_head_pair(pv[0], pv[1], sink_terms[0], sink_terms[1])
        gate = g_ref[0, :, p * LANES:(p + 1) * LANES].astype(F32)
        o_ref[0, :, p * LANES:(p + 1) * LANES] = (out * gate).astype(BF16)


def _context_attention(q, gate, kc, vc, sink):
    b, nc, _ = q.shape
    blk = lambda width: pl.BlockSpec((1, nc, width), lambda bi: (bi, 0, 0))
    return pl.pallas_call(
        _ctx_attn_kernel,
        grid=(b,),
        in_specs=[pl.BlockSpec(memory_space=pltpu.SMEM), blk(ATTN_WIDTH), blk(ATTN_WIDTH),
                  blk(2 * KV_WIDTH), blk(2 * KV_WIDTH)],
        out_specs=blk(ATTN_WIDTH),
        out_shape=jax.ShapeDtypeStruct((b, nc, ATTN_WIDTH), BF16),
        compiler_params=_cparams("arbitrary"),
        name="context_attention",
    )(sink, q, gate, kc, vc)


def _channel_mix_matrix(cc_ref, sc_ref, wf_ref, scale):
    wf = wf_ref[0]
    return (jnp.concatenate([_dot(cc_ref[...], wf), -_dot(sc_ref[...], wf)], axis=1) * scale).astype(BF16)


def _fourier_kernel(uf_ref, gate_ref, cc_ref, sc_ref, wf_ref, g_ref, m2_ref, o_ref, mix_s, y_s, *, n_seq):
    n1_len = n_seq // FFT_N2
    chunk = pl.program_id(1)
    halves = FOURIER_WIDTH // LANES

    @pl.when(chunk == 0)
    def _():
        zero = jnp.zeros((LANES, LANES), F32)
        scale = (n_seq * FOURIER_GROUP_DIM) ** -0.5
        re = [_dot(cc_ref[...], wf_ref[h]) * scale for h in range(halves)]
        im = [_dot(sc_ref[...], wf_ref[h]) * -scale for h in range(halves)]
        for h in range(halves):
            row = [re[h] if j == h else zero for j in range(halves)]
            row += [im[h] if j == h else zero for j in range(halves)]
            mix_s[h * LANES:(h + 1) * LANES] = jnp.concatenate(row, axis=1).astype(BF16)

    zs = [_dot(uf_ref[0, u].astype(BF16), mix_s[...]) for u in range(N2_CHUNK)]
    for u, z in enumerate(zs):
        rhs = jnp.concatenate([z[:, :FOURIER_WIDTH], z[:, FOURIER_WIDTH:]], axis=0).astype(BF16)
        y = _dot(g_ref[chunk * N2_CHUNK + u], rhs)
        y_s[chunk * N2_CHUNK + u] = y.astype(BF16).reshape(n1_len // K1_GROUP, 2 * K1_GROUP, FOURIER_WIDTH)

    @pl.when(chunk == pl.num_programs(1) - 1)
    def _():
        per_store = BF16_ROWS // K1_GROUP

        def stage2(t, carry):
            outs = []
            for u in range(per_store):
                rhs = y_s[:, t * per_store + u].reshape(FFT_N2 * 2 * K1_GROUP, FOURIER_WIDTH)
                outs.append(_dot(m2_ref[...], rhs).reshape(FFT_N2, K1_GROUP, FOURIER_WIDTH))
            r0 = pl.multiple_of(t * BF16_ROWS, BF16_ROWS)
            gate = gate_ref[0, :, pl.ds(r0, BF16_ROWS), :].astype(F32)
            o_ref[0, :, pl.ds(r0, BF16_ROWS), :] = (jnp.concatenate(outs, axis=1) * gate).astype(BF16)
            return carry

        lax.fori_loop(0, n1_len // BF16_ROWS, stage2, 0, unroll=4)


def _fourier_mix(uf, gate, wf_half, consts):
    b, _, n1_len, _ = uf.shape
    n = FFT_N2 * n1_len
    cc, sc, g, m2 = consts
    full = lambda shape: pl.BlockSpec(shape, lambda bi, c: (0,) * len(shape))
    whole = pl.BlockSpec((1, FFT_N2, n1_len, FOURIER_WIDTH), lambda bi, c: (bi, 0, 0, 0))
    out = pl.pallas_call(
        functools.partial(_fourier_kernel, n_seq=n),
        grid=(b, FFT_N2 // N2_CHUNK),
        in_specs=[pl.BlockSpec((1, N2_CHUNK, n1_len, FOURIER_WIDTH), lambda bi, c: (bi, c, 0, 0)),
                  whole, full((LANES, LANES)), full((LANES, LANES)), full(wf_half.shape),
                  full(g.shape), full(m2.shape)],
        out_specs=whole,
        out_shape=jax.ShapeDtypeStruct((b, FFT_N2, n1_len, FOURIER_WIDTH), BF16),
        scratch_shapes=[pltpu.VMEM((FOURIER_WIDTH, 2 * FOURIER_WIDTH), BF16),
                        pltpu.VMEM((FFT_N2, n1_len // K1_GROUP, 2 * K1_GROUP, FOURIER_WIDTH), BF16)],
        compiler_params=_cparams("arbitrary", "arbitrary"),
        name="fourier_mix",
    )(uf, gate.reshape(b, FFT_N2, n1_len, FOURIER_WIDTH), cc, sc, wf_half, g, m2)
    return out.reshape(b, n, FOURIER_WIDTH)


def _ctx_fourier_kernel(uf_ref, gate_ref, cc_ref, sc_ref, wf_ref, dft_ref, o_ref, *, n_seq):
    mix = _channel_mix_matrix(cc_ref, sc_ref, wf_ref, (n_seq * FOURIER_GROUP_DIM) ** -0.5)
    z = _dot(uf_ref[0], mix)
    rhs = jnp.concatenate([z[:, :LANES], z[:, LANES:]], axis=0).astype(BF16)
    o_ref[0] = (_dot(dft_ref[...], rhs) * gate_ref[0].astype(F32)).astype(BF16)


def _ctx_fourier_mix(uf, gate, wf_half, cc, sc, dft):
    b, n, _ = uf.shape
    halves = FOURIER_WIDTH // LANES
    full = lambda shape: pl.BlockSpec(shape, lambda bi, hf: (0,) * len(shape))
    half = pl.BlockSpec((1, n, LANES), lambda bi, hf: (bi, 0, hf))
    return pl.pallas_call(
        functools.partial(_ctx_fourier_kernel, n_seq=n),
        grid=(b, halves),
        in_specs=[half, half, full((LANES, LANES)), full((LANES, LANES)),
                  pl.BlockSpec((1, LANES, LANES), lambda bi, hf: (hf, 0, 0)), full(dft.shape)],
        out_specs=half,
        out_shape=jax.ShapeDtypeStruct((b, n, FOURIER_WIDTH), BF16),
        compiler_params=_cparams("arbitrary", "arbitrary"),
        name="context_fourier_mix",
    )(uf, gate, cc, sc, wf_half, dft)


def _ctx_mix_kernel(x_ref, a_ref, f_ref, t_ref, bg_ref, cw_ref, cb_ref, w_ref, g_ref, gate_ref, o_ref,
                    *, ctx_row):
    edge = jnp.zeros((1, CONV_WIDTH), F32)
    conv = _conv_rows(t_ref[0].astype(F32), edge, edge, bg_ref[0], cw_ref[0], cb_ref[0])
    o_ref[0] = _mix_rows(a_ref[0], f_ref[0], conv, w_ref[0], g_ref[0], gate_ref[0, ctx_row:ctx_row + 1, :],
                         x_ref[0])


def _context_mix(x, a, f, t, bg, conv_w, conv_b, w_out, g_post, mod, layer, *, ctx_row):
    b, n, _ = x.shape
    row3 = lambda width: pl.BlockSpec((1, n, width), lambda bi: (bi, 0, 0))
    lay3 = lambda shape: pl.BlockSpec((1,) + shape, lambda bi: (layer, 0, 0))
    return pl.pallas_call(
        functools.partial(_ctx_mix_kernel, ctx_row=ctx_row),
        grid=(b,),
        in_specs=[row3(D_MODEL), row3(ATTN_WIDTH), row3(FOURIER_WIDTH), row3(CONV_WIDTH), row3(CONV_WIDTH), lay3((3, CONV_WIDTH)), lay3((1, CONV_WIDTH)),
                  lay3((MIX_WIDTH, D_MODEL)), lay3((1, D_MODEL)),
                  pl.BlockSpec((1, MOD_ROWS, D_MODEL), lambda bi: (layer, 0, 2))],
        out_specs=row3(D_MODEL),
        out_shape=jax.ShapeDtypeStruct(x.shape, F32),
        compiler_params=_cparams("arbitrary"),
        name="context_mix",
    )(x, a, f, t, bg, conv_w, conv_b, w_out, g_post, mod)


def kernel(x, c, ctx, c_ctx, w_mod, b_mod, g_pre, g_post, w_in, w_out, sink, w_fourier, conv_w, conv_b):
    depth = w_mod.shape[0]
    b, n, _ = x.shape
    nc = ctx.shape[1]
    assert b + 1 <= MOD_ROWS and n % GRID_W == 0 and n % ATTN_ROWS == 0 and n % TILE_ROWS == 0
    assert (n // FFT_N2) % BF16_ROWS == 0 and FFT_N2 % N2_CHUNK == 0

    w_in_b = w_in.astype(BF16)
    w_out_b = w_out.astype(BF16)

    rope_tabs = tuple(jnp.asarray(t) for t in _rope_tables(n))
    cc, sc = (jnp.asarray(m) for m in _channel_dft())
    stage1 = jnp.asarray(_stage1_mats(n)).astype(BF16)
    stage2 = jnp.asarray(_stage2_mat()).astype(BF16)
    ctx_dft = jnp.asarray(_dense_dft(nc)).astype(BF16)
    groups_per_half = LANES // FOURIER_GROUP_DIM
    wf_half = jnp.zeros((depth, FOURIER_GROUPS // groups_per_half, LANES, LANES), F32)
    for g in range(FOURIER_GROUPS):
        o = (g % groups_per_half) * FOURIER_GROUP_DIM
        wf_half = wf_half.at[:, g // groups_per_half, o:o + FOURIER_GROUP_DIM,
                             o:o + FOURIER_GROUP_DIM].set(w_fourier[:, g])

    c_rows = jnp.zeros((MOD_ROWS, D_MODEL), F32).at[:b].set(c).at[b].set(c_ctx)
    mod = _modulation(c_rows, w_mod, b_mod)
    g_pre3 = g_pre.reshape(depth, 1, D_MODEL)
    g_post3 = g_post.reshape(depth, 1, D_MODEL)
    conv_b3 = conv_b.reshape(depth, 1, CONV_WIDTH)

    for l in range(depth):
        update_ctx = l < depth - 1
        ctx_rows = ctx.reshape(1, b * nc, D_MODEL)
        ctx_proj = _in_projection(ctx_rows, mod, g_pre3, w_in_b, l, rope_tabs=None, ctx_row=b,
                                  tm=b * nc, kv_only=not update_ctx)
        ctx_proj = [y.reshape(b, nc, y.shape[-1]) for y in ctx_proj]
        if update_ctx:
            qc, kc, vc, sgac, ufc, sgfc, tc, bgc = ctx_proj
        else:
            kc, vc = ctx_proj
        q, k, v, sga, uf, sgf, t, bg = _in_projection(
            x, mod, g_pre3, w_in_b, l, rope_tabs=rope_tabs, ctx_row=None, tm=TILE_ROWS)
        f = _fourier_mix(uf, sgf, wf_half[l], (cc, sc, stage1, stage2))
        x = _attention_and_mix(x, q, sga, k, v, kc, vc, sink[l], f, t, bg, conv_w, conv_b3, w_out_b,
                               g_post3, mod, l, tq=ATTN_ROWS)
        if update_ctx:
            ac = _context_attention(qc, sgac, kc, vc, sink[l])
            fc = _ctx_fourier_mix(ufc, sgfc, wf_half[l], cc, sc, ctx_dft)
            ctx = _context_mix(ctx, ac, fc, tc, bgc, conv_w, conv_b3, w_out_b, g_post3, mod, l,
                               ctx_row=b)
    return x
```
